```python
import math, functools
import jax
import jax.numpy as jnp
from jax import lax
import numpy as np

D_MODEL = 1024
BATCH = 32
SEQ = 2048
DEPTH = 4

GRID_W = 64
CTX_LEN = 256
N_BRANCH = 4
BRANCH_WIDTH = D_MODEL // 2
ATT_HEAD_DIM = 64
ATT_HEADS = BRANCH_WIDTH // ATT_HEAD_DIM
ATT_KV_HEADS = ATT_HEADS // 4
Q_BLOCK = 128
ROPE_THETA = 10000.0
DN_HEAD_DIM = 128
DN_HEADS = BRANCH_WIDTH // DN_HEAD_DIM
DN_CONV = 3
RET_HEADS = 4
RET_V_DIM = BRANCH_WIDTH // RET_HEADS
RET_K_DIM = RET_V_DIM // 2
SC_WIDTH = BRANCH_WIDTH
SC_CONV = 3
CHUNK = 64
MLP_HIDDEN = 4 * D_MODEL
EPS = 1e-6

ATT_Q_W = ATT_HEADS * ATT_HEAD_DIM
ATT_KV_W = ATT_KV_HEADS * ATT_HEAD_DIM
DN_W = DN_HEADS * DN_HEAD_DIM
RET_QK_W = RET_HEADS * RET_K_DIM
RET_V_W = RET_HEADS * RET_V_DIM
IN_SPLITS = (ATT_Q_W, ATT_KV_W, ATT_KV_W,
             DN_W, DN_W, DN_W, DN_W,
             DN_HEADS, DN_HEADS, DN_HEADS, DN_HEADS,
             RET_QK_W, RET_QK_W, RET_V_W, RET_V_W,
             SC_WIDTH, SC_WIDTH, SC_WIDTH,
             N_BRANCH * D_MODEL)
N_IN = sum(IN_SPLITS)

kernel_name = 'hybrid_parallel_diffusion_trunk'


def rms_norm(x, gain):
    xf = x.astype(jnp.float32)
    y = xf * lax.rsqrt(jnp.mean(xf * xf, axis=-1, keepdims=True) + EPS)
    return (y * gain.astype(jnp.float32)).astype(x.dtype)


def l2_norm(x):
    xf = x.astype(jnp.float32)
    return xf * lax.rsqrt(jnp.sum(xf * xf, axis=-1, keepdims=True) + EPS)


def head_norm(o):
    mu = jnp.mean(o, axis=-1, keepdims=True)
    var = jnp.mean(jnp.square(o - mu), axis=-1, keepdims=True)
    return (o - mu) * lax.rsqrt(var + EPS)


def rope_tables(rows, head_dim):
    r, col = jnp.meshgrid(jnp.arange(rows), jnp.arange(GRID_W), indexing='ij')
    quarter = head_dim // 4
    inv_freq = ROPE_THETA ** (-jnp.arange(quarter, dtype=jnp.float32) / quarter)
    ang = jnp.concatenate([r.reshape(-1, 1).astype(jnp.float32) * inv_freq,
                           col.reshape(-1, 1).astype(jnp.float32) * inv_freq], axis=-1)
    return jnp.cos(ang), jnp.sin(ang)


def apply_rope(x, cos, sin):
    half = x.shape[-1] // 2
    x1, x2 = x[..., :half], x[..., half:]
    c = cos[None, :, None, :].astype(x.dtype)
    s = sin[None, :, None, :].astype(x.dtype)
    return jnp.concatenate([x1 * c - x2 * s, x1 * s + x2 * c], axis=-1)


def dw_conv(x, w):
    width = w.shape[0]
    return lax.conv_general_dilated(x, w[:, None, :].astype(x.dtype), window_strides=(1,),
                                    padding=[(width // 2, width // 2)],
                                    dimension_numbers=('NWC', 'WIO', 'NWC'),
                                    feature_group_count=x.shape[-1])


def project_in(h, w_in, n_parts):
    bounds = np.cumsum(IN_SPLITS)[:-1].tolist()
    return [h @ w for w in jnp.split(w_in, bounds, axis=1)[:n_parts]]


def _sdpa(q, k, v):
    b, nq, hq, dh = q.shape
    hkv = k.shape[2]
    qg = q.reshape(b, nq, hkv, hq // hkv, dh)
    s = jnp.einsum('bqhgd,bkhd->bhgqk', qg, k).astype(jnp.float32) * (dh ** -0.5)
    p = jax.nn.softmax(s, axis=-1).astype(v.dtype)
    o = jnp.einsum('bhgqk,bkhd->bqhgd', p, v)
    return o.reshape(b, nq, hq * dh)


def attention_mixer(lat, ctx, q_gain, k_gain, rope, need_ctx):
    def heads(t, n_heads):
        return t.reshape(t.shape[0], t.shape[1], n_heads, ATT_HEAD_DIM)
    q_l, k_l, v_l = lat
    q_c, k_c, v_c = ctx
    bsz, n, _ = q_l.shape
    q_l = apply_rope(rms_norm(heads(q_l, ATT_HEADS), q_gain), *rope)
    k_l = apply_rope(rms_norm(heads(k_l, ATT_KV_HEADS), k_gain), *rope)
    q_c = rms_norm(heads(q_c, ATT_HEADS), q_gain)
    k_c = rms_norm(heads(k_c, ATT_KV_HEADS), k_gain)
    v_l, v_c = heads(v_l, ATT_KV_HEADS), heads(v_c, ATT_KV_HEADS)
    keys = jnp.concatenate([k_c, k_l], axis=1)
    vals = jnp.concatenate([v_c, v_l], axis=1)
    q_blocks = jnp.moveaxis(q_l.reshape(bsz, n // Q_BLOCK, Q_BLOCK, ATT_HEADS, ATT_HEAD_DIM), 1, 0)
    o_blocks = lax.map(lambda qb: _sdpa(qb, keys, vals), q_blocks)
    y_l = jnp.moveaxis(o_blocks, 0, 1).reshape(bsz, n, ATT_Q_W)
    y_c = _sdpa(q_c, k_c, v_c) if need_ctx else None
    return y_l, y_c


def _to_chunks(t, n):
    t = t.astype(jnp.float32).reshape((t.shape[0], n, CHUNK) + t.shape[2:])
    return jnp.moveaxis(t, 2, 3)


def _from_chunks(t):
    t = jnp.moveaxis(t, 3, 2)
    return t.reshape(t.shape[0], t.shape[1] * t.shape[2], t.shape[3], t.shape[4])


def gated_delta_rule(q, k, v, log_a, beta, s0):
    bsz, length, _, _ = q.shape
    dv = v.shape[-1]
    n = length // CHUNK
    q, k, v = _to_chunks(q, n), _to_chunks(k, n), _to_chunks(v, n)
    log_a, beta = _to_chunks(log_a, n), _to_chunks(beta, n)
    g = jnp.cumsum(log_a, axis=-1)
    diff = g[..., :, None] - g[..., None, :]
    incl = jnp.tril(jnp.ones((CHUNK, CHUNK), dtype=bool))
    strict = jnp.tril(jnp.ones((CHUNK, CHUNK), dtype=bool), -1)
    dec_incl = jnp.exp(jnp.where(incl, diff, -jnp.inf))
    dec_strict = jnp.where(strict, dec_incl, 0.0)
    kk = jnp.einsum('bnhid,bnhjd->bnhij', k, k)
    a_mat = beta[..., :, None] * kk * dec_strict + jnp.eye(CHUNK, dtype=jnp.float32)
    rhs = jnp.concatenate([beta[..., None] * v, (beta * jnp.exp(g))[..., None] * k], axis=-1)
    sol = lax.linalg.triangular_solve(a_mat, rhs, left_side=True, lower=True, unit_diagonal=True)
    w_v, w_k = sol[..., :dv], sol[..., dv:]
    a_qk = jnp.einsum('bnhid,bnhjd->bnhij', q, k) * dec_incl
    q_dec = q * jnp.exp(g)[..., None]
    g_last = g[..., -1]
    k_dec = k * jnp.exp(g_last[..., None] - g)[..., None]

    def step(s, inp):
        w_v_i, w_k_i, a_i, qd_i, kd_i, gl_i = inp
        u = w_v_i - jnp.einsum('bhcd,bhde->bhce', w_k_i, s)
        o = jnp.einsum('bhcd,bhde->bhce', qd_i, s) + jnp.einsum('bhij,bhje->bhie', a_i, u)
        s = s * jnp.exp(gl_i)[..., None, None] + jnp.einsum('bhcd,bhce->bhde', kd_i, u)
        return s, o

    xs = tuple(jnp.moveaxis(t, 1, 0) for t in (w_v, w_k, a_qk, q_dec, k_dec, g_last))
    s_fin, o = lax.scan(step, s0.astype(jnp.float32), xs)
    return _from_chunks(jnp.moveaxis(o, 0, 1)), s_fin


def retention_chunked(q, k, v, s0, log_g):
    length = q.shape[1]
    n = length // CHUNK
    q, k, v = _to_chunks(q, n), _to_chunks(k, n), _to_chunks(v, n)
    idx = jnp.arange(CHUNK, dtype=jnp.float32)
    lg = log_g.astype(jnp.float32)[:, None]
    rel = idx[:, None] - idx[None, :]
    dmask = jnp.exp(jnp.where(rel >= 0, rel[None] * lg[:, :, None], -jnp.inf))
    scores = jnp.einsum('bnhid,bnhjd->bnhij', q, k) * dmask
    o = jnp.einsum('bnhij,bnhje->bnhie', scores, v)
    k_dec = k * jnp.exp((CHUNK - 1 - idx)[None, :] * lg)[..., None]
    kv = jnp.einsum('bnhcd,bnhce->bnhde', k_dec, v)
    chunk_decay = jnp.exp(CHUNK * lg)[:, :, None]

    def step(s, kv_i):
        return s * chunk_decay + kv_i, s

    s_fin, s_prev = lax.scan(step, s0.astype(jnp.float32), jnp.moveaxis(kv, 1, 0))
    q_dec = q * jnp.exp((idx + 1)[None, :] * lg)[..., None]
    o = o + jnp.einsum('bnhcd,bnhde->bnhce', q_dec, jnp.moveaxis(s_prev, 0, 1))
    return _from_chunks(o), s_fin


def _ctx_then_latent(fn, ctx_args, lat_args, s0, reverse):
    def flip(ts):
        return tuple(jnp.flip(t, axis=1) for t in ts) if reverse else tuple(ts)
    o_c, s_c = fn(*flip(ctx_args), s0)
    o_l, _ = fn(*flip(lat_args), s_c)
    if reverse:
        o_c, o_l = jnp.flip(o_c, axis=1), jnp.flip(o_l, axis=1)
    return o_l, o_c


def deltanet_mixer(lat, ctx, conv_w, a_log, dt_bias, norm_gain, need_ctx):
    a_coef = -jnp.exp(a_log.astype(jnp.float32))

    def prep(q, k, v, z, a_f, a_b, b_f, b_b):
        bsz, n, _ = q.shape
        qkv = jax.nn.silu(dw_conv(jnp.concatenate([q, k, v], axis=-1), conv_w))
        q, k, v = jnp.split(qkv, 3, axis=-1)
        q = l2_norm(q.reshape(bsz, n, DN_HEADS, DN_HEAD_DIM)) * (DN_HEAD_DIM ** -0.5)
        k = l2_norm(k.reshape(bsz, n, DN_HEADS, DN_HEAD_DIM))
        v = v.reshape(bsz, n, DN_HEADS, DN_HEAD_DIM)
        dirs = []
        for d, (a_in, b_in) in enumerate(((a_f, b_f), (a_b, b_b))):
            log_a = a_coef[d] * jax.nn.softplus(a_in.astype(jnp.float32) + dt_bias[d].astype(jnp.float32))
            dirs.append((q, k, v, log_a, jax.nn.sigmoid(b_in.astype(jnp.float32))))
        return dirs, z

    lat_dirs, z_l = prep(*lat)
    ctx_dirs, z_c = prep(*ctx)
    bsz = z_l.shape[0]
    s0 = jnp.zeros((bsz, DN_HEADS, DN_HEAD_DIM, DN_HEAD_DIM), jnp.float32)
    fwd = _ctx_then_latent(gated_delta_rule, ctx_dirs[0], lat_dirs[0], s0, False)
    bwd = _ctx_then_latent(gated_delta_rule, ctx_dirs[1], lat_dirs[1], s0, True)

    def finish(o, z):
        zf = jax.nn.silu(z.astype(jnp.float32)).reshape(o.shape)
        y = rms_norm(o, norm_gain) * zf
        return y.reshape(o.shape[0], o.shape[1], DN_W).astype(z.dtype)

    y_l = finish(fwd[0] + bwd[0], z_l)
    y_c = finish(fwd[1] + bwd[1], z_c) if need_ctx else None
    return y_l, y_c


def retention_mixer(lat, ctx, ret_decay, rope, need_ctx):
    def prep(q, k, v, g, with_pos):
        bsz, n, _ = q.shape
        q = q.reshape(bsz, n, RET_HEADS, RET_K_DIM)
        k = k.reshape(bsz, n, RET_HEADS, RET_K_DIM) * (RET_K_DIM ** -0.5)
        if with_pos:
            q, k = apply_rope(q, *rope), apply_rope(k, *rope)
        v = v.reshape(bsz, n, RET_HEADS, RET_V_DIM)
        return (q, k, v), g

    lat_args, g_l = prep(*lat, True)
    ctx_args, g_c = prep(*ctx, False)
    log_g = -jnp.exp(ret_decay.astype(jnp.float32))
    bsz = g_l.shape[0]
    s0 = jnp.zeros((bsz, RET_HEADS, RET_K_DIM, RET_V_DIM), jnp.float32)
    fwd = _ctx_then_latent(functools.partial(retention_chunked, log_g=log_g[0]), ctx_args, lat_args, s0, False)
    bwd = _ctx_then_latent(functools.partial(retention_chunked, log_g=log_g[1]), ctx_args, lat_args, s0, True)

    def finish(o, g):
        y = head_norm(o).reshape(o.shape[0], o.shape[1], RET_V_W) * jax.nn.silu(g.astype(jnp.float32))
        return y.astype(g.dtype)

    y_l = finish(fwd[0] + bwd[0], g_l)
    y_c = finish(fwd[1] + bwd[1], g_c) if need_ctx else None
    return y_l, y_c


def shortconv_mixer(b_gate, c_gate, x_in, w):
    return b_gate * dw_conv(c_gate * x_in, w)


def merge_branches(ys, gates, w_branch, w_out):
    g = jax.nn.sigmoid(gates.reshape(gates.shape[:-1] + (N_BRANCH, D_MODEL)))
    u = g[..., 0, :] * (ys[0] @ w_branch[0])
    for i in range(1, N_BRANCH):
        u = u + g[..., i, :] * (ys[i] @ w_branch[i])
    return u @ w_out


def sq_relu_mlp(h, w1, w2):
    return jnp.square(jax.nn.relu(h @ w1)) @ w2


def layer(x, xc, c, c_ctx, w_mod, b_mod, g_norm, w_in, att_q_gain, att_k_gain, dn_conv, dn_a_log,
          dn_dt_bias, dn_norm_gain, ret_decay, sc_conv, w_branch, w_out, w_mlp_in, w_mlp_out,
          att_rope, ret_rope, need_ctx):
    mod = jax.nn.silu(c) @ w_mod + b_mod
    mod_c = jax.nn.silu(c_ctx) @ w_mod + b_mod
    sh1, sc1, ga1, sh2, sc2, ga2 = jnp.split(mod[:, None, :], 6, axis=-1)
    sh1c, sc1c, ga1c, sh2c, sc2c, ga2c = jnp.split(mod_c, 6, axis=-1)

    h = rms_norm(x, g_norm[0]) * (1 + sc1) + sh1
    hc = rms_norm(xc, g_norm[0]) * (1 + sc1c) + sh1c
    zl = project_in(h, w_in, len(IN_SPLITS))
    zc = project_in(hc, w_in, len(IN_SPLITS) if need_ctx else 15)
    y_att, yc_att = attention_mixer(zl[0:3], zc[0:3], att_q_gain, att_k_gain, att_rope, need_ctx)
    y_dn, yc_dn = deltanet_mixer(zl[3:11], zc[3:11], dn_conv, dn_a_log, dn_dt_bias, dn_norm_gain, need_ctx)
    y_ret, yc_ret = retention_mixer(zl[11:15], zc[11:15], ret_decay, ret_rope, need_ctx)
    y_sc = shortconv_mixer(zl[15], zl[16], zl[17], sc_conv)
    y = merge_branches((y_att, y_dn, y_ret, y_sc), zl[18], w_branch, w_out)
    x = x + ga1 * rms_norm(y, g_norm[1])

    h2 = rms_norm(x, g_norm[2]) * (1 + sc2) + sh2
    x = x + ga2 * rms_norm(sq_relu_mlp(h2, w_mlp_in, w_mlp_out), g_norm[3])

    if need_ctx:
        yc_sc = shortconv_mixer(zc[15], zc[16], zc[17], sc_conv)
        yc = merge_branches((yc_att, yc_dn, yc_ret, yc_sc), zc[18], w_branch, w_out)
        xc = xc + ga1c * rms_norm(yc, g_norm[1])
        h2c = rms_norm(xc, g_norm[2]) * (1 + sc2c) + sh2c
        xc = xc + ga2c * rms_norm(sq_relu_mlp(h2c, w_mlp_in, w_mlp_out), g_norm[3])
    return x, xc


def _fwd_setup_inputs(seed: int = 0) -> dict:
    key = jax.random.key(seed)
    ks = jax.random.split(key, 24)
    f32 = jnp.float32

    def nrm(k, shape, scale):
        return jax.random.normal(k, shape, f32) * scale

    x = nrm(ks[0], (BATCH, SEQ, D_MODEL), 1.0)
    c = nrm(ks[1], (BATCH, D_MODEL), 1.0)
    ctx = nrm(ks[2], (BATCH, CTX_LEN, D_MODEL), 1.0)
    c_ctx = nrm(ks[3], (D_MODEL,), 1.0)
    w_mod = nrm(ks[4], (DEPTH, D_MODEL, 6 * D_MODEL), 0.5 * D_MODEL ** -0.5)
    b_mod = nrm(ks[5], (DEPTH, 6 * D_MODEL), 0.01)
    g_norm = 1.0 + nrm(ks[6], (DEPTH, 4, D_MODEL), 0.05)
    w_in = nrm(ks[7], (DEPTH, D_MODEL, N_IN), D_MODEL ** -0.5)
    att_q_gain = 1.0 + nrm(ks[8], (DEPTH, ATT_HEAD_DIM), 0.05)
    att_k_gain = 1.0 + nrm(ks[9], (DEPTH, ATT_HEAD_DIM), 0.05)
    dn_conv = nrm(ks[10], (DEPTH, DN_CONV, 3 * DN_W), DN_CONV ** -0.5)
    dn_a_log = jnp.log(jax.random.uniform(ks[11], (DEPTH, 2, DN_HEADS), f32, 1.0, 16.0))
    dt = jnp.exp(jax.random.uniform(ks[12], (DEPTH, 2, DN_HEADS), f32, math.log(1e-3), math.log(0.1)))
    dn_dt_bias = dt + jnp.log(-jnp.expm1(-dt))
    dn_norm_gain = 1.0 + nrm(ks[13], (DEPTH, DN_HEAD_DIM), 0.05)
    base = jnp.log(-jnp.log(1.0 - 2.0 ** (-5.0 - jnp.arange(RET_HEADS, dtype=f32))))
    ret_decay = base + nrm(ks[14], (DEPTH, 2, RET_HEADS), 0.05)
    sc_conv = nrm(ks[15], (DEPTH, SC_CONV, SC_WIDTH), SC_CONV ** -0.5)
    w_branch = nrm(ks[16], (DEPTH, N_BRANCH, BRANCH_WIDTH, D_MODEL), BRANCH_WIDTH ** -0.5)
    w_out = nrm(ks[17], (DEPTH, D_MODEL, D_MODEL), D_MODEL ** -0.5)
    w_mlp_in = nrm(ks[18], (DEPTH, D_MODEL, MLP_HIDDEN), D_MODEL ** -0.5)
    w_mlp_out = nrm(ks[19], (DEPTH, MLP_HIDDEN, D_MODEL), MLP_HIDDEN ** -0.5)
    return {'x': x, 'c': c, 'ctx': ctx, 'c_ctx': c_ctx, 'w_mod': w_mod, 'b_mod': b_mod,
            'g_norm': g_norm, 'w_in': w_in, 'att_q_gain': att_q_gain, 'att_k_gain': att_k_gain,
            'dn_conv': dn_conv, 'dn_a_log': dn_a_log, 'dn_dt_bias': dn_dt_bias,
            'dn_norm_gain': dn_norm_gain, 'ret_decay': ret_decay, 'sc_conv': sc_conv,
            'w_branch': w_branch, 'w_out': w_out, 'w_mlp_in': w_mlp_in, 'w_mlp_out': w_mlp_out}


def _fwd_reference(x, c, ctx, c_ctx, w_mod, b_mod, g_norm, w_in, att_q_gain, att_k_gain, dn_conv,
              dn_a_log, dn_dt_bias, dn_norm_gain, ret_decay, sc_conv, w_branch, w_out,
              w_mlp_in, w_mlp_out):
    ROWS = x.shape[1] // GRID_W
    att_rope = rope_tables(ROWS, ATT_HEAD_DIM)
    ret_rope = rope_tables(ROWS, RET_K_DIM)
    xc = ctx
    for l in range(DEPTH):
        x, xc = layer(x, xc, c, c_ctx, w_mod[l], b_mod[l], g_norm[l], w_in[l], att_q_gain[l],
                      att_k_gain[l], dn_conv[l], dn_a_log[l], dn_dt_bias[l], dn_norm_gain[l],
                      ret_decay[l], sc_conv[l], w_branch[l], w_out[l], w_mlp_in[l], w_mlp_out[l],
                      att_rope, ret_rope, l < DEPTH - 1)
    return x


import jax as _jax
import jax.numpy as _jnp

TWIN_FORMAT = 'train_step'
FWD_PARAMS = ['x', 'c', 'ctx', 'c_ctx', 'w_mod', 'b_mod', 'g_norm', 'w_in', 'att_q_gain', 'att_k_gain', 'dn_conv', 'dn_a_log', 'dn_dt_bias', 'dn_norm_gain', 'ret_decay', 'sc_conv', 'w_branch', 'w_out', 'w_mlp_in', 'w_mlp_out']
TWIN_WEIGHTS = ['c_ctx', 'w_mod', 'b_mod', 'g_norm', 'w_in', 'att_q_gain', 'att_k_gain', 'dn_conv', 'dn_a_log', 'dn_dt_bias', 'dn_norm_gain', 'ret_decay', 'sc_conv', 'w_branch', 'w_out', 'w_mlp_in', 'w_mlp_out']
TWIN_DIFF_INPUT = 'x'
TWIN_INPUTS = ['x', 'c', 'ctx', 'c_ctx', 'w_mod', 'b_mod', 'g_norm', 'w_in', 'att_q_gain', 'att_k_gain', 'dn_conv', 'dn_a_log', 'dn_dt_bias', 'dn_norm_gain', 'ret_decay', 'sc_conv', 'w_branch', 'w_out', 'w_mlp_in', 'w_mlp_out', 'loss_target', 'm_c_ctx', 'm_w_mod', 'm_b_mod', 'm_g_norm', 'm_w_in', 'm_att_q_gain', 'm_att_k_gain', 'm_dn_conv', 'm_dn_a_log', 'm_dn_dt_bias', 'm_dn_norm_gain', 'm_ret_decay', 'm_sc_conv', 'm_w_branch', 'm_w_out', 'm_w_mlp_in', 'm_w_mlp_out', 'v_c_ctx', 'v_w_mod', 'v_b_mod', 'v_g_norm', 'v_w_in', 'v_att_q_gain', 'v_att_k_gain', 'v_dn_conv', 'v_dn_a_log', 'v_dn_dt_bias', 'v_dn_norm_gain', 'v_ret_decay', 'v_sc_conv', 'v_w_branch', 'v_w_out', 'v_w_mlp_in', 'v_w_mlp_out']
TWIN_OUTPUTS = ['loss', 'grad_x', 'grad_c_ctx', 'grad_w_mod', 'grad_b_mod', 'grad_g_norm', 'grad_w_in', 'grad_att_q_gain', 'grad_att_k_gain', 'grad_dn_conv', 'grad_dn_a_log', 'grad_dn_dt_bias', 'grad_dn_norm_gain', 'grad_ret_decay', 'grad_sc_conv', 'grad_w_branch', 'grad_w_out', 'grad_w_mlp_in', 'grad_w_mlp_out', 'delta_c_ctx', 'delta_w_mod', 'delta_b_mod', 'delta_g_norm', 'delta_w_in', 'delta_att_q_gain', 'delta_att_k_gain', 'delta_dn_conv', 'delta_dn_a_log', 'delta_dn_dt_bias', 'delta_dn_norm_gain', 'delta_ret_decay', 'delta_sc_conv', 'delta_w_branch', 'delta_w_out', 'delta_w_mlp_in', 'delta_w_mlp_out', 'new_m_c_ctx', 'new_m_w_mod', 'new_m_b_mod', 'new_m_g_norm', 'new_m_w_in', 'new_m_att_q_gain', 'new_m_att_k_gain', 'new_m_dn_conv', 'new_m_dn_a_log', 'new_m_dn_dt_bias', 'new_m_dn_norm_gain', 'new_m_ret_decay', 'new_m_sc_conv', 'new_m_w_branch', 'new_m_w_out', 'new_m_w_mlp_in', 'new_m_w_mlp_out', 'new_v_c_ctx', 'new_v_w_mod', 'new_v_b_mod', 'new_v_g_norm', 'new_v_w_in', 'new_v_att_q_gain', 'new_v_att_k_gain', 'new_v_dn_conv', 'new_v_dn_a_log', 'new_v_dn_dt_bias', 'new_v_dn_norm_gain', 'new_v_ret_decay', 'new_v_sc_conv', 'new_v_w_branch', 'new_v_w_out', 'new_v_w_mlp_in', 'new_v_w_mlp_out']
TWIN_LEAF_KINDS = {'loss': 'loss', 'grad_x': 'grad_x', 'grad_c_ctx': 'grad_w', 'grad_w_mod': 'grad_w', 'grad_b_mod': 'grad_w', 'grad_g_norm': 'grad_w', 'grad_w_in': 'grad_w', 'grad_att_q_gain': 'grad_w', 'grad_att_k_gain': 'grad_w', 'grad_dn_conv': 'grad_w', 'grad_dn_a_log': 'grad_w', 'grad_dn_dt_bias': 'grad_w', 'grad_dn_norm_gain': 'grad_w', 'grad_ret_decay': 'grad_w', 'grad_sc_conv': 'grad_w', 'grad_w_branch': 'grad_w', 'grad_w_out': 'grad_w', 'grad_w_mlp_in': 'grad_w', 'grad_w_mlp_out': 'grad_w', 'delta_c_ctx': 'delta_w', 'delta_w_mod': 'delta_w', 'delta_b_mod': 'delta_w', 'delta_g_norm': 'delta_w', 'delta_w_in': 'delta_w', 'delta_att_q_gain': 'delta_w', 'delta_att_k_gain': 'delta_w', 'delta_dn_conv': 'delta_w', 'delta_dn_a_log': 'delta_w', 'delta_dn_dt_bias': 'delta_w', 'delta_dn_norm_gain': 'delta_w', 'delta_ret_decay': 'delta_w', 'delta_sc_conv': 'delta_w', 'delta_w_branch': 'delta_w', 'delta_w_out': 'delta_w', 'delta_w_mlp_in': 'delta_w', 'delta_w_mlp_out': 'delta_w', 'new_m_c_ctx': 'new_m', 'new_m_w_mod': 'new_m', 'new_m_b_mod': 'new_m', 'new_m_g_norm': 'new_m', 'new_m_w_in': 'new_m', 'new_m_att_q_gain': 'new_m', 'new_m_att_k_gain': 'new_m', 'new_m_dn_conv': 'new_m', 'new_m_dn_a_log': 'new_m', 'new_m_dn_dt_bias': 'new_m', 'new_m_dn_norm_gain': 'new_m', 'new_m_ret_decay': 'new_m', 'new_m_sc_conv': 'new_m', 'new_m_w_branch': 'new_m', 'new_m_w_out': 'new_m', 'new_m_w_mlp_in': 'new_m', 'new_m_w_mlp_out': 'new_m', 'new_v_c_ctx': 'new_v', 'new_v_w_mod': 'new_v', 'new_v_b_mod': 'new_v', 'new_v_g_norm': 'new_v', 'new_v_w_in': 'new_v', 'new_v_att_q_gain': 'new_v', 'new_v_att_k_gain': 'new_v', 'new_v_dn_conv': 'new_v', 'new_v_dn_a_log': 'new_v', 'new_v_dn_dt_bias': 'new_v', 'new_v_dn_norm_gain': 'new_v', 'new_v_ret_decay': 'new_v', 'new_v_sc_conv': 'new_v', 'new_v_w_branch': 'new_v', 'new_v_w_out': 'new_v', 'new_v_w_mlp_in': 'new_v', 'new_v_w_mlp_out': 'new_v'}


def _forward(args):
    return _fwd_reference(*[args[k] for k in FWD_PARAMS])


def _output_shape():
    out = _jax.eval_shape(lambda: _forward(_fwd_setup_inputs(0)))
    return out.shape, out.dtype

N_MICROBATCH = 1
ADAM_LR = 0.001
ADAM_B1 = 0.9
ADAM_B2 = 0.999
ADAM_EPS = 1e-08
ADAM_WD = 0.01
ADAM_STEP = 10
PER_EXAMPLE_BATCH_AXIS = {'x': 0, 'c': 0, 'ctx': 0, 'loss_target': 0}
SHARED_INPUTS = []
_WEIGHT_DTYPES = {'c_ctx': _jnp.float32, 'w_mod': _jnp.float32, 'b_mod': _jnp.float32, 'g_norm': _jnp.float32, 'w_in': _jnp.float32, 'att_q_gain': _jnp.float32, 'att_k_gain': _jnp.float32, 'dn_conv': _jnp.float32, 'dn_a_log': _jnp.float32, 'dn_dt_bias': _jnp.float32, 'dn_norm_gain': _jnp.float32, 'ret_decay': _jnp.float32, 'sc_conv': _jnp.float32, 'w_branch': _jnp.float32, 'w_out': _jnp.float32, 'w_mlp_in': _jnp.float32, 'w_mlp_out': _jnp.float32}
MOMENT_SCALE = {'c_ctx': 1.364930e-01, 'w_mod': 3.645704e+00, 'b_mod': 6.845360e+00, 'g_norm': 5.247387e+00, 'w_in': 1.514897e-01, 'att_q_gain': 4.222987e-02, 'att_k_gain': 4.186795e-02, 'dn_conv': 1.856770e-01, 'dn_a_log': 2.330008e-01, 'dn_dt_bias': 2.298228e-01, 'dn_norm_gain': 8.788099e-01, 'ret_decay': 5.960366e-01, 'sc_conv': 2.121791e-01, 'w_branch': 2.272999e-01, 'w_out': 4.591031e-01, 'w_mlp_in': 2.735998e-01, 'w_mlp_out': 1.439271e+00}


def _to_microbatches(a, axis):
    t = _jnp.moveaxis(a, axis, 0)
    t = t.reshape((N_MICROBATCH, t.shape[0] // N_MICROBATCH) + t.shape[1:])
    return _jnp.moveaxis(t, 1, axis + 1)


def setup_inputs(seed: int = 0) -> dict:
    inp = _fwd_setup_inputs(seed)
    key = _jax.random.fold_in(_jax.random.key(seed), 7919)
    shape, _ = _output_shape()
    out = dict(inp)
    out["loss_target"] = _jax.random.normal(_jax.random.fold_in(key, 0), shape, _jnp.float32)
    for i, name in enumerate(TWIN_WEIGHTS):
        w = inp[name].astype(_jnp.float32)
        if MOMENT_SCALE is None:
            s = _jnp.sqrt(_jnp.mean(_jnp.square(w)) + 1e-30)
        else:
            s = MOMENT_SCALE[name]
        km, kv = _jax.random.split(_jax.random.fold_in(key, i + 1))
        out[name] = w
        out["m_" + name] = s * _jax.random.normal(km, w.shape, _jnp.float32)
        out["v_" + name] = (s * s) * _jax.random.uniform(kv, w.shape, _jnp.float32, 0.5, 1.5)
    if N_MICROBATCH > 1:
        for name, axis in PER_EXAMPLE_BATCH_AXIS.items():
            out[name] = _to_microbatches(out[name], axis)
    return {'x': out['x'], 'c': out['c'], 'ctx': out['ctx'], 'c_ctx': out['c_ctx'], 'w_mod': out['w_mod'], 'b_mod': out['b_mod'], 'g_norm': out['g_norm'], 'w_in': out['w_in'], 'att_q_gain': out['att_q_gain'], 'att_k_gain': out['att_k_gain'], 'dn_conv': out['dn_conv'], 'dn_a_log': out['dn_a_log'], 'dn_dt_bias': out['dn_dt_bias'], 'dn_norm_gain': out['dn_norm_gain'], 'ret_decay': out['ret_decay'], 'sc_conv': out['sc_conv'], 'w_branch': out['w_branch'], 'w_out': out['w_out'], 'w_mlp_in': out['w_mlp_in'], 'w_mlp_out': out['w_mlp_out'], 'loss_target': out['loss_target'], 'm_c_ctx': out['m_c_ctx'], 'm_w_mod': out['m_w_mod'], 'm_b_mod': out['m_b_mod'], 'm_g_norm': out['m_g_norm'], 'm_w_in': out['m_w_in'], 'm_att_q_gain': out['m_att_q_gain'], 'm_att_k_gain': out['m_att_k_gain'], 'm_dn_conv': out['m_dn_conv'], 'm_dn_a_log': out['m_dn_a_log'], 'm_dn_dt_bias': out['m_dn_dt_bias'], 'm_dn_norm_gain': out['m_dn_norm_gain'], 'm_ret_decay': out['m_ret_decay'], 'm_sc_conv': out['m_sc_conv'], 'm_w_branch': out['m_w_branch'], 'm_w_out': out['m_w_out'], 'm_w_mlp_in': out['m_w_mlp_in'], 'm_w_mlp_out': out['m_w_mlp_out'], 'v_c_ctx': out['v_c_ctx'], 'v_w_mod': out['v_w_mod'], 'v_b_mod': out['v_b_mod'], 'v_g_norm': out['v_g_norm'], 'v_w_in': out['v_w_in'], 'v_att_q_gain': out['v_att_q_gain'], 'v_att_k_gain': out['v_att_k_gain'], 'v_dn_conv': out['v_dn_conv'], 'v_dn_a_log': out['v_dn_a_log'], 'v_dn_dt_bias': out['v_dn_dt_bias'], 'v_dn_norm_gain': out['v_dn_norm_gain'], 'v_ret_decay': out['v_ret_decay'], 'v_sc_conv': out['v_sc_conv'], 'v_w_branch': out['v_w_branch'], 'v_w_out': out['v_w_out'], 'v_w_mlp_in': out['v_w_mlp_in'], 'v_w_mlp_out': out['v_w_mlp_out']}


def _loss(weights, diff, rest, loss_target):
    with _jax.named_scope("forward"):
        args = {**rest, TWIN_DIFF_INPUT: diff, **{k: w.astype(_WEIGHT_DTYPES[k]) for k, w in weights.items()}}
        y = _forward(args)
    with _jax.named_scope("loss_head"):
        err = _jnp.square(y.astype(_jnp.float32) - loss_target)
        return 0.5 * _jnp.sum(_jnp.mean(err, axis=-1)) if err.ndim else 0.5 * err


def _adamw(w, g, m, v):
    m = ADAM_B1 * m + (1.0 - ADAM_B1) * g
    v = ADAM_B2 * v + (1.0 - ADAM_B2) * _jnp.square(g)
    m_hat = m / (1.0 - ADAM_B1 ** ADAM_STEP)
    v_hat = v / (1.0 - ADAM_B2 ** ADAM_STEP)
    delta = -ADAM_LR * (m_hat / (_jnp.sqrt(v_hat) + ADAM_EPS) + ADAM_WD * w)
    return delta, m, v


def reference(x, c, ctx, c_ctx, w_mod, b_mod, g_norm, w_in, att_q_gain, att_k_gain, dn_conv, dn_a_log, dn_dt_bias, dn_norm_gain, ret_decay, sc_conv, w_branch, w_out, w_mlp_in, w_mlp_out, loss_target, m_c_ctx, m_w_mod, m_b_mod, m_g_norm, m_w_in, m_att_q_gain, m_att_k_gain, m_dn_conv, m_dn_a_log, m_dn_dt_bias, m_dn_norm_gain, m_ret_decay, m_sc_conv, m_w_branch, m_w_out, m_w_mlp_in, m_w_mlp_out, v_c_ctx, v_w_mod, v_b_mod, v_g_norm, v_w_in, v_att_q_gain, v_att_k_gain, v_dn_conv, v_dn_a_log, v_dn_dt_bias, v_dn_norm_gain, v_ret_decay, v_sc_conv, v_w_branch, v_w_out, v_w_mlp_in, v_w_mlp_out):
    given = dict(x=x, c=c, ctx=ctx, c_ctx=c_ctx, w_mod=w_mod, b_mod=b_mod, g_norm=g_norm, w_in=w_in, att_q_gain=att_q_gain, att_k_gain=att_k_gain, dn_conv=dn_conv, dn_a_log=dn_a_log, dn_dt_bias=dn_dt_bias, dn_norm_gain=dn_norm_gain, ret_decay=ret_decay, sc_conv=sc_conv, w_branch=w_branch, w_out=w_out, w_mlp_in=w_mlp_in, w_mlp_out=w_mlp_out, loss_target=loss_target, m_c_ctx=m_c_ctx, m_w_mod=m_w_mod, m_b_mod=m_b_mod, m_g_norm=m_g_norm, m_w_in=m_w_in, m_att_q_gain=m_att_q_gain, m_att_k_gain=m_att_k_gain, m_dn_conv=m_dn_conv, m_dn_a_log=m_dn_a_log, m_dn_dt_bias=m_dn_dt_bias, m_dn_norm_gain=m_dn_norm_gain, m_ret_decay=m_ret_decay, m_sc_conv=m_sc_conv, m_w_branch=m_w_branch, m_w_out=m_w_out, m_w_mlp_in=m_w_mlp_in, m_w_mlp_out=m_w_mlp_out, v_c_ctx=v_c_ctx, v_w_mod=v_w_mod, v_b_mod=v_b_mod, v_g_norm=v_g_norm, v_w_in=v_w_in, v_att_q_gain=v_att_q_gain, v_att_k_gain=v_att_k_gain, v_dn_conv=v_dn_conv, v_dn_a_log=v_dn_a_log, v_dn_dt_bias=v_dn_dt_bias, v_dn_norm_gain=v_dn_norm_gain, v_ret_decay=v_ret_decay, v_sc_conv=v_sc_conv, v_w_branch=v_w_branch, v_w_out=v_w_out, v_w_mlp_in=v_w_mlp_in, v_w_mlp_out=v_w_mlp_out)
    weights = {n: given[n] for n in TWIN_WEIGHTS}
    shared = {n: given[n] for n in SHARED_INPUTS}
    per_example = {n: given[n] for n in ['x', 'c', 'ctx']}
    grad_fn = _jax.value_and_grad(_loss, argnums=(0, 1))

    def one_microbatch(ex, loss_target):
        ex = dict(ex)
        diff = ex.pop(TWIN_DIFF_INPUT)
        return grad_fn(weights, diff, {**shared, **ex}, loss_target)

    if N_MICROBATCH == 1:
        loss, (grad_w, grad_x) = one_microbatch(per_example, given["loss_target"])
    else:
        def body(carry, xs):
            loss_sum, grad_sum = carry
            l_k, (gw_k, gx_k) = one_microbatch(xs[0], xs[1])
            with _jax.named_scope("update"):
                return (loss_sum + l_k, _jax.tree.map(_jnp.add, grad_sum, gw_k)), gx_k

        init = (_jnp.zeros((), _jnp.float32), _jax.tree.map(_jnp.zeros_like, weights))
        (loss, grad_w), grad_x = _jax.lax.scan(body, init, (per_example, given["loss_target"]))
    with _jax.named_scope("update"):
        delta_w, new_m, new_v = {}, {}, {}
        for n in TWIN_WEIGHTS:
            delta_w[n], new_m[n], new_v[n] = _adamw(weights[n], grad_w[n], given["m_" + n], given["v_" + n])
    return (loss, grad_x, *[grad_w[n] for n in TWIN_WEIGHTS], *[delta_w[n] for n in TWIN_WEIGHTS],
            *[new_m[n] for n in TWIN_WEIGHTS], *[new_v[n] for n in TWIN_WEIGHTS])
```

```python
import functools
import math

import numpy as np
import jax
import jax.numpy as jnp
from jax import lax
from jax.experimental import pallas as pl
from jax.experimental.pallas import tpu as pltpu

F32, BF16 = jnp.float32, jnp.bfloat16
HIGHEST = lax.Precision.HIGHEST
MESH = pl.DeviceIdType.MESH

D_MODEL = 1024
GRID_W = 64
N_BRANCH = 4
BRANCH_W = 512
HEAD64 = 64
HEAD128 = 128
N_HEAD4 = 4
ATT_HEADS = 8
ATT_KV_HEADS = 2
CHUNK = 64
MLP_HIDDEN = 4 * D_MODEL
ROPE_THETA = 10000.0
EPS = 1e-6
N_IN = 10000
ADAM_LR, ADAM_B1, ADAM_B2, ADAM_EPS, ADAM_WD, ADAM_STEP = 0.001, 0.9, 0.999, 1e-08, 0.01, 10

LANES = 128
VMEM_LIMIT = 56 * 1024 * 1024

_SEGS = (
    ("gates", 5904, 4096),
    ("att_q", 0, 512), ("dn_q", 768, 512), ("dn_k", 1280, 512), ("dn_v", 1792, 512), ("dn_z", 2304, 512),
    ("ret_v", 3344, 512), ("ret_g", 3856, 512), ("sc_b", 4368, 512), ("sc_c", 4880, 512), ("sc_x", 5392, 512),
    ("ret_q", 2832, 256), ("ret_k", 3088, 256),
    ("att_k", 512, 128), ("att_v", 640, 128),
    ("narrow", 2816, 16),
)
NZ = 10240


def _seg_offsets():
    off, out = 0, {}
    for name, _, width in _SEGS:
        out[name] = off
        off += width
    return out


ZOFF = _seg_offsets()


def _pick(n, cands):
    for c in cands:
        if n % c == 0:
            return c
    return n


def _dg(a, b, ca, cb, batch=False):
    dn = (((ca,), (cb,)), ((0,), (0,))) if batch else (((ca,), (cb,)), ((), ()))
    return lax.dot_general(a.astype(BF16), b.astype(BF16), dn, preferred_element_type=F32)


@functools.partial(jax.custom_vjp, nondiff_argnums=(2, 3))
def _mm(a, b, ca, cb):
    return _dg(a, b, ca, cb)


def _mm_fwd(a, b, ca, cb):
    return _dg(a, b, ca, cb), (a, b)


def _mm_bwd(ca, cb, res, g):
    a, b = res
    if ca == 1:
        da = _mm(g, b, 1, 1) if cb == 0 else _mm(g, b, 1, 0)
    else:
        da = _mm(b, g, 1, 1) if cb == 0 else _mm(b, g, 0, 1)
    if cb == 0:
        db = _mm(a, g, 0, 0) if ca == 1 else _mm(a, g, 1, 0)
    else:
        db = _mm(g, a, 0, 0) if ca == 1 else _mm(g, a, 0, 1)
    return da.astype(a.dtype), db.astype(b.dtype)


_mm.defvjp(_mm_fwd, _mm_bwd)


@functools.partial(jax.custom_vjp, nondiff_argnums=(2, 3))
def _bmm(a, b, ca, cb):
    return _dg(a, b, ca, cb, True)


def _bmm_fwd(a, b, ca, cb):
    return _dg(a, b, ca, cb, True), (a, b)


def _bmm_bwd(ca, cb, res, g):
    a, b = res
    if ca == 2:
        da = _bmm(g, b, 2, 2) if cb == 1 else _bmm(g, b, 2, 1)
    else:
        da = _bmm(b, g, 2, 2) if cb == 1 else _bmm(b, g, 1, 2)
    if cb == 1:
        db = _bmm(a, g, 1, 1) if ca == 2 else _bmm(a, g, 2, 1)
    else:
        db = _bmm(g, a, 1, 1) if ca == 2 else _bmm(g, a, 1, 2)
    return da.astype(a.dtype), db.astype(b.dtype)


_bmm.defvjp(_bmm_fwd, _bmm_bwd)


def _split_bf16(x):
    hi = x.astype(BF16)
    lo = (x - hi.astype(F32)).astype(BF16)
    return hi, lo


def _mm3(a, b, ca, cb):
    ah, al = _split_bf16(a)
    bh, bl = _split_bf16(b)
    dn = (((ca,), (cb,)), ((), ()))
    d = lambda u, v: lax.dot_general(u, v, dn, preferred_element_type=F32)
    return d(ah, bh) + (d(ah, bl) + d(al, bh))


def _mm_exact(a, b):
    return jnp.dot(a, b, precision=HIGHEST, preferred_element_type=F32)


class Arg:
    def __init__(self, x, block, imap, diff=True, first=None, gdtype=F32, pieces=None, gshape=None, gimap=None):
        self.x, self.block, self.imap = x, tuple(block), imap
        self.diff, self.first, self.gdtype, self.pieces = diff, first, gdtype, pieces
        self.gshape = tuple(x.shape) if gshape is None else tuple(gshape)
        self.gimap = imap if gimap is None else gimap

    def spec(self):
        return pl.BlockSpec(self.block, self.imap)

    def gspec(self):
        return pl.BlockSpec(self.block, self.gimap)


class Out:
    def __init__(self, shape, dtype, block, imap, pieces=None):
        self.shape, self.dtype, self.block, self.imap, self.pieces = tuple(shape), dtype, tuple(block), imap, pieces

    def spec(self):
        return pl.BlockSpec(self.block, self.imap)

    def sds(self):
        return jax.ShapeDtypeStruct(self.shape, self.dtype)


def _lanes(ref, s, w):
    return (slice(None),) * (len(ref.shape) - 1) + (slice(s, s + w),)


def _load(ref, pieces):
    if pieces is None:
        return ref[...]
    return tuple(ref[_lanes(ref, s, w)] for s, w in pieces)


def _store(ref, val, pieces, accumulate=False):
    if pieces is None:
        if accumulate:
            ref[...] += val.astype(ref.dtype)
        else:
            ref[...] = val.astype(ref.dtype)
        return
    if not accumulate:
        covered = sum(w for _, w in pieces)
        if covered != ref.shape[-1]:
            ref[...] = jnp.zeros(ref.shape, ref.dtype)
    for (s, w), v in zip(pieces, val):
        if accumulate:
            ref[_lanes(ref, s, w)] += v.astype(ref.dtype)
        else:
            ref[_lanes(ref, s, w)] = v.astype(ref.dtype)


def _params(n_grid):
    return pltpu.CompilerParams(dimension_semantics=("arbitrary",) * n_grid, vmem_limit_bytes=VMEM_LIMIT)


def vfwd(name, f, grid, args, outs):
    n_in = len(args)

    def body(*refs):
        ids = tuple(pl.program_id(i) for i in range(len(grid)))
        vals = [_load(r, a.pieces) for r, a in zip(refs[:n_in], args)]
        res = f(ids, *vals)
        for r, o, spec in zip(refs[n_in:], res, outs):
            _store(r, o, spec.pieces)

    return pl.pallas_call(
        body, name=name, grid=grid,
        in_specs=[a.spec() for a in args], out_specs=[o.spec() for o in outs],
        out_shape=[o.sds() for o in outs], compiler_params=_params(len(grid)),
    )(*[a.x for a in args])


def vbwd(name, f, grid, args, outs, cts):
    n_in = len(args)
    diff_idx = [i for i, a in enumerate(args) if a.diff]
    ct_flat = [c for per_out in cts for c in per_out]
    ct_specs = [o.spec() for o, per_out in zip(outs, cts) for _ in per_out]
    n_ct = len(ct_flat)

    def body(*refs):
        ids = tuple(pl.program_id(i) for i in range(len(grid)))
        vals = [_load(r, a.pieces) for r, a in zip(refs[:n_in], args)]
        ct_refs = refs[n_in:n_in + n_ct]
        g_refs = refs[n_in + n_ct:]

        def g(*dvals):
            full = list(vals)
            for i, v in zip(diff_idx, dvals):
                full[i] = v
            return tuple(f(ids, *full))

        _, vjp = jax.vjp(g, *[vals[i] for i in diff_idx])
        ct_vals, k = [], 0
        for o, per_out in zip(outs, cts):
            tot = None
            for _ in per_out:
                v = _load(ct_refs[k], o.pieces)
                k += 1
                v = jax.tree.map(lambda t: t.astype(F32), v)
                tot = v if tot is None else jax.tree.map(jnp.add, tot, v)
            ct_vals.append(tot)
        grads = vjp(tuple(ct_vals))
        for gr, gv, i in zip(g_refs, grads, diff_idx):
            a = args[i]
            if a.first is None:
                _store(gr, gv, a.pieces)
            else:
                is_first = a.first(ids)

                @pl.when(is_first)
                def _():
                    _store(gr, gv, a.pieces)

                @pl.when(jnp.logical_not(is_first))
                def _():
                    _store(gr, gv, a.pieces, accumulate=True)

    g_specs = [args[i].gspec() for i in diff_idx]
    g_shapes = [jax.ShapeDtypeStruct(args[i].gshape, args[i].gdtype) for i in diff_idx]
    return pl.pallas_call(
        body, name=name, grid=grid,
        in_specs=[a.spec() for a in args] + ct_specs, out_specs=g_specs, out_shape=g_shapes,
        compiler_params=_params(len(grid)),
    )(*[a.x for a in args], *ct_flat)


def matmul(name, a, b, trans_a=False, out_dtype=F32):
    if trans_a:
        kdim, m = a.shape
    else:
        m, kdim = a.shape
    n = b.shape[1]
    assert b.shape[0] == kdim
    tm = _pick(m, (512, 256, 192, 128, 64))
    tn = _pick(n, (512, 256, 128))
    tk = _pick(kdim, (1024, 512, 256, 192, 128, 64))
    nk = kdim // tk

    def body(a_ref, b_ref, o_ref, acc_ref):
        k = pl.program_id(2)
        part = _dg(a_ref[...], b_ref[...], 0 if trans_a else 1, 0)

        @pl.when(k == 0)
        def _():
            acc_ref[...] = part

        @pl.when(k > 0)
        def _():
            acc_ref[...] += part

        @pl.when(k == nk - 1)
        def _():
            o_ref[...] = acc_ref[...].astype(o_ref.dtype)

    a_spec = pl.BlockSpec((tk, tm), lambda i, j, k: (k, i)) if trans_a else pl.BlockSpec((tm, tk), lambda i, j, k: (i, k))
    return pl.pallas_call(
        body, name=name, grid=(m // tm, n // tn, nk),
        in_specs=[a_spec, pl.BlockSpec((tk, tn), lambda i, j, k: (k, j))],
        out_specs=pl.BlockSpec((tm, tn), lambda i, j, k: (i, j)),
        out_shape=jax.ShapeDtypeStruct((m, n), out_dtype),
        scratch_shapes=[pltpu.VMEM((tm, tn), F32)],
        compiler_params=pltpu.CompilerParams(dimension_semantics=("parallel", "parallel", "arbitrary"),
                                             vmem_limit_bytes=VMEM_LIMIT),
    )(a, b)


def _rms(x, gain):
    return x * lax.rsqrt(jnp.mean(x * x, axis=-1, keepdims=True) + EPS) * gain


def _silu(x):
    return x * jax.nn.sigmoid(x)


def f_modnorm(ids, x, gain, mod):
    return (_rms(x, gain) * (1.0 + mod[1:2]) + mod[0:1],)


def f_resnorm(ids, x, y, g_res, g_next, mod):
    x_new = x + mod[0:1] * _rms(y, g_res)
    return x_new, _rms(x_new, g_next) * (1.0 + mod[2:3]) + mod[1:2]


def f_resid(ids, x, y, g_res, mod):
    return (x + mod[0:1] * _rms(y, g_res),)


def f_act(ids, a):
    r = jnp.maximum(a, 0.0)
    return (r * r,)


def _head_consts(width, head):
    i = lax.broadcasted_iota(jnp.int32, (width, width), 0)
    j = lax.broadcasted_iota(jnp.int32, (width, width), 1)
    shift = int(math.log2(head))
    same = (i >> shift) == (j >> shift)
    group = jnp.where(same, 1.0 / head, 0.0).astype(F32)
    half = head // 2
    ii, jj = i & (head - 1), j & (head - 1)
    rot = jnp.where(same & (ii == jj + half) & (jj < half), -1.0, 0.0) + jnp.where(same & (ii + half == jj) & (jj >= half), 1.0, 0.0)
    ti = lax.broadcasted_iota(jnp.int32, (head, width), 0)
    tj = lax.broadcasted_iota(jnp.int32, (head, width), 1)
    tile = jnp.where(ti == (tj & (head - 1)), 1.0, 0.0).astype(F32)
    return group, rot.astype(F32), tile


def _rope(x, cos, sin, rot):
    return x * cos + _mm_exact(x, rot) * sin


def _softplus(x):
    return jnp.maximum(x, 0.0) + jnp.log(1.0 + jnp.exp(-jnp.abs(x)))


def f_prep(ids, zq, zk, rq, rk, zn, qgain, kgain, alog, dtb, cos, sin):
    grp_q, rot_q, tile_q = _head_consts(ATT_HEADS * HEAD64, HEAD64)
    grp_k, rot_k, tile_k = _head_consts(ATT_KV_HEADS * HEAD64, HEAD64)
    grp_r, rot_r, _ = _head_consts(N_HEAD4 * HEAD64, HEAD64)
    wq, wk, wr = zq.shape[-1], zk.shape[-1], rq.shape[-1]
    qn = zq * lax.rsqrt(_mm_exact(zq * zq, grp_q) + EPS) * _mm_exact(qgain, tile_q)
    kn = zk * lax.rsqrt(_mm_exact(zk * zk, grp_k) + EPS) * _mm_exact(kgain, tile_k)
    q_att = _rope(qn, cos[:, :wq], sin[:, :wq], rot_q)
    k_att = _rope(kn, cos[:, :wk], sin[:, :wk], rot_k)
    q_ret = _rope(rq, cos[:, :wr], sin[:, :wr], rot_r)
    k_ret = _rope(rk * (HEAD64 ** -0.5), cos[:, :wr], sin[:, :wr], rot_r)
    lane = lax.broadcasted_iota(jnp.int32, zn.shape, 1)
    log_a = -jnp.exp(alog) * _softplus(zn + dtb)
    gates = jnp.where(lane < 8, log_a, jnp.where(lane < 16, jax.nn.sigmoid(zn), 0.0))
    return q_att, k_att, q_ret, k_ret, gates


def make_f_attn(tq, n_ctx_blocks, lc):
    def f_attn(ids, q, k, v):
        t = k[0].shape[0]
        key = lax.broadcasted_iota(jnp.int32, (ATT_HEADS // ATT_KV_HEADS * tq, t), 1)
        visible = (ids[1] >= n_ctx_blocks) | (key < lc)
        outs = []
        per = ATT_HEADS // ATT_KV_HEADS
        for g in range(ATT_KV_HEADS):
            qg = jnp.concatenate(q[g * per:(g + 1) * per], axis=0)
            s = _mm(qg, k[g], 1, 1) * (HEAD64 ** -0.5)
            s = jnp.where(visible, s, -1e30)
            p = jnp.exp(s - jnp.max(s, axis=-1, keepdims=True))
            p = p / jnp.sum(p, axis=-1, keepdims=True)
            o = _mm(p, v[g], 1, 0)
            outs += [o[i * tq:(i + 1) * tq] for i in range(per)]
        return (tuple(outs),)
    return f_attn


def _roll_rows(x, shift):
    return pltpu.roll(x, shift, 0)


def make_shifts(t, lc):
    def _down(x):
        row = lax.broadcasted_iota(jnp.int32, x.shape, 0)
        return jnp.where((row == 0) | (row == lc), 0.0, _roll_rows(x, 1))

    def _up(x):
        row = lax.broadcasted_iota(jnp.int32, x.shape, 0)
        return jnp.where((row == lc - 1) | (row == t - 1), 0.0, _roll_rows(x, t - 1))

    @jax.custom_vjp
    def down(x):
        return _down(x)

    @jax.custom_vjp
    def up(x):
        return _up(x)

    down.defvjp(lambda x: (_down(x), None), lambda _, g: (up(g),))
    up.defvjp(lambda x: (_up(x), None), lambda _, g: (down(g),))
    return down, up


def make_conv3(t, lc):
    down, up = make_shifts(t, lc)

    def conv3(x, w):
        return w[0:1] * down(x) + w[1:2] * x + w[2:3] * up(x)
    return conv3


def make_f_dnprep(t, lc):
    conv3 = make_conv3(t, lc)

    def l2n(x):
        return x * lax.rsqrt(jnp.sum(x * x, axis=-1, keepdims=True) + EPS)

    def f_dnprep(ids, q, k, v, wq, wk, wv):
        qn = l2n(_silu(conv3(q, wq))) * (HEAD128 ** -0.5)
        kn = l2n(_silu(conv3(k, wk)))
        return qn, kn, _silu(conv3(v, wv))
    return f_dnprep


def make_f_shortconv(t, lc):
    conv3 = make_conv3(t, lc)

    def f_shortconv(ids, b, c, x, w):
        return (b * conv3(c * x, w),)
    return f_shortconv


def f_finish(ids, o_dn_f, o_dn_b, z_dn, o_rt_f, o_rt_b, g_rt, ngain):
    y_dn, y_rt = [], []
    for h in range(N_HEAD4):
        o = o_dn_f[h] + o_dn_b[h]
        y_dn.append(_rms(o, ngain) * _silu(z_dn[h]))
        r = o_rt_f[h] + o_rt_b[h]
        mu = jnp.mean(r, axis=-1, keepdims=True)
        var = jnp.mean(jnp.square(r - mu), axis=-1, keepdims=True)
        y_rt.append((r - mu) * lax.rsqrt(var + EPS) * _silu(g_rt[h]))
    return tuple(y_dn), tuple(y_rt)


def f_merge(ids, p0, p1, p2, p3, gates):
    u = jax.nn.sigmoid(gates[0]) * p0
    for g, p in zip(gates[1:], (p1, p2, p3)):
        u = u + jax.nn.sigmoid(g) * p
    return (u,)


def _stack_masks():
    n = N_HEAD4 * CHUNK
    i = lax.broadcasted_iota(jnp.int32, (n, n), 0)
    j = lax.broadcasted_iota(jnp.int32, (n, n), 1)
    same = (i >> 6) == (j >> 6)
    pi, pj = i & (CHUNK - 1), j & (CHUNK - 1)
    return same, pi, pj


def _inv_unit_lower(low):
    n = low.shape[0]
    eye = jnp.where(lax.broadcasted_iota(jnp.int32, (n, n), 0) == lax.broadcasted_iota(jnp.int32, (n, n), 1), 1.0, 0.0).astype(F32)
    m = -low
    p = eye + m
    for _ in range(int(math.log2(CHUNK)) - 1):
        m = _mm3(m, m, 1, 0)
        p = p + _mm3(p, m, 1, 0)
    return p


@jax.custom_vjp
def _tri_solve(low, rhs):
    return _mm3(_inv_unit_lower(low), rhs, 1, 0)


def _tri_solve_fwd(low, rhs):
    inv = _inv_unit_lower(low)
    x = _mm3(inv, rhs, 1, 0)
    return x, (inv, x)


def _tri_solve_bwd(res, g):
    inv, x = res
    d_rhs = _mm3(inv, g, 0, 0)
    return -_mm3(d_rhs, x, 1, 1), d_rhs


_tri_solve.defvjp(_tri_solve_fwd, _tri_solve_bwd)


def _heads3(x):
    return x.reshape(N_HEAD4, CHUNK, x.shape[-1])


def dn_chunk(s, toks, params):
    q, k, v, la, beta = toks
    qs, ks, vs = (jnp.concatenate(t, axis=0) for t in (q, k, v))
    la, beta = jnp.concatenate(la, axis=0), jnp.concatenate(beta, axis=0)
    n = N_HEAD4 * CHUNK
    same, pi, pj = _stack_masks()
    incl, strict = same & (pi >= pj), same & (pi > pj)
    g = jnp.sum(jnp.where(incl, jnp.broadcast_to(la, (n, n)).T, 0.0), axis=1, keepdims=True)
    gb = jnp.broadcast_to(g, (n, n))
    gbt = gb.T
    dec_incl = jnp.where(incl, jnp.exp(jnp.where(incl, gb - gbt, 0.0)), 0.0)
    dec_strict = jnp.where(strict, dec_incl, 0.0)
    low = beta * _mm(ks, ks, 1, 1) * dec_strict
    eg = jnp.exp(g)
    sol = _tri_solve(low, jnp.concatenate([beta * vs, (beta * eg) * ks], axis=1))
    w_v, w_k = sol[:, :HEAD128], sol[:, HEAD128:]
    a_qk = _mm(qs, ks, 1, 1) * dec_incl
    g_last = jnp.sum(jnp.where(same & (pj == CHUNK - 1), gbt, 0.0), axis=1, keepdims=True)
    k_dec = ks * jnp.exp(g_last - g)
    u = w_v - _bmm(_heads3(w_k), s, 2, 1).reshape(n, HEAD128)
    o = _bmm(_heads3(qs * eg), s, 2, 1).reshape(n, HEAD128) + _mm(a_qk, u, 1, 0)
    decay = jnp.exp(jnp.mean(_heads3(g_last), axis=1, keepdims=True))
    s_new = s * decay + _bmm(_heads3(k_dec), _heads3(u), 1, 1)
    return (tuple(o[h * CHUNK:(h + 1) * CHUNK] for h in range(N_HEAD4)),), s_new


def make_ret_chunk(direction):
    def ret_chunk(s, toks, params):
        q, k, v = toks
        (decay_log,) = params
        qs, ks, vs = (jnp.concatenate(t, axis=0) for t in (q, k, v))
        n = N_HEAD4 * CHUNK
        same, pi, pj = _stack_masks()
        incl = same & (pi >= pj)
        row_head = lax.broadcasted_iota(jnp.int32, (n, 1), 0) >> 6
        lg = jnp.zeros((n, 1), F32)
        for h in range(N_HEAD4):
            lg = jnp.where(row_head == h, -jnp.exp(decay_log[direction * N_HEAD4 + h]), lg)
        pos = (lax.broadcasted_iota(jnp.int32, (n, 1), 0) & (CHUNK - 1)).astype(F32)
        rel = (pi - pj).astype(F32)
        dmask = jnp.where(incl, jnp.exp(jnp.where(incl, rel * lg, 0.0)), 0.0)
        o = _mm(_mm(qs, ks, 1, 1) * dmask, vs, 1, 0)
        k_dec = ks * jnp.exp((CHUNK - 1.0 - pos) * lg)
        kv = _bmm(_heads3(k_dec), _heads3(vs), 1, 1)
        o = o + _bmm(_heads3(qs * jnp.exp((pos + 1.0) * lg)), s, 2, 1).reshape(n, HEAD128)
        decay = jnp.exp(CHUNK * jnp.mean(_heads3(lg), axis=1, keepdims=True))
        s_new = s * decay + kv
        return (tuple(o[h * CHUNK:(h + 1) * CHUNK] for h in range(N_HEAD4)),), s_new
    return ret_chunk


def _lane_pieces(width, n=N_HEAD4):
    return [(h * width, width) for h in range(n)]


def scan_fwd(name, chunk_fn, toks, params, state_shape, outs):
    bsz, t = toks[0][0].shape[:2]
    nc = t // CHUNK
    n_t, n_p = len(toks), len(params)

    def body(*refs):
        tok_refs, par_refs = refs[:n_t], refs[n_t:n_t + n_p]
        out_refs = refs[n_t + n_p:-2]
        sprev_ref, s_ref = refs[-2], refs[-1]

        @pl.when(pl.program_id(1) == 0)
        def _():
            s_ref[...] = jnp.zeros(s_ref.shape, F32)

        s = s_ref[...]
        sprev_ref[...] = s
        tv = [_load(r, p) for r, (_, p) in zip(tok_refs, toks)]
        pv = [_load(r, p) for r, (_, p) in zip(par_refs, params)]
        res, s_new = chunk_fn(s, tv, pv)
        s_ref[...] = s_new
        for r, o, (_, _, pieces) in zip(out_refs, res, outs):
            _store(r, o, pieces)

    tok_specs = [pl.BlockSpec((None, CHUNK, x.shape[-1]), lambda b, c: (b, c, 0)) for x, _ in toks]
    par_specs = [pl.BlockSpec(x.shape, lambda b, c: (0, 0)) for x, _ in params]
    out_specs = [pl.BlockSpec((None, CHUNK, w), lambda b, c: (b, c, 0)) for w, _, _ in outs]
    out_shapes = [jax.ShapeDtypeStruct((bsz, t, w), dt) for w, dt, _ in outs]
    nstate = len(state_shape)
    out_specs.append(pl.BlockSpec((None, None) + state_shape, lambda b, c: (b, c) + (0,) * nstate))
    out_shapes.append(jax.ShapeDtypeStruct((bsz, nc) + state_shape, F32))
    return pl.pallas_call(
        body, name=name, grid=(bsz, nc),
        in_specs=tok_specs + par_specs, out_specs=out_specs, out_shape=out_shapes,
        scratch_shapes=[pltpu.VMEM(state_shape, F32)], compiler_params=_params(2),
    )(*[x for x, _ in toks], *[x for x, _ in params])


def scan_bwd(name, chunk_fn, toks, params, state_shape, outs, sprev, cts):
    bsz, t = toks[0][0].shape[:2]
    nc = t // CHUNK
    n_t, n_p, n_o = len(toks), len(params), len(outs)

    def body(*refs):
        tok_refs, par_refs = refs[:n_t], refs[n_t:n_t + n_p]
        sprev_ref = refs[n_t + n_p]
        ct_refs = refs[n_t + n_p + 1:n_t + n_p + 1 + n_o]
        g_refs = refs[n_t + n_p + 1 + n_o:-1]
        ds_ref = refs[-1]
        first = pl.program_id(1) == 0

        @pl.when(first)
        def _():
            ds_ref[...] = jnp.zeros(ds_ref.shape, F32)

        tv = [_load(r, p) for r, (_, p) in zip(tok_refs, toks)]
        pv = [_load(r, p) for r, (_, p) in zip(par_refs, params)]
        _, vjp = jax.vjp(chunk_fn, sprev_ref[...], tv, pv)
        ct = tuple(_load(r, pieces) for r, (_, _, pieces) in zip(ct_refs, outs))
        d_s, d_tv, d_pv = vjp((ct, ds_ref[...]))
        ds_ref[...] = d_s
        for r, gv, (_, p) in zip(g_refs[:n_t], d_tv, toks):
            _store(r, gv, p)
        very_first = first & (pl.program_id(0) == 0)
        for r, gv, (_, p) in zip(g_refs[n_t:], d_pv, params):
            @pl.when(very_first)
            def _():
                _store(r, gv, p)

            @pl.when(jnp.logical_not(very_first))
            def _():
                _store(r, gv, p, accumulate=True)

    rev = lambda b, c: (b, nc - 1 - c, 0)
    tok_specs = [pl.BlockSpec((None, CHUNK, x.shape[-1]), rev) for x, _ in toks]
    par_specs = [pl.BlockSpec(x.shape, lambda b, c: (0, 0)) for x, _ in params]
    nstate = len(state_shape)
    sp_spec = pl.BlockSpec((None, None) + state_shape, lambda b, c: (b, nc - 1 - c) + (0,) * nstate)
    ct_specs = [pl.BlockSpec((None, CHUNK, w), rev) for w, _, _ in outs]
    g_shapes = [jax.ShapeDtypeStruct(x.shape, F32) for x, _ in toks] + [jax.ShapeDtypeStruct(x.shape, F32) for x, _ in params]
    return pl.pallas_call(
        body, name=name, grid=(bsz, nc),
        in_specs=tok_specs + par_specs + [sp_spec] + ct_specs, out_specs=tok_specs + par_specs, out_shape=g_shapes,
        scratch_shapes=[pltpu.VMEM(state_shape, F32)], compiler_params=_params(2),
    )(*[x for x, _ in toks], *[x for x, _ in params], sprev, *cts)


class Geo:
    def __init__(self, bsz, t, lc):
        self.b, self.t, self.lc, self.m = bsz, t, lc, bsz * t
        self.tm = _pick(lc, (256, 128, 64))
        self.tq = _pick(lc, (128, 64))
        assert t % self.tm == 0 and t % CHUNK == 0 and lc % CHUNK == 0
        self.nctx, self.nctx_q = lc // self.tm, lc // self.tq
        self.grid = (bsz, t // self.tm)


def a_tok(g, x, tm=None, **kw):
    tm = tm or g.tm
    return Arg(x, (None, tm, x.shape[-1]), lambda b, j: (b, j, 0), **kw)


def a_ztok(g, z, name, width, tm=None, **kw):
    tm = tm or g.tm
    col = ZOFF[name] // width
    return Arg(z, (None, tm, width), lambda b, j: (b, j, col),
               gshape=(g.b, g.t, width), gimap=lambda b, j: (b, j, 0), **kw)


def a_par(x, **kw):
    return Arg(x, x.shape, lambda b, j: (0, 0), first=lambda ids: (ids[0] == 0) & (ids[1] == 0), **kw)


def a_mod(g, x):
    n = g.nctx
    return Arg(x, (None, None, x.shape[2], D_MODEL), lambda b, j: (b, jnp.where(j >= n, 1, 0), 0, 0),
               first=lambda ids: (ids[1] == 0) | (ids[1] == n))


def a_tab(g, x):
    return Arg(x, (g.tm, x.shape[-1]), lambda b, j: (j, 0), diff=False)


def o_tok(g, w, dtype, tm=None, pieces=None):
    tm = tm or g.tm
    return Out((g.b, g.t, w), dtype, (None, tm, w), lambda b, j: (b, j, 0), pieces)


def flipseg(x, lc):
    return jnp.concatenate([jnp.flip(x[:, :lc], 1), jnp.flip(x[:, lc:], 1)], axis=1)


def addn(name, xs, dtype):
    bsz, t, w = xs[0].shape
    tm = _pick(t, (256, 192, 128, 64))

    def body(*refs):
        tot = refs[0][...].astype(F32)
        for r in refs[1:-1]:
            tot = tot + r[...].astype(F32)
        refs[-1][...] = tot.astype(dtype)

    spec = pl.BlockSpec((None, tm, w), lambda b, j: (b, j, 0))
    return pl.pallas_call(body, name=name, grid=(bsz, t // tm), in_specs=[spec] * len(xs), out_specs=spec,
                          out_shape=jax.ShapeDtypeStruct((bsz, t, w), dtype), compiler_params=_params(2))(*xs)


P64x8 = _lane_pieces(HEAD64, ATT_HEADS)
P64x4 = _lane_pieces(HEAD64, N_HEAD4)
P64x2 = _lane_pieces(HEAD64, ATT_KV_HEADS)
P128x4 = _lane_pieces(HEAD128, N_HEAD4)
P1024x4 = _lane_pieces(D_MODEL, N_BRANCH)


def _gate_pieces(direction):
    la = [(direction * N_HEAD4 + h, 1) for h in range(N_HEAD4)]
    beta = [(8 + direction * N_HEAD4 + h, 1) for h in range(N_HEAD4)]
    return la, beta


RET_PIECES = [(i, 1) for i in range(2 * N_HEAD4)]


def _prep_io(g, z, sp, tabs):
    args = [a_ztok(g, z, "att_q", 512, gdtype=BF16), a_ztok(g, z, "att_k", 128, gdtype=BF16),
            a_ztok(g, z, "ret_q", 256, gdtype=BF16), a_ztok(g, z, "ret_k", 256, gdtype=BF16),
            a_ztok(g, z, "narrow", 128, gdtype=BF16),
            a_par(sp["qgain"]), a_par(sp["kgain"]), a_par(sp["alog"]), a_par(sp["dtb"]),
            a_tab(g, tabs[0]), a_tab(g, tabs[1])]
    outs = [o_tok(g, 512, BF16), o_tok(g, 128, BF16), o_tok(g, 256, F32), o_tok(g, 256, F32), o_tok(g, 128, F32)]
    return args, outs


def _attn_io(g, q_att, k_att, z):
    col = ZOFF["att_v"] // 128
    args = [Arg(q_att, (None, g.tq, 512), lambda b, j: (b, j, 0), pieces=P64x8),
            Arg(k_att, (None, g.t, 128), lambda b, j: (b, 0, 0), first=lambda ids: ids[1] == 0, pieces=P64x2),
            Arg(z, (None, g.t, 128), lambda b, j: (b, 0, col), first=lambda ids: ids[1] == 0, pieces=P64x2,
                gshape=(g.b, g.t, 128), gimap=lambda b, j: (b, 0, 0))]
    outs = [Out((g.b, g.t, 512), BF16, (None, g.tq, 512), lambda b, j: (b, j, 0), P64x8)]
    return (g.b, g.t // g.tq), args, outs


def _colgrid_arg(g, z, name, width_total, **kw):
    col = ZOFF[name] // 128
    return Arg(z, (None, g.t, 128), lambda h, b: (b, 0, col + h),
               gshape=(g.b, g.t, width_total), gimap=lambda h, b: (b, 0, h), **kw)


def _dnprep_io(g, z, sp):
    wfirst = lambda ids: ids[1] == 0
    args = [_colgrid_arg(g, z, "dn_q", 512, gdtype=BF16), _colgrid_arg(g, z, "dn_k", 512, gdtype=BF16),
            _colgrid_arg(g, z, "dn_v", 512, gdtype=BF16)]
    for i in range(3):
        args.append(Arg(sp["dn_conv"][i], (3, 128), lambda h, b: (0, h), first=wfirst))
    outs = [Out((g.b, g.t, 512), F32, (None, g.t, 128), lambda h, b: (b, 0, h)) for _ in range(3)]
    return (N_HEAD4, g.b), args, outs


def _shortconv_io(g, z, sp):
    args = [_colgrid_arg(g, z, "sc_b", 512, gdtype=BF16), _colgrid_arg(g, z, "sc_c", 512, gdtype=BF16),
            _colgrid_arg(g, z, "sc_x", 512, gdtype=BF16),
            Arg(sp["sc_conv"], (3, 128), lambda h, b: (0, h), first=lambda ids: ids[1] == 0)]
    outs = [Out((g.b, g.t, 512), BF16, (None, g.t, 128), lambda h, b: (b, 0, h))]
    return (BRANCH_W // 128, g.b), args, outs


def _finish_io(g, o_dn_f, o_dn_b, z, o_rt_f, o_rt_b, sp):
    args = [a_tok(g, o_dn_f, pieces=P128x4), a_tok(g, o_dn_b, pieces=P128x4),
            a_ztok(g, z, "dn_z", 512, gdtype=BF16, pieces=P128x4),
            a_tok(g, o_rt_f, pieces=P128x4), a_tok(g, o_rt_b, pieces=P128x4),
            a_ztok(g, z, "ret_g", 512, gdtype=BF16, pieces=P128x4), a_par(sp["ngain"])]
    outs = [o_tok(g, 512, BF16, pieces=P128x4), o_tok(g, 512, BF16, pieces=P128x4)]
    return args, outs


def _merge_io(g, ps, z):
    args = [a_tok(g, p, gdtype=BF16) for p in ps] + [a_ztok(g, z, "gates", 4096, gdtype=BF16, pieces=P1024x4)]
    return args, [o_tok(g, D_MODEL, BF16)]


def _dn_toks(qn, kn, vn, gates, direction):
    la, beta = _gate_pieces(direction)
    return [(qn, P128x4), (kn, P128x4), (vn, P128x4), (gates, la), (gates, beta)]


DN_STATE = (N_HEAD4, HEAD128, HEAD128)
RET_STATE = (N_HEAD4, HEAD64, HEAD128)
SCAN_OUT = [(512, F32, P128x4)]


def layer_fwd(g, x, h, w, sp, mod_a, mod_b, tabs, g_next):
    tp = {"x": x, "h": h}
    m, lc = g.m, g.lc
    z = matmul("win", h.reshape(m, D_MODEL), w["win"]).reshape(g.b, g.t, NZ)
    tp["z"] = z
    args, outs = _prep_io(g, z, sp, tabs)
    q_att, k_att, q_ret, k_ret, gates = vfwd("prep", f_prep, g.grid, args, outs)
    tp.update(q_att=q_att, k_att=k_att, q_ret=q_ret, k_ret=k_ret, gates=gates)
    agrid, args, outs = _attn_io(g, q_att, k_att, z)
    (y_att,) = vfwd("attn", make_f_attn(g.tq, g.nctx_q, lc), agrid, args, outs)
    dgrid, args, outs = _dnprep_io(g, z, sp)
    qn, kn, vn = vfwd("dnprep", make_f_dnprep(g.t, lc), dgrid, args, outs)
    tp.update(qn=qn, kn=kn, vn=vn)
    o_dn, tp["dn_s"] = [], []
    for d in range(2):
        fl = (lambda a: a) if d == 0 else (lambda a: flipseg(a, lc))
        o, sprev = scan_fwd("dnscan", dn_chunk, _dn_toks(fl(qn), fl(kn), fl(vn), fl(gates), d), [], DN_STATE, SCAN_OUT)
        o_dn.append(fl(o))
        tp["dn_s"].append(sprev)
    ret_v = z[:, :, ZOFF["ret_v"]:ZOFF["ret_v"] + 512]
    o_rt, tp["rt_s"] = [], []
    for d in range(2):
        fl = (lambda a: a) if d == 0 else (lambda a: flipseg(a, lc))
        toks = [(fl(q_ret), P64x4), (fl(k_ret), P64x4), (fl(ret_v), P128x4)]
        o, sprev = scan_fwd("retscan", make_ret_chunk(d), toks, [(sp["ret"], RET_PIECES)], RET_STATE, SCAN_OUT)
        o_rt.append(fl(o))
        tp["rt_s"].append(sprev)
    tp.update(o_dn=o_dn, o_rt=o_rt)
    args, outs = _finish_io(g, o_dn[0], o_dn[1], z, o_rt[0], o_rt[1], sp)
    y_dn, y_rt = vfwd("finish", f_finish, g.grid, args, outs)
    sgrid, args, outs = _shortconv_io(g, z, sp)
    (y_sc,) = vfwd("shortconv", make_f_shortconv(g.t, lc), sgrid, args, outs)
    ys = [y_att, y_dn, y_rt, y_sc]
    tp["ys"] = ys
    ps = [matmul("wbranch", y.reshape(m, BRANCH_W), w["wbr"][i]).reshape(g.b, g.t, D_MODEL) for i, y in enumerate(ys)]
    tp["ps"] = ps
    args, outs = _merge_io(g, ps, z)
    (u,) = vfwd("merge", f_merge, g.grid, args, outs)
    tp["u"] = u
    y = matmul("wout", u.reshape(m, D_MODEL), w["wout"]).reshape(g.b, g.t, D_MODEL)
    tp["y"] = y
    args = [a_tok(g, x), a_tok(g, y), a_par(sp["g1"]), a_par(sp["g2"]), a_mod(g, mod_a)]
    x1, h2 = vfwd("resnorm", f_resnorm, g.grid, args, [o_tok(g, D_MODEL, F32), o_tok(g, D_MODEL, BF16)])
    tp.update(x1=x1, h2=h2)
    a = matmul("wmlp1", h2.reshape(m, D_MODEL), w["w1"]).reshape(g.b, g.t, MLP_HIDDEN)
    tp["a"] = a
    (r,) = vfwd("act", f_act, g.grid, [a_tok(g, a, gdtype=BF16)], [o_tok(g, MLP_HIDDEN, BF16)])
    tp["r"] = r
    mo = matmul("wmlp2", r.reshape(m, MLP_HIDDEN), w["w2"]).reshape(g.b, g.t, D_MODEL)
    tp["mo"] = mo
    if g_next is None:
        args = [a_tok(g, x1), a_tok(g, mo), a_par(sp["g3"]), a_mod(g, mod_b)]
        (x2,) = vfwd("resid", f_resid, g.grid, args, [o_tok(g, D_MODEL, F32)])
        return x2, None, tp
    args = [a_tok(g, x1), a_tok(g, mo), a_par(sp["g3"]), a_par(g_next), a_mod(g, mod_b)]
    x2, h_next = vfwd("resnorm", f_resnorm, g.grid, args, [o_tok(g, D_MODEL, F32), o_tok(g, D_MODEL, BF16)])
    return x2, h_next, tp


def layer_bwd(g, tp, w, sp, mod_a, mod_b, tabs, g_next, dx2, dh_next):
    m, lc = g.m, g.lc
    gw, gs = {}, {}
    x, z = tp["x"], tp["z"]
    if g_next is None:
        args = [a_tok(g, tp["x1"]), a_tok(g, tp["mo"]), a_par(sp["g3"]), a_mod(g, mod_b)]
        dx1, dmo, gs["g3"], dmod_b = vbwd("resid_b", f_resid, g.grid, args, [o_tok(g, D_MODEL, F32)], [[dx2]])
    else:
        args = [a_tok(g, tp["x1"]), a_tok(g, tp["mo"]), a_par(sp["g3"]), a_par(g_next), a_mod(g, mod_b)]
        dx1, dmo, gs["g3"], gs["g0_next"], dmod_b = vbwd(
            "resnorm_b", f_resnorm, g.grid, args, [o_tok(g, D_MODEL, F32), o_tok(g, D_MODEL, BF16)], [[dx2], [dh_next]])
    dmo2 = dmo.reshape(m, D_MODEL)
    dr = matmul("wmlp2_dx", dmo2, w["w2t"]).reshape(g.b, g.t, MLP_HIDDEN)
    gw["w2"] = matmul("wmlp2_dw", tp["r"].reshape(m, MLP_HIDDEN), dmo2, trans_a=True)
    (da,) = vbwd("act_b", f_act, g.grid, [a_tok(g, tp["a"], gdtype=BF16)], [o_tok(g, MLP_HIDDEN, BF16)], [[dr]])
    da2 = da.reshape(m, MLP_HIDDEN)
    dh2 = matmul("wmlp1_dx", da2, w["w1t"]).reshape(g.b, g.t, D_MODEL)
    gw["w1"] = matmul("wmlp1_dw", tp["h2"].reshape(m, D_MODEL), da2, trans_a=True)
    args = [a_tok(g, x), a_tok(g, tp["y"]), a_par(sp["g1"]), a_par(sp["g2"]), a_mod(g, mod_a)]
    dx, dy, gs["g1"], gs["g2"], dmod_a = vbwd(
        "resnorm_b", f_resnorm, g.grid, args, [o_tok(g, D_MODEL, F32), o_tok(g, D_MODEL, BF16)], [[dx1], [dh2]])
    dy2 = dy.reshape(m, D_MODEL)
    du = matmul("wout_dx", dy2, w["woutt"]).reshape(g.b, g.t, D_MODEL)
    gw["wout"] = matmul("wout_dw", tp["u"].reshape(m, D_MODEL), dy2, trans_a=True)
    args, outs = _merge_io(g, tp["ps"], z)
    *dps, dz_gates = vbwd("merge_b", f_merge, g.grid, args, outs, [[du]])
    dys, gwbr = [], []
    for i in range(N_BRANCH):
        dp2 = dps[i].reshape(m, D_MODEL)
        dys.append(matmul("wbranch_dx", dp2, w["wbrt"][i]).reshape(g.b, g.t, BRANCH_W))
        gwbr.append(matmul("wbranch_dw", tp["ys"][i].reshape(m, BRANCH_W), dp2, trans_a=True))
    gw["wbr"] = jnp.stack(gwbr)
    dy_att, dy_dn, dy_rt, dy_sc = dys
    o_dn, o_rt = tp["o_dn"], tp["o_rt"]
    args, outs = _finish_io(g, o_dn[0], o_dn[1], z, o_rt[0], o_rt[1], sp)
    do_dn_f, do_dn_b, dz_dnz, do_rt_f, do_rt_b, dz_retg, gs["ngain"] = vbwd(
        "finish_b", f_finish, g.grid, args, outs, [[dy_dn], [dy_rt]])
    fl1 = lambda a: flipseg(a, lc)
    qn, kn, vn, gates = tp["qn"], tp["kn"], tp["vn"], tp["gates"]
    dqn, dkn, dvn, dgates = [], [], [], []
    for d, do in enumerate((do_dn_f, do_dn_b)):
        fl = (lambda a: a) if d == 0 else fl1
        res = scan_bwd("dnscan_b", dn_chunk, _dn_toks(fl(qn), fl(kn), fl(vn), fl(gates), d), [], DN_STATE, SCAN_OUT,
                       tp["dn_s"][d], [fl(do)])
        dqn.append(fl(res[0])); dkn.append(fl(res[1])); dvn.append(fl(res[2]))
        dgates += [fl(res[3]), fl(res[4])]
    ret_v = z[:, :, ZOFF["ret_v"]:ZOFF["ret_v"] + 512]
    dq_ret, dk_ret, dv_ret, dret = [], [], [], []
    for d, do in enumerate((do_rt_f, do_rt_b)):
        fl = (lambda a: a) if d == 0 else fl1
        toks = [(fl(tp["q_ret"]), P64x4), (fl(tp["k_ret"]), P64x4), (fl(ret_v), P128x4)]
        res = scan_bwd("retscan_b", make_ret_chunk(d), toks, [(sp["ret"], RET_PIECES)], RET_STATE, SCAN_OUT,
                       tp["rt_s"][d], [fl(do)])
        dq_ret.append(fl(res[0])); dk_ret.append(fl(res[1])); dv_ret.append(fl(res[2])); dret.append(res[3])
    gs["ret"] = dret
    dz_retv = addn("sum_retv", dv_ret, BF16)
    sgrid, args, outs = _shortconv_io(g, z, sp)
    dz_scb, dz_scc, dz_scx, gs["sc_conv"] = vbwd("shortconv_b", make_f_shortconv(g.t, lc), sgrid, args, outs, [[dy_sc]])
    agrid, args, outs = _attn_io(g, tp["q_att"], tp["k_att"], z)
    dq_att, dk_att, dv_att = vbwd("attn_b", make_f_attn(g.tq, g.nctx_q, lc), agrid, args, outs, [[dy_att]])
    dz_attv = addn("cast_attv", [dv_att], BF16)
    dgrid, args, outs = _dnprep_io(g, z, sp)
    dz_dnq, dz_dnk, dz_dnv, gc_q, gc_k, gc_v = vbwd("dnprep_b", make_f_dnprep(g.t, lc), dgrid, args, outs, [dqn, dkn, dvn])
    gs["dn_conv"] = [gc_q, gc_k, gc_v]
    args, outs = _prep_io(g, z, sp, tabs)
    dz_attq, dz_attk, dz_retq, dz_retk, dz_nar, gs["qgain"], gs["kgain"], gs["alog"], gs["dtb"] = vbwd(
        "prep_b", f_prep, g.grid, args, outs, [[dq_att], [dk_att], dq_ret, dk_ret, dgates])
    pad = jnp.zeros((g.b, g.t, NZ - (ZOFF["narrow"] + 128)), BF16)
    dz = jnp.concatenate([dz_gates, dz_attq, dz_dnq, dz_dnk, dz_dnv, dz_dnz, dz_retv, dz_retg, dz_scb, dz_scc, dz_scx,
                          dz_retq, dz_retk, dz_attk, dz_attv, dz_nar, pad], axis=-1)
    dz2 = dz.reshape(m, NZ)
    dh = matmul("win_dx", dz2, w["wint"]).reshape(g.b, g.t, D_MODEL)
    gw["win"] = matmul("win_dw", tp["h"].reshape(m, D_MODEL), dz2, trans_a=True)
    return dx, dh, gw, gs, dmod_a, dmod_b


def loss_call(g, xa, tgt):
    n = g.nctx
    inv_d = 1.0 / D_MODEL

    def body(x_ref, t_ref, loss_ref, dx_ref):
        j = pl.program_id(1)

        @pl.when((pl.program_id(0) == 0) & (j == 0))
        def _():
            loss_ref[...] = jnp.zeros(loss_ref.shape, F32)

        @pl.when(j < n)
        def _():
            dx_ref[...] = jnp.zeros(dx_ref.shape, F32)

        @pl.when(j >= n)
        def _():
            e = x_ref[...] - t_ref[...]
            dx_ref[...] = e * inv_d
            s = jnp.sum(jnp.sum(e * e, axis=1, keepdims=True), axis=0, keepdims=True)
            loss_ref[...] += jnp.broadcast_to(0.5 * inv_d * s, loss_ref.shape)

    tok = pl.BlockSpec((None, g.tm, D_MODEL), lambda b, j: (b, j, 0))
    return pl.pallas_call(
        body, name="loss", grid=g.grid,
        in_specs=[tok, pl.BlockSpec((None, g.tm, D_MODEL), lambda b, j: (b, jnp.maximum(j - n, 0), 0))],
        out_specs=[pl.BlockSpec((1, LANES), lambda b, j: (0, 0)), tok],
        out_shape=[jax.ShapeDtypeStruct((1, LANES), F32), jax.ShapeDtypeStruct((g.b, g.t, D_MODEL), F32)],
        compiler_params=_params(2),
    )(xa, tgt)


def rope_tables(g):
    seq = g.t - g.lc
    rows = seq // GRID_W
    r, col = jnp.meshgrid(jnp.arange(rows), jnp.arange(GRID_W), indexing="ij")
    quarter = HEAD64 // 4
    inv_freq = ROPE_THETA ** (-jnp.arange(quarter, dtype=F32) / quarter)
    ang = jnp.concatenate([r.reshape(-1, 1).astype(F32) * inv_freq, col.reshape(-1, 1).astype(F32) * inv_freq], axis=-1)
    cos, sin = jnp.cos(ang), jnp.sin(ang)
    cos = jnp.concatenate([jnp.ones((g.lc, HEAD64 // 2), F32), cos], axis=0)
    sin = jnp.concatenate([jnp.zeros((g.lc, HEAD64 // 2), F32), sin], axis=0)
    reps = 512 // (HEAD64 // 2)
    return jnp.tile(cos, (1, reps)), jnp.tile(sin, (1, reps))


def _row128(v):
    v = v.reshape(1, -1).astype(F32)
    return jnp.pad(v, ((0, 0), (0, LANES - v.shape[1])))


def layer_small(small, l):
    gn = small["g_norm"][l]
    return {
        "g0": gn[0:1], "g1": gn[1:2], "g2": gn[2:3], "g3": gn[3:4],
        "qgain": small["att_q_gain"][l][None], "kgain": small["att_k_gain"][l][None],
        "dn_conv": [small["dn_conv"][l][:, i * 512:(i + 1) * 512] for i in range(3)], "alog": _row128(small["dn_a_log"][l]), "dtb": _row128(small["dn_dt_bias"][l]),
        "ngain": small["dn_norm_gain"][l][None], "ret": _row128(small["ret_decay"][l]), "sc_conv": small["sc_conv"][l],
    }


def permute_win(w):
    parts = [w[..., off:off + width] for _, off, width in _SEGS]
    parts.append(jnp.zeros(w.shape[:-1] + (NZ - N_IN,), w.dtype))
    return jnp.concatenate(parts, axis=-1)


def unpermute_win(gw):
    order = sorted(_SEGS, key=lambda s: s[1])
    return jnp.concatenate([gw[..., ZOFF[name]:ZOFF[name] + width] for name, _, width in order], axis=-1)


def layer_weights(w_in, w_branch, w_out, w_mlp_in, w_mlp_out):
    win = permute_win(w_in.astype(BF16))
    wbr, wout, w1, w2 = (t.astype(BF16) for t in (w_branch, w_out, w_mlp_in, w_mlp_out))
    return {"win": win, "wint": win.T, "wbr": wbr, "wbrt": jnp.swapaxes(wbr, 1, 2), "wout": wout, "woutt": wout.T,
            "w1": w1, "w1t": w1.T, "w2": w2, "w2t": w2.T}


def model_step(g, xa, tgt, mod, wl, small):
    depth = len(wl)
    tabs = rope_tables(g)
    sps = [layer_small(small, l) for l in range(depth)]
    mods_a = [mod[l][:, :, 2:5] for l in range(depth)]
    mods_b = [jnp.concatenate([mod[l][:, :, 5:6], mod[l + 1][:, :, 0:2]], axis=2) if l + 1 < depth else mod[l][:, :, 5:6]
              for l in range(depth)]
    mod0 = mod[0][:, :, 0:2]
    args0 = [a_tok(g, xa), a_par(sps[0]["g0"]), a_mod(g, mod0)]
    (h,) = vfwd("modnorm", f_modnorm, g.grid, args0, [o_tok(g, D_MODEL, BF16)])
    x, tapes = xa, []
    for l in range(depth):
        g_next = sps[l + 1]["g0"] if l + 1 < depth else None
        x, h, tp = layer_fwd(g, x, h, wl[l], sps[l], mods_a[l], mods_b[l], tabs, g_next)
        tapes.append(tp)
    loss_row, dx = loss_call(g, x, tgt)
    dh, gws, gss, dmods = None, [None] * depth, [None] * depth, [None] * depth
    for l in reversed(range(depth)):
        g_next = sps[l + 1]["g0"] if l + 1 < depth else None
        dx, dh, gws[l], gss[l], dma, dmb = layer_bwd(g, tapes[l], wl[l], sps[l], mods_a[l], mods_b[l], tabs, g_next, dx, dh)
        dmods[l] = (dma, dmb)
    dxh, g0_first, dmod0 = vbwd("modnorm_b", f_modnorm, g.grid, args0, [o_tok(g, D_MODEL, BF16)], [[dh]])
    dxa = addn("sum_dx", [dx, dxh], F32)
    dmod = []
    for l in range(depth):
        first2 = dmod0 if l == 0 else dmods[l - 1][1][:, :, 1:3]
        dmod.append(jnp.concatenate([first2, dmods[l][0], dmods[l][1][:, :, 0:1]], axis=2))
    dmod = jnp.stack(dmod)
    def rows(key, n):
        return jnp.stack([gs[key][0, :n] for gs in gss])
    g_norm = jnp.stack([jnp.concatenate([g0_first if l == 0 else gss[l - 1]["g0_next"], gss[l]["g1"], gss[l]["g2"], gss[l]["g3"]], axis=0)
                        for l in range(depth)])
    gsmall = {
        "g_norm": g_norm,
        "att_q_gain": jnp.stack([gs["qgain"][0] for gs in gss]), "att_k_gain": jnp.stack([gs["kgain"][0] for gs in gss]),
        "dn_conv": jnp.stack([jnp.concatenate(gs["dn_conv"], axis=1) for gs in gss]),
        "dn_a_log": rows("alog", 8).reshape(depth, 2, N_HEAD4), "dn_dt_bias": rows("dtb", 8).reshape(depth, 2, N_HEAD4),
        "dn_norm_gain": jnp.stack([gs["ngain"][0] for gs in gss]),
        "ret_decay": jnp.stack([(gs["ret"][0] + gs["ret"][1])[0, :8] for gs in gss]).reshape(depth, 2, N_HEAD4),
        "sc_conv": jnp.stack([gs["sc_conv"] for gs in gss]),
    }
    return loss_row[0, 0], dxa, gws, gsmall, dmod


N_DEV = 8
N_XY = 4
HBM_SPEC = pl.BlockSpec(memory_space=pltpu.HBM)
VMEM_SPEC = pl.BlockSpec(memory_space=pltpu.VMEM)


def _coords():
    return lax.axis_index("x"), lax.axis_index("y"), lax.axis_index("c")


def _flip(coords, k):
    x, y, c = coords
    return (1 - x if k & 4 else x, 1 - y if k & 2 else y, 1 - c if k & 1 else c)


def allgather8(name, v):
    def body(v_ref, out_ref, send_sems, recv_sems, local_sem):
        me3 = _coords()
        me = 4 * me3[0] + 2 * me3[1] + me3[2]
        mine = pltpu.make_async_copy(v_ref, out_ref.at[me], local_sem)
        mine.start()
        sends = []
        for k in range(1, N_DEV):
            cp = pltpu.make_async_remote_copy(src_ref=v_ref, dst_ref=out_ref.at[me], send_sem=send_sems.at[k - 1],
                                              recv_sem=recv_sems.at[k - 1], device_id=_flip(me3, k), device_id_type=MESH)
            cp.start()
            sends.append(cp)
        for k in range(1, N_DEV):
            pltpu.make_async_remote_copy(src_ref=v_ref, dst_ref=out_ref.at[jnp.bitwise_xor(me, k)], send_sem=send_sems.at[k - 1],
                                         recv_sem=recv_sems.at[k - 1], device_id=_flip(me3, k), device_id_type=MESH).wait_recv()
        for cp in sends:
            cp.wait_send()
        mine.wait()

    return pl.pallas_call(
        body, name=name, out_shape=jax.ShapeDtypeStruct((N_DEV,) + v.shape, v.dtype),
        in_specs=[VMEM_SPEC], out_specs=VMEM_SPEC,
        scratch_shapes=[pltpu.SemaphoreType.DMA((N_DEV - 1,)), pltpu.SemaphoreType.DMA((N_DEV - 1,)), pltpu.SemaphoreType.DMA],
        compiler_params=pltpu.CompilerParams(vmem_limit_bytes=VMEM_LIMIT),
    )(v)


def xy_exchange(name, v, gather):
    shape = v.shape if gather else v.shape[1:]

    def body(v_ref, out_ref, send_sems, recv_sems, local_sem):
        me3 = _coords()
        me = 2 * me3[0] + me3[1]
        src = (lambda i: v_ref) if gather else (lambda i: v_ref.at[i])
        mine = pltpu.make_async_copy(src(me), out_ref.at[me], local_sem)
        mine.start()
        sends = []
        for k in range(1, N_XY):
            cp = pltpu.make_async_remote_copy(src_ref=src(jnp.bitwise_xor(me, k)), dst_ref=out_ref.at[me],
                                              send_sem=send_sems.at[k - 1], recv_sem=recv_sems.at[k - 1],
                                              device_id=_flip(me3, 2 * k), device_id_type=MESH)
            cp.start()
            sends.append(cp)
        for k in range(1, N_XY):
            pltpu.make_async_remote_copy(src_ref=src(me), dst_ref=out_ref.at[jnp.bitwise_xor(me, k)],
                                         send_sem=send_sems.at[k - 1], recv_sem=recv_sems.at[k - 1],
                                         device_id=_flip(me3, 2 * k), device_id_type=MESH).wait_recv()
        for cp in sends:
            cp.wait_send()
        mine.wait()

    return pl.pallas_call(
        body, name=name, out_shape=jax.ShapeDtypeStruct((N_XY,) + shape, v.dtype),
        in_specs=[HBM_SPEC], out_specs=HBM_SPEC,
        scratch_shapes=[pltpu.SemaphoreType.DMA((N_XY - 1,)), pltpu.SemaphoreType.DMA((N_XY - 1,)), pltpu.SemaphoreType.DMA],
    )(v)


def swap_c(name, v):
    def body(v_ref, out_ref, send_sem, recv_sem):
        cp = pltpu.make_async_remote_copy(src_ref=v_ref, dst_ref=out_ref, send_sem=send_sem, recv_sem=recv_sem,
                                          device_id=_flip(_coords(), 1), device_id_type=MESH)
        cp.start()
        cp.wait()

    return pl.pallas_call(
        body, name=name, out_shape=jax.ShapeDtypeStruct(v.shape, v.dtype), in_specs=[HBM_SPEC], out_specs=HBM_SPEC,
        scratch_shapes=[pltpu.SemaphoreType.DMA, pltpu.SemaphoreType.DMA],
    )(v)


BLOCK_BYTES = 1 << 20


def _rows_block(rows, cols):
    for tr in (1024, 512, 256, 128, 64, 32, 16, 8):
        if rows % tr == 0 and tr * cols * 4 <= BLOCK_BYTES:
            return tr
    return rows


def sum_slots(name, r):
    s, rows, cols = r.shape
    tr = _rows_block(rows, cols)

    def body(r_ref, o_ref):
        tot = r_ref[0].astype(F32)
        for i in range(1, s):
            tot = tot + r_ref[i].astype(F32)
        o_ref[...] = tot

    return pl.pallas_call(
        body, name=name, grid=(rows // tr,), in_specs=[pl.BlockSpec((s, tr, cols), lambda i: (0, i, 0))],
        out_specs=pl.BlockSpec((tr, cols), lambda i: (i, 0)), out_shape=jax.ShapeDtypeStruct((rows, cols), F32),
        compiler_params=_params(1),
    )(r)


def adamw(name, w, m, v, parts):
    rows, cols = w.shape
    tr = _rows_block(rows, cols)
    n = len(parts)
    c1 = 1.0 - ADAM_B1 ** ADAM_STEP
    c2 = 1.0 - ADAM_B2 ** ADAM_STEP

    def body(*refs):
        w_ref, m_ref, v_ref = refs[:3]
        g_ref, d_ref, nm_ref, nv_ref = refs[3 + n:]
        g = refs[3][...]
        for r in refs[4:3 + n]:
            g = g + r[...]
        nm = ADAM_B1 * m_ref[...] + (1.0 - ADAM_B1) * g
        nv = ADAM_B2 * v_ref[...] + (1.0 - ADAM_B2) * jnp.square(g)
        d_ref[...] = -ADAM_LR * ((nm / c1) / (jnp.sqrt(nv / c2) + ADAM_EPS) + ADAM_WD * w_ref[...])
        g_ref[...], nm_ref[...], nv_ref[...] = g, nm, nv

    spec = pl.BlockSpec((tr, cols), lambda i: (i, 0))
    sds = jax.ShapeDtypeStruct((rows, cols), F32)
    return pl.pallas_call(
        body, name=name, grid=(rows // tr,), in_specs=[spec] * (3 + n), out_specs=[spec] * 4, out_shape=[sds] * 4,
        compiler_params=_params(1),
    )(w, m, v, *parts)


MOD_COLS = 512


def mod_fwd(call, w_mod, b_sh):
    depth, _, cols = w_mod.shape
    nr = call.shape[0]

    def body(c_ref, w_ref, b_ref, o_ref):
        o_ref[...] = _dg(_silu(c_ref[...]), w_ref[...], 1, 0) + b_ref[...]

    return pl.pallas_call(
        body, name="mod_fwd", grid=(depth, cols // MOD_COLS),
        in_specs=[pl.BlockSpec((nr, D_MODEL), lambda l, j: (0, 0)), pl.BlockSpec((None, D_MODEL, MOD_COLS), lambda l, j: (l, 0, j)),
                  pl.BlockSpec((None, 1, MOD_COLS), lambda l, j: (l, 0, j))],
        out_specs=pl.BlockSpec((None, nr, MOD_COLS), lambda l, j: (l, 0, j)),
        out_shape=jax.ShapeDtypeStruct((depth, nr, cols), F32), compiler_params=_params(2),
    )(call, w_mod, b_sh)


def mod_bwd(call, c_ctx, d_lat, d_ctx, w_mod, ctx_row):
    depth, _, cols = w_mod.shape
    nr, ns = call.shape[0], d_ctx.shape[1]

    def body(c_ref, cc_ref, dl_ref, dc_ref, w_ref, gw_ref, gc_ref):
        crow = jnp.sum(dc_ref[...], axis=0, keepdims=True)
        row = lax.broadcasted_iota(jnp.int32, (nr, 1), 0)
        dm = jnp.where(row == ctx_row, crow, dl_ref[...])
        gw_ref[...] = _dg(_silu(c_ref[...]), dm, 0, 0)
        ds = jnp.sum(_dg(jnp.broadcast_to(crow, (8, MOD_COLS)), w_ref[...], 1, 1), axis=0, keepdims=True) * 0.125
        _, vjp = jax.vjp(_silu, cc_ref[...])
        (part,) = vjp(ds)
        first = (pl.program_id(0) == 0) & (pl.program_id(1) == 0)

        @pl.when(first)
        def _():
            gc_ref[...] = part

        @pl.when(jnp.logical_not(first))
        def _():
            gc_ref[...] += part

    return pl.pallas_call(
        body, name="mod_bwd", grid=(depth, cols // MOD_COLS),
        in_specs=[pl.BlockSpec((nr, D_MODEL), lambda l, j: (0, 0)), pl.BlockSpec((1, D_MODEL), lambda l, j: (0, 0)),
                  pl.BlockSpec((None, nr, MOD_COLS), lambda l, j: (l, 0, j)), pl.BlockSpec((None, ns, MOD_COLS), lambda l, j: (l, 0, j)),
                  pl.BlockSpec((None, D_MODEL, MOD_COLS), lambda l, j: (l, 0, j))],
        out_specs=[pl.BlockSpec((None, D_MODEL, MOD_COLS), lambda l, j: (l, 0, j)), pl.BlockSpec((1, D_MODEL), lambda l, j: (0, 0))],
        out_shape=[jax.ShapeDtypeStruct((depth, D_MODEL, cols), F32), jax.ShapeDtypeStruct((1, D_MODEL), F32)],
        compiler_params=_params(2),
    )(call, c_ctx, d_lat, d_ctx, w_mod)


def bmod_grad(dm_all):
    ndev, depth, ns, cols = dm_all.shape

    def body(d_ref, o_ref):
        tot = d_ref[0]
        for i in range(1, ndev):
            tot = tot + d_ref[i]
        o_ref[...] = jnp.sum(tot, axis=0, keepdims=True)

    return pl.pallas_call(
        body, name="bmod_grad", grid=(depth,), in_specs=[pl.BlockSpec((ndev, None, ns, cols), lambda l: (0, l, 0, 0))],
        out_specs=pl.BlockSpec((None, 1, cols), lambda l: (l, 0, 0)), out_shape=jax.ShapeDtypeStruct((depth, 1, cols), F32),
        compiler_params=_params(1),
    )(dm_all)


def small_reduce(gathered, rows_all):
    ndev, rows, lanes = gathered.shape

    def body(g_ref, o_ref):
        tot = g_ref[0, 0:rows_all]
        for i in range(1, ndev):
            tot = tot + g_ref[i, 0:rows_all]
        o_ref[0:rows_all] = tot
        part = g_ref[0, rows_all:rows]
        for i in range(2, ndev, 2):
            part = part + g_ref[i, rows_all:rows]
        o_ref[rows_all:rows] = part

    return pl.pallas_call(body, name="small_reduce", out_shape=jax.ShapeDtypeStruct((rows, lanes), F32),
                          in_specs=[VMEM_SPEC], out_specs=VMEM_SPEC)(gathered)


def pack_rows(arrays, row_multiple=8):
    flat = jnp.concatenate([a.reshape(-1).astype(F32) for a in arrays])
    per = LANES * row_multiple
    padded = -(-flat.shape[0] // per) * per
    return jnp.pad(flat, (0, padded - flat.shape[0])).reshape(-1, LANES)


def unpack_rows(buf, shapes):
    flat, out, off = buf.reshape(-1), [], 0
    for s in shapes:
        n = int(np.prod(s))
        out.append(flat[off:off + n].reshape(s))
        off += n
    return out


def _shards_of(full, kind, depth):
    if kind == "cols":
        l, k, n = full.shape
        return full.reshape(l, k, N_XY, n // N_XY).transpose(2, 0, 1, 3).reshape(N_XY, l * k, n // N_XY)
    if kind == "rows":
        l, k, n = full.shape
        return full.reshape(l, N_XY, k // N_XY, n).transpose(1, 0, 2, 3).reshape(N_XY, l * (k // N_XY), n)
    l, nb, k, n = full.shape
    return full.reshape(l, nb, k, N_XY, n // N_XY).transpose(3, 0, 1, 2, 4).reshape(N_XY, l * nb * k, n // N_XY)


def _full_of(slots, kind, shard_shape):
    if kind == "cols":
        l, k, n = shard_shape
        return slots.reshape(N_XY, l, k, n).transpose(1, 2, 0, 3).reshape(l, k, N_XY * n)
    if kind == "rows":
        l, k, n = shard_shape
        return slots.reshape(N_XY, l, k, n).transpose(1, 0, 2, 3).reshape(l, N_XY * k, n)
    l, nb, k, n = shard_shape
    return slots.reshape(N_XY, l, nb, k, n).transpose(1, 2, 3, 0, 4).reshape(l, nb, k, N_XY * n)


BIG = (("w_in", "cols"), ("w_branch", "branch"), ("w_out", "rows"), ("w_mlp_in", "cols"), ("w_mlp_out", "rows"))
SMALL_SHARDED = ("g_norm", "dn_conv", "sc_conv")
SMALL_ORDER = ("g_norm", "att_q_gain", "att_k_gain", "dn_conv", "dn_a_log", "dn_dt_bias", "dn_norm_gain", "ret_decay", "sc_conv")


def kernel(x, c, ctx, c_ctx, w_mod, b_mod, g_norm, w_in, att_q_gain, att_k_gain, dn_conv, dn_a_log, dn_dt_bias, dn_norm_gain, ret_decay, sc_conv, w_branch, w_out, w_mlp_in, w_mlp_out, loss_target, m_c_ctx, m_w_mod, m_b_mod, m_g_norm, m_w_in, m_att_q_gain, m_att_k_gain, m_dn_conv, m_dn_a_log, m_dn_dt_bias, m_dn_norm_gain, m_ret_decay, m_sc_conv, m_w_branch, m_w_out, m_w_mlp_in, m_w_mlp_out, v_c_ctx, v_w_mod, v_b_mod, v_g_norm, v_w_in, v_att_q_gain, v_att_k_gain, v_dn_conv, v_dn_a_log, v_dn_dt_bias, v_dn_norm_gain, v_ret_decay, v_sc_conv, v_w_branch, v_w_out, v_w_mlp_in, v_w_mlp_out):
    wts = dict(c_ctx=c_ctx, w_mod=w_mod, b_mod=b_mod, g_norm=g_norm, w_in=w_in, att_q_gain=att_q_gain, att_k_gain=att_k_gain,
               dn_conv=dn_conv, dn_a_log=dn_a_log, dn_dt_bias=dn_dt_bias, dn_norm_gain=dn_norm_gain, ret_decay=ret_decay,
               sc_conv=sc_conv, w_branch=w_branch, w_out=w_out, w_mlp_in=w_mlp_in, w_mlp_out=w_mlp_out)
    mom = dict(c_ctx=m_c_ctx, w_mod=m_w_mod, b_mod=m_b_mod, g_norm=m_g_norm, w_in=m_w_in, att_q_gain=m_att_q_gain,
               att_k_gain=m_att_k_gain, dn_conv=m_dn_conv, dn_a_log=m_dn_a_log, dn_dt_bias=m_dn_dt_bias,
               dn_norm_gain=m_dn_norm_gain, ret_decay=m_ret_decay, sc_conv=m_sc_conv, w_branch=m_w_branch, w_out=m_w_out,
               w_mlp_in=m_w_mlp_in, w_mlp_out=m_w_mlp_out)
    var = dict(c_ctx=v_c_ctx, w_mod=v_w_mod, b_mod=v_b_mod, g_norm=v_g_norm, w_in=v_w_in, att_q_gain=v_att_q_gain,
               att_k_gain=v_att_k_gain, dn_conv=v_dn_conv, dn_a_log=v_dn_a_log, dn_dt_bias=v_dn_dt_bias,
               dn_norm_gain=v_dn_norm_gain, ret_decay=v_ret_decay, sc_conv=v_sc_conv, w_branch=v_w_branch, w_out=v_w_out,
               w_mlp_in=v_w_mlp_in, w_mlp_out=v_w_mlp_out)
    names = list(wts)
    depth, bsz, seq, lc = w_mod.shape[0], x.shape[0], x.shape[1], ctx.shape[1]
    g = Geo(bsz, lc + seq, lc)
    xi, yi, ci = _coords()
    dev, xy = 4 * xi + 2 * yi + ci, 2 * xi + yi
    n_batch = N_DEV * bsz
    nr = -(-(n_batch + 1) // 16) * 16
    mod_cols = w_mod.shape[2]

    c_all = allgather8("gather_c", c).reshape(n_batch, D_MODEL)
    call = jnp.concatenate([c_all, c_ctx[None], jnp.zeros((nr - n_batch - 1, D_MODEL), F32)], axis=0)
    b_sh = lax.dynamic_slice_in_dim(b_mod, xy * mod_cols, mod_cols, axis=1)[:, None, :]
    mod_sh = mod_fwd(call, w_mod, b_sh)
    mod_g = allgather8("gather_mod", mod_sh.reshape(depth * nr, mod_cols)).reshape(N_XY, 2, depth, nr, mod_cols)[:, 0]
    mod_all = mod_g.transpose(1, 2, 0, 3).reshape(depth, nr, N_XY * mod_cols)
    mod_lat = lax.dynamic_slice_in_dim(mod_all, dev * bsz, bsz, axis=1).reshape(depth, bsz, 6, D_MODEL)
    mod_ctx = jnp.broadcast_to(mod_all[:, n_batch].reshape(depth, 1, 6, D_MODEL), (depth, bsz, 6, D_MODEL))
    mod = jnp.stack([mod_ctx, mod_lat], axis=2)

    sm_shapes = [wts[k].shape for k in SMALL_SHARDED]
    sm_g = allgather8("gather_small", pack_rows([wts[k] for k in SMALL_SHARDED])).reshape(N_XY, 2, -1)[:, 0]
    small = {k: wts[k] for k in SMALL_ORDER}
    for k, parts in zip(SMALL_SHARDED, zip(*[unpack_rows(sm_g[p], sm_shapes) for p in range(N_XY)])):
        small[k] = jnp.concatenate(parts, axis=-1)

    full = {}
    for k, kind in BIG:
        sh = wts[k].astype(BF16)
        slots = xy_exchange("gather_" + k, sh.reshape(-1, sh.shape[-1]), True)
        full[k] = _full_of(slots, kind, sh.shape)
    wl = [layer_weights(full["w_in"][l], full["w_branch"][l], full["w_out"][l], full["w_mlp_in"][l], full["w_mlp_out"][l])
          for l in range(depth)]

    xa = jnp.concatenate([ctx, x], axis=1)
    loss_part, dxa, gws, gsmall, dmod = model_step(g, xa, loss_target, mod, wl, small)
    loss = lax.psum(loss_part, ("x", "y", "c"))
    grad_x = dxa[:, lc:]

    grads, deltas, new_m, new_v = {}, {}, {}, {}

    def update(k, parts, shape2d):
        res = adamw("adamw_" + k, wts[k].reshape(shape2d), mom[k].reshape(shape2d), var[k].reshape(shape2d), parts)
        grads[k], deltas[k], new_m[k], new_v[k] = (r.reshape(wts[k].shape) for r in res)

    gfull = {"w_in": unpermute_win(jnp.stack([gw["win"] for gw in gws])), "w_branch": jnp.stack([gw["wbr"] for gw in gws]),
             "w_out": jnp.stack([gw["wout"] for gw in gws]), "w_mlp_in": jnp.stack([gw["w1"] for gw in gws]),
             "w_mlp_out": jnp.stack([gw["w2"] for gw in gws])}
    for k, kind in BIG:
        recv = xy_exchange("scatter_" + k, _shards_of(gfull[k], kind, depth), False)
        part = sum_slots("sum_" + k, recv)
        other = swap_c("swap_" + k, part)
        update(k, [part, other], part.shape)

    dm_mine = jnp.concatenate([dmod[:, :, 1], dmod[:, :, 0]], axis=1).reshape(depth * 2 * bsz, 6 * D_MODEL)
    dm_all = allgather8("gather_dmod", dm_mine).reshape(N_DEV, depth, 2 * bsz, 6 * D_MODEL)
    gb = bmod_grad(dm_all).reshape(depth, 6 * D_MODEL)
    dm_cols = lax.dynamic_slice_in_dim(dm_all, xy * mod_cols, mod_cols, axis=3)
    d_lat = dm_cols[:, :, :bsz].transpose(1, 0, 2, 3).reshape(depth, n_batch, mod_cols)
    d_lat = jnp.pad(d_lat, ((0, 0), (0, nr - n_batch), (0, 0)))
    d_ctx = dm_cols[:, :, bsz:].transpose(1, 0, 2, 3).reshape(depth, n_batch, mod_cols)
    gw_mod, gc_part = mod_bwd(call, c_ctx[None], d_lat, d_ctx, w_mod, n_batch)
    update("w_mod", [gw_mod.reshape(depth * D_MODEL, mod_cols)], (depth * D_MODEL, mod_cols))
    update("b_mod", [gb], b_mod.shape)

    pack_all = pack_rows([gsmall[k] for k in SMALL_ORDER])
    pack_xy = pack_rows([gc_part])
    rows_all = pack_all.shape[0]
    tot = small_reduce(allgather8("gather_gsmall", jnp.concatenate([pack_all, pack_xy], axis=0)), rows_all)
    gtot = dict(zip(SMALL_ORDER, unpack_rows(tot[:rows_all], [gsmall[k].shape for k in SMALL_ORDER])))
    gtot["c_ctx"] = unpack_rows(tot[rows_all:], [c_ctx.shape])[0]
    for k in SMALL_SHARDED:
        width = wts[k].shape[-1]
        gtot[k] = lax.dynamic_slice_in_dim(gtot[k], xy * width, width, axis=gtot[k].ndim - 1)
    sm_names = ("c_ctx",) + SMALL_ORDER
    sm_shapes = [wts[k].shape for k in sm_names]
    res = adamw("adamw_small", pack_rows([wts[k] for k in sm_names]), pack_rows([mom[k] for k in sm_names]),
                pack_rows([var[k] for k in sm_names]), [pack_rows([gtot[k] for k in sm_names])])
    for dst, buf in zip((grads, deltas, new_m, new_v), res):
        dst.update(zip(sm_names, unpack_rows(buf, sm_shapes)))

    return (loss, grad_x, *[grads[k] for k in names], *[deltas[k] for k in names], *[new_m[k] for k in names],
            *[new_v[k] for k in names])
```

```python
import functools
import math

import numpy as np
import jax
import jax.numpy as jnp
from jax import lax
from jax.experimental import pallas as pl
from jax.experimental.pallas import tpu as pltpu

F32, BF16 = jnp.float32, jnp.bfloat16
HIGHEST = lax.Precision.HIGHEST
MESH = pl.DeviceIdType.MESH

D_MODEL = 1024
GRID_W = 64
N_BRANCH = 4
BRANCH_W = 512
HEAD64 = 64
HEAD128 = 128
N_HEAD4 = 4
ATT_HEADS = 8
ATT_KV_HEADS = 2
CHUNK = 64
MLP_HIDDEN = 4 * D_MODEL
ROPE_THETA = 10000.0
EPS = 1e-6
N_IN = 10000
ADAM_LR, ADAM_B1, ADAM_B2, ADAM_EPS, ADAM_WD, ADAM_STEP = 0.001, 0.9, 0.999, 1e-08, 0.01, 10

LANES = 128
VMEM_LIMIT = 56 * 1024 * 1024

_SEGS = (
    ("gates", 5904, 4096),
    ("att_q", 0, 512), ("dn_q", 768, 512), ("dn_k", 1280, 512), ("dn_v", 1792, 512), ("dn_z", 2304, 512),
    ("ret_v", 3344, 512), ("ret_g", 3856, 512), ("sc_b", 4368, 512), ("sc_c", 4880, 512), ("sc_x", 5392, 512),
    ("ret_q", 2832, 256), ("ret_k", 3088, 256),
    ("att_k", 512, 128), ("att_v", 640, 128),
    ("narrow", 2816, 16),
)
NZ = 10240


def _seg_offsets():
    off, out = 0, {}
    for name, _, width in _SEGS:
        out[name] = off
        off += width
    return out


ZOFF = _seg_offsets()


def _pick(n, cands):
    for c in cands:
        if n % c == 0:
            return c
    return n


def _dg(a, b, ca, cb, batch=False):
    dn = (((ca,), (cb,)), ((0,), (0,))) if batch else (((ca,), (cb,)), ((), ()))
    return lax.dot_general(a.astype(BF16), b.astype(BF16), dn, preferred_element_type=F32)


@functools.partial(jax.custom_vjp, nondiff_argnums=(2, 3))
def _mm(a, b, ca, cb):
    return _dg(a, b, ca, cb)


def _mm_fwd(a, b, ca, cb):
    return _dg(a, b, ca, cb), (a, b)


def _mm_bwd(ca, cb, res, g):
    a, b = res
    if ca == 1:
        da = _mm(g, b, 1, 1) if cb == 0 else _mm(g, b, 1, 0)
    else:
        da = _mm(b, g, 1, 1) if cb == 0 else _mm(b, g, 0, 1)
    if cb == 0:
        db = _mm(a, g, 0, 0) if ca == 1 else _mm(a, g, 1, 0)
    else:
        db = _mm(g, a, 0, 0) if ca == 1 else _mm(g, a, 0, 1)
    return da.astype(a.dtype), db.astype(b.dtype)


_mm.defvjp(_mm_fwd, _mm_bwd)


@functools.partial(jax.custom_vjp, nondiff_argnums=(2, 3))
def _bmm(a, b, ca, cb):
    return _dg(a, b, ca, cb, True)


def _bmm_fwd(a, b, ca, cb):
    return _dg(a, b, ca, cb, True), (a, b)


def _bmm_bwd(ca, cb, res, g):
    a, b = res
    if ca == 2:
        da = _bmm(g, b, 2, 2) if cb == 1 else _bmm(g, b, 2, 1)
    else:
        da = _bmm(b, g, 2, 2) if cb == 1 else _bmm(b, g, 1, 2)
    if cb == 1:
        db = _bmm(a, g, 1, 1) if ca == 2 else _bmm(a, g, 2, 1)
    else:
        db = _bmm(g, a, 1, 1) if ca == 2 else _bmm(g, a, 1, 2)
    return da.astype(a.dtype), db.astype(b.dtype)


_bmm.defvjp(_bmm_fwd, _bmm_bwd)


def _split_bf16(x):
    hi = x.astype(BF16)
    lo = (x - hi.astype(F32)).astype(BF16)
    return hi, lo


def _mm3(a, b, ca, cb):
    ah, al = _split_bf16(a)
    bh, bl = _split_bf16(b)
    dn = (((ca,), (cb,)), ((), ()))
    d = lambda u, v: lax.dot_general(u, v, dn, preferred_element_type=F32)
    return d(ah, bh) + (d(ah, bl) + d(al, bh))


def _mm_exact(a, b):
    return jnp.dot(a, b, precision=HIGHEST, preferred_element_type=F32)


class Arg:
    def __init__(self, x, block, imap, diff=True, first=None, gdtype=F32, pieces=None, gshape=None, gimap=None):
        self.x, self.block, self.imap = x, tuple(block), imap
        self.diff, self.first, self.gdtype, self.pieces = diff, first, gdtype, pieces
        self.gshape = tuple(x.shape) if gshape is None else tuple(gshape)
        self.gimap = imap if gimap is None else gimap

    def spec(self):
        return pl.BlockSpec(self.block, self.imap)

    def gspec(self):
        return pl.BlockSpec(self.block, self.gimap)


class Out:
    def __init__(self, shape, dtype, block, imap, pieces=None):
        self.shape, self.dtype, self.block, self.imap, self.pieces = tuple(shape), dtype, tuple(block), imap, pieces

    def spec(self):
        return pl.BlockSpec(self.block, self.imap)

    def sds(self):
        return jax.ShapeDtypeStruct(self.shape, self.dtype)


def _lanes(ref, s, w):
    return (slice(None),) * (len(ref.shape) - 1) + (slice(s, s + w),)


def _load(ref, pieces):
    if pieces is None:
        return ref[...]
    return tuple(ref[_lanes(ref, s, w)] for s, w in pieces)


def _store(ref, val, pieces, accumulate=False):
    if pieces is None:
        if accumulate:
            ref[...] += val.astype(ref.dtype)
        else:
            ref[...] = val.astype(ref.dtype)
        return
    if not accumulate:
        covered = sum(w for _, w in pieces)
        if covered != ref.shape[-1]:
            ref[...] = jnp.zeros(ref.shape, ref.dtype)
    for (s, w), v in zip(pieces, val):
        if accumulate:
            ref[_lanes(ref, s, w)] += v.astype(ref.dtype)
        else:
            ref[_lanes(ref, s, w)] = v.astype(ref.dtype)


def _params(n_grid):
    return pltpu.CompilerParams(dimension_semantics=("arbitrary",) * n_grid, vmem_limit_bytes=VMEM_LIMIT)


def vfwd(name, f, grid, args, outs):
    n_in = len(args)

    def body(*refs):
        ids = tuple(pl.program_id(i) for i in range(len(grid)))
        vals = [_load(r, a.pieces) for r, a in zip(refs[:n_in], args)]
        res = f(ids, *vals)
        for r, o, spec in zip(refs[n_in:], res, outs):
            _store(r, o, spec.pieces)

    return pl.pallas_call(
        body, name=name, grid=grid,
        in_specs=[a.spec() for a in args], out_specs=[o.spec() for o in outs],
        out_shape=[o.sds() for o in outs], compiler_params=_params(len(grid)),
    )(*[a.x for a in args])


def vbwd(name, f, grid, args, outs, cts):
    n_in = len(args)
    diff_idx = [i for i, a in enumerate(args) if a.diff]
    ct_flat = [c for per_out in cts for c in per_out]
    ct_specs = [o.spec() for o, per_out in zip(outs, cts) for _ in per_out]
    n_ct = len(ct_flat)

    def body(*refs):
        ids = tuple(pl.program_id(i) for i in range(len(grid)))
        vals = [_load(r, a.pieces) for r, a in zip(refs[:n_in], args)]
        ct_refs = refs[n_in:n_in + n_ct]
        g_refs = refs[n_in + n_ct:]

        def g(*dvals):
            full = list(vals)
            for i, v in zip(diff_idx, dvals):
                full[i] = v
            return tuple(f(ids, *full))

        _, vjp = jax.vjp(g, *[vals[i] for i in diff_idx])
        ct_vals, k = [], 0
        for o, per_out in zip(outs, cts):
            tot = None
            for _ in per_out:
                v = _load(ct_refs[k], o.pieces)
                k += 1
                v = jax.tree.map(lambda t: t.astype(F32), v)
                tot = v if tot is None else jax.tree.map(jnp.add, tot, v)
            ct_vals.append(tot)
        grads = vjp(tuple(ct_vals))
        for gr, gv, i in zip(g_refs, grads, diff_idx):
            a = args[i]
            if a.first is None:
                _store(gr, gv, a.pieces)
            else:
                is_first = a.first(ids)

                @pl.when(is_first)
                def _():
                    _store(gr, gv, a.pieces)

                @pl.when(jnp.logical_not(is_first))
                def _():
                    _store(gr, gv, a.pieces, accumulate=True)

    g_specs = [args[i].gspec() for i in diff_idx]
    g_shapes = [jax.ShapeDtypeStruct(args[i].gshape, args[i].gdtype) for i in diff_idx]
    return pl.pallas_call(
        body, name=name, grid=grid,
        in_specs=[a.spec() for a in args] + ct_specs, out_specs=g_specs, out_shape=g_shapes,
        compiler_params=_params(len(grid)),
    )(*[a.x for a in args], *ct_flat)


def matmul(name, a, b, trans_a=False, out_dtype=F32):
    if trans_a:
        kdim, m = a.shape
    else:
        m, kdim = a.shape
    n = b.shape[1]
    assert b.shape[0] == kdim
    assert out_dtype == F32
    tm = _pick(m, (1024, 512, 256, 192, 128, 64))
    tn = _pick(n, (1024, 512, 256, 128))
    tk = _pick(kdim, (1024, 512, 256, 192, 128, 64))
    nk = kdim // tk

    def body(a_ref, b_ref, o_ref):
        part = _dg(a_ref[...], b_ref[...], 0 if trans_a else 1, 0)
        if nk == 1:
            o_ref[...] = part
        else:
            k = pl.program_id(2)

            @pl.when(k == 0)
            def _():
                o_ref[...] = part

            @pl.when(k > 0)
            def _():
                o_ref[...] += part

    a_spec = pl.BlockSpec((tk, tm), lambda i, j, k: (k, i)) if trans_a else pl.BlockSpec((tm, tk), lambda i, j, k: (i, k))
    return pl.pallas_call(
        body, name=name, grid=(m // tm, n // tn, nk),
        in_specs=[a_spec, pl.BlockSpec((tk, tn), lambda i, j, k: (k, j))],
        out_specs=pl.BlockSpec((tm, tn), lambda i, j, k: (i, j)),
        out_shape=jax.ShapeDtypeStruct((m, n), out_dtype),
        compiler_params=pltpu.CompilerParams(dimension_semantics=("parallel", "parallel", "arbitrary"),
                                             vmem_limit_bytes=VMEM_LIMIT),
    )(a, b)


def _rms(x, gain):
    return x * lax.rsqrt(jnp.mean(x * x, axis=-1, keepdims=True) + EPS) * gain


def _silu(x):
    return x * jax.nn.sigmoid(x)


def f_modnorm(ids, x, gain, mod):
    return (_rms(x, gain) * (1.0 + mod[1:2]) + mod[0:1],)


def f_resnorm(ids, x, y, g_res, g_next, mod):
    x_new = x + mod[0:1] * _rms(y, g_res)
    return x_new, _rms(x_new, g_next) * (1.0 + mod[2:3]) + mod[1:2]


def f_resid(ids, x, y, g_res, mod):
    return (x + mod[0:1] * _rms(y, g_res),)


def f_act(ids, a):
    r = jnp.maximum(a, 0.0)
    return (r * r,)


def _head_consts(width, head):
    i = lax.broadcasted_iota(jnp.int32, (width, width), 0)
    j = lax.broadcasted_iota(jnp.int32, (width, width), 1)
    shift = int(math.log2(head))
    same = (i >> shift) == (j >> shift)
    group = jnp.where(same, 1.0 / head, 0.0).astype(F32)
    half = head // 2
    ii, jj = i & (head - 1), j & (head - 1)
    rot = jnp.where(same & (ii == jj + half) & (jj < half), -1.0, 0.0) + jnp.where(same & (ii + half == jj) & (jj >= half), 1.0, 0.0)
    ti = lax.broadcasted_iota(jnp.int32, (head, width), 0)
    tj = lax.broadcasted_iota(jnp.int32, (head, width), 1)
    tile = jnp.where(ti == (tj & (head - 1)), 1.0, 0.0).astype(F32)
    return group, rot.astype(F32), tile


def _rope(x, cos, sin, rot):
    return x * cos + _mm_exact(x, rot) * sin


def _softplus(x):
    return jnp.maximum(x, 0.0) + jnp.log(1.0 + jnp.exp(-jnp.abs(x)))


def f_prep(ids, zq, zk, rq, rk, zn, qgain, kgain, alog, dtb, cos, sin):
    grp_q, rot_q, tile_q = _head_consts(ATT_HEADS * HEAD64, HEAD64)
    grp_k, rot_k, tile_k = _head_consts(ATT_KV_HEADS * HEAD64, HEAD64)
    grp_r, rot_r, _ = _head_consts(N_HEAD4 * HEAD64, HEAD64)
    wq, wk, wr = zq.shape[-1], zk.shape[-1], rq.shape[-1]
    qn = zq * lax.rsqrt(_mm_exact(zq * zq, grp_q) + EPS) * _mm_exact(qgain, tile_q)
    kn = zk * lax.rsqrt(_mm_exact(zk * zk, grp_k) + EPS) * _mm_exact(kgain, tile_k)
    q_att = _rope(qn, cos[:, :wq], sin[:, :wq], rot_q) * (HEAD64 ** -0.5)
    k_att = _rope(kn, cos[:, :wk], sin[:, :wk], rot_k)
    q_ret = _rope(rq, cos[:, :wr], sin[:, :wr], rot_r)
    k_ret = _rope(rk * (HEAD64 ** -0.5), cos[:, :wr], sin[:, :wr], rot_r)
    lane = lax.broadcasted_iota(jnp.int32, zn.shape, 1)
    log_a = -jnp.exp(alog) * _softplus(zn + dtb)
    gates = jnp.where(lane < 8, log_a, jnp.where(lane < 16, jax.nn.sigmoid(zn), 0.0))
    return q_att, k_att, q_ret, k_ret, gates


def make_f_attn(tq):
    def f_attn(ids, q, k, v):
        outs = []
        per = ATT_HEADS // ATT_KV_HEADS
        for g in range(ATT_KV_HEADS):
            qg = jnp.concatenate(q[g * per:(g + 1) * per], axis=0)
            s = _mm(qg, k[g], 1, 1)
            e = jnp.exp(s - lax.stop_gradient(jnp.max(s, axis=-1, keepdims=True)))
            o = _mm(e, v[g], 1, 0) * (1.0 / jnp.sum(e, axis=-1, keepdims=True))
            outs += [o[i * tq:(i + 1) * tq] for i in range(per)]
        return (tuple(outs),)
    return f_attn


def _roll_rows(x, shift):
    return pltpu.roll(x, shift, 0)


def make_shifts(t, lc):
    def _down(x):
        row = lax.broadcasted_iota(jnp.int32, x.shape, 0)
        return jnp.where((row == 0) | (row == lc), 0.0, _roll_rows(x, 1))

    def _up(x):
        row = lax.broadcasted_iota(jnp.int32, x.shape, 0)
        return jnp.where((row == lc - 1) | (row == t - 1), 0.0, _roll_rows(x, t - 1))

    @jax.custom_vjp
    def down(x):
        return _down(x)

    @jax.custom_vjp
    def up(x):
        return _up(x)

    down.defvjp(lambda x: (_down(x), None), lambda _, g: (up(g),))
    up.defvjp(lambda x: (_up(x), None), lambda _, g: (down(g),))
    return down, up


def make_conv3(t, lc):
    down, up = make_shifts(t, lc)

    def conv3(x, w):
        return w[0:1] * down(x) + w[1:2] * x + w[2:3] * up(x)
    return conv3


def make_f_dnprep(t, lc):
    conv3 = make_conv3(t, lc)

    def l2n(x):
        return x * lax.rsqrt(jnp.sum(x * x, axis=-1, keepdims=True) + EPS)

    def f_dnprep(ids, q, k, v, wq, wk, wv):
        qn = l2n(_silu(conv3(q, wq))) * (HEAD128 ** -0.5)
        kn = l2n(_silu(conv3(k, wk)))
        return qn, kn, _silu(conv3(v, wv))
    return f_dnprep


def make_f_shortconv(t, lc):
    conv3 = make_conv3(t, lc)

    def f_shortconv(ids, b, c, x, w):
        return (b * conv3(c * x, w),)
    return f_shortconv


def f_finish(ids, o_dn_f, o_dn_b, z_dn, o_rt_f, o_rt_b, g_rt, ngain):
    y_dn, y_rt = [], []
    for h in range(N_HEAD4):
        o = o_dn_f[h] + o_dn_b[h]
        y_dn.append(_rms(o, ngain) * _silu(z_dn[h]))
        r = o_rt_f[h] + o_rt_b[h]
        mu = jnp.mean(r, axis=-1, keepdims=True)
        var = jnp.mean(jnp.square(r - mu), axis=-1, keepdims=True)
        y_rt.append((r - mu) * lax.rsqrt(var + EPS) * _silu(g_rt[h]))
    return tuple(y_dn), tuple(y_rt)


def f_merge(ids, p0, p1, p2, p3, gates):
    u = jax.nn.sigmoid(gates[0]) * p0
    for g, p in zip(gates[1:], (p1, p2, p3)):
        u = u + jax.nn.sigmoid(g) * p
    return (u,)


def _stack_masks():
    n = N_HEAD4 * CHUNK
    i = lax.broadcasted_iota(jnp.int32, (n, n), 0)
    j = lax.broadcasted_iota(jnp.int32, (n, n), 1)
    same = (i >> 6) == (j >> 6)
    pi, pj = i & (CHUNK - 1), j & (CHUNK - 1)
    return same, pi, pj


def _inv_unit_lower(low):
    n = low.shape[0]
    eye = jnp.where(lax.broadcasted_iota(jnp.int32, (n, n), 0) == lax.broadcasted_iota(jnp.int32, (n, n), 1), 1.0, 0.0).astype(F32)
    m = -low
    p = eye + m
    for _ in range(int(math.log2(CHUNK)) - 1):
        m = _mm3(m, m, 1, 0)
        p = p + _mm3(p, m, 1, 0)
    return p


@jax.custom_vjp
def _tri_solve(low, rhs):
    return _mm3(_inv_unit_lower(low), rhs, 1, 0)


def _tri_solve_fwd(low, rhs):
    inv = _inv_unit_lower(low)
    x = _mm3(inv, rhs, 1, 0)
    return x, (inv, x)


def _tri_solve_bwd(res, g):
    inv, x = res
    d_rhs = _mm3(inv, g, 0, 0)
    return -_mm3(d_rhs, x, 1, 1), d_rhs


_tri_solve.defvjp(_tri_solve_fwd, _tri_solve_bwd)


def _heads3(x):
    return x.reshape(N_HEAD4, CHUNK, x.shape[-1])


def _time_masks(direction):
    same, pi, pj = _stack_masks()
    if direction == 0:
        return same, same & (pi >= pj), same & (pi > pj), same & (pj == CHUNK - 1), pi - pj
    return same, same & (pi <= pj), same & (pi < pj), same & (pj == 0), pj - pi


def make_dn_chunk(direction):
    def dn_chunk(s, toks, params):
        q, k, v, la, beta = toks
        qs, ks, vs = (jnp.concatenate(t, axis=0) for t in (q, k, v))
        la, beta = jnp.concatenate(la, axis=0), jnp.concatenate(beta, axis=0)
        n = N_HEAD4 * CHUNK
        same, incl, strict, last, _ = _time_masks(direction)
        g = jnp.sum(jnp.where(incl, jnp.broadcast_to(la, (n, n)).T, 0.0), axis=1, keepdims=True)
        gb = jnp.broadcast_to(g, (n, n))
        gbt = gb.T
        dec_incl = jnp.where(incl, jnp.exp(jnp.where(incl, gb - gbt, 0.0)), 0.0)
        dec_strict = jnp.where(strict, dec_incl, 0.0)
        low = beta * _mm(ks, ks, 1, 1) * dec_strict
        eg = jnp.exp(g)
        sol = _tri_solve(low, jnp.concatenate([beta * vs, (beta * eg) * ks], axis=1))
        w_v, w_k = sol[:, :HEAD128], sol[:, HEAD128:]
        a_qk = _mm(qs, ks, 1, 1) * dec_incl
        g_last = jnp.sum(jnp.where(last, gbt, 0.0), axis=1, keepdims=True)
        k_dec = ks * jnp.exp(g_last - g)
        u = w_v - _bmm(_heads3(w_k), s, 2, 1).reshape(n, HEAD128)
        o = _bmm(_heads3(qs * eg), s, 2, 1).reshape(n, HEAD128) + _mm(a_qk, u, 1, 0)
        decay = jnp.exp(jnp.mean(_heads3(g_last), axis=1, keepdims=True))
        s_new = s * decay + _bmm(_heads3(k_dec), _heads3(u), 1, 1)
        return (tuple(o[h * CHUNK:(h + 1) * CHUNK] for h in range(N_HEAD4)),), s_new
    return dn_chunk


def make_ret_chunk(direction):
    def ret_chunk(s, toks, params):
        q, k, v = toks
        (decay_log,) = params
        qs, ks, vs = (jnp.concatenate(t, axis=0) for t in (q, k, v))
        n = N_HEAD4 * CHUNK
        _, incl, _, _, rel = _time_masks(direction)
        row_head = lax.broadcasted_iota(jnp.int32, (n, 1), 0) >> 6
        lg = jnp.zeros((n, 1), F32)
        for h in range(N_HEAD4):
            lg = jnp.where(row_head == h, -jnp.exp(decay_log[direction * N_HEAD4 + h]), lg)
        row = (lax.broadcasted_iota(jnp.int32, (n, 1), 0) & (CHUNK - 1)).astype(F32)
        pos = row if direction == 0 else CHUNK - 1.0 - row
        dmask = jnp.where(incl, jnp.exp(jnp.where(incl, rel.astype(F32) * lg, 0.0)), 0.0)
        o = _mm(_mm(qs, ks, 1, 1) * dmask, vs, 1, 0)
        k_dec = ks * jnp.exp((CHUNK - 1.0 - pos) * lg)
        kv = _bmm(_heads3(k_dec), _heads3(vs), 1, 1)
        o = o + _bmm(_heads3(qs * jnp.exp((pos + 1.0) * lg)), s, 2, 1).reshape(n, HEAD128)
        decay = jnp.exp(CHUNK * jnp.mean(_heads3(lg), axis=1, keepdims=True))
        s_new = s * decay + kv
        return (tuple(o[h * CHUNK:(h + 1) * CHUNK] for h in range(N_HEAD4)),), s_new
    return ret_chunk


def _lane_pieces(width, n=N_HEAD4):
    return [(h * width, width) for h in range(n)]


class Tok:
    def __init__(self, x, pieces, width=None, col=0):
        self.x, self.pieces, self.col = x, pieces, col
        self.width = x.shape[-1] if width is None else width


def _chunk_of(direction, step, nc, nctx):
    if direction == 0:
        return step
    return jnp.where(step < nctx, nctx - 1 - step, nc + nctx - 1 - step)


def scan_fwd(name, dirs, params, state_shape, outs, lc):
    bsz, t = dirs[0][1][0].x.shape[:2]
    nc, nctx = t // CHUNK, lc // CHUNK
    nd, n_t, n_p, n_o = len(dirs), len(dirs[0][1]), len(params), len(outs)

    def body(*refs):
        tok_refs, par_refs = refs[:nd * n_t], refs[nd * n_t:nd * n_t + n_p]
        out_refs = refs[nd * n_t + n_p:-nd]
        s_refs = refs[-nd:]
        pv = [_load(r, p) for r, (_, p) in zip(par_refs, params)]
        for d, (fn, toks) in enumerate(dirs):
            s_ref = s_refs[d]

            @pl.when(pl.program_id(1) == 0)
            def _():
                s_ref[...] = jnp.zeros(s_ref.shape, F32)

            o_refs = out_refs[d * (n_o + 1):(d + 1) * (n_o + 1)]
            s = s_ref[...]
            o_refs[-1][...] = s
            tv = [_load(r, tk.pieces) for r, tk in zip(tok_refs[d * n_t:(d + 1) * n_t], toks)]
            res, s_new = fn(s, tv, pv)
            s_ref[...] = s_new
            for r, o, (_, _, pieces) in zip(o_refs, res, outs):
                _store(r, o, pieces)

    nstate = len(state_shape)
    tok_specs, out_specs, out_shapes, operands = [], [], [], []
    for d, (_, toks) in enumerate(dirs):
        for tk in toks:
            tok_specs.append(pl.BlockSpec((None, CHUNK, tk.width), lambda b, c, d=d, col=tk.col: (b, _chunk_of(d, c, nc, nctx), col)))
            operands.append(tk.x)
        for w, dt, _ in outs:
            out_specs.append(pl.BlockSpec((None, CHUNK, w), lambda b, c, d=d: (b, _chunk_of(d, c, nc, nctx), 0)))
            out_shapes.append(jax.ShapeDtypeStruct((bsz, t, w), dt))
        out_specs.append(pl.BlockSpec((None, None) + state_shape, lambda b, c, d=d: (b, _chunk_of(d, c, nc, nctx)) + (0,) * nstate))
        out_shapes.append(jax.ShapeDtypeStruct((bsz, nc) + state_shape, F32))
    par_specs = [pl.BlockSpec(x.shape, lambda b, c: (0, 0)) for x, _ in params]
    return pl.pallas_call(
        body, name=name, grid=(bsz, nc),
        in_specs=tok_specs + par_specs, out_specs=out_specs, out_shape=out_shapes,
        scratch_shapes=[pltpu.VMEM(state_shape, F32)] * nd, compiler_params=_params(2),
    )(*operands, *[x for x, _ in params])


def scan_bwd(name, dirs, params, state_shape, outs, lc, sprevs, cts):
    bsz, t = dirs[0][1][0].x.shape[:2]
    nc, nctx = t // CHUNK, lc // CHUNK
    nd, n_t, n_p, n_o = len(dirs), len(dirs[0][1]), len(params), len(outs)
    per_in = n_t + 1 + n_o

    def body(*refs):
        par_refs = refs[nd * per_in:nd * per_in + n_p]
        g_refs = refs[nd * per_in + n_p:-nd]
        ds_refs = refs[-nd:]
        first = pl.program_id(1) == 0
        pv = [_load(r, p) for r, (_, p) in zip(par_refs, params)]
        d_par = None
        for d, (fn, toks) in enumerate(dirs):
            ins = refs[d * per_in:(d + 1) * per_in]
            tok_refs, sprev_ref, ct_refs = ins[:n_t], ins[n_t], ins[n_t + 1:]
            ds_ref = ds_refs[d]

            @pl.when(first)
            def _():
                ds_ref[...] = jnp.zeros(ds_ref.shape, F32)

            tv = [_load(r, tk.pieces) for r, tk in zip(tok_refs, toks)]
            _, vjp = jax.vjp(fn, sprev_ref[...], tv, pv)
            ct = tuple(_load(r, pieces) for r, (_, _, pieces) in zip(ct_refs, outs))
            d_s, d_tv, d_pv = vjp((ct, ds_ref[...]))
            ds_ref[...] = d_s
            for r, gv, tk in zip(g_refs[d * n_t:(d + 1) * n_t], d_tv, toks):
                _store(r, gv, tk.pieces)
            d_par = d_pv if d_par is None else jax.tree.map(jnp.add, d_par, d_pv)
        very_first = first & (pl.program_id(0) == 0)
        for r, gv, (_, p) in zip(g_refs[nd * n_t:], d_par, params):
            @pl.when(very_first)
            def _():
                _store(r, gv, p)

            @pl.when(jnp.logical_not(very_first))
            def _():
                _store(r, gv, p, accumulate=True)

    nstate = len(state_shape)
    in_specs, operands, g_specs, g_shapes = [], [], [], []
    for d, (_, toks) in enumerate(dirs):
        chunk = lambda c, d=d: _chunk_of(d, nc - 1 - c, nc, nctx)
        for tk in toks:
            in_specs.append(pl.BlockSpec((None, CHUNK, tk.width), lambda b, c, f=chunk, col=tk.col: (b, f(c), col)))
            operands.append(tk.x)
            g_specs.append(pl.BlockSpec((None, CHUNK, tk.width), lambda b, c, f=chunk: (b, f(c), 0)))
            g_shapes.append(jax.ShapeDtypeStruct((bsz, t, tk.width), F32))
        in_specs.append(pl.BlockSpec((None, None) + state_shape, lambda b, c, f=chunk: (b, f(c)) + (0,) * nstate))
        operands.append(sprevs[d])
        for (w, _, _), ct in zip(outs, cts[d]):
            in_specs.append(pl.BlockSpec((None, CHUNK, w), lambda b, c, f=chunk: (b, f(c), 0)))
            operands.append(ct)
    par_specs = [pl.BlockSpec(x.shape, lambda b, c: (0, 0)) for x, _ in params]
    return pl.pallas_call(
        body, name=name, grid=(bsz, nc),
        in_specs=in_specs + par_specs, out_specs=g_specs + par_specs,
        out_shape=g_shapes + [jax.ShapeDtypeStruct(x.shape, F32) for x, _ in params],
        scratch_shapes=[pltpu.VMEM(state_shape, F32)] * nd, compiler_params=_params(2),
    )(*operands, *[x for x, _ in params])


class Geo:
    def __init__(self, bsz, t, lc):
        self.b, self.t, self.lc, self.m = bsz, t, lc, bsz * t
        self.tm = _pick(lc, (256, 128, 64))
        self.tq = _pick(lc, (128, 64))
        assert t % self.tm == 0 and t % CHUNK == 0 and lc % CHUNK == 0
        self.nctx, self.nctx_q = lc // self.tm, lc // self.tq
        self.grid = (bsz, t // self.tm)


def a_tok(g, x, tm=None, **kw):
    tm = tm or g.tm
    return Arg(x, (None, tm, x.shape[-1]), lambda b, j: (b, j, 0), **kw)


def a_ztok(g, z, name, width, tm=None, **kw):
    tm = tm or g.tm
    col = ZOFF[name] // width
    return Arg(z, (None, tm, width), lambda b, j: (b, j, col),
               gshape=(g.b, g.t, width), gimap=lambda b, j: (b, j, 0), **kw)


def a_par(x, **kw):
    return Arg(x, x.shape, lambda b, j: (0, 0), first=lambda ids: (ids[0] == 0) & (ids[1] == 0), **kw)


def a_mod(g, x):
    n = g.nctx
    return Arg(x, (None, None, x.shape[2], D_MODEL), lambda b, j: (b, jnp.where(j >= n, 1, 0), 0, 0),
               first=lambda ids: (ids[1] == 0) | (ids[1] == n))


def a_tab(g, x):
    return Arg(x, (g.tm, x.shape[-1]), lambda b, j: (j, 0), diff=False)


def o_tok(g, w, dtype, tm=None, pieces=None):
    tm = tm or g.tm
    return Out((g.b, g.t, w), dtype, (None, tm, w), lambda b, j: (b, j, 0), pieces)


def addn(name, xs, dtype):
    bsz, t, w = xs[0].shape
    tm = _pick(t, (256, 192, 128, 64))

    def body(*refs):
        tot = refs[0][...].astype(F32)
        for r in refs[1:-1]:
            tot = tot + r[...].astype(F32)
        refs[-1][...] = tot.astype(dtype)

    spec = pl.BlockSpec((None, tm, w), lambda b, j: (b, j, 0))
    return pl.pallas_call(body, name=name, grid=(bsz, t // tm), in_specs=[spec] * len(xs), out_specs=spec,
                          out_shape=jax.ShapeDtypeStruct((bsz, t, w), dtype), compiler_params=_params(2))(*xs)


P64x8 = _lane_pieces(HEAD64, ATT_HEADS)
P64x4 = _lane_pieces(HEAD64, N_HEAD4)
P64x2 = _lane_pieces(HEAD64, ATT_KV_HEADS)
P128x4 = _lane_pieces(HEAD128, N_HEAD4)
P1024x4 = _lane_pieces(D_MODEL, N_BRANCH)


def _gate_pieces(direction):
    la = [(direction * N_HEAD4 + h, 1) for h in range(N_HEAD4)]
    beta = [(8 + direction * N_HEAD4 + h, 1) for h in range(N_HEAD4)]
    return la, beta


RET_PIECES = [(i, 1) for i in range(2 * N_HEAD4)]


def _prep_io(g, z, sp, tabs):
    args = [a_ztok(g, z, "att_q", 512, gdtype=BF16), a_ztok(g, z, "att_k", 128, gdtype=BF16),
            a_ztok(g, z, "ret_q", 256, gdtype=BF16), a_ztok(g, z, "ret_k", 256, gdtype=BF16),
            a_ztok(g, z, "narrow", 128, gdtype=BF16),
            a_par(sp["qgain"]), a_par(sp["kgain"]), a_par(sp["alog"]), a_par(sp["dtb"]),
            a_tab(g, tabs[0]), a_tab(g, tabs[1])]
    outs = [o_tok(g, 512, BF16), o_tok(g, 128, BF16), o_tok(g, 256, F32), o_tok(g, 256, F32), o_tok(g, 128, F32)]
    return args, outs


def _attn_io(g, q_att, k_att, z, latent):
    col = ZOFF["att_v"] // 128
    first = lambda ids: ids[1] == 0
    if latent:
        rows, off, nq = g.t, g.nctx_q, (g.t - g.lc) // g.tq
    else:
        rows, off, nq = g.lc, 0, g.nctx_q
    args = [Arg(q_att, (None, g.tq, 512), lambda b, j: (b, j + off, 0), pieces=P64x8,
                gshape=(g.b, nq * g.tq, 512), gimap=lambda b, j: (b, j, 0)),
            Arg(k_att, (None, rows, 128), lambda b, j: (b, 0, 0), first=first, pieces=P64x2, gshape=(g.b, rows, 128)),
            Arg(z, (None, rows, 128), lambda b, j: (b, 0, col), first=first, pieces=P64x2,
                gshape=(g.b, rows, 128), gimap=lambda b, j: (b, 0, 0))]
    outs = [Out((g.b, nq * g.tq, 512), BF16, (None, g.tq, 512), lambda b, j: (b, j, 0), P64x8)]
    return (g.b, nq), args, outs


def _pad_rows(x, t):
    return jnp.pad(x, ((0, 0), (0, t - x.shape[1]), (0, 0)))


def _colgrid_arg(g, z, name, width_total, **kw):
    col = ZOFF[name] // 128
    return Arg(z, (None, g.t, 128), lambda h, b: (b, 0, col + h),
               gshape=(g.b, g.t, width_total), gimap=lambda h, b: (b, 0, h), **kw)


def _dnprep_io(g, z, sp):
    wfirst = lambda ids: ids[1] == 0
    args = [_colgrid_arg(g, z, "dn_q", 512, gdtype=BF16), _colgrid_arg(g, z, "dn_k", 512, gdtype=BF16),
            _colgrid_arg(g, z, "dn_v", 512, gdtype=BF16)]
    for i in range(3):
        args.append(Arg(sp["dn_conv"][i], (3, 128), lambda h, b: (0, h), first=wfirst))
    outs = [Out((g.b, g.t, 512), F32, (None, g.t, 128), lambda h, b: (b, 0, h)) for _ in range(3)]
    return (N_HEAD4, g.b), args, outs


def _shortconv_io(g, z, sp):
    args = [_colgrid_arg(g, z, "sc_b", 512, gdtype=BF16), _colgrid_arg(g, z, "sc_c", 512, gdtype=BF16),
            _colgrid_arg(g, z, "sc_x", 512, gdtype=BF16),
            Arg(sp["sc_conv"], (3, 128), lambda h, b: (0, h), first=lambda ids: ids[1] == 0)]
    outs = [Out((g.b, g.t, 512), BF16, (None, g.t, 128), lambda h, b: (b, 0, h))]
    return (BRANCH_W // 128, g.b), args, outs


def _finish_io(g, o_dn_f, o_dn_b, z, o_rt_f, o_rt_b, sp):
    args = [a_tok(g, o_dn_f, pieces=P128x4), a_tok(g, o_dn_b, pieces=P128x4),
            a_ztok(g, z, "dn_z", 512, gdtype=BF16, pieces=P128x4),
            a_tok(g, o_rt_f, pieces=P128x4), a_tok(g, o_rt_b, pieces=P128x4),
            a_ztok(g, z, "ret_g", 512, gdtype=BF16, pieces=P128x4), a_par(sp["ngain"])]
    outs = [o_tok(g, 512, BF16, pieces=P128x4), o_tok(g, 512, BF16, pieces=P128x4)]
    return args, outs


def _merge_io(g, ps, z):
    args = [a_tok(g, p, gdtype=BF16) for p in ps] + [a_ztok(g, z, "gates", 4096, gdtype=BF16, pieces=P1024x4)]
    return args, [o_tok(g, D_MODEL, BF16)]


def _dn_dirs(qn, kn, vn, gates):
    dirs = []
    for d in range(2):
        la, beta = _gate_pieces(d)
        dirs.append((make_dn_chunk(d), [Tok(qn, P128x4), Tok(kn, P128x4), Tok(vn, P128x4), Tok(gates, la), Tok(gates, beta)]))
    return dirs


def _ret_dirs(q_ret, k_ret, z):
    col = ZOFF["ret_v"] // 512
    return [(make_ret_chunk(d), [Tok(q_ret, P64x4), Tok(k_ret, P64x4), Tok(z, P128x4, 512, col)]) for d in range(2)]


DN_STATE = (N_HEAD4, HEAD128, HEAD128)
RET_STATE = (N_HEAD4, HEAD64, HEAD128)
SCAN_OUT = [(512, F32, P128x4)]


def layer_fwd(g, x, h, w, sp, mod_a, mod_b, tabs, g_next):
    tp = {"x": x, "h": h}
    m, lc = g.m, g.lc
    z = matmul("win", h.reshape(m, D_MODEL), w["win"]).reshape(g.b, g.t, NZ)
    tp["z"] = z
    args, outs = _prep_io(g, z, sp, tabs)
    q_att, k_att, q_ret, k_ret, gates = vfwd("prep", f_prep, g.grid, args, outs)
    tp.update(q_att=q_att, k_att=k_att, q_ret=q_ret, k_ret=k_ret, gates=gates)
    y_att = []
    for latent in (False, True):
        agrid, args, outs = _attn_io(g, q_att, k_att, z, latent)
        y_att += vfwd("attn", make_f_attn(g.tq), agrid, args, outs)
    y_att = jnp.concatenate(y_att, axis=1)
    dgrid, args, outs = _dnprep_io(g, z, sp)
    qn, kn, vn = vfwd("dnprep", make_f_dnprep(g.t, lc), dgrid, args, outs)
    tp.update(qn=qn, kn=kn, vn=vn)
    o_f, s_f, o_b, s_b = scan_fwd("dnscan", _dn_dirs(qn, kn, vn, gates), [], DN_STATE, SCAN_OUT, lc)
    o_dn, tp["dn_s"] = [o_f, o_b], [s_f, s_b]
    o_f, s_f, o_b, s_b = scan_fwd("retscan", _ret_dirs(q_ret, k_ret, z), [(sp["ret"], RET_PIECES)], RET_STATE, SCAN_OUT, lc)
    o_rt, tp["rt_s"] = [o_f, o_b], [s_f, s_b]
    tp.update(o_dn=o_dn, o_rt=o_rt)
    args, outs = _finish_io(g, o_dn[0], o_dn[1], z, o_rt[0], o_rt[1], sp)
    y_dn, y_rt = vfwd("finish", f_finish, g.grid, args, outs)
    sgrid, args, outs = _shortconv_io(g, z, sp)
    (y_sc,) = vfwd("shortconv", make_f_shortconv(g.t, lc), sgrid, args, outs)
    ys = [y_att, y_dn, y_rt, y_sc]
    tp["ys"] = ys
    ps = [matmul("wbranch", y.reshape(m, BRANCH_W), w["wbr"][i]).reshape(g.b, g.t, D_MODEL) for i, y in enumerate(ys)]
    tp["ps"] = ps
    args, outs = _merge_io(g, ps, z)
    (u,) = vfwd("merge", f_merge, g.grid, args, outs)
    tp["u"] = u
    y = matmul("wout", u.reshape(m, D_MODEL), w["wout"]).reshape(g.b, g.t, D_MODEL)
    tp["y"] = y
    args = [a_tok(g, x), a_tok(g, y), a_par(sp["g1"]), a_par(sp["g2"]), a_mod(g, mod_a)]
    x1, h2 = vfwd("resnorm", f_resnorm, g.grid, args, [o_tok(g, D_MODEL, F32), o_tok(g, D_MODEL, BF16)])
    tp.update(x1=x1, h2=h2)
    a = matmul("wmlp1", h2.reshape(m, D_MODEL), w["w1"]).reshape(g.b, g.t, MLP_HIDDEN)
    tp["a"] = a
    (r,) = vfwd("act", f_act, g.grid, [a_tok(g, a, gdtype=BF16)], [o_tok(g, MLP_HIDDEN, BF16)])
    tp["r"] = r
    mo = matmul("wmlp2", r.reshape(m, MLP_HIDDEN), w["w2"]).reshape(g.b, g.t, D_MODEL)
    tp["mo"] = mo
    if g_next is None:
        args = [a_tok(g, x1), a_tok(g, mo), a_par(sp["g3"]), a_mod(g, mod_b)]
        (x2,) = vfwd("resid", f_resid, g.grid, args, [o_tok(g, D_MODEL, F32)])
        return x2, None, tp
    args = [a_tok(g, x1), a_tok(g, mo), a_par(sp["g3"]), a_par(g_next), a_mod(g, mod_b)]
    x2, h_next = vfwd("resnorm", f_resnorm, g.grid, args, [o_tok(g, D_MODEL, F32), o_tok(g, D_MODEL, BF16)])
    return x2, h_next, tp


def layer_bwd(g, tp, w, sp, mod_a, mod_b, tabs, g_next, dx2, dh_next):
    m, lc = g.m, g.lc
    gw, gs = {}, {}
    x, z = tp["x"], tp["z"]
    if g_next is None:
        args = [a_tok(g, tp["x1"]), a_tok(g, tp["mo"]), a_par(sp["g3"]), a_mod(g, mod_b)]
        dx1, dmo, gs["g3"], dmod_b = vbwd("resid_b", f_resid, g.grid, args, [o_tok(g, D_MODEL, F32)], [[dx2]])
    else:
        args = [a_tok(g, tp["x1"]), a_tok(g, tp["mo"]), a_par(sp["g3"]), a_par(g_next), a_mod(g, mod_b)]
        dx1, dmo, gs["g3"], gs["g0_next"], dmod_b = vbwd(
            "resnorm_b", f_resnorm, g.grid, args, [o_tok(g, D_MODEL, F32), o_tok(g, D_MODEL, BF16)], [[dx2], [dh_next]])
    dmo2 = dmo.reshape(m, D_MODEL)
    dr = matmul("wmlp2_dx", dmo2, w["w2t"]).reshape(g.b, g.t, MLP_HIDDEN)
    gw["w2"] = matmul("wmlp2_dw", tp["r"].reshape(m, MLP_HIDDEN), dmo2, trans_a=True)
    (da,) = vbwd("act_b", f_act, g.grid, [a_tok(g, tp["a"], gdtype=BF16)], [o_tok(g, MLP_HIDDEN, BF16)], [[dr]])
    da2 = da.reshape(m, MLP_HIDDEN)
    dh2 = matmul("wmlp1_dx", da2, w["w1t"]).reshape(g.b, g.t, D_MODEL)
    gw["w1"] = matmul("wmlp1_dw", tp["h2"].reshape(m, D_MODEL), da2, trans_a=True)
    args = [a_tok(g, x), a_tok(g, tp["y"]), a_par(sp["g1"]), a_par(sp["g2"]), a_mod(g, mod_a)]
    dx, dy, gs["g1"], gs["g2"], dmod_a = vbwd(
        "resnorm_b", f_resnorm, g.grid, args, [o_tok(g, D_MODEL, F32), o_tok(g, D_MODEL, BF16)], [[dx1], [dh2]])
    dy2 = dy.reshape(m, D_MODEL)
    du = matmul("wout_dx", dy2, w["woutt"]).reshape(g.b, g.t, D_MODEL)
    gw["wout"] = matmul("wout_dw", tp["u"].reshape(m, D_MODEL), dy2, trans_a=True)
    args, outs = _merge_io(g, tp["ps"], z)
    *dps, dz_gates = vbwd("merge_b", f_merge, g.grid, args, outs, [[du]])
    dys, gwbr = [], []
    for i in range(N_BRANCH):
        dp2 = dps[i].reshape(m, D_MODEL)
        dys.append(matmul("wbranch_dx", dp2, w["wbrt"][i]).reshape(g.b, g.t, BRANCH_W))
        gwbr.append(matmul("wbranch_dw", tp["ys"][i].reshape(m, BRANCH_W), dp2, trans_a=True))
    gw["wbr"] = jnp.stack(gwbr)
    dy_att, dy_dn, dy_rt, dy_sc = dys
    o_dn, o_rt = tp["o_dn"], tp["o_rt"]
    args, outs = _finish_io(g, o_dn[0], o_dn[1], z, o_rt[0], o_rt[1], sp)
    do_dn_f, do_dn_b, dz_dnz, do_rt_f, do_rt_b, dz_retg, gs["ngain"] = vbwd(
        "finish_b", f_finish, g.grid, args, outs, [[dy_dn], [dy_rt]])
    res = scan_bwd("dnscan_b", _dn_dirs(tp["qn"], tp["kn"], tp["vn"], tp["gates"]), [], DN_STATE, SCAN_OUT, lc,
                   tp["dn_s"], [[do_dn_f], [do_dn_b]])
    dqn, dkn, dvn, dgates = [res[0], res[5]], [res[1], res[6]], [res[2], res[7]], [res[3], res[4], res[8], res[9]]
    res = scan_bwd("retscan_b", _ret_dirs(tp["q_ret"], tp["k_ret"], z), [(sp["ret"], RET_PIECES)], RET_STATE, SCAN_OUT, lc,
                   tp["rt_s"], [[do_rt_f], [do_rt_b]])
    dq_ret, dk_ret, gs["ret"] = [res[0], res[3]], [res[1], res[4]], res[6]
    dz_retv = addn("sum_retv", [res[2], res[5]], BF16)
    sgrid, args, outs = _shortconv_io(g, z, sp)
    dz_scb, dz_scc, dz_scx, gs["sc_conv"] = vbwd("shortconv_b", make_f_shortconv(g.t, lc), sgrid, args, outs, [[dy_sc]])
    dq_att, dk_att, dv_att = [], [], []
    for latent, dy in ((False, dy_att[:, :lc]), (True, dy_att[:, lc:])):
        agrid, args, outs = _attn_io(g, tp["q_att"], tp["k_att"], z, latent)
        dq, dk, dv = vbwd("attn_b", make_f_attn(g.tq), agrid, args, outs, [[dy]])
        dq_att.append(dq); dk_att.append(_pad_rows(dk, g.t)); dv_att.append(_pad_rows(dv, g.t))
    dq_att = jnp.concatenate(dq_att, axis=1)
    dz_attv = addn("sum_attv", dv_att, BF16)
    dgrid, args, outs = _dnprep_io(g, z, sp)
    dz_dnq, dz_dnk, dz_dnv, gc_q, gc_k, gc_v = vbwd("dnprep_b", make_f_dnprep(g.t, lc), dgrid, args, outs, [dqn, dkn, dvn])
    gs["dn_conv"] = [gc_q, gc_k, gc_v]
    args, outs = _prep_io(g, z, sp, tabs)
    dz_attq, dz_attk, dz_retq, dz_retk, dz_nar, gs["qgain"], gs["kgain"], gs["alog"], gs["dtb"] = vbwd(
        "prep_b", f_prep, g.grid, args, outs, [[dq_att], dk_att, dq_ret, dk_ret, dgates])
    pad = jnp.zeros((g.b, g.t, NZ - (ZOFF["narrow"] + 128)), BF16)
    dz = jnp.concatenate([dz_gates, dz_attq, dz_dnq, dz_dnk, dz_dnv, dz_dnz, dz_retv, dz_retg, dz_scb, dz_scc, dz_scx,
                          dz_retq, dz_retk, dz_attk, dz_attv, dz_nar, pad], axis=-1)
    dz2 = dz.reshape(m, NZ)
    dh = matmul("win_dx", dz2, w["wint"]).reshape(g.b, g.t, D_MODEL)
    gw["win"] = matmul("win_dw", tp["h"].reshape(m, D_MODEL), dz2, trans_a=True)
    return dx, dh, gw, gs, dmod_a, dmod_b


def loss_call(g, xa, tgt):
    n = g.nctx
    inv_d = 1.0 / D_MODEL

    def body(x_ref, t_ref, loss_ref, dx_ref):
        j = pl.program_id(1)

        @pl.when((pl.program_id(0) == 0) & (j == 0))
        def _():
            loss_ref[...] = jnp.zeros(loss_ref.shape, F32)

        @pl.when(j < n)
        def _():
            dx_ref[...] = jnp.zeros(dx_ref.shape, F32)

        @pl.when(j >= n)
        def _():
            e = x_ref[...] - t_ref[...]
            dx_ref[...] = e * inv_d
            s = jnp.sum(jnp.sum(e * e, axis=1, keepdims=True), axis=0, keepdims=True)
            loss_ref[...] += jnp.broadcast_to(0.5 * inv_d * s, loss_ref.shape)

    tok = pl.BlockSpec((None, g.tm, D_MODEL), lambda b, j: (b, j, 0))
    return pl.pallas_call(
        body, name="loss", grid=g.grid,
        in_specs=[tok, pl.BlockSpec((None, g.tm, D_MODEL), lambda b, j: (b, jnp.maximum(j - n, 0), 0))],
        out_specs=[pl.BlockSpec((1, LANES), lambda b, j: (0, 0)), tok],
        out_shape=[jax.ShapeDtypeStruct((1, LANES), F32), jax.ShapeDtypeStruct((g.b, g.t, D_MODEL), F32)],
        compiler_params=_params(2),
    )(xa, tgt)


def rope_tables(g):
    seq = g.t - g.lc
    rows = seq // GRID_W
    r, col = jnp.meshgrid(jnp.arange(rows), jnp.arange(GRID_W), indexing="ij")
    quarter = HEAD64 // 4
    inv_freq = ROPE_THETA ** (-jnp.arange(quarter, dtype=F32) / quarter)
    ang = jnp.concatenate([r.reshape(-1, 1).astype(F32) * inv_freq, col.reshape(-1, 1).astype(F32) * inv_freq], axis=-1)
    cos, sin = jnp.cos(ang), jnp.sin(ang)
    cos = jnp.concatenate([jnp.ones((g.lc, HEAD64 // 2), F32), cos], axis=0)
    sin = jnp.concatenate([jnp.zeros((g.lc, HEAD64 // 2), F32), sin], axis=0)
    reps = 512 // (HEAD64 // 2)
    return jnp.tile(cos, (1, reps)), jnp.tile(sin, (1, reps))


def _row128(v):
    v = v.reshape(1, -1).astype(F32)
    return jnp.pad(v, ((0, 0), (0, LANES - v.shape[1])))


def layer_small(small, l):
    gn = small["g_norm"][l]
    return {
        "g0": gn[0:1], "g1": gn[1:2], "g2": gn[2:3], "g3": gn[3:4],
        "qgain": small["att_q_gain"][l][None], "kgain": small["att_k_gain"][l][None],
        "dn_conv": [small["dn_conv"][l][:, i * 512:(i + 1) * 512] for i in range(3)], "alog": _row128(small["dn_a_log"][l]), "dtb": _row128(small["dn_dt_bias"][l]),
        "ngain": small["dn_norm_gain"][l][None], "ret": _row128(small["ret_decay"][l]), "sc_conv": small["sc_conv"][l],
    }


def permute_win(w):
    parts = [w[..., off:off + width] for _, off, width in _SEGS]
    parts.append(jnp.zeros(w.shape[:-1] + (NZ - N_IN,), w.dtype))
    return jnp.concatenate(parts, axis=-1)


def unpermute_win(gw):
    order = sorted(_SEGS, key=lambda s: s[1])
    return jnp.concatenate([gw[..., ZOFF[name]:ZOFF[name] + width] for name, _, width in order], axis=-1)


def layer_weights(w_in, w_branch, w_out, w_mlp_in, w_mlp_out):
    win = permute_win(w_in.astype(BF16))
    wbr, wout, w1, w2 = (t.astype(BF16) for t in (w_branch, w_out, w_mlp_in, w_mlp_out))
    return {"win": win, "wint": win.T, "wbr": wbr, "wbrt": jnp.swapaxes(wbr, 1, 2), "wout": wout, "woutt": wout.T,
            "w1": w1, "w1t": w1.T, "w2": w2, "w2t": w2.T}


def model_step(g, xa, tgt, mod, wl, small):
    depth = len(wl)
    tabs = rope_tables(g)
    sps = [layer_small(small, l) for l in range(depth)]
    mods_a = [mod[l][:, :, 2:5] for l in range(depth)]
    mods_b = [jnp.concatenate([mod[l][:, :, 5:6], mod[l + 1][:, :, 0:2]], axis=2) if l + 1 < depth else mod[l][:, :, 5:6]
              for l in range(depth)]
    mod0 = mod[0][:, :, 0:2]
    args0 = [a_tok(g, xa), a_par(sps[0]["g0"]), a_mod(g, mod0)]
    (h,) = vfwd("modnorm", f_modnorm, g.grid, args0, [o_tok(g, D_MODEL, BF16)])
    x, tapes = xa, []
    for l in range(depth):
        g_next = sps[l + 1]["g0"] if l + 1 < depth else None
        x, h, tp = layer_fwd(g, x, h, wl[l], sps[l], mods_a[l], mods_b[l], tabs, g_next)
        tapes.append(tp)
    loss_row, dx = loss_call(g, x, tgt)
    dh, gws, gss, dmods = None, [None] * depth, [None] * depth, [None] * depth
    for l in reversed(range(depth)):
        g_next = sps[l + 1]["g0"] if l + 1 < depth else None
        dx, dh, gws[l], gss[l], dma, dmb = layer_bwd(g, tapes[l], wl[l], sps[l], mods_a[l], mods_b[l], tabs, g_next, dx, dh)
        dmods[l] = (dma, dmb)
    dxh, g0_first, dmod0 = vbwd("modnorm_b", f_modnorm, g.grid, args0, [o_tok(g, D_MODEL, BF16)], [[dh]])
    dxa = addn("sum_dx", [dx, dxh], F32)
    dmod = []
    for l in range(depth):
        first2 = dmod0 if l == 0 else dmods[l - 1][1][:, :, 1:3]
        dmod.append(jnp.concatenate([first2, dmods[l][0], dmods[l][1][:, :, 0:1]], axis=2))
    dmod = jnp.stack(dmod)
    def rows(key, n):
        return jnp.stack([gs[key][0, :n] for gs in gss])
    g_norm = jnp.stack([jnp.concatenate([g0_first if l == 0 else gss[l - 1]["g0_next"], gss[l]["g1"], gss[l]["g2"], gss[l]["g3"]], axis=0)
                        for l in range(depth)])
    gsmall = {
        "g_norm": g_norm,
        "att_q_gain": jnp.stack([gs["qgain"][0] for gs in gss]), "att_k_gain": jnp.stack([gs["kgain"][0] for gs in gss]),
        "dn_conv": jnp.stack([jnp.concatenate(gs["dn_conv"], axis=1) for gs in gss]),
        "dn_a_log": rows("alog", 8).reshape(depth, 2, N_HEAD4), "dn_dt_bias": rows("dtb", 8).reshape(depth, 2, N_HEAD4),
        "dn_norm_gain": jnp.stack([gs["ngain"][0] for gs in gss]),
        "ret_decay": rows("ret", 8).reshape(depth, 2, N_HEAD4),
        "sc_conv": jnp.stack([gs["sc_conv"] for gs in gss]),
    }
    return loss_row[0, 0], dxa, gws, gsmall, dmod


N_DEV = 8
N_XY = 4
HBM_SPEC = pl.BlockSpec(memory_space=pltpu.HBM)
VMEM_SPEC = pl.BlockSpec(memory_space=pltpu.VMEM)


def _coords():
    return lax.axis_index("x"), lax.axis_index("y"), lax.axis_index("c")


def _flip(coords, k):
    x, y, c = coords
    return (1 - x if k & 4 else x, 1 - y if k & 2 else y, 1 - c if k & 1 else c)


def allgather8(name, v):
    def body(v_ref, out_ref, send_sems, recv_sems, local_sem):
        me3 = _coords()
        me = 4 * me3[0] + 2 * me3[1] + me3[2]
        mine = pltpu.make_async_copy(v_ref, out_ref.at[me], local_sem)
        mine.start()
        sends = []
        for k in range(1, N_DEV):
            cp = pltpu.make_async_remote_copy(src_ref=v_ref, dst_ref=out_ref.at[me], send_sem=send_sems.at[k - 1],
                                              recv_sem=recv_sems.at[k - 1], device_id=_flip(me3, k), device_id_type=MESH)
            cp.start()
            sends.append(cp)
        for k in range(1, N_DEV):
            pltpu.make_async_remote_copy(src_ref=v_ref, dst_ref=out_ref.at[jnp.bitwise_xor(me, k)], send_sem=send_sems.at[k - 1],
                                         recv_sem=recv_sems.at[k - 1], device_id=_flip(me3, k), device_id_type=MESH).wait_recv()
        for cp in sends:
            cp.wait_send()
        mine.wait()

    return pl.pallas_call(
        body, name=name, out_shape=jax.ShapeDtypeStruct((N_DEV,) + v.shape, v.dtype),
        in_specs=[VMEM_SPEC], out_specs=VMEM_SPEC,
        scratch_shapes=[pltpu.SemaphoreType.DMA((N_DEV - 1,)), pltpu.SemaphoreType.DMA((N_DEV - 1,)), pltpu.SemaphoreType.DMA],
        compiler_params=pltpu.CompilerParams(vmem_limit_bytes=VMEM_LIMIT),
    )(v)


def xy_exchange(name, v, gather):
    shape = v.shape if gather else v.shape[1:]

    def body(v_ref, out_ref, send_sems, recv_sems, local_sem):
        me3 = _coords()
        me = 2 * me3[0] + me3[1]
        src = (lambda i: v_ref) if gather else (lambda i: v_ref.at[i])
        mine = pltpu.make_async_copy(src(me), out_ref.at[me], local_sem)
        mine.start()
        sends = []
        for k in range(1, N_XY):
            cp = pltpu.make_async_remote_copy(src_ref=src(jnp.bitwise_xor(me, k)), dst_ref=out_ref.at[me],
                                              send_sem=send_sems.at[k - 1], recv_sem=recv_sems.at[k - 1],
                                              device_id=_flip(me3, 2 * k), device_id_type=MESH)
            cp.start()
            sends.append(cp)
        for k in range(1, N_XY):
            pltpu.make_async_remote_copy(src_ref=src(me), dst_ref=out_ref.at[jnp.bitwise_xor(me, k)],
                                         send_sem=send_sems.at[k - 1], recv_sem=recv_sems.at[k - 1],
                                         device_id=_flip(me3, 2 * k), device_id_type=MESH).wait_recv()
        for cp in sends:
            cp.wait_send()
        mine.wait()

    return pl.pallas_call(
        body, name=name, out_shape=jax.ShapeDtypeStruct((N_XY,) + shape, v.dtype),
        in_specs=[HBM_SPEC], out_specs=HBM_SPEC,
        scratch_shapes=[pltpu.SemaphoreType.DMA((N_XY - 1,)), pltpu.SemaphoreType.DMA((N_XY - 1,)), pltpu.SemaphoreType.DMA],
    )(v)


def swap_c(name, v):
    def body(v_ref, out_ref, send_sem, recv_sem):
        cp = pltpu.make_async_remote_copy(src_ref=v_ref, dst_ref=out_ref, send_sem=send_sem, recv_sem=recv_sem,
                                          device_id=_flip(_coords(), 1), device_id_type=MESH)
        cp.start()
        cp.wait()

    return pl.pallas_call(
        body, name=name, out_shape=jax.ShapeDtypeStruct(v.shape, v.dtype), in_specs=[HBM_SPEC], out_specs=HBM_SPEC,
        scratch_shapes=[pltpu.SemaphoreType.DMA, pltpu.SemaphoreType.DMA],
    )(v)


BLOCK_BYTES = 1 << 20


def _rows_block(rows, cols):
    for tr in (1024, 512, 256, 128, 64, 32, 16, 8):
        if rows % tr == 0 and tr * cols * 4 <= BLOCK_BYTES:
            return tr
    return rows


def sum_slots(name, r):
    s, rows, cols = r.shape
    tr = _rows_block(rows, cols)

    def body(r_ref, o_ref):
        tot = r_ref[0].astype(F32)
        for i in range(1, s):
            tot = tot + r_ref[i].astype(F32)
        o_ref[...] = tot

    return pl.pallas_call(
        body, name=name, grid=(rows // tr,), in_specs=[pl.BlockSpec((s, tr, cols), lambda i: (0, i, 0))],
        out_specs=pl.BlockSpec((tr, cols), lambda i: (i, 0)), out_shape=jax.ShapeDtypeStruct((rows, cols), F32),
        compiler_params=_params(1),
    )(r)


def adamw(name, w, m, v, parts):
    rows, cols = w.shape
    tr = _rows_block(rows, cols)
    n = len(parts)
    c1 = 1.0 - ADAM_B1 ** ADAM_STEP
    c2 = 1.0 - ADAM_B2 ** ADAM_STEP

    def body(*refs):
        w_ref, m_ref, v_ref = refs[:3]
        g_ref, d_ref, nm_ref, nv_ref = refs[3 + n:]
        g = refs[3][...]
        for r in refs[4:3 + n]:
            g = g + r[...]
        nm = ADAM_B1 * m_ref[...] + (1.0 - ADAM_B1) * g
        nv = ADAM_B2 * v_ref[...] + (1.0 - ADAM_B2) * jnp.square(g)
        d_ref[...] = -ADAM_LR * ((nm / c1) / (jnp.sqrt(nv / c2) + ADAM_EPS) + ADAM_WD * w_ref[...])
        g_ref[...], nm_ref[...], nv_ref[...] = g, nm, nv

    spec = pl.BlockSpec((tr, cols), lambda i: (i, 0))
    sds = jax.ShapeDtypeStruct((rows, cols), F32)
    return pl.pallas_call(
        body, name=name, grid=(rows // tr,), in_specs=[spec] * (3 + n), out_specs=[spec] * 4, out_shape=[sds] * 4,
        compiler_params=_params(1),
    )(w, m, v, *parts)


MOD_COLS = 512


def mod_fwd(call, w_mod, b_sh):
    depth, _, cols = w_mod.shape
    nr = call.shape[0]

    def body(c_ref, w_ref, b_ref, o_ref):
        o_ref[...] = _dg(_silu(c_ref[...]), w_ref[...], 1, 0) + b_ref[...]

    return pl.pallas_call(
        body, name="mod_fwd", grid=(depth, cols // MOD_COLS),
        in_specs=[pl.BlockSpec((nr, D_MODEL), lambda l, j: (0, 0)), pl.BlockSpec((None, D_MODEL, MOD_COLS), lambda l, j: (l, 0, j)),
                  pl.BlockSpec((None, 1, MOD_COLS), lambda l, j: (l, 0, j))],
        out_specs=pl.BlockSpec((None, nr, MOD_COLS), lambda l, j: (l, 0, j)),
        out_shape=jax.ShapeDtypeStruct((depth, nr, cols), F32), compiler_params=_params(2),
    )(call, w_mod, b_sh)


def mod_bwd(call, c_ctx, d_lat, d_ctx, w_mod, ctx_row):
    depth, _, cols = w_mod.shape
    nr, ns = call.shape[0], d_ctx.shape[1]

    def body(c_ref, cc_ref, dl_ref, dc_ref, w_ref, gw_ref, gc_ref):
        crow = jnp.sum(dc_ref[...], axis=0, keepdims=True)
        row = lax.broadcasted_iota(jnp.int32, (nr, 1), 0)
        dm = jnp.where(row == ctx_row, crow, dl_ref[...])
        gw_ref[...] = _dg(_silu(c_ref[...]), dm, 0, 0)
        ds = jnp.sum(_dg(jnp.broadcast_to(crow, (8, MOD_COLS)), w_ref[...], 1, 1), axis=0, keepdims=True) * 0.125
        _, vjp = jax.vjp(_silu, cc_ref[...])
        (part,) = vjp(ds)
        first = (pl.program_id(0) == 0) & (pl.program_id(1) == 0)

        @pl.when(first)
        def _():
            gc_ref[...] = part

        @pl.when(jnp.logical_not(first))
        def _():
            gc_ref[...] += part

    return pl.pallas_call(
        body, name="mod_bwd", grid=(depth, cols // MOD_COLS),
        in_specs=[pl.BlockSpec((nr, D_MODEL), lambda l, j: (0, 0)), pl.BlockSpec((1, D_MODEL), lambda l, j: (0, 0)),
                  pl.BlockSpec((None, nr, MOD_COLS), lambda l, j: (l, 0, j)), pl.BlockSpec((None, ns, MOD_COLS), lambda l, j: (l, 0, j)),
                  pl.BlockSpec((None, D_MODEL, MOD_COLS), lambda l, j: (l, 0, j))],
        out_specs=[pl.BlockSpec((None, D_MODEL, MOD_COLS), lambda l, j: (l, 0, j)), pl.BlockSpec((1, D_MODEL), lambda l, j: (0, 0))],
        out_shape=[jax.ShapeDtypeStruct((depth, D_MODEL, cols), F32), jax.ShapeDtypeStruct((1, D_MODEL), F32)],
        compiler_params=_params(2),
    )(call, c_ctx, d_lat, d_ctx, w_mod)


def bmod_grad(dm_all):
    ndev, depth, ns, cols = dm_all.shape

    def body(d_ref, o_ref):
        tot = d_ref[0]
        for i in range(1, ndev):
            tot = tot + d_ref[i]
        o_ref[...] = jnp.sum(tot, axis=0, keepdims=True)

    return pl.pallas_call(
        body, name="bmod_grad", grid=(depth,), in_specs=[pl.BlockSpec((ndev, None, ns, cols), lambda l: (0, l, 0, 0))],
        out_specs=pl.BlockSpec((None, 1, cols), lambda l: (l, 0, 0)), out_shape=jax.ShapeDtypeStruct((depth, 1, cols), F32),
        compiler_params=_params(1),
    )(dm_all)


def small_reduce(gathered, rows_all):
    ndev, rows, lanes = gathered.shape

    def body(g_ref, o_ref):
        tot = g_ref[0, 0:rows_all]
        for i in range(1, ndev):
            tot = tot + g_ref[i, 0:rows_all]
        o_ref[0:rows_all] = tot
        part = g_ref[0, rows_all:rows]
        for i in range(2, ndev, 2):
            part = part + g_ref[i, rows_all:rows]
        o_ref[rows_all:rows] = part

    return pl.pallas_call(body, name="small_reduce", out_shape=jax.ShapeDtypeStruct((rows, lanes), F32),
                          in_specs=[VMEM_SPEC], out_specs=VMEM_SPEC)(gathered)


def pack_rows(arrays, row_multiple=8):
    flat = jnp.concatenate([a.reshape(-1).astype(F32) for a in arrays])
    per = LANES * row_multiple
    padded = -(-flat.shape[0] // per) * per
    return jnp.pad(flat, (0, padded - flat.shape[0])).reshape(-1, LANES)


def unpack_rows(buf, shapes):
    flat, out, off = buf.reshape(-1), [], 0
    for s in shapes:
        n = int(np.prod(s))
        out.append(flat[off:off + n].reshape(s))
        off += n
    return out


def _shards_of(full, kind, depth):
    if kind == "cols":
        l, k, n = full.shape
        return full.reshape(l, k, N_XY, n // N_XY).transpose(2, 0, 1, 3).reshape(N_XY, l * k, n // N_XY)
    if kind == "rows":
        l, k, n = full.shape
        return full.reshape(l, N_XY, k // N_XY, n).transpose(1, 0, 2, 3).reshape(N_XY, l * (k // N_XY), n)
    l, nb, k, n = full.shape
    return full.reshape(l, nb, k, N_XY, n // N_XY).transpose(3, 0, 1, 2, 4).reshape(N_XY, l * nb * k, n // N_XY)


def _full_of(slots, kind, shard_shape):
    if kind == "cols":
        l, k, n = shard_shape
        return slots.reshape(N_XY, l, k, n).transpose(1, 2, 0, 3).reshape(l, k, N_XY * n)
    if kind == "rows":
        l, k, n = shard_shape
        return slots.reshape(N_XY, l, k, n).transpose(1, 0, 2, 3).reshape(l, N_XY * k, n)
    l, nb, k, n = shard_shape
    return slots.reshape(N_XY, l, nb, k, n).transpose(1, 2, 3, 0, 4).reshape(l, nb, k, N_XY * n)


BIG = (("w_in", "cols"), ("w_branch", "branch"), ("w_out", "rows"), ("w_mlp_in", "cols"), ("w_mlp_out", "rows"))
SMALL_SHARDED = ("g_norm", "dn_conv", "sc_conv")
SMALL_ORDER = ("g_norm", "att_q_gain", "att_k_gain", "dn_conv", "dn_a_log", "dn_dt_bias", "dn_norm_gain", "ret_decay", "sc_conv")


def kernel(x, c, ctx, c_ctx, w_mod, b_mod, g_norm, w_in, att_q_gain, att_k_gain, dn_conv, dn_a_log, dn_dt_bias, dn_norm_gain, ret_decay, sc_conv, w_branch, w_out, w_mlp_in, w_mlp_out, loss_target, m_c_ctx, m_w_mod, m_b_mod, m_g_norm, m_w_in, m_att_q_gain, m_att_k_gain, m_dn_conv, m_dn_a_log, m_dn_dt_bias, m_dn_norm_gain, m_ret_decay, m_sc_conv, m_w_branch, m_w_out, m_w_mlp_in, m_w_mlp_out, v_c_ctx, v_w_mod, v_b_mod, v_g_norm, v_w_in, v_att_q_gain, v_att_k_gain, v_dn_conv, v_dn_a_log, v_dn_dt_bias, v_dn_norm_gain, v_ret_decay, v_sc_conv, v_w_branch, v_w_out, v_w_mlp_in, v_w_mlp_out):
    wts = dict(c_ctx=c_ctx, w_mod=w_mod, b_mod=b_mod, g_norm=g_norm, w_in=w_in, att_q_gain=att_q_gain, att_k_gain=att_k_gain,
               dn_conv=dn_conv, dn_a_log=dn_a_log, dn_dt_bias=dn_dt_bias, dn_norm_gain=dn_norm_gain, ret_decay=ret_decay,
               sc_conv=sc_conv, w_branch=w_branch, w_out=w_out, w_mlp_in=w_mlp_in, w_mlp_out=w_mlp_out)
    mom = dict(c_ctx=m_c_ctx, w_mod=m_w_mod, b_mod=m_b_mod, g_norm=m_g_norm, w_in=m_w_in, att_q_gain=m_att_q_gain,
               att_k_gain=m_att_k_gain, dn_conv=m_dn_conv, dn_a_log=m_dn_a_log, dn_dt_bias=m_dn_dt_bias,
               dn_norm_gain=m_dn_norm_gain, ret_decay=m_ret_decay, sc_conv=m_sc_conv, w_branch=m_w_branch, w_out=m_w_out,
               w_mlp_in=m_w_mlp_in, w_mlp_out=m_w_mlp_out)
    var = dict(c_ctx=v_c_ctx, w_mod=v_w_mod, b_mod=v_b_mod, g_norm=v_g_norm, w_in=v_w_in, att_q_gain=v_att_q_gain,
               att_k_gain=v_att_k_gain, dn_conv=v_dn_conv, dn_a_log=v_dn_a_log, dn_dt_bias=v_dn_dt_bias,
               dn_norm_gain=v_dn_norm_gain, ret_decay=v_ret_decay, sc_conv=v_sc_conv, w_branch=v_w_branch, w_out=v_w_out,
               w_mlp_in=v_w_mlp_in, w_mlp_out=v_w_mlp_out)
    names = list(wts)
    depth, bsz, seq, lc = w_mod.shape[0], x.shape[0], x.shape[1], ctx.shape[1]
    g = Geo(bsz, lc + seq, lc)
    xi, yi, ci = _coords()
    dev, xy = 4 * xi + 2 * yi + ci, 2 * xi + yi
    n_batch = N_DEV * bsz
    nr = -(-(n_batch + 1) // 16) * 16
    mod_cols = w_mod.shape[2]

    c_all = allgather8("gather_c", c).reshape(n_batch, D_MODEL)
    call = jnp.concatenate([c_all, c_ctx[None], jnp.zeros((nr - n_batch - 1, D_MODEL), F32)], axis=0)
    b_sh = lax.dynamic_slice_in_dim(b_mod, xy * mod_cols, mod_cols, axis=1)[:, None, :]
    mod_sh = mod_fwd(call, w_mod, b_sh)
    mod_g = allgather8("gather_mod", mod_sh.reshape(depth * nr, mod_cols)).reshape(N_XY, 2, depth, nr, mod_cols)[:, 0]
    mod_all = mod_g.transpose(1, 2, 0, 3).reshape(depth, nr, N_XY * mod_cols)
    mod_lat = lax.dynamic_slice_in_dim(mod_all, dev * bsz, bsz, axis=1).reshape(depth, bsz, 6, D_MODEL)
    mod_ctx = jnp.broadcast_to(mod_all[:, n_batch].reshape(depth, 1, 6, D_MODEL), (depth, bsz, 6, D_MODEL))
    mod = jnp.stack([mod_ctx, mod_lat], axis=2)

    sm_shapes = [wts[k].shape for k in SMALL_SHARDED]
    sm_g = allgather8("gather_small", pack_rows([wts[k] for k in SMALL_SHARDED])).reshape(N_XY, 2, -1)[:, 0]
    small = {k: wts[k] for k in SMALL_ORDER}
    for k, parts in zip(SMALL_SHARDED, zip(*[unpack_rows(sm_g[p], sm_shapes) for p in range(N_XY)])):
        small[k] = jnp.concatenate(parts, axis=-1)

    full = {}
    for k, kind in BIG:
        sh = wts[k].astype(BF16)
        slots = xy_exchange("gather_" + k, sh.reshape(-1, sh.shape[-1]), True)
        full[k] = _full_of(slots, kind, sh.shape)
    wl = [layer_weights(full["w_in"][l], full["w_branch"][l], full["w_out"][l], full["w_mlp_in"][l], full["w_mlp_out"][l])
          for l in range(depth)]

    xa = jnp.concatenate([ctx, x], axis=1)
    loss_part, dxa, gws, gsmall, dmod = model_step(g, xa, loss_target, mod, wl, small)
    loss = lax.psum(loss_part, ("x", "y", "c"))
    grad_x = dxa[:, lc:]

    grads, deltas, new_m, new_v = {}, {}, {}, {}

    def update(k, parts, shape2d):
        res = adamw("adamw_" + k, wts[k].reshape(shape2d), mom[k].reshape(shape2d), var[k].reshape(shape2d), parts)
        grads[k], deltas[k], new_m[k], new_v[k] = (r.reshape(wts[k].shape) for r in res)

    gfull = {"w_in": unpermute_win(jnp.stack([gw["win"] for gw in gws])), "w_branch": jnp.stack([gw["wbr"] for gw in gws]),
             "w_out": jnp.stack([gw["wout"] for gw in gws]), "w_mlp_in": jnp.stack([gw["w1"] for gw in gws]),
             "w_mlp_out": jnp.stack([gw["w2"] for gw in gws])}
    for k, kind in BIG:
        recv = xy_exchange("scatter_" + k, _shards_of(gfull[k], kind, depth).astype(BF16), False)
        part = sum_slots("sum_" + k, recv)
        other = swap_c("swap_" + k, part)
        update(k, [part, other], part.shape)

    dm_mine = jnp.concatenate([dmod[:, :, 1], dmod[:, :, 0]], axis=1).reshape(depth * 2 * bsz, 6 * D_MODEL)
    dm_all = allgather8("gather_dmod", dm_mine).reshape(N_DEV, depth, 2 * bsz, 6 * D_MODEL)
    gb = bmod_grad(dm_all).reshape(depth, 6 * D_MODEL)
    dm_cols = lax.dynamic_slice_in_dim(dm_all, xy * mod_cols, mod_cols, axis=3)
    d_lat = dm_cols[:, :, :bsz].transpose(1, 0, 2, 3).reshape(depth, n_batch, mod_cols)
    d_lat = jnp.pad(d_lat, ((0, 0), (0, nr - n_batch), (0, 0)))
    d_ctx = dm_cols[:, :, bsz:].transpose(1, 0, 2, 3).reshape(depth, n_batch, mod_cols)
    gw_mod, gc_part = mod_bwd(call, c_ctx[None], d_lat, d_ctx, w_mod, n_batch)
    update("w_mod", [gw_mod.reshape(depth * D_MODEL, mod_cols)], (depth * D_MODEL, mod_cols))
    update("b_mod", [gb], b_mod.shape)

    pack_all = pack_rows([gsmall[k] for k in SMALL_ORDER])
    pack_xy = pack_rows([gc_part])
    rows_all = pack_all.shape[0]
    tot = small_reduce(allgather8("gather_gsmall", jnp.concatenate([pack_all, pack_xy], axis=0)), rows_all)
    gtot = dict(zip(SMALL_ORDER, unpack_rows(tot[:rows_all], [gsmall[k].shape for k in SMALL_ORDER])))
    gtot["c_ctx"] = unpack_rows(tot[rows_all:], [c_ctx.shape])[0]
    for k in SMALL_SHARDED:
        width = wts[k].shape[-1]
        gtot[k] = lax.dynamic_slice_in_dim(gtot[k], xy * width, width, axis=gtot[k].ndim - 1)
    sm_names = ("c_ctx",) + SMALL_ORDER
    sm_shapes = [wts[k].shape for k in sm_names]
    res = adamw("adamw_small", pack_rows([wts[k] for k in sm_names]), pack_rows([mom[k] for k in sm_names]),
                pack_rows([var[k] for k in sm_names]), [pack_rows([gtot[k] for k in sm_names])])
    for dst, buf in zip((grads, deltas, new_m, new_v), res):
        dst.update(zip(sm_names, unpack_rows(buf, sm_shapes)))

    return (loss, grad_x, *[grads[k] for k in names], *[deltas[k] for k in names], *[new_m[k] for k in names],
            *[new_v[k] for k in names])
```

```python
import functools
import math

import numpy as np
import jax
import jax.numpy as jnp
from jax import lax
from jax.experimental import pallas as pl
from jax.experimental.pallas import tpu as pltpu

F32, BF16 = jnp.float32, jnp.bfloat16
HIGHEST = lax.Precision.HIGHEST
MESH = pl.DeviceIdType.MESH

D_MODEL = 1024
GRID_W = 64
N_BRANCH = 4
BRANCH_W = 512
HEAD64 = 64
HEAD128 = 128
N_HEAD4 = 4
ATT_HEADS = 8
ATT_KV_HEADS = 2
CHUNK = 64
MLP_HIDDEN = 4 * D_MODEL
ROPE_THETA = 10000.0
EPS = 1e-6
N_IN = 10000
ADAM_LR, ADAM_B1, ADAM_B2, ADAM_EPS, ADAM_WD, ADAM_STEP = 0.001, 0.9, 0.999, 1e-08, 0.01, 10

LANES = 128
VMEM_LIMIT = 56 * 1024 * 1024

_SEGS = (
    ("gates", 5904, 4096),
    ("att_q", 0, 512), ("dn_q", 768, 512), ("dn_k", 1280, 512), ("dn_v", 1792, 512), ("dn_z", 2304, 512),
    ("ret_v", 3344, 512), ("ret_g", 3856, 512), ("sc_b", 4368, 512), ("sc_c", 4880, 512), ("sc_x", 5392, 512),
    ("ret_q", 2832, 256), ("ret_k", 3088, 256),
    ("att_k", 512, 128), ("att_v", 640, 128),
    ("narrow", 2816, 16),
)
NZ = 10240


def _seg_offsets():
    off, out = 0, {}
    for name, _, width in _SEGS:
        out[name] = off
        off += width
    return out


ZOFF = _seg_offsets()


def _pick(n, cands):
    for c in cands:
        if n % c == 0:
            return c
    return n


def _dg(a, b, ca, cb, batch=False):
    dn = (((ca,), (cb,)), ((0,), (0,))) if batch else (((ca,), (cb,)), ((), ()))
    return lax.dot_general(a.astype(BF16), b.astype(BF16), dn, preferred_element_type=F32)


@functools.partial(jax.custom_vjp, nondiff_argnums=(2, 3))
def _mm(a, b, ca, cb):
    return _dg(a, b, ca, cb)


def _mm_fwd(a, b, ca, cb):
    return _dg(a, b, ca, cb), (a, b)


def _mm_bwd(ca, cb, res, g):
    a, b = res
    if ca == 1:
        da = _mm(g, b, 1, 1) if cb == 0 else _mm(g, b, 1, 0)
    else:
        da = _mm(b, g, 1, 1) if cb == 0 else _mm(b, g, 0, 1)
    if cb == 0:
        db = _mm(a, g, 0, 0) if ca == 1 else _mm(a, g, 1, 0)
    else:
        db = _mm(g, a, 0, 0) if ca == 1 else _mm(g, a, 0, 1)
    return da.astype(a.dtype), db.astype(b.dtype)


_mm.defvjp(_mm_fwd, _mm_bwd)


@functools.partial(jax.custom_vjp, nondiff_argnums=(2, 3))
def _bmm(a, b, ca, cb):
    return _dg(a, b, ca, cb, True)


def _bmm_fwd(a, b, ca, cb):
    return _dg(a, b, ca, cb, True), (a, b)


def _bmm_bwd(ca, cb, res, g):
    a, b = res
    if ca == 2:
        da = _bmm(g, b, 2, 2) if cb == 1 else _bmm(g, b, 2, 1)
    else:
        da = _bmm(b, g, 2, 2) if cb == 1 else _bmm(b, g, 1, 2)
    if cb == 1:
        db = _bmm(a, g, 1, 1) if ca == 2 else _bmm(a, g, 2, 1)
    else:
        db = _bmm(g, a, 1, 1) if ca == 2 else _bmm(g, a, 1, 2)
    return da.astype(a.dtype), db.astype(b.dtype)


_bmm.defvjp(_bmm_fwd, _bmm_bwd)


def _split_bf16(x):
    hi = x.astype(BF16)
    lo = (x - hi.astype(F32)).astype(BF16)
    return hi, lo


def _mm3(a, b, ca, cb):
    ah, al = _split_bf16(a)
    bh, bl = _split_bf16(b)
    dn = (((ca,), (cb,)), ((), ()))
    d = lambda u, v: lax.dot_general(u, v, dn, preferred_element_type=F32)
    return d(ah, bh) + (d(ah, bl) + d(al, bh))


def _mm_exact(a, b):
    return jnp.dot(a, b, precision=HIGHEST, preferred_element_type=F32)


class Arg:
    def __init__(self, x, block, imap, diff=True, first=None, gdtype=F32, pieces=None, gshape=None, gimap=None):
        self.x, self.block, self.imap = x, tuple(block), imap
        self.diff, self.first, self.gdtype, self.pieces = diff, first, gdtype, pieces
        self.gshape = tuple(x.shape) if gshape is None else tuple(gshape)
        self.gimap = imap if gimap is None else gimap

    def spec(self):
        return pl.BlockSpec(self.block, self.imap)

    def gspec(self):
        return pl.BlockSpec(self.block, self.gimap)


class Out:
    def __init__(self, shape, dtype, block, imap, pieces=None):
        self.shape, self.dtype, self.block, self.imap, self.pieces = tuple(shape), dtype, tuple(block), imap, pieces

    def spec(self):
        return pl.BlockSpec(self.block, self.imap)

    def sds(self):
        return jax.ShapeDtypeStruct(self.shape, self.dtype)


def _lanes(ref, s, w):
    return (slice(None),) * (len(ref.shape) - 1) + (slice(s, s + w),)


def _load(ref, pieces):
    if pieces is None:
        return ref[...]
    return tuple(ref[_lanes(ref, s, w)] for s, w in pieces)


def _store(ref, val, pieces, accumulate=False):
    if pieces is None:
        if accumulate:
            ref[...] += val.astype(ref.dtype)
        else:
            ref[...] = val.astype(ref.dtype)
        return
    if not accumulate:
        covered = sum(w for _, w in pieces)
        if covered != ref.shape[-1]:
            ref[...] = jnp.zeros(ref.shape, ref.dtype)
    for (s, w), v in zip(pieces, val):
        if accumulate:
            ref[_lanes(ref, s, w)] += v.astype(ref.dtype)
        else:
            ref[_lanes(ref, s, w)] = v.astype(ref.dtype)


def _params(n_grid):
    return pltpu.CompilerParams(dimension_semantics=("arbitrary",) * n_grid, vmem_limit_bytes=VMEM_LIMIT)


def vfwd(name, f, grid, args, outs):
    n_in = len(args)

    def body(*refs):
        ids = tuple(pl.program_id(i) for i in range(len(grid)))
        vals = [_load(r, a.pieces) for r, a in zip(refs[:n_in], args)]
        res = f(ids, *vals)
        for r, o, spec in zip(refs[n_in:], res, outs):
            _store(r, o, spec.pieces)

    return pl.pallas_call(
        body, name=name, grid=grid,
        in_specs=[a.spec() for a in args], out_specs=[o.spec() for o in outs],
        out_shape=[o.sds() for o in outs], compiler_params=_params(len(grid)),
    )(*[a.x for a in args])


def vbwd(name, f, grid, args, outs, cts):
    n_in = len(args)
    diff_idx = [i for i, a in enumerate(args) if a.diff]
    ct_flat = [c for per_out in cts for c in per_out]
    ct_specs = [o.spec() for o, per_out in zip(outs, cts) for _ in per_out]
    n_ct = len(ct_flat)

    def body(*refs):
        ids = tuple(pl.program_id(i) for i in range(len(grid)))
        vals = [_load(r, a.pieces) for r, a in zip(refs[:n_in], args)]
        ct_refs = refs[n_in:n_in + n_ct]
        g_refs = refs[n_in + n_ct:]

        def g(*dvals):
            full = list(vals)
            for i, v in zip(diff_idx, dvals):
                full[i] = v
            return tuple(f(ids, *full))

        _, vjp = jax.vjp(g, *[vals[i] for i in diff_idx])
        ct_vals, k = [], 0
        for o, per_out in zip(outs, cts):
            tot = None
            for _ in per_out:
                v = _load(ct_refs[k], o.pieces)
                k += 1
                v = jax.tree.map(lambda t: t.astype(F32), v)
                tot = v if tot is None else jax.tree.map(jnp.add, tot, v)
            ct_vals.append(tot)
        grads = vjp(tuple(ct_vals))
        for gr, gv, i in zip(g_refs, grads, diff_idx):
            a = args[i]
            if a.first is None:
                _store(gr, gv, a.pieces)
            else:
                is_first = a.first(ids)

                @pl.when(is_first)
                def _():
                    _store(gr, gv, a.pieces)

                @pl.when(jnp.logical_not(is_first))
                def _():
                    _store(gr, gv, a.pieces, accumulate=True)

    g_specs = [args[i].gspec() for i in diff_idx]
    g_shapes = [jax.ShapeDtypeStruct(args[i].gshape, args[i].gdtype) for i in diff_idx]
    return pl.pallas_call(
        body, name=name, grid=grid,
        in_specs=[a.spec() for a in args] + ct_specs, out_specs=g_specs, out_shape=g_shapes,
        compiler_params=_params(len(grid)),
    )(*[a.x for a in args], *ct_flat)


def matmul(name, a, b, trans_a=False, out_dtype=F32):
    if trans_a:
        kdim, m = a.shape
    else:
        m, kdim = a.shape
    n = b.shape[1]
    assert b.shape[0] == kdim
    assert out_dtype == F32
    tm = _pick(m, (1024, 512, 256, 192, 128, 64))
    tn = _pick(n, (1024, 512, 256, 128))
    tk = _pick(kdim, (1024, 512, 256, 192, 128, 64))
    nk = kdim // tk

    def body(a_ref, b_ref, o_ref):
        part = _dg(a_ref[...], b_ref[...], 0 if trans_a else 1, 0)
        if nk == 1:
            o_ref[...] = part
        else:
            k = pl.program_id(2)

            @pl.when(k == 0)
            def _():
                o_ref[...] = part

            @pl.when(k > 0)
            def _():
                o_ref[...] += part

    a_spec = pl.BlockSpec((tk, tm), lambda i, j, k: (k, i)) if trans_a else pl.BlockSpec((tm, tk), lambda i, j, k: (i, k))
    return pl.pallas_call(
        body, name=name, grid=(m // tm, n // tn, nk),
        in_specs=[a_spec, pl.BlockSpec((tk, tn), lambda i, j, k: (k, j))],
        out_specs=pl.BlockSpec((tm, tn), lambda i, j, k: (i, j)),
        out_shape=jax.ShapeDtypeStruct((m, n), out_dtype),
        compiler_params=pltpu.CompilerParams(dimension_semantics=("parallel", "parallel", "arbitrary"),
                                             vmem_limit_bytes=VMEM_LIMIT),
    )(a, b)


def _rms(x, gain):
    return x * lax.rsqrt(jnp.mean(x * x, axis=-1, keepdims=True) + EPS) * gain


def _silu(x):
    return x * jax.nn.sigmoid(x)


def f_modnorm(ids, x, gain, mod):
    return (_rms(x, gain) * (1.0 + mod[1:2]) + mod[0:1],)


def f_resnorm(ids, x, y, g_res, g_next, mod):
    x_new = x + mod[0:1] * _rms(y, g_res)
    return x_new, _rms(x_new, g_next) * (1.0 + mod[2:3]) + mod[1:2]


def f_resid(ids, x, y, g_res, mod):
    return (x + mod[0:1] * _rms(y, g_res),)


def f_act(ids, a):
    r = jnp.maximum(a, 0.0)
    return (r * r,)


def _head_consts(width, head):
    i = lax.broadcasted_iota(jnp.int32, (width, width), 0)
    j = lax.broadcasted_iota(jnp.int32, (width, width), 1)
    shift = int(math.log2(head))
    same = (i >> shift) == (j >> shift)
    group = jnp.where(same, 1.0 / head, 0.0).astype(F32)
    half = head // 2
    ii, jj = i & (head - 1), j & (head - 1)
    rot = jnp.where(same & (ii == jj + half) & (jj < half), -1.0, 0.0) + jnp.where(same & (ii + half == jj) & (jj >= half), 1.0, 0.0)
    ti = lax.broadcasted_iota(jnp.int32, (head, width), 0)
    tj = lax.broadcasted_iota(jnp.int32, (head, width), 1)
    tile = jnp.where(ti == (tj & (head - 1)), 1.0, 0.0).astype(F32)
    return group, rot.astype(F32), tile


def _rope(x, cos, sin, rot):
    return x * cos + _mm_exact(x, rot) * sin


def _softplus(x):
    return jnp.maximum(x, 0.0) + jnp.log(1.0 + jnp.exp(-jnp.abs(x)))


def f_prep(ids, zq, zk, rq, rk, zn, qgain, kgain, alog, dtb, cos, sin):
    grp_q, rot_q, tile_q = _head_consts(ATT_HEADS * HEAD64, HEAD64)
    grp_k, rot_k, tile_k = _head_consts(ATT_KV_HEADS * HEAD64, HEAD64)
    grp_r, rot_r, _ = _head_consts(N_HEAD4 * HEAD64, HEAD64)
    wq, wk, wr = zq.shape[-1], zk.shape[-1], rq.shape[-1]
    qn = zq * lax.rsqrt(_mm_exact(zq * zq, grp_q) + EPS) * _mm_exact(qgain, tile_q)
    kn = zk * lax.rsqrt(_mm_exact(zk * zk, grp_k) + EPS) * _mm_exact(kgain, tile_k)
    q_att = _rope(qn, cos[:, :wq], sin[:, :wq], rot_q) * (HEAD64 ** -0.5)
    k_att = _rope(kn, cos[:, :wk], sin[:, :wk], rot_k)
    q_ret = _rope(rq, cos[:, :wr], sin[:, :wr], rot_r)
    k_ret = _rope(rk * (HEAD64 ** -0.5), cos[:, :wr], sin[:, :wr], rot_r)
    lane = lax.broadcasted_iota(jnp.int32, zn.shape, 1)
    log_a = -jnp.exp(alog) * _softplus(zn + dtb)
    gates = jnp.where(lane < 8, log_a, jnp.where(lane < 16, jax.nn.sigmoid(zn), 0.0))
    return q_att, k_att, q_ret, k_ret, gates


def make_f_attn(tq):
    def f_attn(ids, q, k, v):
        outs = []
        per = ATT_HEADS // ATT_KV_HEADS
        for g in range(ATT_KV_HEADS):
            qg = jnp.concatenate(q[g * per:(g + 1) * per], axis=0)
            s = _mm(qg, k[g], 1, 1)
            e = jnp.exp(s - lax.stop_gradient(jnp.max(s, axis=-1, keepdims=True)))
            o = _mm(e, v[g], 1, 0) * (1.0 / jnp.sum(e, axis=-1, keepdims=True))
            outs += [o[i * tq:(i + 1) * tq] for i in range(per)]
        return (tuple(outs),)
    return f_attn


def _roll_rows(x, shift):
    return pltpu.roll(x, shift, 0)


def make_shifts(t, lc):
    def _down(x):
        row = lax.broadcasted_iota(jnp.int32, x.shape, 0)
        return jnp.where((row == 0) | (row == lc), 0.0, _roll_rows(x, 1))

    def _up(x):
        row = lax.broadcasted_iota(jnp.int32, x.shape, 0)
        return jnp.where((row == lc - 1) | (row == t - 1), 0.0, _roll_rows(x, t - 1))

    @jax.custom_vjp
    def down(x):
        return _down(x)

    @jax.custom_vjp
    def up(x):
        return _up(x)

    down.defvjp(lambda x: (_down(x), None), lambda _, g: (up(g),))
    up.defvjp(lambda x: (_up(x), None), lambda _, g: (down(g),))
    return down, up


def make_conv3(t, lc):
    down, up = make_shifts(t, lc)

    def conv3(x, w):
        return w[0:1] * down(x) + w[1:2] * x + w[2:3] * up(x)
    return conv3


def make_f_dnprep(t, lc):
    conv3 = make_conv3(t, lc)

    def l2n(x):
        return x * lax.rsqrt(jnp.sum(x * x, axis=-1, keepdims=True) + EPS)

    def f_dnprep(ids, q, k, v, wq, wk, wv):
        qn = l2n(_silu(conv3(q, wq))) * (HEAD128 ** -0.5)
        kn = l2n(_silu(conv3(k, wk)))
        return qn, kn, _silu(conv3(v, wv))
    return f_dnprep


def make_f_shortconv(t, lc):
    conv3 = make_conv3(t, lc)

    def f_shortconv(ids, b, c, x, w):
        return (b * conv3(c * x, w),)
    return f_shortconv


def f_finish(ids, o_dn_f, o_dn_b, z_dn, o_rt_f, o_rt_b, g_rt, ngain):
    y_dn, y_rt = [], []
    for h in range(N_HEAD4):
        o = o_dn_f[h] + o_dn_b[h]
        y_dn.append(_rms(o, ngain) * _silu(z_dn[h]))
        r = o_rt_f[h] + o_rt_b[h]
        mu = jnp.mean(r, axis=-1, keepdims=True)
        var = jnp.mean(jnp.square(r - mu), axis=-1, keepdims=True)
        y_rt.append((r - mu) * lax.rsqrt(var + EPS) * _silu(g_rt[h]))
    return tuple(y_dn), tuple(y_rt)


def f_merge(ids, p0, p1, p2, p3, gates):
    u = jax.nn.sigmoid(gates[0]) * p0
    for g, p in zip(gates[1:], (p1, p2, p3)):
        u = u + jax.nn.sigmoid(g) * p
    return (u,)


def _stack_masks():
    n = N_HEAD4 * CHUNK
    i = lax.broadcasted_iota(jnp.int32, (n, n), 0)
    j = lax.broadcasted_iota(jnp.int32, (n, n), 1)
    same = (i >> 6) == (j >> 6)
    pi, pj = i & (CHUNK - 1), j & (CHUNK - 1)
    return same, pi, pj


def _inv_unit_lower(low):
    n = low.shape[0]
    eye = jnp.where(lax.broadcasted_iota(jnp.int32, (n, n), 0) == lax.broadcasted_iota(jnp.int32, (n, n), 1), 1.0, 0.0).astype(F32)
    m = -low
    p = eye + m
    for _ in range(int(math.log2(CHUNK)) - 1):
        m = _mm3(m, m, 1, 0)
        p = p + _mm3(p, m, 1, 0)
    return p


@jax.custom_vjp
def _tri_solve(low, rhs):
    return _mm3(_inv_unit_lower(low), rhs, 1, 0)


def _tri_solve_fwd(low, rhs):
    inv = _inv_unit_lower(low)
    x = _mm3(inv, rhs, 1, 0)
    return x, (inv, x)


def _tri_solve_bwd(res, g):
    inv, x = res
    d_rhs = _mm3(inv, g, 0, 0)
    return -_mm3(d_rhs, x, 1, 1), d_rhs


_tri_solve.defvjp(_tri_solve_fwd, _tri_solve_bwd)


def _heads3(x):
    return x.reshape(N_HEAD4, CHUNK, x.shape[-1])


def _time_masks(direction):
    same, pi, pj = _stack_masks()
    if direction == 0:
        return same, same & (pi >= pj), same & (pi > pj), same & (pj == CHUNK - 1), pi - pj
    return same, same & (pi <= pj), same & (pi < pj), same & (pj == 0), pj - pi


def make_dn_chunk(direction):
    def dn_chunk(s, toks, params):
        q, k, v, la, beta = toks
        qs, ks, vs = (jnp.concatenate(t, axis=0) for t in (q, k, v))
        la, beta = jnp.concatenate(la, axis=0), jnp.concatenate(beta, axis=0)
        n = N_HEAD4 * CHUNK
        same, incl, strict, last, _ = _time_masks(direction)
        g = jnp.sum(jnp.where(incl, jnp.broadcast_to(la, (n, n)).T, 0.0), axis=1, keepdims=True)
        gb = jnp.broadcast_to(g, (n, n))
        gbt = gb.T
        dec_incl = jnp.where(incl, jnp.exp(jnp.where(incl, gb - gbt, 0.0)), 0.0)
        dec_strict = jnp.where(strict, dec_incl, 0.0)
        low = beta * _mm(ks, ks, 1, 1) * dec_strict
        eg = jnp.exp(g)
        sol = _tri_solve(low, jnp.concatenate([beta * vs, (beta * eg) * ks], axis=1))
        w_v, w_k = sol[:, :HEAD128], sol[:, HEAD128:]
        a_qk = _mm(qs, ks, 1, 1) * dec_incl
        g_last = jnp.sum(jnp.where(last, gbt, 0.0), axis=1, keepdims=True)
        k_dec = ks * jnp.exp(g_last - g)
        u = w_v - _bmm(_heads3(w_k), s, 2, 1).reshape(n, HEAD128)
        o = _bmm(_heads3(qs * eg), s, 2, 1).reshape(n, HEAD128) + _mm(a_qk, u, 1, 0)
        decay = jnp.exp(jnp.mean(_heads3(g_last), axis=1, keepdims=True))
        s_new = s * decay + _bmm(_heads3(k_dec), _heads3(u), 1, 1)
        return (tuple(o[h * CHUNK:(h + 1) * CHUNK] for h in range(N_HEAD4)),), s_new
    return dn_chunk


def make_ret_chunk(direction):
    def ret_chunk(s, toks, params):
        q, k, v = toks
        (decay_log,) = params
        qs, ks, vs = (jnp.concatenate(t, axis=0) for t in (q, k, v))
        n = N_HEAD4 * CHUNK
        _, incl, _, _, rel = _time_masks(direction)
        row_head = lax.broadcasted_iota(jnp.int32, (n, 1), 0) >> 6
        lg = jnp.zeros((n, 1), F32)
        for h in range(N_HEAD4):
            lg = jnp.where(row_head == h, -jnp.exp(decay_log[direction * N_HEAD4 + h]), lg)
        row = (lax.broadcasted_iota(jnp.int32, (n, 1), 0) & (CHUNK - 1)).astype(F32)
        pos = row if direction == 0 else CHUNK - 1.0 - row
        dmask = jnp.where(incl, jnp.exp(jnp.where(incl, rel.astype(F32) * lg, 0.0)), 0.0)
        o = _mm(_mm(qs, ks, 1, 1) * dmask, vs, 1, 0)
        k_dec = ks * jnp.exp((CHUNK - 1.0 - pos) * lg)
        kv = _bmm(_heads3(k_dec), _heads3(vs), 1, 1)
        o = o + _bmm(_heads3(qs * jnp.exp((pos + 1.0) * lg)), s, 2, 1).reshape(n, HEAD128)
        decay = jnp.exp(CHUNK * jnp.mean(_heads3(lg), axis=1, keepdims=True))
        s_new = s * decay + kv
        return (tuple(o[h * CHUNK:(h + 1) * CHUNK] for h in range(N_HEAD4)),), s_new
    return ret_chunk


def _lane_pieces(width, n=N_HEAD4):
    return [(h * width, width) for h in range(n)]


class Tok:
    def __init__(self, x, pieces, width=None, col=0):
        self.x, self.pieces, self.col = x, pieces, col
        self.width = x.shape[-1] if width is None else width


def _chunk_of(direction, step, nc, nctx):
    if direction == 0:
        return step
    return jnp.where(step < nctx, nctx - 1 - step, nc + nctx - 1 - step)


def scan_fwd(name, dirs, params, state_shape, outs, lc):
    bsz, t = dirs[0][1][0].x.shape[:2]
    nc, nctx = t // CHUNK, lc // CHUNK
    nd, n_t, n_p, n_o = len(dirs), len(dirs[0][1]), len(params), len(outs)

    def body(*refs):
        tok_refs, par_refs = refs[:nd * n_t], refs[nd * n_t:nd * n_t + n_p]
        out_refs = refs[nd * n_t + n_p:-nd]
        s_refs = refs[-nd:]
        pv = [_load(r, p) for r, (_, p) in zip(par_refs, params)]
        for d, (fn, toks) in enumerate(dirs):
            s_ref = s_refs[d]

            @pl.when(pl.program_id(1) == 0)
            def _():
                s_ref[...] = jnp.zeros(s_ref.shape, F32)

            o_refs = out_refs[d * (n_o + 1):(d + 1) * (n_o + 1)]
            s = s_ref[...]
            o_refs[-1][...] = s
            tv = [_load(r, tk.pieces) for r, tk in zip(tok_refs[d * n_t:(d + 1) * n_t], toks)]
            res, s_new = fn(s, tv, pv)
            s_ref[...] = s_new
            for r, o, (_, _, pieces) in zip(o_refs, res, outs):
                _store(r, o, pieces)

    nstate = len(state_shape)
    tok_specs, out_specs, out_shapes, operands = [], [], [], []
    for d, (_, toks) in enumerate(dirs):
        for tk in toks:
            tok_specs.append(pl.BlockSpec((None, CHUNK, tk.width), lambda b, c, d=d, col=tk.col: (b, _chunk_of(d, c, nc, nctx), col)))
            operands.append(tk.x)
        for w, dt, _ in outs:
            out_specs.append(pl.BlockSpec((None, CHUNK, w), lambda b, c, d=d: (b, _chunk_of(d, c, nc, nctx), 0)))
            out_shapes.append(jax.ShapeDtypeStruct((bsz, t, w), dt))
        out_specs.append(pl.BlockSpec((None, None) + state_shape, lambda b, c, d=d: (b, _chunk_of(d, c, nc, nctx)) + (0,) * nstate))
        out_shapes.append(jax.ShapeDtypeStruct((bsz, nc) + state_shape, F32))
    par_specs = [pl.BlockSpec(x.shape, lambda b, c: (0, 0)) for x, _ in params]
    return pl.pallas_call(
        body, name=name, grid=(bsz, nc),
        in_specs=tok_specs + par_specs, out_specs=out_specs, out_shape=out_shapes,
        scratch_shapes=[pltpu.VMEM(state_shape, F32)] * nd, compiler_params=_params(2),
    )(*operands, *[x for x, _ in params])


def scan_bwd(name, dirs, params, state_shape, outs, lc, sprevs, cts):
    bsz, t = dirs[0][1][0].x.shape[:2]
    nc, nctx = t // CHUNK, lc // CHUNK
    nd, n_t, n_p, n_o = len(dirs), len(dirs[0][1]), len(params), len(outs)
    per_in = n_t + 1 + n_o

    def body(*refs):
        par_refs = refs[nd * per_in:nd * per_in + n_p]
        g_refs = refs[nd * per_in + n_p:-nd]
        ds_refs = refs[-nd:]
        first = pl.program_id(1) == 0
        pv = [_load(r, p) for r, (_, p) in zip(par_refs, params)]
        d_par = None
        for d, (fn, toks) in enumerate(dirs):
            ins = refs[d * per_in:(d + 1) * per_in]
            tok_refs, sprev_ref, ct_refs = ins[:n_t], ins[n_t], ins[n_t + 1:]
            ds_ref = ds_refs[d]

            @pl.when(first)
            def _():
                ds_ref[...] = jnp.zeros(ds_ref.shape, F32)

            tv = [_load(r, tk.pieces) for r, tk in zip(tok_refs, toks)]
            _, vjp = jax.vjp(fn, sprev_ref[...], tv, pv)
            ct = tuple(_load(r, pieces) for r, (_, _, pieces) in zip(ct_refs, outs))
            d_s, d_tv, d_pv = vjp((ct, ds_ref[...]))
            ds_ref[...] = d_s
            for r, gv, tk in zip(g_refs[d * n_t:(d + 1) * n_t], d_tv, toks):
                _store(r, gv, tk.pieces)
            d_par = d_pv if d_par is None else jax.tree.map(jnp.add, d_par, d_pv)
        very_first = first & (pl.program_id(0) == 0)
        for r, gv, (_, p) in zip(g_refs[nd * n_t:], d_par, params):
            @pl.when(very_first)
            def _():
                _store(r, gv, p)

            @pl.when(jnp.logical_not(very_first))
            def _():
                _store(r, gv, p, accumulate=True)

    nstate = len(state_shape)
    in_specs, operands, g_specs, g_shapes = [], [], [], []
    for d, (_, toks) in enumerate(dirs):
        chunk = lambda c, d=d: _chunk_of(d, nc - 1 - c, nc, nctx)
        for tk in toks:
            in_specs.append(pl.BlockSpec((None, CHUNK, tk.width), lambda b, c, f=chunk, col=tk.col: (b, f(c), col)))
            operands.append(tk.x)
            g_specs.append(pl.BlockSpec((None, CHUNK, tk.width), lambda b, c, f=chunk: (b, f(c), 0)))
            g_shapes.append(jax.ShapeDtypeStruct((bsz, t, tk.width), F32))
        in_specs.append(pl.BlockSpec((None, None) + state_shape, lambda b, c, f=chunk: (b, f(c)) + (0,) * nstate))
        operands.append(sprevs[d])
        for (w, _, _), ct in zip(outs, cts[d]):
            in_specs.append(pl.BlockSpec((None, CHUNK, w), lambda b, c, f=chunk: (b, f(c), 0)))
            operands.append(ct)
    par_specs = [pl.BlockSpec(x.shape, lambda b, c: (0, 0)) for x, _ in params]
    return pl.pallas_call(
        body, name=name, grid=(bsz, nc),
        in_specs=in_specs + par_specs, out_specs=g_specs + par_specs,
        out_shape=g_shapes + [jax.ShapeDtypeStruct(x.shape, F32) for x, _ in params],
        scratch_shapes=[pltpu.VMEM(state_shape, F32)] * nd, compiler_params=_params(2),
    )(*operands, *[x for x, _ in params])


ROWS = N_HEAD4 * CHUNK


def _bmm3(a, b, ca, cb):
    ah, al = _split_bf16(a)
    bh, bl = _split_bf16(b)
    dn = (((ca,), (cb,)), ((0,), (0,)))
    d = lambda u, v: lax.dot_general(u, v, dn, preferred_element_type=F32)
    return d(ah, bh) + (d(ah, bl) + d(al, bh))


def _inv_unit_tri_b(low):
    n = low.shape[-1]
    eye = (lax.broadcasted_iota(jnp.int32, (1, n, n), 1) == lax.broadcasted_iota(jnp.int32, (1, n, n), 2)).astype(F32)
    m = -low
    p = eye + m
    for _ in range(int(math.log2(CHUNK)) - 1):
        m = _bmm3(m, m, 2, 1)
        p = p + _bmm3(p, m, 2, 1)
    return p


@jax.custom_vjp
def _tri_solve_b(low, rhs):
    return _bmm3(_inv_unit_tri_b(low), rhs, 2, 1)


def _tri_solve_b_fwd(low, rhs):
    inv = _inv_unit_tri_b(low)
    x = _bmm3(inv, rhs, 2, 1)
    return x, (inv, x)


def _tri_solve_b_bwd(res, g):
    inv, x = res
    d_rhs = _bmm3(inv, g, 1, 1)
    return -_bmm3(d_rhs, x, 2, 2), d_rhs


_tri_solve_b.defvjp(_tri_solve_b_fwd, _tri_solve_b_bwd)


def _stack_dirs(toks, i):
    return jnp.concatenate([jnp.concatenate(t[i], axis=1) for t in toks], axis=0)


def _problem_masks(p, nb):
    shape = (p, ROWS, ROWS)
    up = lax.broadcasted_iota(jnp.int32, shape, 0) >= nb
    i = lax.broadcasted_iota(jnp.int32, shape, 1)
    j = lax.broadcasted_iota(jnp.int32, shape, 2)
    same = (i >> 6) == (j >> 6)
    pi, pj = i & (CHUNK - 1), j & (CHUNK - 1)
    rel = jnp.where(up, pj - pi, pi - pj)
    last = same & (pj == jnp.where(up, 0, CHUNK - 1))
    return same & (rel >= 0), same & (rel > 0), last, rel


def _split_states(s_new, nd, nb):
    return [s_new[d * nb * N_HEAD4:(d + 1) * nb * N_HEAD4].reshape((nb, N_HEAD4) + s_new.shape[1:]) for d in range(nd)]


def _split_outs(o, nd, nb):
    return [(tuple(o[d * nb:(d + 1) * nb, h * CHUNK:(h + 1) * CHUNK] for h in range(N_HEAD4)),) for d in range(nd)]


def dn_chunks(states, toks, params):
    nd, nb = len(toks), states[0].shape[0]
    p = nd * nb
    qs, ks, vs, la, beta = (_stack_dirs(toks, i) for i in range(5))
    incl, strict, last, _ = _problem_masks(p, nb)
    sq = (p, ROWS, ROWS)
    g = jnp.sum(jnp.where(incl, jnp.swapaxes(jnp.broadcast_to(la, sq), 1, 2), 0.0), axis=2, keepdims=True)
    gb = jnp.broadcast_to(g, sq)
    gbt = jnp.swapaxes(gb, 1, 2)
    dec_incl = jnp.where(incl, jnp.exp(jnp.where(incl, gb - gbt, 0.0)), 0.0)
    dec_strict = jnp.where(strict, dec_incl, 0.0)
    low = beta * _bmm(ks, ks, 2, 2) * dec_strict
    eg = jnp.exp(g)
    sol = _tri_solve_b(low, jnp.concatenate([beta * vs, (beta * eg) * ks], axis=2))
    w_v, w_k = sol[:, :, :HEAD128], sol[:, :, HEAD128:]
    a_qk = _bmm(qs, ks, 2, 2) * dec_incl
    g_last = jnp.sum(jnp.where(last, gbt, 0.0), axis=2, keepdims=True)
    k_dec = ks * jnp.exp(g_last - g)
    s = jnp.concatenate(states, axis=0).reshape(p * N_HEAD4, HEAD128, HEAD128)
    h3 = lambda x: x.reshape(p * N_HEAD4, CHUNK, x.shape[-1])
    u = w_v - _bmm(h3(w_k), s, 2, 1).reshape(p, ROWS, HEAD128)
    o = _bmm(h3(qs * eg), s, 2, 1).reshape(p, ROWS, HEAD128) + _bmm(a_qk, u, 2, 1)
    decay = jnp.exp(jnp.mean(h3(g_last), axis=1, keepdims=True))
    s_new = s * decay + _bmm(h3(k_dec), h3(u), 1, 1)
    return _split_outs(o, nd, nb), _split_states(s_new, nd, nb)


def ret_chunks(states, toks, params):
    nd, nb = len(toks), states[0].shape[0]
    p = nd * nb
    (decay_log,) = params
    qs, ks, vs = (_stack_dirs(toks, i) for i in range(3))
    incl, _, _, rel = _problem_masks(p, nb)
    col = (p, ROWS, 1)
    up = lax.broadcasted_iota(jnp.int32, col, 0) >= nb
    row = lax.broadcasted_iota(jnp.int32, col, 1)
    head = row >> 6
    lg = jnp.zeros(col, F32)
    for h in range(N_HEAD4):
        rate = jnp.where(up, -jnp.exp(decay_log[N_HEAD4 + h]), -jnp.exp(decay_log[h]))
        lg = jnp.where(head == h, rate, lg)
    place = row & (CHUNK - 1)
    pos = jnp.where(up, CHUNK - 1 - place, place).astype(F32)
    dmask = jnp.where(incl, jnp.exp(jnp.where(incl, rel.astype(F32) * lg, 0.0)), 0.0)
    o = _bmm(_bmm(qs, ks, 2, 2) * dmask, vs, 2, 1)
    s = jnp.concatenate(states, axis=0).reshape(p * N_HEAD4, HEAD64, HEAD128)
    h3 = lambda x: x.reshape(p * N_HEAD4, CHUNK, x.shape[-1])
    kv = _bmm(h3(ks * jnp.exp((CHUNK - 1.0 - pos) * lg)), h3(vs), 1, 1)
    o = o + _bmm(h3(qs * jnp.exp((pos + 1.0) * lg)), s, 2, 1).reshape(p, ROWS, HEAD128)
    decay = jnp.exp(CHUNK * jnp.mean(h3(lg), axis=1, keepdims=True))
    s_new = s * decay + kv
    return _split_outs(o, nd, nb), _split_states(s_new, nd, nb)


def _scan_nb(bsz):
    return 2 if bsz % 2 == 0 else 1


def _carried(refs, n_in, n_out, comm):
    if comm is None:
        return refs[:n_in], refs[n_in:n_in + n_out], refs[n_in + n_out:], None
    ins, src = refs[:n_in], refs[n_in]
    outs, dst = refs[n_in + 1:n_in + 1 + n_out], refs[n_in + 1 + n_out]
    scratch, sems = refs[n_in + 2 + n_out:-3], refs[-3:]
    return ins, outs, scratch, (src, dst) + tuple(sems)


def _carry_start(xrefs, comm, grid):
    if comm is None:
        return
    first = functools.reduce(jnp.logical_and, [pl.program_id(i) == 0 for i in range(len(grid))])

    @pl.when(first)
    def _():
        _xy_start(xrefs, comm[1])


def _carry_wait(xrefs, comm, grid):
    if comm is None:
        return
    last = functools.reduce(jnp.logical_and, [pl.program_id(i) == n - 1 for i, n in enumerate(grid)])

    @pl.when(last)
    def _():
        _xy_wait(xrefs, comm[1])


def scan2_fwd(name, chunks_fn, dirs, params, state_shape, outs, lc, comm=None):
    bsz, t = dirs[0][0].x.shape[:2]
    nb = _scan_nb(bsz)
    nc, nctx = t // CHUNK, lc // CHUNK
    nd, n_t, n_p, n_o = len(dirs), len(dirs[0]), len(params), len(outs)
    grid = (bsz // nb, nc)

    def body(*refs):
        ins, out_refs, s_refs, xrefs = _carried(refs, nd * n_t + n_p, nd * (n_o + 1), comm)
        tok_refs, par_refs = ins[:nd * n_t], ins[nd * n_t:]
        _carry_start(xrefs, comm, grid)

        @pl.when(pl.program_id(1) == 0)
        def _():
            for s_ref in s_refs:
                s_ref[...] = jnp.zeros(s_ref.shape, F32)

        pv = [_load(r, p) for r, (_, p) in zip(par_refs, params)]
        states = [s_ref[...] for s_ref in s_refs]
        tv = [[_load(r, tk.pieces) for r, tk in zip(tok_refs[d * n_t:(d + 1) * n_t], dirs[d])] for d in range(nd)]
        res, s_new = chunks_fn(states, tv, pv)
        for d in range(nd):
            o_refs = out_refs[d * (n_o + 1):(d + 1) * (n_o + 1)]
            o_refs[-1][...] = states[d]
            s_refs[d][...] = s_new[d]
            for r, o, (_, _, pieces) in zip(o_refs, res[d], outs):
                _store(r, o, pieces)
        _carry_wait(xrefs, comm, grid)

    nstate = len(state_shape)
    tok_specs, out_specs, out_shapes, operands = [], [], [], []
    for d, toks in enumerate(dirs):
        chunk = lambda c, d=d: _chunk_of(d, c, nc, nctx)
        for tk in toks:
            tok_specs.append(pl.BlockSpec((nb, CHUNK, tk.width), lambda b, c, f=chunk, col=tk.col: (b, f(c), col)))
            operands.append(tk.x)
        for w, dt, _ in outs:
            out_specs.append(pl.BlockSpec((nb, CHUNK, w), lambda b, c, f=chunk: (b, f(c), 0)))
            out_shapes.append(jax.ShapeDtypeStruct((bsz, t, w), dt))
        out_specs.append(pl.BlockSpec((nb, None) + state_shape, lambda b, c, f=chunk: (b, f(c)) + (0,) * nstate))
        out_shapes.append(jax.ShapeDtypeStruct((bsz, nc) + state_shape, F32))
    par_specs = [pl.BlockSpec(x.shape, lambda b, c: (0, 0)) for x, _ in params]
    operands += [x for x, _ in params]
    return _carrier_call(body, name, grid, tok_specs + par_specs, operands, out_specs, out_shapes,
                         [pltpu.VMEM((nb,) + state_shape, F32)] * nd, comm)


def _carrier_call(body, name, grid, in_specs, operands, out_specs, out_shapes, scratch, comm):
    if comm is not None:
        in_specs, operands = in_specs + [HBM_SPEC], operands + [comm[0]]
        out_specs, out_shapes = out_specs + [HBM_SPEC], out_shapes + [_xy_out_shape(*comm)]
        scratch = scratch + _xy_sems()
    return pl.pallas_call(body, name=name, grid=grid, in_specs=in_specs, out_specs=out_specs, out_shape=out_shapes,
                          scratch_shapes=scratch, compiler_params=_params(len(grid)))(*operands)


def scan2_bwd(name, chunks_fn, dirs, params, state_shape, outs, lc, sprevs, cts, comm=None):
    bsz, t = dirs[0][0].x.shape[:2]
    nb = _scan_nb(bsz)
    nc, nctx = t // CHUNK, lc // CHUNK
    nd, n_t, n_p, n_o = len(dirs), len(dirs[0]), len(params), len(outs)
    per_in = n_t + 1 + n_o
    grid = (bsz // nb, nc)

    def body(*refs):
        refs, g_refs, ds_refs, xrefs = _carried(refs, nd * per_in + n_p, nd * n_t + n_p, comm)
        par_refs = refs[nd * per_in:]
        _carry_start(xrefs, comm, grid)
        first = pl.program_id(1) == 0

        @pl.when(first)
        def _():
            for ds_ref in ds_refs:
                ds_ref[...] = jnp.zeros(ds_ref.shape, F32)

        pv = [_load(r, p) for r, (_, p) in zip(par_refs, params)]
        ins = [refs[d * per_in:(d + 1) * per_in] for d in range(nd)]
        tv = [[_load(r, tk.pieces) for r, tk in zip(ins[d][:n_t], dirs[d])] for d in range(nd)]
        states = [ins[d][n_t][...] for d in range(nd)]
        _, vjp = jax.vjp(chunks_fn, states, tv, pv)
        ct = [tuple(_load(r, pieces) for r, (_, _, pieces) in zip(ins[d][n_t + 1:], outs)) for d in range(nd)]
        d_s, d_tv, d_pv = vjp((ct, [ds_ref[...] for ds_ref in ds_refs]))
        for d in range(nd):
            ds_refs[d][...] = d_s[d]
            for r, gv, tk in zip(g_refs[d * n_t:(d + 1) * n_t], d_tv[d], dirs[d]):
                _store(r, gv, tk.pieces)
        very_first = first & (pl.program_id(0) == 0)
        for r, gv, (_, p) in zip(g_refs[nd * n_t:], d_pv, params):
            @pl.when(very_first)
            def _():
                _store(r, gv, p)

            @pl.when(jnp.logical_not(very_first))
            def _():
                _store(r, gv, p, accumulate=True)
        _carry_wait(xrefs, comm, grid)

    nstate = len(state_shape)
    in_specs, operands, g_specs, g_shapes = [], [], [], []
    for d, toks in enumerate(dirs):
        chunk = lambda c, d=d: _chunk_of(d, nc - 1 - c, nc, nctx)
        for tk in toks:
            in_specs.append(pl.BlockSpec((nb, CHUNK, tk.width), lambda b, c, f=chunk, col=tk.col: (b, f(c), col)))
            operands.append(tk.x)
            g_specs.append(pl.BlockSpec((nb, CHUNK, tk.width), lambda b, c, f=chunk: (b, f(c), 0)))
            g_shapes.append(jax.ShapeDtypeStruct((bsz, t, tk.width), F32))
        in_specs.append(pl.BlockSpec((nb, None) + state_shape, lambda b, c, f=chunk: (b, f(c)) + (0,) * nstate))
        operands.append(sprevs[d])
        for (w, _, _), ct in zip(outs, cts[d]):
            in_specs.append(pl.BlockSpec((nb, CHUNK, w), lambda b, c, f=chunk: (b, f(c), 0)))
            operands.append(ct)
    par_specs = [pl.BlockSpec(x.shape, lambda b, c: (0, 0)) for x, _ in params]
    operands += [x for x, _ in params]
    return _carrier_call(body, name, grid, in_specs + par_specs, operands, g_specs + par_specs,
                         g_shapes + [jax.ShapeDtypeStruct(x.shape, F32) for x, _ in params],
                         [pltpu.VMEM((nb,) + state_shape, F32)] * nd, comm)


class Geo:
    def __init__(self, bsz, t, lc):
        self.b, self.t, self.lc, self.m = bsz, t, lc, bsz * t
        self.tm = _pick(lc, (256, 128, 64))
        self.tq = _pick(lc, (128, 64))
        assert t % self.tm == 0 and t % CHUNK == 0 and lc % CHUNK == 0
        self.nctx, self.nctx_q = lc // self.tm, lc // self.tq
        self.grid = (bsz, t // self.tm)


def a_tok(g, x, tm=None, **kw):
    tm = tm or g.tm
    return Arg(x, (None, tm, x.shape[-1]), lambda b, j: (b, j, 0), **kw)


def a_ztok(g, z, name, width, tm=None, **kw):
    tm = tm or g.tm
    col = ZOFF[name] // width
    return Arg(z, (None, tm, width), lambda b, j: (b, j, col),
               gshape=(g.b, g.t, width), gimap=lambda b, j: (b, j, 0), **kw)


def a_par(x, **kw):
    return Arg(x, x.shape, lambda b, j: (0, 0), first=lambda ids: (ids[0] == 0) & (ids[1] == 0), **kw)


def a_mod(g, x):
    n = g.nctx
    return Arg(x, (None, None, x.shape[2], D_MODEL), lambda b, j: (b, jnp.where(j >= n, 1, 0), 0, 0),
               first=lambda ids: (ids[1] == 0) | (ids[1] == n))


def a_tab(g, x):
    return Arg(x, (g.tm, x.shape[-1]), lambda b, j: (j, 0), diff=False)


def o_tok(g, w, dtype, tm=None, pieces=None):
    tm = tm or g.tm
    return Out((g.b, g.t, w), dtype, (None, tm, w), lambda b, j: (b, j, 0), pieces)


def addn(name, xs, dtype):
    bsz, t, w = xs[0].shape
    tm = _pick(t, (256, 192, 128, 64))

    def body(*refs):
        tot = refs[0][...].astype(F32)
        for r in refs[1:-1]:
            tot = tot + r[...].astype(F32)
        refs[-1][...] = tot.astype(dtype)

    spec = pl.BlockSpec((None, tm, w), lambda b, j: (b, j, 0))
    return pl.pallas_call(body, name=name, grid=(bsz, t // tm), in_specs=[spec] * len(xs), out_specs=spec,
                          out_shape=jax.ShapeDtypeStruct((bsz, t, w), dtype), compiler_params=_params(2))(*xs)


P64x8 = _lane_pieces(HEAD64, ATT_HEADS)
P64x4 = _lane_pieces(HEAD64, N_HEAD4)
P64x2 = _lane_pieces(HEAD64, ATT_KV_HEADS)
P128x4 = _lane_pieces(HEAD128, N_HEAD4)
P1024x4 = _lane_pieces(D_MODEL, N_BRANCH)


def _gate_pieces(direction):
    la = [(direction * N_HEAD4 + h, 1) for h in range(N_HEAD4)]
    beta = [(8 + direction * N_HEAD4 + h, 1) for h in range(N_HEAD4)]
    return la, beta


RET_PIECES = [(i, 1) for i in range(2 * N_HEAD4)]


def _prep_io(g, z, sp, tabs):
    args = [a_ztok(g, z, "att_q", 512, gdtype=BF16), a_ztok(g, z, "att_k", 128, gdtype=BF16),
            a_ztok(g, z, "ret_q", 256, gdtype=BF16), a_ztok(g, z, "ret_k", 256, gdtype=BF16),
            a_ztok(g, z, "narrow", 128, gdtype=BF16),
            a_par(sp["qgain"]), a_par(sp["kgain"]), a_par(sp["alog"]), a_par(sp["dtb"]),
            a_tab(g, tabs[0]), a_tab(g, tabs[1])]
    outs = [o_tok(g, 512, BF16), o_tok(g, 128, BF16), o_tok(g, 256, F32), o_tok(g, 256, F32), o_tok(g, 128, F32)]
    return args, outs


def _attn_io(g, q_att, k_att, z, latent):
    col = ZOFF["att_v"] // 128
    first = lambda ids: ids[1] == 0
    if latent:
        rows, off, nq = g.t, g.nctx_q, (g.t - g.lc) // g.tq
    else:
        rows, off, nq = g.lc, 0, g.nctx_q
    args = [Arg(q_att, (None, g.tq, 512), lambda b, j: (b, j + off, 0), pieces=P64x8,
                gshape=(g.b, nq * g.tq, 512), gimap=lambda b, j: (b, j, 0)),
            Arg(k_att, (None, rows, 128), lambda b, j: (b, 0, 0), first=first, pieces=P64x2, gshape=(g.b, rows, 128)),
            Arg(z, (None, rows, 128), lambda b, j: (b, 0, col), first=first, pieces=P64x2,
                gshape=(g.b, rows, 128), gimap=lambda b, j: (b, 0, 0))]
    outs = [Out((g.b, nq * g.tq, 512), BF16, (None, g.tq, 512), lambda b, j: (b, j, 0), P64x8)]
    return (g.b, nq), args, outs


def _pad_rows(x, t):
    return jnp.pad(x, ((0, 0), (0, t - x.shape[1]), (0, 0)))


def _colgrid_arg(g, z, name, width_total, **kw):
    col = ZOFF[name] // 128
    return Arg(z, (None, g.t, 128), lambda h, b: (b, 0, col + h),
               gshape=(g.b, g.t, width_total), gimap=lambda h, b: (b, 0, h), **kw)


def _dnprep_io(g, z, sp):
    wfirst = lambda ids: ids[1] == 0
    args = [_colgrid_arg(g, z, "dn_q", 512, gdtype=BF16), _colgrid_arg(g, z, "dn_k", 512, gdtype=BF16),
            _colgrid_arg(g, z, "dn_v", 512, gdtype=BF16)]
    for i in range(3):
        args.append(Arg(sp["dn_conv"][i], (3, 128), lambda h, b: (0, h), first=wfirst))
    outs = [Out((g.b, g.t, 512), F32, (None, g.t, 128), lambda h, b: (b, 0, h)) for _ in range(3)]
    return (N_HEAD4, g.b), args, outs


def _shortconv_io(g, z, sp):
    args = [_colgrid_arg(g, z, "sc_b", 512, gdtype=BF16), _colgrid_arg(g, z, "sc_c", 512, gdtype=BF16),
            _colgrid_arg(g, z, "sc_x", 512, gdtype=BF16),
            Arg(sp["sc_conv"], (3, 128), lambda h, b: (0, h), first=lambda ids: ids[1] == 0)]
    outs = [Out((g.b, g.t, 512), BF16, (None, g.t, 128), lambda h, b: (b, 0, h))]
    return (BRANCH_W // 128, g.b), args, outs


def _finish_io(g, o_dn_f, o_dn_b, z, o_rt_f, o_rt_b, sp):
    args = [a_tok(g, o_dn_f, pieces=P128x4), a_tok(g, o_dn_b, pieces=P128x4),
            a_ztok(g, z, "dn_z", 512, gdtype=BF16, pieces=P128x4),
            a_tok(g, o_rt_f, pieces=P128x4), a_tok(g, o_rt_b, pieces=P128x4),
            a_ztok(g, z, "ret_g", 512, gdtype=BF16, pieces=P128x4), a_par(sp["ngain"])]
    outs = [o_tok(g, 512, BF16, pieces=P128x4), o_tok(g, 512, BF16, pieces=P128x4)]
    return args, outs


def _merge_io(g, ps, z):
    args = [a_tok(g, p, gdtype=BF16) for p in ps] + [a_ztok(g, z, "gates", 4096, gdtype=BF16, pieces=P1024x4)]
    return args, [o_tok(g, D_MODEL, BF16)]


def _dn_dirs(qn, kn, vn, gates):
    dirs = []
    for d in range(2):
        la, beta = _gate_pieces(d)
        dirs.append([Tok(qn, P128x4), Tok(kn, P128x4), Tok(vn, P128x4), Tok(gates, la), Tok(gates, beta)])
    return dirs


def _ret_dirs(q_ret, k_ret, z):
    col = ZOFF["ret_v"] // 512
    return [[Tok(q_ret, P64x4), Tok(k_ret, P64x4), Tok(z, P128x4, 512, col)] for _ in range(2)]


DN_STATE = (N_HEAD4, HEAD128, HEAD128)
RET_STATE = (N_HEAD4, HEAD64, HEAD128)
SCAN_OUT = [(512, F32, P128x4)]


def layer_fwd(g, x, h, w, sp, mod_a, mod_b, tabs, g_next, comm=None):
    tp = {"x": x, "h": h}
    m, lc = g.m, g.lc
    z = matmul("win", h.reshape(m, D_MODEL), w["win"]).reshape(g.b, g.t, NZ)
    tp["z"] = z
    args, outs = _prep_io(g, z, sp, tabs)
    q_att, k_att, q_ret, k_ret, gates = vfwd("prep", f_prep, g.grid, args, outs)
    tp.update(q_att=q_att, k_att=k_att, q_ret=q_ret, k_ret=k_ret, gates=gates)
    y_att = []
    for latent in (False, True):
        agrid, args, outs = _attn_io(g, q_att, k_att, z, latent)
        y_att += vfwd("attn", make_f_attn(g.tq), agrid, args, outs)
    y_att = jnp.concatenate(y_att, axis=1)
    dgrid, args, outs = _dnprep_io(g, z, sp)
    qn, kn, vn = vfwd("dnprep", make_f_dnprep(g.t, lc), dgrid, args, outs)
    tp.update(qn=qn, kn=kn, vn=vn)
    o_f, s_f, o_b, s_b, *carried = scan2_fwd("dnscan", dn_chunks, _dn_dirs(qn, kn, vn, gates), [], DN_STATE, SCAN_OUT, lc, comm)
    tp["carried"] = carried[0] if carried else None
    o_dn, tp["dn_s"] = [o_f, o_b], [s_f, s_b]
    o_f, s_f, o_b, s_b = scan2_fwd("retscan", ret_chunks, _ret_dirs(q_ret, k_ret, z), [(sp["ret"], RET_PIECES)], RET_STATE, SCAN_OUT, lc)
    o_rt, tp["rt_s"] = [o_f, o_b], [s_f, s_b]
    tp.update(o_dn=o_dn, o_rt=o_rt)
    args, outs = _finish_io(g, o_dn[0], o_dn[1], z, o_rt[0], o_rt[1], sp)
    y_dn, y_rt = vfwd("finish", f_finish, g.grid, args, outs)
    sgrid, args, outs = _shortconv_io(g, z, sp)
    (y_sc,) = vfwd("shortconv", make_f_shortconv(g.t, lc), sgrid, args, outs)
    ys = [y_att, y_dn, y_rt, y_sc]
    tp["ys"] = ys
    ps = [matmul("wbranch", y.reshape(m, BRANCH_W), w["wbr"][i]).reshape(g.b, g.t, D_MODEL) for i, y in enumerate(ys)]
    tp["ps"] = ps
    args, outs = _merge_io(g, ps, z)
    (u,) = vfwd("merge", f_merge, g.grid, args, outs)
    tp["u"] = u
    y = matmul("wout", u.reshape(m, D_MODEL), w["wout"]).reshape(g.b, g.t, D_MODEL)
    tp["y"] = y
    args = [a_tok(g, x), a_tok(g, y), a_par(sp["g1"]), a_par(sp["g2"]), a_mod(g, mod_a)]
    x1, h2 = vfwd("resnorm", f_resnorm, g.grid, args, [o_tok(g, D_MODEL, F32), o_tok(g, D_MODEL, BF16)])
    tp.update(x1=x1, h2=h2)
    a = matmul("wmlp1", h2.reshape(m, D_MODEL), w["w1"]).reshape(g.b, g.t, MLP_HIDDEN)
    tp["a"] = a
    (r,) = vfwd("act", f_act, g.grid, [a_tok(g, a, gdtype=BF16)], [o_tok(g, MLP_HIDDEN, BF16)])
    tp["r"] = r
    mo = matmul("wmlp2", r.reshape(m, MLP_HIDDEN), w["w2"]).reshape(g.b, g.t, D_MODEL)
    tp["mo"] = mo
    if g_next is None:
        args = [a_tok(g, x1), a_tok(g, mo), a_par(sp["g3"]), a_mod(g, mod_b)]
        (x2,) = vfwd("resid", f_resid, g.grid, args, [o_tok(g, D_MODEL, F32)])
        return x2, None, tp
    args = [a_tok(g, x1), a_tok(g, mo), a_par(sp["g3"]), a_par(g_next), a_mod(g, mod_b)]
    x2, h_next = vfwd("resnorm", f_resnorm, g.grid, args, [o_tok(g, D_MODEL, F32), o_tok(g, D_MODEL, BF16)])
    return x2, h_next, tp


def layer_bwd(g, tp, w, sp, mod_a, mod_b, tabs, g_next, dx2, dh_next, comm=None):
    m, lc = g.m, g.lc
    gw, gs = {}, {}
    x, z = tp["x"], tp["z"]
    if g_next is None:
        args = [a_tok(g, tp["x1"]), a_tok(g, tp["mo"]), a_par(sp["g3"]), a_mod(g, mod_b)]
        dx1, dmo, gs["g3"], dmod_b = vbwd("resid_b", f_resid, g.grid, args, [o_tok(g, D_MODEL, F32)], [[dx2]])
    else:
        args = [a_tok(g, tp["x1"]), a_tok(g, tp["mo"]), a_par(sp["g3"]), a_par(g_next), a_mod(g, mod_b)]
        dx1, dmo, gs["g3"], gs["g0_next"], dmod_b = vbwd(
            "resnorm_b", f_resnorm, g.grid, args, [o_tok(g, D_MODEL, F32), o_tok(g, D_MODEL, BF16)], [[dx2], [dh_next]])
    dmo2 = dmo.reshape(m, D_MODEL)
    dr = matmul("wmlp2_dx", dmo2, w["w2t"]).reshape(g.b, g.t, MLP_HIDDEN)
    gw["w2"] = matmul("wmlp2_dw", tp["r"].reshape(m, MLP_HIDDEN), dmo2, trans_a=True)
    (da,) = vbwd("act_b", f_act, g.grid, [a_tok(g, tp["a"], gdtype=BF16)], [o_tok(g, MLP_HIDDEN, BF16)], [[dr]])
    da2 = da.reshape(m, MLP_HIDDEN)
    dh2 = matmul("wmlp1_dx", da2, w["w1t"]).reshape(g.b, g.t, D_MODEL)
    gw["w1"] = matmul("wmlp1_dw", tp["h2"].reshape(m, D_MODEL), da2, trans_a=True)
    args = [a_tok(g, x), a_tok(g, tp["y"]), a_par(sp["g1"]), a_par(sp["g2"]), a_mod(g, mod_a)]
    dx, dy, gs["g1"], gs["g2"], dmod_a = vbwd(
        "resnorm_b", f_resnorm, g.grid, args, [o_tok(g, D_MODEL, F32), o_tok(g, D_MODEL, BF16)], [[dx1], [dh2]])
    dy2 = dy.reshape(m, D_MODEL)
    du = matmul("wout_dx", dy2, w["woutt"]).reshape(g.b, g.t, D_MODEL)
    gw["wout"] = matmul("wout_dw", tp["u"].reshape(m, D_MODEL), dy2, trans_a=True)
    args, outs = _merge_io(g, tp["ps"], z)
    *dps, dz_gates = vbwd("merge_b", f_merge, g.grid, args, outs, [[du]])
    dys, gwbr = [], []
    for i in range(N_BRANCH):
        dp2 = dps[i].reshape(m, D_MODEL)
        dys.append(matmul("wbranch_dx", dp2, w["wbrt"][i]).reshape(g.b, g.t, BRANCH_W))
        gwbr.append(matmul("wbranch_dw", tp["ys"][i].reshape(m, BRANCH_W), dp2, trans_a=True))
    gw["wbr"] = jnp.stack(gwbr)
    dy_att, dy_dn, dy_rt, dy_sc = dys
    o_dn, o_rt = tp["o_dn"], tp["o_rt"]
    args, outs = _finish_io(g, o_dn[0], o_dn[1], z, o_rt[0], o_rt[1], sp)
    do_dn_f, do_dn_b, dz_dnz, do_rt_f, do_rt_b, dz_retg, gs["ngain"] = vbwd(
        "finish_b", f_finish, g.grid, args, outs, [[dy_dn], [dy_rt]])
    res = scan2_bwd("dnscan_b", dn_chunks, _dn_dirs(tp["qn"], tp["kn"], tp["vn"], tp["gates"]), [], DN_STATE, SCAN_OUT, lc,
                   tp["dn_s"], [[do_dn_f], [do_dn_b]], comm)
    dqn, dkn, dvn, dgates = [res[0], res[5]], [res[1], res[6]], [res[2], res[7]], [res[3], res[4], res[8], res[9]]
    carried = res[10] if comm is not None else None
    res = scan2_bwd("retscan_b", ret_chunks, _ret_dirs(tp["q_ret"], tp["k_ret"], z), [(sp["ret"], RET_PIECES)], RET_STATE, SCAN_OUT, lc,
                   tp["rt_s"], [[do_rt_f], [do_rt_b]])
    dq_ret, dk_ret, gs["ret"] = [res[0], res[3]], [res[1], res[4]], res[6]
    dz_retv = addn("sum_retv", [res[2], res[5]], BF16)
    sgrid, args, outs = _shortconv_io(g, z, sp)
    dz_scb, dz_scc, dz_scx, gs["sc_conv"] = vbwd("shortconv_b", make_f_shortconv(g.t, lc), sgrid, args, outs, [[dy_sc]])
    dq_att, dk_att, dv_att = [], [], []
    for latent, dy in ((False, dy_att[:, :lc]), (True, dy_att[:, lc:])):
        agrid, args, outs = _attn_io(g, tp["q_att"], tp["k_att"], z, latent)
        dq, dk, dv = vbwd("attn_b", make_f_attn(g.tq), agrid, args, outs, [[dy]])
        dq_att.append(dq); dk_att.append(_pad_rows(dk, g.t)); dv_att.append(_pad_rows(dv, g.t))
    dq_att = jnp.concatenate(dq_att, axis=1)
    dz_attv = addn("sum_attv", dv_att, BF16)
    dgrid, args, outs = _dnprep_io(g, z, sp)
    dz_dnq, dz_dnk, dz_dnv, gc_q, gc_k, gc_v = vbwd("dnprep_b", make_f_dnprep(g.t, lc), dgrid, args, outs, [dqn, dkn, dvn])
    gs["dn_conv"] = [gc_q, gc_k, gc_v]
    args, outs = _prep_io(g, z, sp, tabs)
    dz_attq, dz_attk, dz_retq, dz_retk, dz_nar, gs["qgain"], gs["kgain"], gs["alog"], gs["dtb"] = vbwd(
        "prep_b", f_prep, g.grid, args, outs, [[dq_att], dk_att, dq_ret, dk_ret, dgates])
    pad = jnp.zeros((g.b, g.t, NZ - (ZOFF["narrow"] + 128)), BF16)
    dz = jnp.concatenate([dz_gates, dz_attq, dz_dnq, dz_dnk, dz_dnv, dz_dnz, dz_retv, dz_retg, dz_scb, dz_scc, dz_scx,
                          dz_retq, dz_retk, dz_attk, dz_attv, dz_nar, pad], axis=-1)
    dz2 = dz.reshape(m, NZ)
    dh = matmul("win_dx", dz2, w["wint"]).reshape(g.b, g.t, D_MODEL)
    gw["win"] = matmul("win_dw", tp["h"].reshape(m, D_MODEL), dz2, trans_a=True)
    return dx, dh, gw, gs, dmod_a, dmod_b, carried


def loss_call(g, xa, tgt):
    n = g.nctx
    inv_d = 1.0 / D_MODEL

    def body(x_ref, t_ref, loss_ref, dx_ref):
        j = pl.program_id(1)

        @pl.when((pl.program_id(0) == 0) & (j == 0))
        def _():
            loss_ref[...] = jnp.zeros(loss_ref.shape, F32)

        @pl.when(j < n)
        def _():
            dx_ref[...] = jnp.zeros(dx_ref.shape, F32)

        @pl.when(j >= n)
        def _():
            e = x_ref[...] - t_ref[...]
            dx_ref[...] = e * inv_d
            s = jnp.sum(jnp.sum(e * e, axis=1, keepdims=True), axis=0, keepdims=True)
            loss_ref[...] += jnp.broadcast_to(0.5 * inv_d * s, loss_ref.shape)

    tok = pl.BlockSpec((None, g.tm, D_MODEL), lambda b, j: (b, j, 0))
    return pl.pallas_call(
        body, name="loss", grid=g.grid,
        in_specs=[tok, pl.BlockSpec((None, g.tm, D_MODEL), lambda b, j: (b, jnp.maximum(j - n, 0), 0))],
        out_specs=[pl.BlockSpec((1, LANES), lambda b, j: (0, 0)), tok],
        out_shape=[jax.ShapeDtypeStruct((1, LANES), F32), jax.ShapeDtypeStruct((g.b, g.t, D_MODEL), F32)],
        compiler_params=_params(2),
    )(xa, tgt)


def rope_tables(g):
    seq = g.t - g.lc
    rows = seq // GRID_W
    r, col = jnp.meshgrid(jnp.arange(rows), jnp.arange(GRID_W), indexing="ij")
    quarter = HEAD64 // 4
    inv_freq = ROPE_THETA ** (-jnp.arange(quarter, dtype=F32) / quarter)
    ang = jnp.concatenate([r.reshape(-1, 1).astype(F32) * inv_freq, col.reshape(-1, 1).astype(F32) * inv_freq], axis=-1)
    cos, sin = jnp.cos(ang), jnp.sin(ang)
    cos = jnp.concatenate([jnp.ones((g.lc, HEAD64 // 2), F32), cos], axis=0)
    sin = jnp.concatenate([jnp.zeros((g.lc, HEAD64 // 2), F32), sin], axis=0)
    reps = 512 // (HEAD64 // 2)
    return jnp.tile(cos, (1, reps)), jnp.tile(sin, (1, reps))


def _row128(v):
    v = v.reshape(1, -1).astype(F32)
    return jnp.pad(v, ((0, 0), (0, LANES - v.shape[1])))


def layer_small(small, l):
    gn = small["g_norm"][l]
    return {
        "g0": gn[0:1], "g1": gn[1:2], "g2": gn[2:3], "g3": gn[3:4],
        "qgain": small["att_q_gain"][l][None], "kgain": small["att_k_gain"][l][None],
        "dn_conv": [small["dn_conv"][l][:, i * 512:(i + 1) * 512] for i in range(3)], "alog": _row128(small["dn_a_log"][l]), "dtb": _row128(small["dn_dt_bias"][l]),
        "ngain": small["dn_norm_gain"][l][None], "ret": _row128(small["ret_decay"][l]), "sc_conv": small["sc_conv"][l],
    }


def permute_win(w):
    parts = [w[..., off:off + width] for _, off, width in _SEGS]
    parts.append(jnp.zeros(w.shape[:-1] + (NZ - N_IN,), w.dtype))
    return jnp.concatenate(parts, axis=-1)


def unpermute_win(gw):
    order = sorted(_SEGS, key=lambda s: s[1])
    return jnp.concatenate([gw[..., ZOFF[name]:ZOFF[name] + width] for name, _, width in order], axis=-1)


def layer_weights(w_in, w_branch, w_out, w_mlp_in, w_mlp_out):
    win = permute_win(w_in.astype(BF16))
    wbr, wout, w1, w2 = (t.astype(BF16) for t in (w_branch, w_out, w_mlp_in, w_mlp_out))
    return {"win": win, "wint": win.T, "wbr": wbr, "wbrt": jnp.swapaxes(wbr, 1, 2), "wout": wout, "woutt": wout.T,
            "w1": w1, "w1t": w1.T, "w2": w2, "w2t": w2.T}


PACK_COLS = 1024
BIG = (("w_in", -1), ("w_branch", -1), ("w_out", -2), ("w_mlp_in", -1), ("w_mlp_out", -2))


PACK_ROWS = 256


def pack_layer(parts):
    flat = jnp.concatenate([p.reshape(-1) for p in parts])
    per = PACK_COLS * PACK_ROWS
    return jnp.pad(flat, (0, -flat.shape[0] % per)).reshape(-1, PACK_COLS)


def unpack_layer(buf, shapes):
    flat, out, off = buf.reshape(-1), [], 0
    for s in shapes:
        n = int(np.prod(s))
        out.append(flat[off:off + n].reshape(s))
        off += n
    return out


def weights_of_slots(slots, shard_shapes):
    per = [unpack_layer(slots[p], shard_shapes) for p in range(N_XY)]
    full = [jnp.concatenate([per[p][i] for p in range(N_XY)], axis=axis) for i, (_, axis) in enumerate(BIG)]
    return layer_weights(*full)


def grad_pack(gw):
    full = [unpermute_win(gw["win"]), gw["wbr"], gw["wout"], gw["w1"], gw["w2"]]
    pieces = [jnp.split(t, N_XY, axis=axis) for t, (_, axis) in zip(full, BIG)]
    return jnp.stack([pack_layer([pc[p].astype(BF16) for pc in pieces]) for p in range(N_XY)])


def model_step(g, xa, tgt, mod, packs, shard_shapes, small):
    depth = len(packs)
    local = packs[0].ndim == 3
    tabs = rope_tables(g)
    sps = [layer_small(small, l) for l in range(depth)]
    mods_a = [mod[l][:, :, 2:5] for l in range(depth)]
    mods_b = [jnp.concatenate([mod[l][:, :, 5:6], mod[l + 1][:, :, 0:2]], axis=2) if l + 1 < depth else mod[l][:, :, 5:6]
              for l in range(depth)]
    mod0 = mod[0][:, :, 0:2]
    args0 = [a_tok(g, xa), a_par(sps[0]["g0"]), a_mod(g, mod0)]
    (h,) = vfwd("modnorm", f_modnorm, g.grid, args0, [o_tok(g, D_MODEL, BF16)])
    x, tapes, wl = xa, [], []
    slots = packs[0] if local else xy_exchange("gather_w", packs[0], True)
    for l in range(depth):
        g_next = sps[l + 1]["g0"] if l + 1 < depth else None
        wl.append(weights_of_slots(slots, shard_shapes))
        comm = (packs[l + 1], True) if l + 1 < depth and not local else None
        x, h, tp = layer_fwd(g, x, h, wl[l], sps[l], mods_a[l], mods_b[l], tabs, g_next, comm)
        slots = packs[l + 1] if local and l + 1 < depth else tp["carried"]
        tapes.append(tp)
    loss_row, dx = loss_call(g, x, tgt)
    dh, gws, gss, dmods = None, [None] * depth, [None] * depth, [None] * depth
    gpack = None
    for l in reversed(range(depth)):
        g_next = sps[l + 1]["g0"] if l + 1 < depth else None
        comm = (gpack, False) if gpack is not None and not local else None
        dx, dh, gw, gss[l], dma, dmb, carried = layer_bwd(g, tapes[l], wl[l], sps[l], mods_a[l], mods_b[l], tabs, g_next,
                                                          dx, dh, comm)
        if l + 1 < depth:
            gws[l + 1] = gpack if local else carried
        gpack = grad_pack(gw)
        dmods[l] = (dma, dmb)
    gws[0] = gpack if local else xy_exchange("scatter_w", gpack, False)
    dxh, g0_first, dmod0 = vbwd("modnorm_b", f_modnorm, g.grid, args0, [o_tok(g, D_MODEL, BF16)], [[dh]])
    dxa = addn("sum_dx", [dx, dxh], F32)
    dmod = []
    for l in range(depth):
        first2 = dmod0 if l == 0 else dmods[l - 1][1][:, :, 1:3]
        dmod.append(jnp.concatenate([first2, dmods[l][0], dmods[l][1][:, :, 0:1]], axis=2))
    dmod = jnp.stack(dmod)
    def rows(key, n):
        return jnp.stack([gs[key][0, :n] for gs in gss])
    g_norm = jnp.stack([jnp.concatenate([g0_first if l == 0 else gss[l - 1]["g0_next"], gss[l]["g1"], gss[l]["g2"], gss[l]["g3"]], axis=0)
                        for l in range(depth)])
    gsmall = {
        "g_norm": g_norm,
        "att_q_gain": jnp.stack([gs["qgain"][0] for gs in gss]), "att_k_gain": jnp.stack([gs["kgain"][0] for gs in gss]),
        "dn_conv": jnp.stack([jnp.concatenate(gs["dn_conv"], axis=1) for gs in gss]),
        "dn_a_log": rows("alog", 8).reshape(depth, 2, N_HEAD4), "dn_dt_bias": rows("dtb", 8).reshape(depth, 2, N_HEAD4),
        "dn_norm_gain": jnp.stack([gs["ngain"][0] for gs in gss]),
        "ret_decay": rows("ret", 8).reshape(depth, 2, N_HEAD4),
        "sc_conv": jnp.stack([gs["sc_conv"] for gs in gss]),
    }
    return loss_row[0, 0], dxa, gws, gsmall, dmod


N_DEV = 8
N_XY = 4
HBM_SPEC = pl.BlockSpec(memory_space=pltpu.HBM)
VMEM_SPEC = pl.BlockSpec(memory_space=pltpu.VMEM)


def _coords():
    return lax.axis_index("x"), lax.axis_index("y"), lax.axis_index("c")


def _flip(coords, k):
    x, y, c = coords
    return (1 - x if k & 4 else x, 1 - y if k & 2 else y, 1 - c if k & 1 else c)


def allgather8(name, v):
    def body(v_ref, out_ref, send_sems, recv_sems, local_sem):
        me3 = _coords()
        me = 4 * me3[0] + 2 * me3[1] + me3[2]
        mine = pltpu.make_async_copy(v_ref, out_ref.at[me], local_sem)
        mine.start()
        sends = []
        for k in range(1, N_DEV):
            cp = pltpu.make_async_remote_copy(src_ref=v_ref, dst_ref=out_ref.at[me], send_sem=send_sems.at[k - 1],
                                              recv_sem=recv_sems.at[k - 1], device_id=_flip(me3, k), device_id_type=MESH)
            cp.start()
            sends.append(cp)
        for k in range(1, N_DEV):
            pltpu.make_async_remote_copy(src_ref=v_ref, dst_ref=out_ref.at[jnp.bitwise_xor(me, k)], send_sem=send_sems.at[k - 1],
                                         recv_sem=recv_sems.at[k - 1], device_id=_flip(me3, k), device_id_type=MESH).wait_recv()
        for cp in sends:
            cp.wait_send()
        mine.wait()

    return pl.pallas_call(
        body, name=name, out_shape=jax.ShapeDtypeStruct((N_DEV,) + v.shape, v.dtype),
        in_specs=[VMEM_SPEC], out_specs=VMEM_SPEC,
        scratch_shapes=[pltpu.SemaphoreType.DMA((N_DEV - 1,)), pltpu.SemaphoreType.DMA((N_DEV - 1,)), pltpu.SemaphoreType.DMA],
        compiler_params=pltpu.CompilerParams(vmem_limit_bytes=VMEM_LIMIT),
    )(v)


def xy_exchange(name, v, gather):
    def body(v_ref, out_ref, send_sems, recv_sems, local_sem):
        refs = (v_ref, out_ref, send_sems, recv_sems, local_sem)
        _xy_start(refs, gather)
        _xy_wait(refs, gather)

    return pl.pallas_call(
        body, name=name, out_shape=_xy_out_shape(v, gather), in_specs=[HBM_SPEC], out_specs=HBM_SPEC,
        scratch_shapes=_xy_sems(),
    )(v)


def _xy_out_shape(v, gather):
    return jax.ShapeDtypeStruct((N_XY,) + (v.shape if gather else v.shape[1:]), v.dtype)


def _xy_sems():
    return [pltpu.SemaphoreType.DMA((N_XY - 1,)), pltpu.SemaphoreType.DMA((N_XY - 1,)), pltpu.SemaphoreType.DMA]


def _xy_copies(refs, gather):
    v_ref, out_ref, send_sems, recv_sems, local_sem = refs
    me3 = _coords()
    me = 2 * me3[0] + me3[1]
    src = (lambda i: v_ref) if gather else (lambda i: v_ref.at[i])

    def remote(k, src_slot, dst_slot):
        return pltpu.make_async_remote_copy(src_ref=src(src_slot), dst_ref=out_ref.at[dst_slot], send_sem=send_sems.at[k - 1],
                                            recv_sem=recv_sems.at[k - 1], device_id=_flip(me3, 2 * k), device_id_type=MESH)

    local = pltpu.make_async_copy(src(me), out_ref.at[me], local_sem)
    sends = [remote(k, jnp.bitwise_xor(me, k), me) for k in range(1, N_XY)]
    recvs = [remote(k, me, jnp.bitwise_xor(me, k)) for k in range(1, N_XY)]
    return local, sends, recvs


def _xy_start(refs, gather):
    local, sends, _ = _xy_copies(refs, gather)
    local.start()
    for cp in sends:
        cp.start()


def _xy_wait(refs, gather):
    local, sends, recvs = _xy_copies(refs, gather)
    for cp in recvs:
        cp.wait_recv()
    for cp in sends:
        cp.wait_send()
    local.wait()


def swap_c(name, v):
    def body(v_ref, out_ref, send_sem, recv_sem):
        cp = pltpu.make_async_remote_copy(src_ref=v_ref, dst_ref=out_ref, send_sem=send_sem, recv_sem=recv_sem,
                                          device_id=_flip(_coords(), 1), device_id_type=MESH)
        cp.start()
        cp.wait()

    return pl.pallas_call(
        body, name=name, out_shape=jax.ShapeDtypeStruct(v.shape, v.dtype), in_specs=[HBM_SPEC], out_specs=HBM_SPEC,
        scratch_shapes=[pltpu.SemaphoreType.DMA, pltpu.SemaphoreType.DMA],
    )(v)


BLOCK_BYTES = 1 << 20


def _rows_block(rows, cols):
    for tr in (1024, 512, 256, 128, 64, 32, 16, 8):
        if rows % tr == 0 and tr * cols * 4 <= BLOCK_BYTES:
            return tr
    return rows


def sum_slots(name, r):
    s, rows, cols = r.shape
    tr = _rows_block(rows, cols)

    def body(r_ref, o_ref):
        tot = r_ref[0].astype(F32)
        for i in range(1, s):
            tot = tot + r_ref[i].astype(F32)
        o_ref[...] = tot

    return pl.pallas_call(
        body, name=name, grid=(rows // tr,), in_specs=[pl.BlockSpec((s, tr, cols), lambda i: (0, i, 0))],
        out_specs=pl.BlockSpec((tr, cols), lambda i: (i, 0)), out_shape=jax.ShapeDtypeStruct((rows, cols), F32),
        compiler_params=_params(1),
    )(r)


def adamw(name, w, m, v, parts):
    rows, cols = w.shape
    tr = _rows_block(rows, cols)
    n = len(parts)
    c1 = 1.0 - ADAM_B1 ** ADAM_STEP
    c2 = 1.0 - ADAM_B2 ** ADAM_STEP

    def body(*refs):
        w_ref, m_ref, v_ref = refs[:3]
        g_ref, d_ref, nm_ref, nv_ref = refs[3 + n:]
        g = refs[3][...]
        for r in refs[4:3 + n]:
            g = g + r[...]
        nm = ADAM_B1 * m_ref[...] + (1.0 - ADAM_B1) * g
        nv = ADAM_B2 * v_ref[...] + (1.0 - ADAM_B2) * jnp.square(g)
        d_ref[...] = -ADAM_LR * ((nm / c1) / (jnp.sqrt(nv / c2) + ADAM_EPS) + ADAM_WD * w_ref[...])
        g_ref[...], nm_ref[...], nv_ref[...] = g, nm, nv

    spec = pl.BlockSpec((tr, cols), lambda i: (i, 0))
    sds = jax.ShapeDtypeStruct((rows, cols), F32)
    return pl.pallas_call(
        body, name=name, grid=(rows // tr,), in_specs=[spec] * (3 + n), out_specs=[spec] * 4, out_shape=[sds] * 4,
        compiler_params=_params(1),
    )(w, m, v, *parts)


MOD_COLS = 512


def mod_fwd(call, w_mod, b_sh):
    depth, _, cols = w_mod.shape
    nr = call.shape[0]

    def body(c_ref, w_ref, b_ref, o_ref):
        o_ref[...] = _dg(_silu(c_ref[...]), w_ref[...], 1, 0) + b_ref[...]

    return pl.pallas_call(
        body, name="mod_fwd", grid=(depth, cols // MOD_COLS),
        in_specs=[pl.BlockSpec((nr, D_MODEL), lambda l, j: (0, 0)), pl.BlockSpec((None, D_MODEL, MOD_COLS), lambda l, j: (l, 0, j)),
                  pl.BlockSpec((None, 1, MOD_COLS), lambda l, j: (l, 0, j))],
        out_specs=pl.BlockSpec((None, nr, MOD_COLS), lambda l, j: (l, 0, j)),
        out_shape=jax.ShapeDtypeStruct((depth, nr, cols), F32), compiler_params=_params(2),
    )(call, w_mod, b_sh)


def mod_bwd(call, c_ctx, d_lat, d_ctx, w_mod, ctx_row):
    depth, _, cols = w_mod.shape
    nr, ns = call.shape[0], d_ctx.shape[1]

    def body(c_ref, cc_ref, dl_ref, dc_ref, w_ref, gw_ref, gc_ref):
        crow = jnp.sum(dc_ref[...], axis=0, keepdims=True)
        row = lax.broadcasted_iota(jnp.int32, (nr, 1), 0)
        dm = jnp.where(row == ctx_row, crow, dl_ref[...])
        gw_ref[...] = _dg(_silu(c_ref[...]), dm, 0, 0)
        ds = jnp.sum(_dg(jnp.broadcast_to(crow, (8, MOD_COLS)), w_ref[...], 1, 1), axis=0, keepdims=True) * 0.125
        _, vjp = jax.vjp(_silu, cc_ref[...])
        (part,) = vjp(ds)
        first = (pl.program_id(0) == 0) & (pl.program_id(1) == 0)

        @pl.when(first)
        def _():
            gc_ref[...] = part

        @pl.when(jnp.logical_not(first))
        def _():
            gc_ref[...] += part

    return pl.pallas_call(
        body, name="mod_bwd", grid=(depth, cols // MOD_COLS),
        in_specs=[pl.BlockSpec((nr, D_MODEL), lambda l, j: (0, 0)), pl.BlockSpec((1, D_MODEL), lambda l, j: (0, 0)),
                  pl.BlockSpec((None, nr, MOD_COLS), lambda l, j: (l, 0, j)), pl.BlockSpec((None, ns, MOD_COLS), lambda l, j: (l, 0, j)),
                  pl.BlockSpec((None, D_MODEL, MOD_COLS), lambda l, j: (l, 0, j))],
        out_specs=[pl.BlockSpec((None, D_MODEL, MOD_COLS), lambda l, j: (l, 0, j)), pl.BlockSpec((1, D_MODEL), lambda l, j: (0, 0))],
        out_shape=[jax.ShapeDtypeStruct((depth, D_MODEL, cols), F32), jax.ShapeDtypeStruct((1, D_MODEL), F32)],
        compiler_params=_params(2),
    )(call, c_ctx, d_lat, d_ctx, w_mod)


def bmod_grad(dm_all):
    ndev, depth, ns, cols = dm_all.shape

    def body(d_ref, o_ref):
        tot = d_ref[0]
        for i in range(1, ndev):
            tot = tot + d_ref[i]
        o_ref[...] = jnp.sum(tot, axis=0, keepdims=True)

    return pl.pallas_call(
        body, name="bmod_grad", grid=(depth,), in_specs=[pl.BlockSpec((ndev, None, ns, cols), lambda l: (0, l, 0, 0))],
        out_specs=pl.BlockSpec((None, 1, cols), lambda l: (l, 0, 0)), out_shape=jax.ShapeDtypeStruct((depth, 1, cols), F32),
        compiler_params=_params(1),
    )(dm_all)


def small_reduce(gathered, rows_all):
    ndev, rows, lanes = gathered.shape

    def body(g_ref, o_ref):
        tot = g_ref[0, 0:rows_all]
        for i in range(1, ndev):
            tot = tot + g_ref[i, 0:rows_all]
        o_ref[0:rows_all] = tot
        part = g_ref[0, rows_all:rows]
        for i in range(2, ndev, 2):
            part = part + g_ref[i, rows_all:rows]
        o_ref[rows_all:rows] = part

    return pl.pallas_call(body, name="small_reduce", out_shape=jax.ShapeDtypeStruct((rows, lanes), F32),
                          in_specs=[VMEM_SPEC], out_specs=VMEM_SPEC)(gathered)


def pack_rows(arrays, row_multiple=8):
    flat = jnp.concatenate([a.reshape(-1).astype(F32) for a in arrays])
    per = LANES * row_multiple
    padded = -(-flat.shape[0] // per) * per
    return jnp.pad(flat, (0, padded - flat.shape[0])).reshape(-1, LANES)


def unpack_rows(buf, shapes):
    flat, out, off = buf.reshape(-1), [], 0
    for s in shapes:
        n = int(np.prod(s))
        out.append(flat[off:off + n].reshape(s))
        off += n
    return out


SMALL_SHARDED = ("g_norm", "dn_conv", "sc_conv")
SMALL_ORDER = ("g_norm", "att_q_gain", "att_k_gain", "dn_conv", "dn_a_log", "dn_dt_bias", "dn_norm_gain", "ret_decay", "sc_conv")


def kernel(x, c, ctx, c_ctx, w_mod, b_mod, g_norm, w_in, att_q_gain, att_k_gain, dn_conv, dn_a_log, dn_dt_bias, dn_norm_gain, ret_decay, sc_conv, w_branch, w_out, w_mlp_in, w_mlp_out, loss_target, m_c_ctx, m_w_mod, m_b_mod, m_g_norm, m_w_in, m_att_q_gain, m_att_k_gain, m_dn_conv, m_dn_a_log, m_dn_dt_bias, m_dn_norm_gain, m_ret_decay, m_sc_conv, m_w_branch, m_w_out, m_w_mlp_in, m_w_mlp_out, v_c_ctx, v_w_mod, v_b_mod, v_g_norm, v_w_in, v_att_q_gain, v_att_k_gain, v_dn_conv, v_dn_a_log, v_dn_dt_bias, v_dn_norm_gain, v_ret_decay, v_sc_conv, v_w_branch, v_w_out, v_w_mlp_in, v_w_mlp_out):
    wts = dict(c_ctx=c_ctx, w_mod=w_mod, b_mod=b_mod, g_norm=g_norm, w_in=w_in, att_q_gain=att_q_gain, att_k_gain=att_k_gain,
               dn_conv=dn_conv, dn_a_log=dn_a_log, dn_dt_bias=dn_dt_bias, dn_norm_gain=dn_norm_gain, ret_decay=ret_decay,
               sc_conv=sc_conv, w_branch=w_branch, w_out=w_out, w_mlp_in=w_mlp_in, w_mlp_out=w_mlp_out)
    mom = dict(c_ctx=m_c_ctx, w_mod=m_w_mod, b_mod=m_b_mod, g_norm=m_g_norm, w_in=m_w_in, att_q_gain=m_att_q_gain,
               att_k_gain=m_att_k_gain, dn_conv=m_dn_conv, dn_a_log=m_dn_a_log, dn_dt_bias=m_dn_dt_bias,
               dn_norm_gain=m_dn_norm_gain, ret_decay=m_ret_decay, sc_conv=m_sc_conv, w_branch=m_w_branch, w_out=m_w_out,
               w_mlp_in=m_w_mlp_in, w_mlp_out=m_w_mlp_out)
    var = dict(c_ctx=v_c_ctx, w_mod=v_w_mod, b_mod=v_b_mod, g_norm=v_g_norm, w_in=v_w_in, att_q_gain=v_att_q_gain,
               att_k_gain=v_att_k_gain, dn_conv=v_dn_conv, dn_a_log=v_dn_a_log, dn_dt_bias=v_dn_dt_bias,
               dn_norm_gain=v_dn_norm_gain, ret_decay=v_ret_decay, sc_conv=v_sc_conv, w_branch=v_w_branch, w_out=v_w_out,
               w_mlp_in=v_w_mlp_in, w_mlp_out=v_w_mlp_out)
    names = list(wts)
    depth, bsz, seq, lc = w_mod.shape[0], x.shape[0], x.shape[1], ctx.shape[1]
    g = Geo(bsz, lc + seq, lc)
    xi, yi, ci = _coords()
    dev, xy = 4 * xi + 2 * yi + ci, 2 * xi + yi
    n_batch = N_DEV * bsz
    nr = -(-(n_batch + 1) // 16) * 16
    mod_cols = w_mod.shape[2]

    c_all = allgather8("gather_c", c).reshape(n_batch, D_MODEL)
    call = jnp.concatenate([c_all, c_ctx[None], jnp.zeros((nr - n_batch - 1, D_MODEL), F32)], axis=0)
    b_sh = lax.dynamic_slice_in_dim(b_mod, xy * mod_cols, mod_cols, axis=1)[:, None, :]
    mod_sh = mod_fwd(call, w_mod, b_sh)
    mod_g = allgather8("gather_mod", mod_sh.reshape(depth * nr, mod_cols)).reshape(N_XY, 2, depth, nr, mod_cols)[:, 0]
    mod_all = mod_g.transpose(1, 2, 0, 3).reshape(depth, nr, N_XY * mod_cols)
    mod_lat = lax.dynamic_slice_in_dim(mod_all, dev * bsz, bsz, axis=1).reshape(depth, bsz, 6, D_MODEL)
    mod_ctx = jnp.broadcast_to(mod_all[:, n_batch].reshape(depth, 1, 6, D_MODEL), (depth, bsz, 6, D_MODEL))
    mod = jnp.stack([mod_ctx, mod_lat], axis=2)

    sm_shapes = [wts[k].shape for k in SMALL_SHARDED]
    sm_g = allgather8("gather_small", pack_rows([wts[k] for k in SMALL_SHARDED])).reshape(N_XY, 2, -1)[:, 0]
    small = {k: wts[k] for k in SMALL_ORDER}
    for k, parts in zip(SMALL_SHARDED, zip(*[unpack_rows(sm_g[p], sm_shapes) for p in range(N_XY)])):
        small[k] = jnp.concatenate(parts, axis=-1)

    shard_shapes = [wts[k].shape[1:] for k, _ in BIG]
    packs = [pack_layer([wts[k][l].astype(BF16) for k, _ in BIG]) for l in range(depth)]

    xa = jnp.concatenate([ctx, x], axis=1)
    loss_part, dxa, recvs, gsmall, dmod = model_step(g, xa, loss_target, mod, packs, shard_shapes, small)
    loss = lax.psum(loss_part, ("x", "y", "c"))
    grad_x = dxa[:, lc:]

    grads, deltas, new_m, new_v = {}, {}, {}, {}

    def update(k, parts, shape2d):
        res = adamw("adamw_" + k, wts[k].reshape(shape2d), mom[k].reshape(shape2d), var[k].reshape(shape2d), parts)
        grads[k], deltas[k], new_m[k], new_v[k] = (r.reshape(wts[k].shape) for r in res)

    mine, theirs = [], []
    for l in range(depth):
        part = sum_slots("sum_w", recvs[l])
        mine.append(unpack_layer(part, shard_shapes))
        theirs.append(unpack_layer(swap_c("swap_w", part), shard_shapes))
    for i, (k, _) in enumerate(BIG):
        shape2d = (-1, wts[k].shape[-1])
        update(k, [jnp.stack([m[i] for m in mine]).reshape(shape2d), jnp.stack([t[i] for t in theirs]).reshape(shape2d)], shape2d)

    dm_mine = jnp.concatenate([dmod[:, :, 1], dmod[:, :, 0]], axis=1).reshape(depth * 2 * bsz, 6 * D_MODEL)
    dm_all = allgather8("gather_dmod", dm_mine).reshape(N_DEV, depth, 2 * bsz, 6 * D_MODEL)
    gb = bmod_grad(dm_all).reshape(depth, 6 * D_MODEL)
    dm_cols = lax.dynamic_slice_in_dim(dm_all, xy * mod_cols, mod_cols, axis=3)
    d_lat = dm_cols[:, :, :bsz].transpose(1, 0, 2, 3).reshape(depth, n_batch, mod_cols)
    d_lat = jnp.pad(d_lat, ((0, 0), (0, nr - n_batch), (0, 0)))
    d_ctx = dm_cols[:, :, bsz:].transpose(1, 0, 2, 3).reshape(depth, n_batch, mod_cols)
    gw_mod, gc_part = mod_bwd(call, c_ctx[None], d_lat, d_ctx, w_mod, n_batch)
    update("w_mod", [gw_mod.reshape(depth * D_MODEL, mod_cols)], (depth * D_MODEL, mod_cols))
    update("b_mod", [gb], b_mod.shape)

    pack_all = pack_rows([gsmall[k] for k in SMALL_ORDER])
    pack_xy = pack_rows([gc_part])
    rows_all = pack_all.shape[0]
    tot = small_reduce(allgather8("gather_gsmall", jnp.concatenate([pack_all, pack_xy], axis=0)), rows_all)
    gtot = dict(zip(SMALL_ORDER, unpack_rows(tot[:rows_all], [gsmall[k].shape for k in SMALL_ORDER])))
    gtot["c_ctx"] = unpack_rows(tot[rows_all:], [c_ctx.shape])[0]
    for k in SMALL_SHARDED:
        width = wts[k].shape[-1]
        gtot[k] = lax.dynamic_slice_in_dim(gtot[k], xy * width, width, axis=gtot[k].ndim - 1)
    sm_names = ("c_ctx",) + SMALL_ORDER
    sm_shapes = [wts[k].shape for k in sm_names]
    res = adamw("adamw_small", pack_rows([wts[k] for k in sm_names]), pack_rows([mom[k] for k in sm_names]),
                pack_rows([var[k] for k in sm_names]), [pack_rows([gtot[k] for k in sm_names])])
    for dst, buf in zip((grads, deltas, new_m, new_v), res):
        dst.update(zip(sm_names, unpack_rows(buf, sm_shapes)))

    return (loss, grad_x, *[grads[k] for k in names], *[deltas[k] for k in names], *[new_m[k] for k in names],
            *[new_v[k] for k in names])
```

```python
import functools
import math

import numpy as np
import jax
import jax.numpy as jnp
from jax import lax
from jax.experimental import pallas as pl
from jax.experimental.pallas import tpu as pltpu

F32, BF16 = jnp.float32, jnp.bfloat16
HIGHEST = lax.Precision.HIGHEST
MESH = pl.DeviceIdType.MESH

D_MODEL = 1024
GRID_W = 64
N_BRANCH = 4
BRANCH_W = 512
HEAD64 = 64
HEAD128 = 128
N_HEAD4 = 4
ATT_HEADS = 8
ATT_KV_HEADS = 2
CHUNK = 64
MLP_HIDDEN = 4 * D_MODEL
ROPE_THETA = 10000.0
EPS = 1e-6
N_IN = 10000
ADAM_LR, ADAM_B1, ADAM_B2, ADAM_EPS, ADAM_WD, ADAM_STEP = 0.001, 0.9, 0.999, 1e-08, 0.01, 10

LANES = 128
VMEM_LIMIT = 56 * 1024 * 1024

_SEGS = (
    ("gates", 5904, 4096),
    ("att_q", 0, 512), ("dn_q", 768, 512), ("dn_k", 1280, 512), ("dn_v", 1792, 512), ("dn_z", 2304, 512),
    ("ret_v", 3344, 512), ("ret_g", 3856, 512), ("sc_b", 4368, 512), ("sc_c", 4880, 512), ("sc_x", 5392, 512),
    ("ret_q", 2832, 256), ("ret_k", 3088, 256),
    ("att_k", 512, 128), ("att_v", 640, 128),
    ("narrow", 2816, 16),
)
NZ = 10240


def _seg_offsets():
    off, out = 0, {}
    for name, _, width in _SEGS:
        out[name] = off
        off += width
    return out


ZOFF = _seg_offsets()


def _pick(n, cands):
    for c in cands:
        if n % c == 0:
            return c
    return n


def _dg(a, b, ca, cb, batch=False):
    dn = (((ca,), (cb,)), ((0,), (0,))) if batch else (((ca,), (cb,)), ((), ()))
    return lax.dot_general(a.astype(BF16), b.astype(BF16), dn, preferred_element_type=F32)


@functools.partial(jax.custom_vjp, nondiff_argnums=(2, 3))
def _mm(a, b, ca, cb):
    return _dg(a, b, ca, cb)


def _mm_fwd(a, b, ca, cb):
    return _dg(a, b, ca, cb), (a, b)


def _mm_bwd(ca, cb, res, g):
    a, b = res
    if ca == 1:
        da = _mm(g, b, 1, 1) if cb == 0 else _mm(g, b, 1, 0)
    else:
        da = _mm(b, g, 1, 1) if cb == 0 else _mm(b, g, 0, 1)
    if cb == 0:
        db = _mm(a, g, 0, 0) if ca == 1 else _mm(a, g, 1, 0)
    else:
        db = _mm(g, a, 0, 0) if ca == 1 else _mm(g, a, 0, 1)
    return da.astype(a.dtype), db.astype(b.dtype)


_mm.defvjp(_mm_fwd, _mm_bwd)


@functools.partial(jax.custom_vjp, nondiff_argnums=(2, 3))
def _bmm(a, b, ca, cb):
    return _dg(a, b, ca, cb, True)


def _bmm_fwd(a, b, ca, cb):
    return _dg(a, b, ca, cb, True), (a, b)


def _bmm_bwd(ca, cb, res, g):
    a, b = res
    if ca == 2:
        da = _bmm(g, b, 2, 2) if cb == 1 else _bmm(g, b, 2, 1)
    else:
        da = _bmm(b, g, 2, 2) if cb == 1 else _bmm(b, g, 1, 2)
    if cb == 1:
        db = _bmm(a, g, 1, 1) if ca == 2 else _bmm(a, g, 2, 1)
    else:
        db = _bmm(g, a, 1, 1) if ca == 2 else _bmm(g, a, 1, 2)
    return da.astype(a.dtype), db.astype(b.dtype)


_bmm.defvjp(_bmm_fwd, _bmm_bwd)


def _split_bf16(x):
    hi = x.astype(BF16)
    lo = (x - hi.astype(F32)).astype(BF16)
    return hi, lo


def _mm3(a, b, ca, cb):
    ah, al = _split_bf16(a)
    bh, bl = _split_bf16(b)
    dn = (((ca,), (cb,)), ((), ()))
    d = lambda u, v: lax.dot_general(u, v, dn, preferred_element_type=F32)
    return d(ah, bh) + (d(ah, bl) + d(al, bh))


def _mm_exact(a, b):
    return jnp.dot(a, b, precision=HIGHEST, preferred_element_type=F32)


class Arg:
    def __init__(self, x, block, imap, diff=True, first=None, gdtype=F32, pieces=None, gshape=None, gimap=None):
        self.x, self.block, self.imap = x, tuple(block), imap
        self.diff, self.first, self.gdtype, self.pieces = diff, first, gdtype, pieces
        self.gshape = tuple(x.shape) if gshape is None else tuple(gshape)
        self.gimap = imap if gimap is None else gimap

    def spec(self):
        return pl.BlockSpec(self.block, self.imap)

    def gspec(self):
        return pl.BlockSpec(self.block, self.gimap)


class Out:
    def __init__(self, shape, dtype, block, imap, pieces=None):
        self.shape, self.dtype, self.block, self.imap, self.pieces = tuple(shape), dtype, tuple(block), imap, pieces

    def spec(self):
        return pl.BlockSpec(self.block, self.imap)

    def sds(self):
        return jax.ShapeDtypeStruct(self.shape, self.dtype)


def _lanes(ref, s, w):
    return (slice(None),) * (len(ref.shape) - 1) + (slice(s, s + w),)


def _load(ref, pieces):
    if pieces is None:
        return ref[...]
    return tuple(ref[_lanes(ref, s, w)] for s, w in pieces)


def _store(ref, val, pieces, accumulate=False):
    if pieces is None:
        if accumulate:
            ref[...] += val.astype(ref.dtype)
        else:
            ref[...] = val.astype(ref.dtype)
        return
    if not accumulate:
        covered = sum(w for _, w in pieces)
        if covered != ref.shape[-1]:
            ref[...] = jnp.zeros(ref.shape, ref.dtype)
    for (s, w), v in zip(pieces, val):
        if accumulate:
            ref[_lanes(ref, s, w)] += v.astype(ref.dtype)
        else:
            ref[_lanes(ref, s, w)] = v.astype(ref.dtype)


def _params(n_grid):
    return pltpu.CompilerParams(dimension_semantics=("arbitrary",) * n_grid, vmem_limit_bytes=VMEM_LIMIT)


def vfwd(name, f, grid, args, outs):
    n_in = len(args)

    def body(*refs):
        ids = tuple(pl.program_id(i) for i in range(len(grid)))
        vals = [_load(r, a.pieces) for r, a in zip(refs[:n_in], args)]
        res = f(ids, *vals)
        for r, o, spec in zip(refs[n_in:], res, outs):
            _store(r, o, spec.pieces)

    return pl.pallas_call(
        body, name=name, grid=grid,
        in_specs=[a.spec() for a in args], out_specs=[o.spec() for o in outs],
        out_shape=[o.sds() for o in outs], compiler_params=_params(len(grid)),
    )(*[a.x for a in args])


def vbwd(name, f, grid, args, outs, cts):
    n_in = len(args)
    diff_idx = [i for i, a in enumerate(args) if a.diff]
    ct_flat = [c for per_out in cts for c in per_out]
    ct_specs = [o.spec() for o, per_out in zip(outs, cts) for _ in per_out]
    n_ct = len(ct_flat)

    def body(*refs):
        ids = tuple(pl.program_id(i) for i in range(len(grid)))
        vals = [_load(r, a.pieces) for r, a in zip(refs[:n_in], args)]
        ct_refs = refs[n_in:n_in + n_ct]
        g_refs = refs[n_in + n_ct:]

        def g(*dvals):
            full = list(vals)
            for i, v in zip(diff_idx, dvals):
                full[i] = v
            return tuple(f(ids, *full))

        _, vjp = jax.vjp(g, *[vals[i] for i in diff_idx])
        ct_vals, k = [], 0
        for o, per_out in zip(outs, cts):
            tot = None
            for _ in per_out:
                v = _load(ct_refs[k], o.pieces)
                k += 1
                v = jax.tree.map(lambda t: t.astype(F32), v)
                tot = v if tot is None else jax.tree.map(jnp.add, tot, v)
            ct_vals.append(tot)
        grads = vjp(tuple(ct_vals))
        for gr, gv, i in zip(g_refs, grads, diff_idx):
            a = args[i]
            if a.first is None:
                _store(gr, gv, a.pieces)
            else:
                is_first = a.first(ids)

                @pl.when(is_first)
                def _():
                    _store(gr, gv, a.pieces)

                @pl.when(jnp.logical_not(is_first))
                def _():
                    _store(gr, gv, a.pieces, accumulate=True)

    g_specs = [args[i].gspec() for i in diff_idx]
    g_shapes = [jax.ShapeDtypeStruct(args[i].gshape, args[i].gdtype) for i in diff_idx]
    return pl.pallas_call(
        body, name=name, grid=grid,
        in_specs=[a.spec() for a in args] + ct_specs, out_specs=g_specs, out_shape=g_shapes,
        compiler_params=_params(len(grid)),
    )(*[a.x for a in args], *ct_flat)


def matmul(name, a, b, trans_a=False, trans_b=False, out_dtype=F32):
    if trans_a:
        kdim, m = a.shape
    else:
        m, kdim = a.shape
    n = b.shape[0] if trans_b else b.shape[1]
    assert b.shape[1 if trans_b else 0] == kdim
    assert out_dtype == F32
    tm = _pick(m, (1024, 512, 256, 192, 128, 64))
    tn = _pick(n, (1024, 512, 256, 128))
    tk = _pick(kdim, (1024, 512, 256, 192, 128, 64))
    nk = kdim // tk

    def body(a_ref, b_ref, o_ref):
        part = _dg(a_ref[...], b_ref[...], 0 if trans_a else 1, 1 if trans_b else 0)
        if nk == 1:
            o_ref[...] = part
        else:
            k = pl.program_id(2)

            @pl.when(k == 0)
            def _():
                o_ref[...] = part

            @pl.when(k > 0)
            def _():
                o_ref[...] += part

    a_spec = pl.BlockSpec((tk, tm), lambda i, j, k: (k, i)) if trans_a else pl.BlockSpec((tm, tk), lambda i, j, k: (i, k))
    b_spec = pl.BlockSpec((tn, tk), lambda i, j, k: (j, k)) if trans_b else pl.BlockSpec((tk, tn), lambda i, j, k: (k, j))
    return pl.pallas_call(
        body, name=name, grid=(m // tm, n // tn, nk),
        in_specs=[a_spec, b_spec],
        out_specs=pl.BlockSpec((tm, tn), lambda i, j, k: (i, j)),
        out_shape=jax.ShapeDtypeStruct((m, n), out_dtype),
        compiler_params=pltpu.CompilerParams(dimension_semantics=("parallel", "parallel", "arbitrary"),
                                             vmem_limit_bytes=VMEM_LIMIT),
    )(a, b)


def _rms(x, gain):
    return x * lax.rsqrt(jnp.mean(x * x, axis=-1, keepdims=True) + EPS) * gain


def _silu(x):
    return x * jax.nn.sigmoid(x)


def f_modnorm(ids, x, gain, mod):
    return (_rms(x, gain) * (1.0 + mod[1:2]) + mod[0:1],)


def f_resnorm(ids, x, y, g_res, g_next, mod):
    x_new = x + mod[0:1] * _rms(y, g_res)
    return x_new, _rms(x_new, g_next) * (1.0 + mod[2:3]) + mod[1:2]


def f_resid(ids, x, y, g_res, mod):
    return (x + mod[0:1] * _rms(y, g_res),)


def f_act(ids, a):
    r = jnp.maximum(a, 0.0)
    return (r * r,)


def _head_consts(width, head):
    i = lax.broadcasted_iota(jnp.int32, (width, width), 0)
    j = lax.broadcasted_iota(jnp.int32, (width, width), 1)
    shift = int(math.log2(head))
    same = (i >> shift) == (j >> shift)
    group = jnp.where(same, 1.0 / head, 0.0).astype(F32)
    half = head // 2
    ii, jj = i & (head - 1), j & (head - 1)
    rot = jnp.where(same & (ii == jj + half) & (jj < half), -1.0, 0.0) + jnp.where(same & (ii + half == jj) & (jj >= half), 1.0, 0.0)
    ti = lax.broadcasted_iota(jnp.int32, (head, width), 0)
    tj = lax.broadcasted_iota(jnp.int32, (head, width), 1)
    tile = jnp.where(ti == (tj & (head - 1)), 1.0, 0.0).astype(F32)
    return group, rot.astype(F32), tile


def _rope(x, cos, sin, rot):
    return x * cos + _mm_exact(x, rot) * sin


def _softplus(x):
    return jnp.maximum(x, 0.0) + jnp.log(1.0 + jnp.exp(-jnp.abs(x)))


def f_prep(ids, zq, zk, rq, rk, zn, qgain, kgain, alog, dtb, cos, sin):
    grp_q, rot_q, tile_q = _head_consts(ATT_HEADS * HEAD64, HEAD64)
    grp_k, rot_k, tile_k = _head_consts(ATT_KV_HEADS * HEAD64, HEAD64)
    grp_r, rot_r, _ = _head_consts(N_HEAD4 * HEAD64, HEAD64)
    wq, wk, wr = zq.shape[-1], zk.shape[-1], rq.shape[-1]
    qn = zq * lax.rsqrt(_mm_exact(zq * zq, grp_q) + EPS) * _mm_exact(qgain, tile_q)
    kn = zk * lax.rsqrt(_mm_exact(zk * zk, grp_k) + EPS) * _mm_exact(kgain, tile_k)
    q_att = _rope(qn, cos[:, :wq], sin[:, :wq], rot_q) * (HEAD64 ** -0.5)
    k_att = _rope(kn, cos[:, :wk], sin[:, :wk], rot_k)
    q_ret = _rope(rq, cos[:, :wr], sin[:, :wr], rot_r)
    k_ret = _rope(rk * (HEAD64 ** -0.5), cos[:, :wr], sin[:, :wr], rot_r)
    lane = lax.broadcasted_iota(jnp.int32, zn.shape, 1)
    log_a = -jnp.exp(alog) * _softplus(zn + dtb)
    gates = jnp.where(lane < 8, log_a, jnp.where(lane < 16, jax.nn.sigmoid(zn), 0.0))
    return q_att, k_att, q_ret, k_ret, gates


def make_f_attn(tq):
    def f_attn(ids, q, k, v):
        outs = []
        per = ATT_HEADS // ATT_KV_HEADS
        for g in range(ATT_KV_HEADS):
            qg = jnp.concatenate(q[g * per:(g + 1) * per], axis=0)
            s = _mm(qg, k[g], 1, 1)
            e = jnp.exp(s - lax.stop_gradient(jnp.max(s, axis=-1, keepdims=True)))
            o = _mm(e, v[g], 1, 0) * (1.0 / jnp.sum(e, axis=-1, keepdims=True))
            outs += [o[i * tq:(i + 1) * tq] for i in range(per)]
        return (tuple(outs),)
    return f_attn


def _roll_rows(x, shift):
    return pltpu.roll(x, shift, 0)


def make_shifts(t, lc):
    def _down(x):
        row = lax.broadcasted_iota(jnp.int32, x.shape, 0)
        return jnp.where((row == 0) | (row == lc), 0.0, _roll_rows(x, 1))

    def _up(x):
        row = lax.broadcasted_iota(jnp.int32, x.shape, 0)
        return jnp.where((row == lc - 1) | (row == t - 1), 0.0, _roll_rows(x, t - 1))

    @jax.custom_vjp
    def down(x):
        return _down(x)

    @jax.custom_vjp
    def up(x):
        return _up(x)

    down.defvjp(lambda x: (_down(x), None), lambda _, g: (up(g),))
    up.defvjp(lambda x: (_up(x), None), lambda _, g: (down(g),))
    return down, up


def make_conv3(t, lc):
    down, up = make_shifts(t, lc)

    def conv3(x, w):
        return w[0:1] * down(x) + w[1:2] * x + w[2:3] * up(x)
    return conv3


def make_f_dnprep(t, lc):
    conv3 = make_conv3(t, lc)

    def l2n(x):
        return x * lax.rsqrt(jnp.sum(x * x, axis=-1, keepdims=True) + EPS)

    def f_dnprep(ids, q, k, v, wq, wk, wv):
        qn = l2n(_silu(conv3(q, wq))) * (HEAD128 ** -0.5)
        kn = l2n(_silu(conv3(k, wk)))
        return qn, kn, _silu(conv3(v, wv))
    return f_dnprep


def make_f_shortconv(t, lc):
    conv3 = make_conv3(t, lc)

    def f_shortconv(ids, b, c, x, w):
        return (b * conv3(c * x, w),)
    return f_shortconv


def f_finish(ids, o_dn_f, o_dn_b, z_dn, o_rt_f, o_rt_b, g_rt, ngain):
    y_dn, y_rt = [], []
    for h in range(N_HEAD4):
        o = o_dn_f[h] + o_dn_b[h]
        y_dn.append(_rms(o, ngain) * _silu(z_dn[h]))
        r = o_rt_f[h] + o_rt_b[h]
        mu = jnp.mean(r, axis=-1, keepdims=True)
        var = jnp.mean(jnp.square(r - mu), axis=-1, keepdims=True)
        y_rt.append((r - mu) * lax.rsqrt(var + EPS) * _silu(g_rt[h]))
    return tuple(y_dn), tuple(y_rt)


def f_merge(ids, p0, p1, p2, p3, gates):
    u = jax.nn.sigmoid(gates[0]) * p0
    for g, p in zip(gates[1:], (p1, p2, p3)):
        u = u + jax.nn.sigmoid(g) * p
    return (u,)


def _stack_masks():
    n = N_HEAD4 * CHUNK
    i = lax.broadcasted_iota(jnp.int32, (n, n), 0)
    j = lax.broadcasted_iota(jnp.int32, (n, n), 1)
    same = (i >> 6) == (j >> 6)
    pi, pj = i & (CHUNK - 1), j & (CHUNK - 1)
    return same, pi, pj


def _inv_unit_lower(low):
    n = low.shape[0]
    eye = jnp.where(lax.broadcasted_iota(jnp.int32, (n, n), 0) == lax.broadcasted_iota(jnp.int32, (n, n), 1), 1.0, 0.0).astype(F32)
    m = -low
    p = eye + m
    for _ in range(int(math.log2(CHUNK)) - 1):
        m = _mm3(m, m, 1, 0)
        p = p + _mm3(p, m, 1, 0)
    return p


@jax.custom_vjp
def _tri_solve(low, rhs):
    return _mm3(_inv_unit_lower(low), rhs, 1, 0)


def _tri_solve_fwd(low, rhs):
    inv = _inv_unit_lower(low)
    x = _mm3(inv, rhs, 1, 0)
    return x, (inv, x)


def _tri_solve_bwd(res, g):
    inv, x = res
    d_rhs = _mm3(inv, g, 0, 0)
    return -_mm3(d_rhs, x, 1, 1), d_rhs


_tri_solve.defvjp(_tri_solve_fwd, _tri_solve_bwd)


def _heads3(x):
    return x.reshape(N_HEAD4, CHUNK, x.shape[-1])


def _time_masks(direction):
    same, pi, pj = _stack_masks()
    if direction == 0:
        return same, same & (pi >= pj), same & (pi > pj), same & (pj == CHUNK - 1), pi - pj
    return same, same & (pi <= pj), same & (pi < pj), same & (pj == 0), pj - pi


def make_dn_chunk(direction):
    def dn_chunk(s, toks, params):
        q, k, v, la, beta = toks
        qs, ks, vs = (jnp.concatenate(t, axis=0) for t in (q, k, v))
        la, beta = jnp.concatenate(la, axis=0), jnp.concatenate(beta, axis=0)
        n = N_HEAD4 * CHUNK
        same, incl, strict, last, _ = _time_masks(direction)
        g = jnp.sum(jnp.where(incl, jnp.broadcast_to(la, (n, n)).T, 0.0), axis=1, keepdims=True)
        gb = jnp.broadcast_to(g, (n, n))
        gbt = gb.T
        dec_incl = jnp.where(incl, jnp.exp(jnp.where(incl, gb - gbt, 0.0)), 0.0)
        dec_strict = jnp.where(strict, dec_incl, 0.0)
        low = beta * _mm(ks, ks, 1, 1) * dec_strict
        eg = jnp.exp(g)
        sol = _tri_solve(low, jnp.concatenate([beta * vs, (beta * eg) * ks], axis=1))
        w_v, w_k = sol[:, :HEAD128], sol[:, HEAD128:]
        a_qk = _mm(qs, ks, 1, 1) * dec_incl
        g_last = jnp.sum(jnp.where(last, gbt, 0.0), axis=1, keepdims=True)
        k_dec = ks * jnp.exp(g_last - g)
        u = w_v - _bmm(_heads3(w_k), s, 2, 1).reshape(n, HEAD128)
        o = _bmm(_heads3(qs * eg), s, 2, 1).reshape(n, HEAD128) + _mm(a_qk, u, 1, 0)
        decay = jnp.exp(jnp.mean(_heads3(g_last), axis=1, keepdims=True))
        s_new = s * decay + _bmm(_heads3(k_dec), _heads3(u), 1, 1)
        return (tuple(o[h * CHUNK:(h + 1) * CHUNK] for h in range(N_HEAD4)),), s_new
    return dn_chunk


def make_ret_chunk(direction):
    def ret_chunk(s, toks, params):
        q, k, v = toks
        (decay_log,) = params
        qs, ks, vs = (jnp.concatenate(t, axis=0) for t in (q, k, v))
        n = N_HEAD4 * CHUNK
        _, incl, _, _, rel = _time_masks(direction)
        row_head = lax.broadcasted_iota(jnp.int32, (n, 1), 0) >> 6
        lg = jnp.zeros((n, 1), F32)
        for h in range(N_HEAD4):
            lg = jnp.where(row_head == h, -jnp.exp(decay_log[direction * N_HEAD4 + h]), lg)
        row = (lax.broadcasted_iota(jnp.int32, (n, 1), 0) & (CHUNK - 1)).astype(F32)
        pos = row if direction == 0 else CHUNK - 1.0 - row
        dmask = jnp.where(incl, jnp.exp(jnp.where(incl, rel.astype(F32) * lg, 0.0)), 0.0)
        o = _mm(_mm(qs, ks, 1, 1) * dmask, vs, 1, 0)
        k_dec = ks * jnp.exp((CHUNK - 1.0 - pos) * lg)
        kv = _bmm(_heads3(k_dec), _heads3(vs), 1, 1)
        o = o + _bmm(_heads3(qs * jnp.exp((pos + 1.0) * lg)), s, 2, 1).reshape(n, HEAD128)
        decay = jnp.exp(CHUNK * jnp.mean(_heads3(lg), axis=1, keepdims=True))
        s_new = s * decay + kv
        return (tuple(o[h * CHUNK:(h + 1) * CHUNK] for h in range(N_HEAD4)),), s_new
    return ret_chunk


def _lane_pieces(width, n=N_HEAD4):
    return [(h * width, width) for h in range(n)]


class Tok:
    def __init__(self, x, pieces, width=None, col=0):
        self.x, self.pieces, self.col = x, pieces, col
        self.width = x.shape[-1] if width is None else width


def _chunk_of(direction, step, nc, nctx):
    if direction == 0:
        return step
    return jnp.where(step < nctx, nctx - 1 - step, nc + nctx - 1 - step)


def scan_fwd(name, dirs, params, state_shape, outs, lc):
    bsz, t = dirs[0][1][0].x.shape[:2]
    nc, nctx = t // CHUNK, lc // CHUNK
    nd, n_t, n_p, n_o = len(dirs), len(dirs[0][1]), len(params), len(outs)

    def body(*refs):
        tok_refs, par_refs = refs[:nd * n_t], refs[nd * n_t:nd * n_t + n_p]
        out_refs = refs[nd * n_t + n_p:-nd]
        s_refs = refs[-nd:]
        pv = [_load(r, p) for r, (_, p) in zip(par_refs, params)]
        for d, (fn, toks) in enumerate(dirs):
            s_ref = s_refs[d]

            @pl.when(pl.program_id(1) == 0)
            def _():
                s_ref[...] = jnp.zeros(s_ref.shape, F32)

            o_refs = out_refs[d * (n_o + 1):(d + 1) * (n_o + 1)]
            s = s_ref[...]
            o_refs[-1][...] = s
            tv = [_load(r, tk.pieces) for r, tk in zip(tok_refs[d * n_t:(d + 1) * n_t], toks)]
            res, s_new = fn(s, tv, pv)
            s_ref[...] = s_new
            for r, o, (_, _, pieces) in zip(o_refs, res, outs):
                _store(r, o, pieces)

    nstate = len(state_shape)
    tok_specs, out_specs, out_shapes, operands = [], [], [], []
    for d, (_, toks) in enumerate(dirs):
        for tk in toks:
            tok_specs.append(pl.BlockSpec((None, CHUNK, tk.width), lambda b, c, d=d, col=tk.col: (b, _chunk_of(d, c, nc, nctx), col)))
            operands.append(tk.x)
        for w, dt, _ in outs:
            out_specs.append(pl.BlockSpec((None, CHUNK, w), lambda b, c, d=d: (b, _chunk_of(d, c, nc, nctx), 0)))
            out_shapes.append(jax.ShapeDtypeStruct((bsz, t, w), dt))
        out_specs.append(pl.BlockSpec((None, None) + state_shape, lambda b, c, d=d: (b, _chunk_of(d, c, nc, nctx)) + (0,) * nstate))
        out_shapes.append(jax.ShapeDtypeStruct((bsz, nc) + state_shape, F32))
    par_specs = [pl.BlockSpec(x.shape, lambda b, c: (0, 0)) for x, _ in params]
    return pl.pallas_call(
        body, name=name, grid=(bsz, nc),
        in_specs=tok_specs + par_specs, out_specs=out_specs, out_shape=out_shapes,
        scratch_shapes=[pltpu.VMEM(state_shape, F32)] * nd, compiler_params=_params(2),
    )(*operands, *[x for x, _ in params])


def scan_bwd(name, dirs, params, state_shape, outs, lc, sprevs, cts):
    bsz, t = dirs[0][1][0].x.shape[:2]
    nc, nctx = t // CHUNK, lc // CHUNK
    nd, n_t, n_p, n_o = len(dirs), len(dirs[0][1]), len(params), len(outs)
    per_in = n_t + 1 + n_o

    def body(*refs):
        par_refs = refs[nd * per_in:nd * per_in + n_p]
        g_refs = refs[nd * per_in + n_p:-nd]
        ds_refs = refs[-nd:]
        first = pl.program_id(1) == 0
        pv = [_load(r, p) for r, (_, p) in zip(par_refs, params)]
        d_par = None
        for d, (fn, toks) in enumerate(dirs):
            ins = refs[d * per_in:(d + 1) * per_in]
            tok_refs, sprev_ref, ct_refs = ins[:n_t], ins[n_t], ins[n_t + 1:]
            ds_ref = ds_refs[d]

            @pl.when(first)
            def _():
                ds_ref[...] = jnp.zeros(ds_ref.shape, F32)

            tv = [_load(r, tk.pieces) for r, tk in zip(tok_refs, toks)]
            _, vjp = jax.vjp(fn, sprev_ref[...], tv, pv)
            ct = tuple(_load(r, pieces) for r, (_, _, pieces) in zip(ct_refs, outs))
            d_s, d_tv, d_pv = vjp((ct, ds_ref[...]))
            ds_ref[...] = d_s
            for r, gv, tk in zip(g_refs[d * n_t:(d + 1) * n_t], d_tv, toks):
                _store(r, gv, tk.pieces)
            d_par = d_pv if d_par is None else jax.tree.map(jnp.add, d_par, d_pv)
        very_first = first & (pl.program_id(0) == 0)
        for r, gv, (_, p) in zip(g_refs[nd * n_t:], d_par, params):
            @pl.when(very_first)
            def _():
                _store(r, gv, p)

            @pl.when(jnp.logical_not(very_first))
            def _():
                _store(r, gv, p, accumulate=True)

    nstate = len(state_shape)
    in_specs, operands, g_specs, g_shapes = [], [], [], []
    for d, (_, toks) in enumerate(dirs):
        chunk = lambda c, d=d: _chunk_of(d, nc - 1 - c, nc, nctx)
        for tk in toks:
            in_specs.append(pl.BlockSpec((None, CHUNK, tk.width), lambda b, c, f=chunk, col=tk.col: (b, f(c), col)))
            operands.append(tk.x)
            g_specs.append(pl.BlockSpec((None, CHUNK, tk.width), lambda b, c, f=chunk: (b, f(c), 0)))
            g_shapes.append(jax.ShapeDtypeStruct((bsz, t, tk.width), F32))
        in_specs.append(pl.BlockSpec((None, None) + state_shape, lambda b, c, f=chunk: (b, f(c)) + (0,) * nstate))
        operands.append(sprevs[d])
        for (w, _, _), ct in zip(outs, cts[d]):
            in_specs.append(pl.BlockSpec((None, CHUNK, w), lambda b, c, f=chunk: (b, f(c), 0)))
            operands.append(ct)
    par_specs = [pl.BlockSpec(x.shape, lambda b, c: (0, 0)) for x, _ in params]
    return pl.pallas_call(
        body, name=name, grid=(bsz, nc),
        in_specs=in_specs + par_specs, out_specs=g_specs + par_specs,
        out_shape=g_shapes + [jax.ShapeDtypeStruct(x.shape, F32) for x, _ in params],
        scratch_shapes=[pltpu.VMEM(state_shape, F32)] * nd, compiler_params=_params(2),
    )(*operands, *[x for x, _ in params])


ROWS = N_HEAD4 * CHUNK


def _bmm3(a, b, ca, cb):
    ah, al = _split_bf16(a)
    bh, bl = _split_bf16(b)
    dn = (((ca,), (cb,)), ((0,), (0,)))
    d = lambda u, v: lax.dot_general(u, v, dn, preferred_element_type=F32)
    return d(ah, bh) + (d(ah, bl) + d(al, bh))


def _inv_unit_tri_b(low):
    n = low.shape[-1]
    eye = (lax.broadcasted_iota(jnp.int32, (1, n, n), 1) == lax.broadcasted_iota(jnp.int32, (1, n, n), 2)).astype(F32)
    m = -low
    p = eye + m
    for _ in range(int(math.log2(CHUNK)) - 1):
        m = _bmm3(m, m, 2, 1)
        p = p + _bmm3(p, m, 2, 1)
    return p


@jax.custom_vjp
def _tri_solve_b(low, rhs):
    return _bmm3(_inv_unit_tri_b(low), rhs, 2, 1)


def _tri_solve_b_fwd(low, rhs):
    inv = _inv_unit_tri_b(low)
    x = _bmm3(inv, rhs, 2, 1)
    return x, (inv, x)


def _tri_solve_b_bwd(res, g):
    inv, x = res
    d_rhs = _bmm3(inv, g, 1, 1)
    return -_bmm3(d_rhs, x, 2, 2), d_rhs


_tri_solve_b.defvjp(_tri_solve_b_fwd, _tri_solve_b_bwd)


def _stack_dirs(toks, i):
    return jnp.concatenate([jnp.concatenate(t[i], axis=1) for t in toks], axis=0)


def _problem_masks(p, nb):
    shape = (p, ROWS, ROWS)
    up = lax.broadcasted_iota(jnp.int32, shape, 0) >= nb
    i = lax.broadcasted_iota(jnp.int32, shape, 1)
    j = lax.broadcasted_iota(jnp.int32, shape, 2)
    same = (i >> 6) == (j >> 6)
    pi, pj = i & (CHUNK - 1), j & (CHUNK - 1)
    rel = jnp.where(up, pj - pi, pi - pj)
    last = same & (pj == jnp.where(up, 0, CHUNK - 1))
    return same & (rel >= 0), same & (rel > 0), last, rel


def _split_states(s_new, nd, nb):
    return [s_new[d * nb * N_HEAD4:(d + 1) * nb * N_HEAD4].reshape((nb, N_HEAD4) + s_new.shape[1:]) for d in range(nd)]


def _split_outs(o, nd, nb):
    return [(tuple(o[d * nb:(d + 1) * nb, h * CHUNK:(h + 1) * CHUNK] for h in range(N_HEAD4)),) for d in range(nd)]


def dn_chunks(states, toks, params):
    nd, nb = len(toks), states[0].shape[0]
    p = nd * nb
    qs, ks, vs, la, beta = (_stack_dirs(toks, i) for i in range(5))
    incl, strict, last, _ = _problem_masks(p, nb)
    sq = (p, ROWS, ROWS)
    g = jnp.sum(jnp.where(incl, jnp.swapaxes(jnp.broadcast_to(la, sq), 1, 2), 0.0), axis=2, keepdims=True)
    gb = jnp.broadcast_to(g, sq)
    gbt = jnp.swapaxes(gb, 1, 2)
    dec_incl = jnp.where(incl, jnp.exp(jnp.where(incl, gb - gbt, 0.0)), 0.0)
    dec_strict = jnp.where(strict, dec_incl, 0.0)
    low = beta * _bmm(ks, ks, 2, 2) * dec_strict
    eg = jnp.exp(g)
    sol = _tri_solve_b(low, jnp.concatenate([beta * vs, (beta * eg) * ks], axis=2))
    w_v, w_k = sol[:, :, :HEAD128], sol[:, :, HEAD128:]
    a_qk = _bmm(qs, ks, 2, 2) * dec_incl
    g_last = jnp.sum(jnp.where(last, gbt, 0.0), axis=2, keepdims=True)
    k_dec = ks * jnp.exp(g_last - g)
    s = jnp.concatenate(states, axis=0).reshape(p * N_HEAD4, HEAD128, HEAD128)
    h3 = lambda x: x.reshape(p * N_HEAD4, CHUNK, x.shape[-1])
    u = w_v - _bmm(h3(w_k), s, 2, 1).reshape(p, ROWS, HEAD128)
    o = _bmm(h3(qs * eg), s, 2, 1).reshape(p, ROWS, HEAD128) + _bmm(a_qk, u, 2, 1)
    decay = jnp.exp(jnp.mean(h3(g_last), axis=1, keepdims=True))
    s_new = s * decay + _bmm(h3(k_dec), h3(u), 1, 1)
    return _split_outs(o, nd, nb), _split_states(s_new, nd, nb)


def ret_chunks(states, toks, params):
    nd, nb = len(toks), states[0].shape[0]
    p = nd * nb
    (decay_log,) = params
    qs, ks, vs = (_stack_dirs(toks, i) for i in range(3))
    incl, _, _, rel = _problem_masks(p, nb)
    col = (p, ROWS, 1)
    up = lax.broadcasted_iota(jnp.int32, col, 0) >= nb
    row = lax.broadcasted_iota(jnp.int32, col, 1)
    head = row >> 6
    lg = jnp.zeros(col, F32)
    for h in range(N_HEAD4):
        rate = jnp.where(up, -jnp.exp(decay_log[N_HEAD4 + h]), -jnp.exp(decay_log[h]))
        lg = jnp.where(head == h, rate, lg)
    place = row & (CHUNK - 1)
    pos = jnp.where(up, CHUNK - 1 - place, place).astype(F32)
    dmask = jnp.where(incl, jnp.exp(jnp.where(incl, rel.astype(F32) * lg, 0.0)), 0.0)
    o = _bmm(_bmm(qs, ks, 2, 2) * dmask, vs, 2, 1)
    s = jnp.concatenate(states, axis=0).reshape(p * N_HEAD4, HEAD64, HEAD128)
    h3 = lambda x: x.reshape(p * N_HEAD4, CHUNK, x.shape[-1])
    kv = _bmm(h3(ks * jnp.exp((CHUNK - 1.0 - pos) * lg)), h3(vs), 1, 1)
    o = o + _bmm(h3(qs * jnp.exp((pos + 1.0) * lg)), s, 2, 1).reshape(p, ROWS, HEAD128)
    decay = jnp.exp(CHUNK * jnp.mean(h3(lg), axis=1, keepdims=True))
    s_new = s * decay + kv
    return _split_outs(o, nd, nb), _split_states(s_new, nd, nb)


def _scan_nb(bsz):
    return 2 if bsz % 2 == 0 else 1


def _carried(refs, n_in, n_out, comm):
    if comm is None:
        return refs[:n_in], refs[n_in:n_in + n_out], refs[n_in + n_out:], None
    n = len(comm[1])
    ins, srcs = refs[:n_in], refs[n_in:n_in + n]
    outs, dsts = refs[n_in + n:n_in + n + n_out], refs[n_in + n + n_out:n_in + 2 * n + n_out]
    scratch, sems = refs[n_in + 2 * n + n_out:-3], refs[-3:]
    return ins, outs, scratch, (srcs, dsts) + tuple(sems)


def _carry_start(xrefs, comm, grid):
    if comm is None:
        return
    first = functools.reduce(jnp.logical_and, [pl.program_id(i) == 0 for i in range(len(grid))])

    @pl.when(first)
    def _():
        _exchange_start(xrefs, comm[0])


def _carry_wait(xrefs, comm, grid):
    if comm is None:
        return
    last = functools.reduce(jnp.logical_and, [pl.program_id(i) == n - 1 for i, n in enumerate(grid)])

    @pl.when(last)
    def _():
        _exchange_wait(xrefs, comm[0])


def scan2_fwd(name, chunks_fn, dirs, params, state_shape, outs, lc, comm=None):
    bsz, t = dirs[0][0].x.shape[:2]
    nb = _scan_nb(bsz)
    nc, nctx = t // CHUNK, lc // CHUNK
    nd, n_t, n_p, n_o = len(dirs), len(dirs[0]), len(params), len(outs)
    grid = (bsz // nb, nc)

    def body(*refs):
        ins, out_refs, s_refs, xrefs = _carried(refs, nd * n_t + n_p, nd * (n_o + 1), comm)
        tok_refs, par_refs = ins[:nd * n_t], ins[nd * n_t:]
        _carry_start(xrefs, comm, grid)

        @pl.when(pl.program_id(1) == 0)
        def _():
            for s_ref in s_refs:
                s_ref[...] = jnp.zeros(s_ref.shape, F32)

        pv = [_load(r, p) for r, (_, p) in zip(par_refs, params)]
        states = [s_ref[...] for s_ref in s_refs]
        tv = [[_load(r, tk.pieces) for r, tk in zip(tok_refs[d * n_t:(d + 1) * n_t], dirs[d])] for d in range(nd)]
        res, s_new = chunks_fn(states, tv, pv)
        for d in range(nd):
            o_refs = out_refs[d * (n_o + 1):(d + 1) * (n_o + 1)]
            o_refs[-1][...] = states[d]
            s_refs[d][...] = s_new[d]
            for r, o, (_, _, pieces) in zip(o_refs, res[d], outs):
                _store(r, o, pieces)
        _carry_wait(xrefs, comm, grid)

    nstate = len(state_shape)
    tok_specs, out_specs, out_shapes, operands = [], [], [], []
    for d, toks in enumerate(dirs):
        chunk = lambda c, d=d: _chunk_of(d, c, nc, nctx)
        for tk in toks:
            tok_specs.append(pl.BlockSpec((nb, CHUNK, tk.width), lambda b, c, f=chunk, col=tk.col: (b, f(c), col)))
            operands.append(tk.x)
        for w, dt, _ in outs:
            out_specs.append(pl.BlockSpec((nb, CHUNK, w), lambda b, c, f=chunk: (b, f(c), 0)))
            out_shapes.append(jax.ShapeDtypeStruct((bsz, t, w), dt))
        out_specs.append(pl.BlockSpec((nb, None) + state_shape, lambda b, c, f=chunk: (b, f(c)) + (0,) * nstate))
        out_shapes.append(jax.ShapeDtypeStruct((bsz, nc) + state_shape, F32))
    par_specs = [pl.BlockSpec(x.shape, lambda b, c: (0, 0)) for x, _ in params]
    operands += [x for x, _ in params]
    return _carrier_call(body, name, grid, tok_specs + par_specs, operands, out_specs, out_shapes,
                         [pltpu.VMEM((nb,) + state_shape, F32)] * nd, comm)


def _carrier_call(body, name, grid, in_specs, operands, out_specs, out_shapes, scratch, comm):
    if comm is not None:
        kind, arrays = comm
        in_specs, operands = in_specs + [HBM_SPEC] * len(arrays), operands + list(arrays)
        out_specs, out_shapes = out_specs + [HBM_SPEC] * len(arrays), out_shapes + [_exchange_shape(kind, a) for a in arrays]
        scratch = scratch + _exchange_sems(len(arrays))
    return pl.pallas_call(body, name=name, grid=grid, in_specs=in_specs, out_specs=out_specs, out_shape=out_shapes,
                          scratch_shapes=scratch, compiler_params=_params(len(grid)))(*operands)


def scan2_bwd(name, chunks_fn, dirs, params, state_shape, outs, lc, sprevs, cts, comm=None):
    bsz, t = dirs[0][0].x.shape[:2]
    nb = _scan_nb(bsz)
    nc, nctx = t // CHUNK, lc // CHUNK
    nd, n_t, n_p, n_o = len(dirs), len(dirs[0]), len(params), len(outs)
    per_in = n_t + 1 + n_o
    grid = (bsz // nb, nc)

    def body(*refs):
        refs, g_refs, ds_refs, xrefs = _carried(refs, nd * per_in + n_p, nd * n_t + n_p, comm)
        par_refs = refs[nd * per_in:]
        _carry_start(xrefs, comm, grid)
        first = pl.program_id(1) == 0

        @pl.when(first)
        def _():
            for ds_ref in ds_refs:
                ds_ref[...] = jnp.zeros(ds_ref.shape, F32)

        pv = [_load(r, p) for r, (_, p) in zip(par_refs, params)]
        ins = [refs[d * per_in:(d + 1) * per_in] for d in range(nd)]
        tv = [[_load(r, tk.pieces) for r, tk in zip(ins[d][:n_t], dirs[d])] for d in range(nd)]
        states = [ins[d][n_t][...] for d in range(nd)]
        _, vjp = jax.vjp(chunks_fn, states, tv, pv)
        ct = [tuple(_load(r, pieces) for r, (_, _, pieces) in zip(ins[d][n_t + 1:], outs)) for d in range(nd)]
        d_s, d_tv, d_pv = vjp((ct, [ds_ref[...] for ds_ref in ds_refs]))
        for d in range(nd):
            ds_refs[d][...] = d_s[d]
            for r, gv, tk in zip(g_refs[d * n_t:(d + 1) * n_t], d_tv[d], dirs[d]):
                _store(r, gv, tk.pieces)
        very_first = first & (pl.program_id(0) == 0)
        for r, gv, (_, p) in zip(g_refs[nd * n_t:], d_pv, params):
            @pl.when(very_first)
            def _():
                _store(r, gv, p)

            @pl.when(jnp.logical_not(very_first))
            def _():
                _store(r, gv, p, accumulate=True)
        _carry_wait(xrefs, comm, grid)

    nstate = len(state_shape)
    in_specs, operands, g_specs, g_shapes = [], [], [], []
    for d, toks in enumerate(dirs):
        chunk = lambda c, d=d: _chunk_of(d, nc - 1 - c, nc, nctx)
        for tk in toks:
            in_specs.append(pl.BlockSpec((nb, CHUNK, tk.width), lambda b, c, f=chunk, col=tk.col: (b, f(c), col)))
            operands.append(tk.x)
            g_specs.append(pl.BlockSpec((nb, CHUNK, tk.width), lambda b, c, f=chunk: (b, f(c), 0)))
            g_shapes.append(jax.ShapeDtypeStruct((bsz, t, tk.width), F32))
        in_specs.append(pl.BlockSpec((nb, None) + state_shape, lambda b, c, f=chunk: (b, f(c)) + (0,) * nstate))
        operands.append(sprevs[d])
        for (w, _, _), ct in zip(outs, cts[d]):
            in_specs.append(pl.BlockSpec((nb, CHUNK, w), lambda b, c, f=chunk: (b, f(c), 0)))
            operands.append(ct)
    par_specs = [pl.BlockSpec(x.shape, lambda b, c: (0, 0)) for x, _ in params]
    operands += [x for x, _ in params]
    return _carrier_call(body, name, grid, in_specs + par_specs, operands, g_specs + par_specs,
                         g_shapes + [jax.ShapeDtypeStruct(x.shape, F32) for x, _ in params],
                         [pltpu.VMEM((nb,) + state_shape, F32)] * nd, comm)


class Geo:
    def __init__(self, bsz, t, lc):
        self.b, self.t, self.lc, self.m = bsz, t, lc, bsz * t
        self.tm = _pick(lc, (256, 128, 64))
        self.tq = _pick(lc, (128, 64))
        assert t % self.tm == 0 and t % CHUNK == 0 and lc % CHUNK == 0
        self.nctx, self.nctx_q = lc // self.tm, lc // self.tq
        self.grid = (bsz, t // self.tm)


def a_tok(g, x, tm=None, **kw):
    tm = tm or g.tm
    return Arg(x, (None, tm, x.shape[-1]), lambda b, j: (b, j, 0), **kw)


def a_ztok(g, z, name, width, tm=None, **kw):
    tm = tm or g.tm
    col = ZOFF[name] // width
    return Arg(z, (None, tm, width), lambda b, j: (b, j, col),
               gshape=(g.b, g.t, width), gimap=lambda b, j: (b, j, 0), **kw)


def a_par(x, **kw):
    return Arg(x, x.shape, lambda b, j: (0, 0), first=lambda ids: (ids[0] == 0) & (ids[1] == 0), **kw)


def a_mod(g, x):
    n = g.nctx
    return Arg(x, (None, None, x.shape[2], D_MODEL), lambda b, j: (b, jnp.where(j >= n, 1, 0), 0, 0),
               first=lambda ids: (ids[1] == 0) | (ids[1] == n))


def a_tab(g, x):
    return Arg(x, (g.tm, x.shape[-1]), lambda b, j: (j, 0), diff=False)


def o_tok(g, w, dtype, tm=None, pieces=None):
    tm = tm or g.tm
    return Out((g.b, g.t, w), dtype, (None, tm, w), lambda b, j: (b, j, 0), pieces)


def addn(name, xs, dtype):
    bsz, t, w = xs[0].shape
    tm = _pick(t, (256, 192, 128, 64))

    def body(*refs):
        tot = refs[0][...].astype(F32)
        for r in refs[1:-1]:
            tot = tot + r[...].astype(F32)
        refs[-1][...] = tot.astype(dtype)

    spec = pl.BlockSpec((None, tm, w), lambda b, j: (b, j, 0))
    return pl.pallas_call(body, name=name, grid=(bsz, t // tm), in_specs=[spec] * len(xs), out_specs=spec,
                          out_shape=jax.ShapeDtypeStruct((bsz, t, w), dtype), compiler_params=_params(2))(*xs)


P64x8 = _lane_pieces(HEAD64, ATT_HEADS)
P64x4 = _lane_pieces(HEAD64, N_HEAD4)
P64x2 = _lane_pieces(HEAD64, ATT_KV_HEADS)
P128x4 = _lane_pieces(HEAD128, N_HEAD4)
P1024x4 = _lane_pieces(D_MODEL, N_BRANCH)


def _gate_pieces(direction):
    la = [(direction * N_HEAD4 + h, 1) for h in range(N_HEAD4)]
    beta = [(8 + direction * N_HEAD4 + h, 1) for h in range(N_HEAD4)]
    return la, beta


RET_PIECES = [(i, 1) for i in range(2 * N_HEAD4)]


def _prep_io(g, z, sp, tabs):
    args = [a_ztok(g, z, "att_q", 512, gdtype=BF16), a_ztok(g, z, "att_k", 128, gdtype=BF16),
            a_ztok(g, z, "ret_q", 256, gdtype=BF16), a_ztok(g, z, "ret_k", 256, gdtype=BF16),
            a_ztok(g, z, "narrow", 128, gdtype=BF16),
            a_par(sp["qgain"]), a_par(sp["kgain"]), a_par(sp["alog"]), a_par(sp["dtb"]),
            a_tab(g, tabs[0]), a_tab(g, tabs[1])]
    outs = [o_tok(g, 512, BF16), o_tok(g, 128, BF16), o_tok(g, 256, F32), o_tok(g, 256, F32), o_tok(g, 128, F32)]
    return args, outs


def _attn_io(g, q_att, k_att, z, latent):
    col = ZOFF["att_v"] // 128
    first = lambda ids: ids[1] == 0
    if latent:
        rows, off, nq = g.t, g.nctx_q, (g.t - g.lc) // g.tq
    else:
        rows, off, nq = g.lc, 0, g.nctx_q
    args = [Arg(q_att, (None, g.tq, 512), lambda b, j: (b, j + off, 0), pieces=P64x8,
                gshape=(g.b, nq * g.tq, 512), gimap=lambda b, j: (b, j, 0)),
            Arg(k_att, (None, rows, 128), lambda b, j: (b, 0, 0), first=first, pieces=P64x2, gshape=(g.b, rows, 128)),
            Arg(z, (None, rows, 128), lambda b, j: (b, 0, col), first=first, pieces=P64x2,
                gshape=(g.b, rows, 128), gimap=lambda b, j: (b, 0, 0))]
    outs = [Out((g.b, nq * g.tq, 512), BF16, (None, g.tq, 512), lambda b, j: (b, j, 0), P64x8)]
    return (g.b, nq), args, outs


def _pad_rows(x, t):
    return jnp.pad(x, ((0, 0), (0, t - x.shape[1]), (0, 0)))


def _colgrid_arg(g, z, name, width_total, **kw):
    col = ZOFF[name] // 128
    return Arg(z, (None, g.t, 128), lambda h, b: (b, 0, col + h),
               gshape=(g.b, g.t, width_total), gimap=lambda h, b: (b, 0, h), **kw)


def _dnprep_io(g, z, sp):
    wfirst = lambda ids: ids[1] == 0
    args = [_colgrid_arg(g, z, "dn_q", 512, gdtype=BF16), _colgrid_arg(g, z, "dn_k", 512, gdtype=BF16),
            _colgrid_arg(g, z, "dn_v", 512, gdtype=BF16)]
    for i in range(3):
        args.append(Arg(sp["dn_conv"][i], (3, 128), lambda h, b: (0, h), first=wfirst))
    outs = [Out((g.b, g.t, 512), F32, (None, g.t, 128), lambda h, b: (b, 0, h)) for _ in range(3)]
    return (N_HEAD4, g.b), args, outs


def _shortconv_io(g, z, sp):
    args = [_colgrid_arg(g, z, "sc_b", 512, gdtype=BF16), _colgrid_arg(g, z, "sc_c", 512, gdtype=BF16),
            _colgrid_arg(g, z, "sc_x", 512, gdtype=BF16),
            Arg(sp["sc_conv"], (3, 128), lambda h, b: (0, h), first=lambda ids: ids[1] == 0)]
    outs = [Out((g.b, g.t, 512), BF16, (None, g.t, 128), lambda h, b: (b, 0, h))]
    return (BRANCH_W // 128, g.b), args, outs


def _finish_io(g, o_dn_f, o_dn_b, z, o_rt_f, o_rt_b, sp):
    args = [a_tok(g, o_dn_f, pieces=P128x4), a_tok(g, o_dn_b, pieces=P128x4),
            a_ztok(g, z, "dn_z", 512, gdtype=BF16, pieces=P128x4),
            a_tok(g, o_rt_f, pieces=P128x4), a_tok(g, o_rt_b, pieces=P128x4),
            a_ztok(g, z, "ret_g", 512, gdtype=BF16, pieces=P128x4), a_par(sp["ngain"])]
    outs = [o_tok(g, 512, BF16, pieces=P128x4), o_tok(g, 512, BF16, pieces=P128x4)]
    return args, outs


def _merge_io(g, ps, z):
    args = [a_tok(g, p, gdtype=BF16) for p in ps] + [a_ztok(g, z, "gates", 4096, gdtype=BF16, pieces=P1024x4)]
    return args, [o_tok(g, D_MODEL, BF16)]


def _dn_dirs(qn, kn, vn, gates):
    dirs = []
    for d in range(2):
        la, beta = _gate_pieces(d)
        dirs.append([Tok(qn, P128x4), Tok(kn, P128x4), Tok(vn, P128x4), Tok(gates, la), Tok(gates, beta)])
    return dirs


def _ret_dirs(q_ret, k_ret, z):
    col = ZOFF["ret_v"] // 512
    return [[Tok(q_ret, P64x4), Tok(k_ret, P64x4), Tok(z, P128x4, 512, col)] for _ in range(2)]


DN_STATE = (N_HEAD4, HEAD128, HEAD128)
RET_STATE = (N_HEAD4, HEAD64, HEAD128)
SCAN_OUT = [(512, F32, P128x4)]


def layer_fwd(g, x, h, w, sp, mod_a, mod_b, tabs, g_next, comm=None):
    tp = {"x": x, "h": h}
    m, lc = g.m, g.lc
    z = matmul("win", h.reshape(m, D_MODEL), w["win"]).reshape(g.b, g.t, NZ)
    tp["z"] = z
    args, outs = _prep_io(g, z, sp, tabs)
    q_att, k_att, q_ret, k_ret, gates = vfwd("prep", f_prep, g.grid, args, outs)
    tp.update(q_att=q_att, k_att=k_att, q_ret=q_ret, k_ret=k_ret, gates=gates)
    y_att = []
    for latent in (False, True):
        agrid, args, outs = _attn_io(g, q_att, k_att, z, latent)
        y_att += vfwd("attn", make_f_attn(g.tq), agrid, args, outs)
    y_att = jnp.concatenate(y_att, axis=1)
    dgrid, args, outs = _dnprep_io(g, z, sp)
    qn, kn, vn = vfwd("dnprep", make_f_dnprep(g.t, lc), dgrid, args, outs)
    tp.update(qn=qn, kn=kn, vn=vn)
    o_f, s_f, o_b, s_b, *carried = scan2_fwd("dnscan", dn_chunks, _dn_dirs(qn, kn, vn, gates), [], DN_STATE, SCAN_OUT, lc, comm)
    tp["carried"] = carried if carried else None
    o_dn, tp["dn_s"] = [o_f, o_b], [s_f, s_b]
    o_f, s_f, o_b, s_b = scan2_fwd("retscan", ret_chunks, _ret_dirs(q_ret, k_ret, z), [(sp["ret"], RET_PIECES)], RET_STATE, SCAN_OUT, lc)
    o_rt, tp["rt_s"] = [o_f, o_b], [s_f, s_b]
    tp.update(o_dn=o_dn, o_rt=o_rt)
    args, outs = _finish_io(g, o_dn[0], o_dn[1], z, o_rt[0], o_rt[1], sp)
    y_dn, y_rt = vfwd("finish", f_finish, g.grid, args, outs)
    sgrid, args, outs = _shortconv_io(g, z, sp)
    (y_sc,) = vfwd("shortconv", make_f_shortconv(g.t, lc), sgrid, args, outs)
    ys = [y_att, y_dn, y_rt, y_sc]
    tp["ys"] = ys
    ps = [matmul("wbranch", y.reshape(m, BRANCH_W), w["wbr"][i]).reshape(g.b, g.t, D_MODEL) for i, y in enumerate(ys)]
    tp["ps"] = ps
    args, outs = _merge_io(g, ps, z)
    (u,) = vfwd("merge", f_merge, g.grid, args, outs)
    tp["u"] = u
    y = matmul("wout", u.reshape(m, D_MODEL), w["wout"]).reshape(g.b, g.t, D_MODEL)
    tp["y"] = y
    args = [a_tok(g, x), a_tok(g, y), a_par(sp["g1"]), a_par(sp["g2"]), a_mod(g, mod_a)]
    x1, h2 = vfwd("resnorm", f_resnorm, g.grid, args, [o_tok(g, D_MODEL, F32), o_tok(g, D_MODEL, BF16)])
    tp.update(x1=x1, h2=h2)
    a = matmul("wmlp1", h2.reshape(m, D_MODEL), w["w1"]).reshape(g.b, g.t, MLP_HIDDEN)
    tp["a"] = a
    (r,) = vfwd("act", f_act, g.grid, [a_tok(g, a, gdtype=BF16)], [o_tok(g, MLP_HIDDEN, BF16)])
    tp["r"] = r
    mo = matmul("wmlp2", r.reshape(m, MLP_HIDDEN), w["w2"]).reshape(g.b, g.t, D_MODEL)
    tp["mo"] = mo
    if g_next is None:
        args = [a_tok(g, x1), a_tok(g, mo), a_par(sp["g3"]), a_mod(g, mod_b)]
        (x2,) = vfwd("resid", f_resid, g.grid, args, [o_tok(g, D_MODEL, F32)])
        return x2, None, tp
    args = [a_tok(g, x1), a_tok(g, mo), a_par(sp["g3"]), a_par(g_next), a_mod(g, mod_b)]
    x2, h_next = vfwd("resnorm", f_resnorm, g.grid, args, [o_tok(g, D_MODEL, F32), o_tok(g, D_MODEL, BF16)])
    return x2, h_next, tp


def layer_bwd(g, tp, w, sp, mod_a, mod_b, tabs, g_next, dx2, dh_next, comm=None):
    m, lc = g.m, g.lc
    gw, gs = {}, {}
    x, z = tp["x"], tp["z"]
    if g_next is None:
        args = [a_tok(g, tp["x1"]), a_tok(g, tp["mo"]), a_par(sp["g3"]), a_mod(g, mod_b)]
        dx1, dmo, gs["g3"], dmod_b = vbwd("resid_b", f_resid, g.grid, args, [o_tok(g, D_MODEL, F32)], [[dx2]])
    else:
        args = [a_tok(g, tp["x1"]), a_tok(g, tp["mo"]), a_par(sp["g3"]), a_par(g_next), a_mod(g, mod_b)]
        dx1, dmo, gs["g3"], gs["g0_next"], dmod_b = vbwd(
            "resnorm_b", f_resnorm, g.grid, args, [o_tok(g, D_MODEL, F32), o_tok(g, D_MODEL, BF16)], [[dx2], [dh_next]])
    dmo2 = dmo.reshape(m, D_MODEL)
    dr = matmul("wmlp2_dx", dmo2, w["w2"], trans_b=True).reshape(g.b, g.t, MLP_HIDDEN)
    gw["w2"] = matmul("wmlp2_dw", tp["r"].reshape(m, MLP_HIDDEN), dmo2, trans_a=True)
    (da,) = vbwd("act_b", f_act, g.grid, [a_tok(g, tp["a"], gdtype=BF16)], [o_tok(g, MLP_HIDDEN, BF16)], [[dr]])
    da2 = da.reshape(m, MLP_HIDDEN)
    dh2 = matmul("wmlp1_dx", da2, w["w1"], trans_b=True).reshape(g.b, g.t, D_MODEL)
    gw["w1"] = matmul("wmlp1_dw", tp["h2"].reshape(m, D_MODEL), da2, trans_a=True)
    args = [a_tok(g, x), a_tok(g, tp["y"]), a_par(sp["g1"]), a_par(sp["g2"]), a_mod(g, mod_a)]
    dx, dy, gs["g1"], gs["g2"], dmod_a = vbwd(
        "resnorm_b", f_resnorm, g.grid, args, [o_tok(g, D_MODEL, F32), o_tok(g, D_MODEL, BF16)], [[dx1], [dh2]])
    dy2 = dy.reshape(m, D_MODEL)
    du = matmul("wout_dx", dy2, w["wout"], trans_b=True).reshape(g.b, g.t, D_MODEL)
    gw["wout"] = matmul("wout_dw", tp["u"].reshape(m, D_MODEL), dy2, trans_a=True)
    args, outs = _merge_io(g, tp["ps"], z)
    *dps, dz_gates = vbwd("merge_b", f_merge, g.grid, args, outs, [[du]])
    dys, gwbr = [], []
    for i in range(N_BRANCH):
        dp2 = dps[i].reshape(m, D_MODEL)
        dys.append(matmul("wbranch_dx", dp2, w["wbr"][i], trans_b=True).reshape(g.b, g.t, BRANCH_W))
        gwbr.append(matmul("wbranch_dw", tp["ys"][i].reshape(m, BRANCH_W), dp2, trans_a=True))
    gw["wbr"] = jnp.stack(gwbr)
    dy_att, dy_dn, dy_rt, dy_sc = dys
    o_dn, o_rt = tp["o_dn"], tp["o_rt"]
    args, outs = _finish_io(g, o_dn[0], o_dn[1], z, o_rt[0], o_rt[1], sp)
    do_dn_f, do_dn_b, dz_dnz, do_rt_f, do_rt_b, dz_retg, gs["ngain"] = vbwd(
        "finish_b", f_finish, g.grid, args, outs, [[dy_dn], [dy_rt]])
    res = scan2_bwd("dnscan_b", dn_chunks, _dn_dirs(tp["qn"], tp["kn"], tp["vn"], tp["gates"]), [], DN_STATE, SCAN_OUT, lc,
                   tp["dn_s"], [[do_dn_f], [do_dn_b]], comm)
    dqn, dkn, dvn, dgates = [res[0], res[5]], [res[1], res[6]], [res[2], res[7]], [res[3], res[4], res[8], res[9]]
    received = list(res[10:]) if comm is not None else None
    swap = ("swap", received) if comm is not None else None
    res = scan2_bwd("retscan_b", ret_chunks, _ret_dirs(tp["q_ret"], tp["k_ret"], z), [(sp["ret"], RET_PIECES)], RET_STATE, SCAN_OUT, lc,
                   tp["rt_s"], [[do_rt_f], [do_rt_b]], swap)
    dq_ret, dk_ret, gs["ret"] = [res[0], res[3]], [res[1], res[4]], res[6]
    carried = (received, list(res[7:])) if comm is not None else None
    dz_retv = addn("sum_retv", [res[2], res[5]], BF16)
    sgrid, args, outs = _shortconv_io(g, z, sp)
    dz_scb, dz_scc, dz_scx, gs["sc_conv"] = vbwd("shortconv_b", make_f_shortconv(g.t, lc), sgrid, args, outs, [[dy_sc]])
    dq_att, dk_att, dv_att = [], [], []
    for latent, dy in ((False, dy_att[:, :lc]), (True, dy_att[:, lc:])):
        agrid, args, outs = _attn_io(g, tp["q_att"], tp["k_att"], z, latent)
        dq, dk, dv = vbwd("attn_b", make_f_attn(g.tq), agrid, args, outs, [[dy]])
        dq_att.append(dq); dk_att.append(_pad_rows(dk, g.t)); dv_att.append(_pad_rows(dv, g.t))
    dq_att = jnp.concatenate(dq_att, axis=1)
    dz_attv = addn("sum_attv", dv_att, BF16)
    dgrid, args, outs = _dnprep_io(g, z, sp)
    dz_dnq, dz_dnk, dz_dnv, gc_q, gc_k, gc_v = vbwd("dnprep_b", make_f_dnprep(g.t, lc), dgrid, args, outs, [dqn, dkn, dvn])
    gs["dn_conv"] = [gc_q, gc_k, gc_v]
    args, outs = _prep_io(g, z, sp, tabs)
    dz_attq, dz_attk, dz_retq, dz_retk, dz_nar, gs["qgain"], gs["kgain"], gs["alog"], gs["dtb"] = vbwd(
        "prep_b", f_prep, g.grid, args, outs, [[dq_att], dk_att, dq_ret, dk_ret, dgates])
    pad = jnp.zeros((g.b, g.t, NZ - (ZOFF["narrow"] + 128)), BF16)
    dz = jnp.concatenate([dz_gates, dz_attq, dz_dnq, dz_dnk, dz_dnv, dz_dnz, dz_retv, dz_retg, dz_scb, dz_scc, dz_scx,
                          dz_retq, dz_retk, dz_attk, dz_attv, dz_nar, pad], axis=-1)
    dz2 = dz.reshape(m, NZ)
    dh = matmul("win_dx", dz2, w["win"], trans_b=True).reshape(g.b, g.t, D_MODEL)
    gw["win"] = matmul("win_dw", tp["h"].reshape(m, D_MODEL), dz2, trans_a=True)
    return dx, dh, gw, gs, dmod_a, dmod_b, carried


def loss_call(g, xa, tgt):
    n = g.nctx
    inv_d = 1.0 / D_MODEL

    def body(x_ref, t_ref, loss_ref, dx_ref):
        j = pl.program_id(1)

        @pl.when((pl.program_id(0) == 0) & (j == 0))
        def _():
            loss_ref[...] = jnp.zeros(loss_ref.shape, F32)

        @pl.when(j < n)
        def _():
            dx_ref[...] = jnp.zeros(dx_ref.shape, F32)

        @pl.when(j >= n)
        def _():
            e = x_ref[...] - t_ref[...]
            dx_ref[...] = e * inv_d
            s = jnp.sum(jnp.sum(e * e, axis=1, keepdims=True), axis=0, keepdims=True)
            loss_ref[...] += jnp.broadcast_to(0.5 * inv_d * s, loss_ref.shape)

    tok = pl.BlockSpec((None, g.tm, D_MODEL), lambda b, j: (b, j, 0))
    return pl.pallas_call(
        body, name="loss", grid=g.grid,
        in_specs=[tok, pl.BlockSpec((None, g.tm, D_MODEL), lambda b, j: (b, jnp.maximum(j - n, 0), 0))],
        out_specs=[pl.BlockSpec((1, LANES), lambda b, j: (0, 0)), tok],
        out_shape=[jax.ShapeDtypeStruct((1, LANES), F32), jax.ShapeDtypeStruct((g.b, g.t, D_MODEL), F32)],
        compiler_params=_params(2),
    )(xa, tgt)


def rope_tables(g):
    seq = g.t - g.lc
    rows = seq // GRID_W
    r, col = jnp.meshgrid(jnp.arange(rows), jnp.arange(GRID_W), indexing="ij")
    quarter = HEAD64 // 4
    inv_freq = ROPE_THETA ** (-jnp.arange(quarter, dtype=F32) / quarter)
    ang = jnp.concatenate([r.reshape(-1, 1).astype(F32) * inv_freq, col.reshape(-1, 1).astype(F32) * inv_freq], axis=-1)
    cos, sin = jnp.cos(ang), jnp.sin(ang)
    cos = jnp.concatenate([jnp.ones((g.lc, HEAD64 // 2), F32), cos], axis=0)
    sin = jnp.concatenate([jnp.zeros((g.lc, HEAD64 // 2), F32), sin], axis=0)
    reps = 512 // (HEAD64 // 2)
    return jnp.tile(cos, (1, reps)), jnp.tile(sin, (1, reps))


def _row128(v):
    v = v.reshape(1, -1).astype(F32)
    return jnp.pad(v, ((0, 0), (0, LANES - v.shape[1])))


def layer_small(small, l):
    gn = small["g_norm"][l]
    return {
        "g0": gn[0:1], "g1": gn[1:2], "g2": gn[2:3], "g3": gn[3:4],
        "qgain": small["att_q_gain"][l][None], "kgain": small["att_k_gain"][l][None],
        "dn_conv": [small["dn_conv"][l][:, i * 512:(i + 1) * 512] for i in range(3)], "alog": _row128(small["dn_a_log"][l]), "dtb": _row128(small["dn_dt_bias"][l]),
        "ngain": small["dn_norm_gain"][l][None], "ret": _row128(small["ret_decay"][l]), "sc_conv": small["sc_conv"][l],
    }


def permute_win(w):
    parts = [w[..., off:off + width] for _, off, width in _SEGS]
    parts.append(jnp.zeros(w.shape[:-1] + (NZ - N_IN,), w.dtype))
    return jnp.concatenate(parts, axis=-1)


def unpermute_win(gw):
    order = sorted(_SEGS, key=lambda s: s[1])
    return jnp.concatenate([gw[..., ZOFF[name]:ZOFF[name] + width] for name, _, width in order], axis=-1)


def layer_weights(w_in, w_branch, w_out, w_mlp_in, w_mlp_out):
    wbr, wout, w1, w2 = (t.astype(BF16) for t in (w_branch, w_out, w_mlp_in, w_mlp_out))
    return {"win": permute_win(w_in.astype(BF16)), "wbr": wbr, "wout": wout, "w1": w1, "w2": w2}


BIG = (("w_in", -1), ("w_branch", -1), ("w_out", -2), ("w_mlp_in", -1), ("w_mlp_out", -2))


def weights_of_slots(slots):
    return layer_weights(*[jnp.concatenate([s[p] for p in range(N_XY)], axis=axis) for s, (_, axis) in zip(slots, BIG)])


def grad_slots(gw):
    full = [unpermute_win(gw["win"]), gw["wbr"], gw["wout"], gw["w1"], gw["w2"]]
    return [jnp.stack(jnp.split(t, N_XY, axis=axis)).astype(BF16) for t, (_, axis) in zip(full, BIG)]


def model_step(g, xa, tgt, mod, shards, small, local=False):
    depth = len(shards)
    tabs = rope_tables(g)
    sps = [layer_small(small, l) for l in range(depth)]
    mods_a = [mod[l][:, :, 2:5] for l in range(depth)]
    mods_b = [jnp.concatenate([mod[l][:, :, 5:6], mod[l + 1][:, :, 0:2]], axis=2) if l + 1 < depth else mod[l][:, :, 5:6]
              for l in range(depth)]
    mod0 = mod[0][:, :, 0:2]
    args0 = [a_tok(g, xa), a_par(sps[0]["g0"]), a_mod(g, mod0)]
    (h,) = vfwd("modnorm", f_modnorm, g.grid, args0, [o_tok(g, D_MODEL, BF16)])
    x, tapes, wl = xa, [], []
    slots = shards[0] if local else exchange("gather_w", "gather", shards[0])
    for l in range(depth):
        g_next = sps[l + 1]["g0"] if l + 1 < depth else None
        wl.append(weights_of_slots(slots))
        comm = ("gather", shards[l + 1]) if l + 1 < depth and not local else None
        x, h, tp = layer_fwd(g, x, h, wl[l], sps[l], mods_a[l], mods_b[l], tabs, g_next, comm)
        slots = shards[l + 1] if local and l + 1 < depth else tp["carried"]
        tapes.append(tp)
    loss_row, dx = loss_call(g, x, tgt)
    dh, gss, dmods = None, [None] * depth, [None] * depth
    mine, theirs = [None] * depth, [None] * depth
    gslots = None
    for l in reversed(range(depth)):
        g_next = sps[l + 1]["g0"] if l + 1 < depth else None
        comm = ("scatter", gslots) if gslots is not None and not local else None
        dx, dh, gw, gss[l], dma, dmb, carried = layer_bwd(g, tapes[l], wl[l], sps[l], mods_a[l], mods_b[l], tabs, g_next,
                                                          dx, dh, comm)
        if l + 1 < depth:
            mine[l + 1], theirs[l + 1] = (gslots, None) if local else carried
        gslots = grad_slots(gw)
        dmods[l] = (dma, dmb)
    if local:
        mine[0] = gslots
    else:
        mine[0] = exchange("scatter_w", "scatter", gslots)
        theirs[0] = exchange("swap_w", "swap", mine[0])
    dxh, g0_first, dmod0 = vbwd("modnorm_b", f_modnorm, g.grid, args0, [o_tok(g, D_MODEL, BF16)], [[dh]])
    dxa = addn("sum_dx", [dx, dxh], F32)
    dmod = []
    for l in range(depth):
        first2 = dmod0 if l == 0 else dmods[l - 1][1][:, :, 1:3]
        dmod.append(jnp.concatenate([first2, dmods[l][0], dmods[l][1][:, :, 0:1]], axis=2))
    dmod = jnp.stack(dmod)
    def rows(key, n):
        return jnp.stack([gs[key][0, :n] for gs in gss])
    g_norm = jnp.stack([jnp.concatenate([g0_first if l == 0 else gss[l - 1]["g0_next"], gss[l]["g1"], gss[l]["g2"], gss[l]["g3"]], axis=0)
                        for l in range(depth)])
    gsmall = {
        "g_norm": g_norm,
        "att_q_gain": jnp.stack([gs["qgain"][0] for gs in gss]), "att_k_gain": jnp.stack([gs["kgain"][0] for gs in gss]),
        "dn_conv": jnp.stack([jnp.concatenate(gs["dn_conv"], axis=1) for gs in gss]),
        "dn_a_log": rows("alog", 8).reshape(depth, 2, N_HEAD4), "dn_dt_bias": rows("dtb", 8).reshape(depth, 2, N_HEAD4),
        "dn_norm_gain": jnp.stack([gs["ngain"][0] for gs in gss]),
        "ret_decay": rows("ret", 8).reshape(depth, 2, N_HEAD4),
        "sc_conv": jnp.stack([gs["sc_conv"] for gs in gss]),
    }
    return loss_row[0, 0], dxa, mine, theirs, gsmall, dmod


N_DEV = 8
N_XY = 4
HBM_SPEC = pl.BlockSpec(memory_space=pltpu.HBM)
VMEM_SPEC = pl.BlockSpec(memory_space=pltpu.VMEM)


def _coords():
    return lax.axis_index("x"), lax.axis_index("y"), lax.axis_index("c")


def _flip(coords, k):
    x, y, c = coords
    return (1 - x if k & 4 else x, 1 - y if k & 2 else y, 1 - c if k & 1 else c)


def allgather8(name, v):
    def body(v_ref, out_ref, send_sems, recv_sems, local_sem):
        me3 = _coords()
        me = 4 * me3[0] + 2 * me3[1] + me3[2]
        mine = pltpu.make_async_copy(v_ref, out_ref.at[me], local_sem)
        mine.start()
        sends = []
        for k in range(1, N_DEV):
            cp = pltpu.make_async_remote_copy(src_ref=v_ref, dst_ref=out_ref.at[me], send_sem=send_sems.at[k - 1],
                                              recv_sem=recv_sems.at[k - 1], device_id=_flip(me3, k), device_id_type=MESH)
            cp.start()
            sends.append(cp)
        for k in range(1, N_DEV):
            pltpu.make_async_remote_copy(src_ref=v_ref, dst_ref=out_ref.at[jnp.bitwise_xor(me, k)], send_sem=send_sems.at[k - 1],
                                         recv_sem=recv_sems.at[k - 1], device_id=_flip(me3, k), device_id_type=MESH).wait_recv()
        for cp in sends:
            cp.wait_send()
        mine.wait()

    return pl.pallas_call(
        body, name=name, out_shape=jax.ShapeDtypeStruct((N_DEV,) + v.shape, v.dtype),
        in_specs=[VMEM_SPEC], out_specs=VMEM_SPEC,
        scratch_shapes=[pltpu.SemaphoreType.DMA((N_DEV - 1,)), pltpu.SemaphoreType.DMA((N_DEV - 1,)), pltpu.SemaphoreType.DMA],
        compiler_params=pltpu.CompilerParams(vmem_limit_bytes=VMEM_LIMIT),
    )(v)


def exchange(name, kind, arrays):
    n = len(arrays)

    def body(*refs):
        xrefs = (refs[:n], refs[n:2 * n]) + tuple(refs[2 * n:])
        _exchange_start(xrefs, kind)
        _exchange_wait(xrefs, kind)

    return pl.pallas_call(
        body, name=name, out_shape=[_exchange_shape(kind, a) for a in arrays], in_specs=[HBM_SPEC] * n,
        out_specs=[HBM_SPEC] * n, scratch_shapes=_exchange_sems(n),
    )(*arrays)


def _exchange_shape(kind, a):
    return jax.ShapeDtypeStruct((N_XY,) + a.shape if kind == "gather" else a.shape, a.dtype)


def _exchange_sems(n):
    return [pltpu.SemaphoreType.DMA((n, N_XY - 1)), pltpu.SemaphoreType.DMA((n, N_XY - 1)), pltpu.SemaphoreType.DMA((n,))]


def _exchange_copies(xrefs, kind):
    srcs, dsts, send_sems, recv_sems, local_sems = xrefs
    me3 = _coords()
    me = 2 * me3[0] + me3[1]
    locals_, sends, recvs = [], [], []
    for i, (v_ref, out_ref) in enumerate(zip(srcs, dsts)):
        if kind == "swap":
            cp = pltpu.make_async_remote_copy(src_ref=v_ref, dst_ref=out_ref, send_sem=send_sems.at[i, 0],
                                              recv_sem=recv_sems.at[i, 0], device_id=_flip(me3, 1), device_id_type=MESH)
            sends.append(cp)
            recvs.append(cp)
            continue
        src = (lambda s, r=v_ref: r) if kind == "gather" else (lambda s, r=v_ref: r.at[s])

        def remote(k, src_slot, dst_slot):
            return pltpu.make_async_remote_copy(src_ref=src(src_slot), dst_ref=out_ref.at[dst_slot], send_sem=send_sems.at[i, k - 1],
                                                recv_sem=recv_sems.at[i, k - 1], device_id=_flip(me3, 2 * k), device_id_type=MESH)

        locals_.append(pltpu.make_async_copy(src(me), out_ref.at[me], local_sems.at[i]))
        sends += [remote(k, jnp.bitwise_xor(me, k), me) for k in range(1, N_XY)]
        recvs += [remote(k, me, jnp.bitwise_xor(me, k)) for k in range(1, N_XY)]
    return locals_, sends, recvs


def _exchange_start(xrefs, kind):
    locals_, sends, _ = _exchange_copies(xrefs, kind)
    for cp in locals_ + sends:
        cp.start()


def _exchange_wait(xrefs, kind):
    locals_, sends, recvs = _exchange_copies(xrefs, kind)
    for cp in recvs:
        cp.wait_recv()
    for cp in sends:
        cp.wait_send()
    for cp in locals_:
        cp.wait()


BLOCK_BYTES = 1 << 20


def _rows_block(rows, cols, limit=BLOCK_BYTES):
    for tr in (1024, 512, 256, 128, 64, 32, 16, 8):
        if rows % tr == 0 and tr * cols * 4 <= limit:
            return tr
    return rows


def sum_slots(name, r):
    s, rows, cols = r.shape
    tr = _rows_block(rows, cols)

    def body(r_ref, o_ref):
        tot = r_ref[0].astype(F32)
        for i in range(1, s):
            tot = tot + r_ref[i].astype(F32)
        o_ref[...] = tot

    return pl.pallas_call(
        body, name=name, grid=(rows // tr,), in_specs=[pl.BlockSpec((s, tr, cols), lambda i: (0, i, 0))],
        out_specs=pl.BlockSpec((tr, cols), lambda i: (i, 0)), out_shape=jax.ShapeDtypeStruct((rows, cols), F32),
        compiler_params=_params(1),
    )(r)


def adamw(name, w, m, v, parts):
    rows, cols = w.shape
    tr = _rows_block(rows, cols)
    n = len(parts)
    c1 = 1.0 - ADAM_B1 ** ADAM_STEP
    c2 = 1.0 - ADAM_B2 ** ADAM_STEP

    def body(*refs):
        w_ref, m_ref, v_ref = refs[:3]
        g_ref, d_ref, nm_ref, nv_ref = refs[3 + n:]
        g = refs[3][...]
        for r in refs[4:3 + n]:
            g = g + r[...]
        nm = ADAM_B1 * m_ref[...] + (1.0 - ADAM_B1) * g
        nv = ADAM_B2 * v_ref[...] + (1.0 - ADAM_B2) * jnp.square(g)
        d_ref[...] = -ADAM_LR * ((nm / c1) / (jnp.sqrt(nv / c2) + ADAM_EPS) + ADAM_WD * w_ref[...])
        g_ref[...], nm_ref[...], nv_ref[...] = g, nm, nv

    spec = pl.BlockSpec((tr, cols), lambda i: (i, 0))
    sds = jax.ShapeDtypeStruct((rows, cols), F32)
    return pl.pallas_call(
        body, name=name, grid=(rows // tr,), in_specs=[spec] * (3 + n), out_specs=[spec] * 4, out_shape=[sds] * 4,
        compiler_params=_params(1),
    )(w, m, v, *parts)


def adamw_slots(name, w, m, v, mine, theirs):
    depth = len(mine)
    rows, cols = mine[0].shape[1:]
    tr = _rows_block(rows, cols, BLOCK_BYTES // 2)
    nblk = rows // tr
    c1 = 1.0 - ADAM_B1 ** ADAM_STEP
    c2 = 1.0 - ADAM_B2 ** ADAM_STEP

    def body(*refs):
        w_ref, m_ref, v_ref = refs[:3]
        slot_refs = refs[3:3 + 2 * depth]
        g_ref, d_ref, nm_ref, nv_ref = refs[3 + 2 * depth:]
        for j in range(depth):
            @pl.when(pl.program_id(0) == j)
            def _():
                def total(ref):
                    tot = ref[0].astype(F32)
                    for s in range(1, N_XY):
                        tot = tot + ref[s].astype(F32)
                    return tot
                g = total(slot_refs[j]) + total(slot_refs[depth + j])
                nm = ADAM_B1 * m_ref[...] + (1.0 - ADAM_B1) * g
                nv = ADAM_B2 * v_ref[...] + (1.0 - ADAM_B2) * jnp.square(g)
                d_ref[...] = -ADAM_LR * ((nm / c1) / (jnp.sqrt(nv / c2) + ADAM_EPS) + ADAM_WD * w_ref[...])
                g_ref[...], nm_ref[...], nv_ref[...] = g, nm, nv

    spec = pl.BlockSpec((tr, cols), lambda l, i: (l * nblk + i, 0))
    slot_specs = [pl.BlockSpec((N_XY, tr, cols), lambda l, i, j=j: (0, jnp.where(l == j, i, 0), 0)) for j in range(depth)] * 2
    sds = jax.ShapeDtypeStruct((depth * rows, cols), F32)
    return pl.pallas_call(
        body, name=name, grid=(depth, nblk), in_specs=[spec] * 3 + slot_specs, out_specs=[spec] * 4, out_shape=[sds] * 4,
        compiler_params=_params(2),
    )(w, m, v, *mine, *theirs)


MOD_COLS = 512


def mod_fwd(call, w_mod, b_sh):
    depth, _, cols = w_mod.shape
    nr = call.shape[0]

    def body(c_ref, w_ref, b_ref, o_ref):
        o_ref[...] = _dg(_silu(c_ref[...]), w_ref[...], 1, 0) + b_ref[...]

    return pl.pallas_call(
        body, name="mod_fwd", grid=(depth, cols // MOD_COLS),
        in_specs=[pl.BlockSpec((nr, D_MODEL), lambda l, j: (0, 0)), pl.BlockSpec((None, D_MODEL, MOD_COLS), lambda l, j: (l, 0, j)),
                  pl.BlockSpec((None, 1, MOD_COLS), lambda l, j: (l, 0, j))],
        out_specs=pl.BlockSpec((None, nr, MOD_COLS), lambda l, j: (l, 0, j)),
        out_shape=jax.ShapeDtypeStruct((depth, nr, cols), F32), compiler_params=_params(2),
    )(call, w_mod, b_sh)


def mod_bwd(call, c_ctx, d_lat, d_ctx, w_mod, ctx_row):
    depth, _, cols = w_mod.shape
    nr, ns = call.shape[0], d_ctx.shape[1]

    def body(c_ref, cc_ref, dl_ref, dc_ref, w_ref, gw_ref, gc_ref):
        crow = jnp.sum(dc_ref[...], axis=0, keepdims=True)
        row = lax.broadcasted_iota(jnp.int32, (nr, 1), 0)
        dm = jnp.where(row == ctx_row, crow, dl_ref[...])
        gw_ref[...] = _dg(_silu(c_ref[...]), dm, 0, 0)
        ds = jnp.sum(_dg(jnp.broadcast_to(crow, (8, MOD_COLS)), w_ref[...], 1, 1), axis=0, keepdims=True) * 0.125
        _, vjp = jax.vjp(_silu, cc_ref[...])
        (part,) = vjp(ds)
        first = (pl.program_id(0) == 0) & (pl.program_id(1) == 0)

        @pl.when(first)
        def _():
            gc_ref[...] = part

        @pl.when(jnp.logical_not(first))
        def _():
            gc_ref[...] += part

    return pl.pallas_call(
        body, name="mod_bwd", grid=(depth, cols // MOD_COLS),
        in_specs=[pl.BlockSpec((nr, D_MODEL), lambda l, j: (0, 0)), pl.BlockSpec((1, D_MODEL), lambda l, j: (0, 0)),
                  pl.BlockSpec((None, nr, MOD_COLS), lambda l, j: (l, 0, j)), pl.BlockSpec((None, ns, MOD_COLS), lambda l, j: (l, 0, j)),
                  pl.BlockSpec((None, D_MODEL, MOD_COLS), lambda l, j: (l, 0, j))],
        out_specs=[pl.BlockSpec((None, D_MODEL, MOD_COLS), lambda l, j: (l, 0, j)), pl.BlockSpec((1, D_MODEL), lambda l, j: (0, 0))],
        out_shape=[jax.ShapeDtypeStruct((depth, D_MODEL, cols), F32), jax.ShapeDtypeStruct((1, D_MODEL), F32)],
        compiler_params=_params(2),
    )(call, c_ctx, d_lat, d_ctx, w_mod)


def bmod_grad(dm_all):
    ndev, depth, ns, cols = dm_all.shape

    def body(d_ref, o_ref):
        tot = d_ref[0]
        for i in range(1, ndev):
            tot = tot + d_ref[i]
        o_ref[...] = jnp.sum(tot, axis=0, keepdims=True)

    return pl.pallas_call(
        body, name="bmod_grad", grid=(depth,), in_specs=[pl.BlockSpec((ndev, None, ns, cols), lambda l: (0, l, 0, 0))],
        out_specs=pl.BlockSpec((None, 1, cols), lambda l: (l, 0, 0)), out_shape=jax.ShapeDtypeStruct((depth, 1, cols), F32),
        compiler_params=_params(1),
    )(dm_all)


def small_reduce(gathered, rows_all):
    ndev, rows, lanes = gathered.shape

    def body(g_ref, o_ref):
        tot = g_ref[0, 0:rows_all]
        for i in range(1, ndev):
            tot = tot + g_ref[i, 0:rows_all]
        o_ref[0:rows_all] = tot
        part = g_ref[0, rows_all:rows]
        for i in range(2, ndev, 2):
            part = part + g_ref[i, rows_all:rows]
        o_ref[rows_all:rows] = part

    return pl.pallas_call(body, name="small_reduce", out_shape=jax.ShapeDtypeStruct((rows, lanes), F32),
                          in_specs=[VMEM_SPEC], out_specs=VMEM_SPEC)(gathered)


def pack_rows(arrays, row_multiple=8):
    flat = jnp.concatenate([a.reshape(-1).astype(F32) for a in arrays])
    per = LANES * row_multiple
    padded = -(-flat.shape[0] // per) * per
    return jnp.pad(flat, (0, padded - flat.shape[0])).reshape(-1, LANES)


def unpack_rows(buf, shapes):
    flat, out, off = buf.reshape(-1), [], 0
    for s in shapes:
        n = int(np.prod(s))
        out.append(flat[off:off + n].reshape(s))
        off += n
    return out


SMALL_SHARDED = ("g_norm", "dn_conv", "sc_conv")
SMALL_ORDER = ("g_norm", "att_q_gain", "att_k_gain", "dn_conv", "dn_a_log", "dn_dt_bias", "dn_norm_gain", "ret_decay", "sc_conv")


def kernel(x, c, ctx, c_ctx, w_mod, b_mod, g_norm, w_in, att_q_gain, att_k_gain, dn_conv, dn_a_log, dn_dt_bias, dn_norm_gain, ret_decay, sc_conv, w_branch, w_out, w_mlp_in, w_mlp_out, loss_target, m_c_ctx, m_w_mod, m_b_mod, m_g_norm, m_w_in, m_att_q_gain, m_att_k_gain, m_dn_conv, m_dn_a_log, m_dn_dt_bias, m_dn_norm_gain, m_ret_decay, m_sc_conv, m_w_branch, m_w_out, m_w_mlp_in, m_w_mlp_out, v_c_ctx, v_w_mod, v_b_mod, v_g_norm, v_w_in, v_att_q_gain, v_att_k_gain, v_dn_conv, v_dn_a_log, v_dn_dt_bias, v_dn_norm_gain, v_ret_decay, v_sc_conv, v_w_branch, v_w_out, v_w_mlp_in, v_w_mlp_out):
    wts = dict(c_ctx=c_ctx, w_mod=w_mod, b_mod=b_mod, g_norm=g_norm, w_in=w_in, att_q_gain=att_q_gain, att_k_gain=att_k_gain,
               dn_conv=dn_conv, dn_a_log=dn_a_log, dn_dt_bias=dn_dt_bias, dn_norm_gain=dn_norm_gain, ret_decay=ret_decay,
               sc_conv=sc_conv, w_branch=w_branch, w_out=w_out, w_mlp_in=w_mlp_in, w_mlp_out=w_mlp_out)
    mom = dict(c_ctx=m_c_ctx, w_mod=m_w_mod, b_mod=m_b_mod, g_norm=m_g_norm, w_in=m_w_in, att_q_gain=m_att_q_gain,
               att_k_gain=m_att_k_gain, dn_conv=m_dn_conv, dn_a_log=m_dn_a_log, dn_dt_bias=m_dn_dt_bias,
               dn_norm_gain=m_dn_norm_gain, ret_decay=m_ret_decay, sc_conv=m_sc_conv, w_branch=m_w_branch, w_out=m_w_out,
               w_mlp_in=m_w_mlp_in, w_mlp_out=m_w_mlp_out)
    var = dict(c_ctx=v_c_ctx, w_mod=v_w_mod, b_mod=v_b_mod, g_norm=v_g_norm, w_in=v_w_in, att_q_gain=v_att_q_gain,
               att_k_gain=v_att_k_gain, dn_conv=v_dn_conv, dn_a_log=v_dn_a_log, dn_dt_bias=v_dn_dt_bias,
               dn_norm_gain=v_dn_norm_gain, ret_decay=v_ret_decay, sc_conv=v_sc_conv, w_branch=v_w_branch, w_out=v_w_out,
               w_mlp_in=v_w_mlp_in, w_mlp_out=v_w_mlp_out)
    names = list(wts)
    depth, bsz, seq, lc = w_mod.shape[0], x.shape[0], x.shape[1], ctx.shape[1]
    g = Geo(bsz, lc + seq, lc)
    xi, yi, ci = _coords()
    dev, xy = 4 * xi + 2 * yi + ci, 2 * xi + yi
    n_batch = N_DEV * bsz
    nr = -(-(n_batch + 1) // 16) * 16
    mod_cols = w_mod.shape[2]

    c_all = allgather8("gather_c", c).reshape(n_batch, D_MODEL)
    call = jnp.concatenate([c_all, c_ctx[None], jnp.zeros((nr - n_batch - 1, D_MODEL), F32)], axis=0)
    b_sh = lax.dynamic_slice_in_dim(b_mod, xy * mod_cols, mod_cols, axis=1)[:, None, :]
    mod_sh = mod_fwd(call, w_mod, b_sh)
    mod_g = allgather8("gather_mod", mod_sh.reshape(depth * nr, mod_cols)).reshape(N_XY, 2, depth, nr, mod_cols)[:, 0]
    mod_all = mod_g.transpose(1, 2, 0, 3).reshape(depth, nr, N_XY * mod_cols)
    mod_lat = lax.dynamic_slice_in_dim(mod_all, dev * bsz, bsz, axis=1).reshape(depth, bsz, 6, D_MODEL)
    mod_ctx = jnp.broadcast_to(mod_all[:, n_batch].reshape(depth, 1, 6, D_MODEL), (depth, bsz, 6, D_MODEL))
    mod = jnp.stack([mod_ctx, mod_lat], axis=2)

    sm_shapes = [wts[k].shape for k in SMALL_SHARDED]
    sm_g = allgather8("gather_small", pack_rows([wts[k] for k in SMALL_SHARDED])).reshape(N_XY, 2, -1)[:, 0]
    small = {k: wts[k] for k in SMALL_ORDER}
    for k, parts in zip(SMALL_SHARDED, zip(*[unpack_rows(sm_g[p], sm_shapes) for p in range(N_XY)])):
        small[k] = jnp.concatenate(parts, axis=-1)

    shards = [[wts[k][l].astype(BF16) for k, _ in BIG] for l in range(depth)]

    xa = jnp.concatenate([ctx, x], axis=1)
    loss_part, dxa, mine, theirs, gsmall, dmod = model_step(g, xa, loss_target, mod, shards, small)
    loss = lax.psum(loss_part, ("x", "y", "c"))
    grad_x = dxa[:, lc:]

    grads, deltas, new_m, new_v = {}, {}, {}, {}

    def update(k, parts, shape2d):
        res = adamw("adamw_" + k, wts[k].reshape(shape2d), mom[k].reshape(shape2d), var[k].reshape(shape2d), parts)
        grads[k], deltas[k], new_m[k], new_v[k] = (r.reshape(wts[k].shape) for r in res)

    for i, (k, _) in enumerate(BIG):
        cols = wts[k].shape[-1]
        slots = lambda per_layer: [s[i].reshape(N_XY, -1, cols) for s in per_layer]
        res = adamw_slots("adamw_" + k, wts[k].reshape(-1, cols), mom[k].reshape(-1, cols), var[k].reshape(-1, cols),
                          slots(mine), slots(theirs))
        grads[k], deltas[k], new_m[k], new_v[k] = (r.reshape(wts[k].shape) for r in res)

    dm_mine = jnp.concatenate([dmod[:, :, 1], dmod[:, :, 0]], axis=1).reshape(depth * 2 * bsz, 6 * D_MODEL)
    dm_all = allgather8("gather_dmod", dm_mine).reshape(N_DEV, depth, 2 * bsz, 6 * D_MODEL)
    gb = bmod_grad(dm_all).reshape(depth, 6 * D_MODEL)
    dm_cols = lax.dynamic_slice_in_dim(dm_all, xy * mod_cols, mod_cols, axis=3)
    d_lat = dm_cols[:, :, :bsz].transpose(1, 0, 2, 3).reshape(depth, n_batch, mod_cols)
    d_lat = jnp.pad(d_lat, ((0, 0), (0, nr - n_batch), (0, 0)))
    d_ctx = dm_cols[:, :, bsz:].transpose(1, 0, 2, 3).reshape(depth, n_batch, mod_cols)
    gw_mod, gc_part = mod_bwd(call, c_ctx[None], d_lat, d_ctx, w_mod, n_batch)
    update("w_mod", [gw_mod.reshape(depth * D_MODEL, mod_cols)], (depth * D_MODEL, mod_cols))
    update("b_mod", [gb], b_mod.shape)

    pack_all = pack_rows([gsmall[k] for k in SMALL_ORDER])
    pack_xy = pack_rows([gc_part])
    rows_all = pack_all.shape[0]
    tot = small_reduce(allgather8("gather_gsmall", jnp.concatenate([pack_all, pack_xy], axis=0)), rows_all)
    gtot = dict(zip(SMALL_ORDER, unpack_rows(tot[:rows_all], [gsmall[k].shape for k in SMALL_ORDER])))
    gtot["c_ctx"] = unpack_rows(tot[rows_all:], [c_ctx.shape])[0]
    for k in SMALL_SHARDED:
        width = wts[k].shape[-1]
        gtot[k] = lax.dynamic_slice_in_dim(gtot[k], xy * width, width, axis=gtot[k].ndim - 1)
    sm_names = ("c_ctx",) + SMALL_ORDER
    sm_shapes = [wts[k].shape for k in sm_names]
    res = adamw("adamw_small", pack_rows([wts[k] for k in sm_names]), pack_rows([mom[k] for k in sm_names]),
                pack_rows([var[k] for k in sm_names]), [pack_rows([gtot[k] for k in sm_names])])
    for dst, buf in zip((grads, deltas, new_m, new_v), res):
        dst.update(zip(sm_names, unpack_rows(buf, sm_shapes)))

    return (loss, grad_x, *[grads[k] for k in names], *[deltas[k] for k in names], *[new_m[k] for k in names],
            *[new_v[k] for k in names])
```

```python
import functools
import math

import numpy as np
import jax
import jax.numpy as jnp
from jax import lax
from jax.experimental import pallas as pl
from jax.experimental.pallas import tpu as pltpu

F32, BF16 = jnp.float32, jnp.bfloat16
HIGHEST = lax.Precision.HIGHEST
MESH = pl.DeviceIdType.MESH

D_MODEL = 1024
GRID_W = 64
N_BRANCH = 4
BRANCH_W = 512
HEAD64 = 64
HEAD128 = 128
N_HEAD4 = 4
ATT_HEADS = 8
ATT_KV_HEADS = 2
CHUNK = 64
MLP_HIDDEN = 4 * D_MODEL
ROPE_THETA = 10000.0
EPS = 1e-6
N_IN = 10000
ADAM_LR, ADAM_B1, ADAM_B2, ADAM_EPS, ADAM_WD, ADAM_STEP = 0.001, 0.9, 0.999, 1e-08, 0.01, 10

LANES = 128
VMEM_LIMIT = 56 * 1024 * 1024

_SEGS = (
    ("gates", 5904, 4096),
    ("att_q", 0, 512), ("dn_q", 768, 512), ("dn_k", 1280, 512), ("dn_v", 1792, 512), ("dn_z", 2304, 512),
    ("ret_v", 3344, 512), ("ret_g", 3856, 512), ("sc_b", 4368, 512), ("sc_c", 4880, 512), ("sc_x", 5392, 512),
    ("ret_q", 2832, 256), ("ret_k", 3088, 256),
    ("att_k", 512, 128), ("att_v", 640, 128),
    ("narrow", 2816, 16),
)
NZ = 10240


def _seg_offsets():
    off, out = 0, {}
    for name, _, width in _SEGS:
        out[name] = off
        off += width
    return out


ZOFF = _seg_offsets()


def _pick(n, cands):
    for c in cands:
        if n % c == 0:
            return c
    return n


def _dg(a, b, ca, cb, batch=False):
    dn = (((ca,), (cb,)), ((0,), (0,))) if batch else (((ca,), (cb,)), ((), ()))
    return lax.dot_general(a.astype(BF16), b.astype(BF16), dn, preferred_element_type=F32)


@functools.partial(jax.custom_vjp, nondiff_argnums=(2, 3))
def _mm(a, b, ca, cb):
    return _dg(a, b, ca, cb)


def _mm_fwd(a, b, ca, cb):
    return _dg(a, b, ca, cb), (a, b)


def _mm_bwd(ca, cb, res, g):
    a, b = res
    if ca == 1:
        da = _mm(g, b, 1, 1) if cb == 0 else _mm(g, b, 1, 0)
    else:
        da = _mm(b, g, 1, 1) if cb == 0 else _mm(b, g, 0, 1)
    if cb == 0:
        db = _mm(a, g, 0, 0) if ca == 1 else _mm(a, g, 1, 0)
    else:
        db = _mm(g, a, 0, 0) if ca == 1 else _mm(g, a, 0, 1)
    return da.astype(a.dtype), db.astype(b.dtype)


_mm.defvjp(_mm_fwd, _mm_bwd)


@functools.partial(jax.custom_vjp, nondiff_argnums=(2, 3))
def _bmm(a, b, ca, cb):
    return _dg(a, b, ca, cb, True)


def _bmm_fwd(a, b, ca, cb):
    return _dg(a, b, ca, cb, True), (a, b)


def _bmm_bwd(ca, cb, res, g):
    a, b = res
    if ca == 2:
        da = _bmm(g, b, 2, 2) if cb == 1 else _bmm(g, b, 2, 1)
    else:
        da = _bmm(b, g, 2, 2) if cb == 1 else _bmm(b, g, 1, 2)
    if cb == 1:
        db = _bmm(a, g, 1, 1) if ca == 2 else _bmm(a, g, 2, 1)
    else:
        db = _bmm(g, a, 1, 1) if ca == 2 else _bmm(g, a, 1, 2)
    return da.astype(a.dtype), db.astype(b.dtype)


_bmm.defvjp(_bmm_fwd, _bmm_bwd)


def _split_bf16(x):
    hi = x.astype(BF16)
    lo = (x - hi.astype(F32)).astype(BF16)
    return hi, lo


def _mm3(a, b, ca, cb):
    ah, al = _split_bf16(a)
    bh, bl = _split_bf16(b)
    dn = (((ca,), (cb,)), ((), ()))
    d = lambda u, v: lax.dot_general(u, v, dn, preferred_element_type=F32)
    return d(ah, bh) + (d(ah, bl) + d(al, bh))


def _mm_exact(a, b):
    return jnp.dot(a, b, precision=HIGHEST, preferred_element_type=F32)


class Arg:
    def __init__(self, x, block, imap, diff=True, first=None, gdtype=F32, pieces=None, gshape=None, gimap=None):
        self.x, self.block, self.imap = x, tuple(block), imap
        self.diff, self.first, self.gdtype, self.pieces = diff, first, gdtype, pieces
        self.gshape = tuple(x.shape) if gshape is None else tuple(gshape)
        self.gimap = imap if gimap is None else gimap

    def spec(self):
        return pl.BlockSpec(self.block, self.imap)

    def gspec(self):
        return pl.BlockSpec(self.block, self.gimap)


class Out:
    def __init__(self, shape, dtype, block, imap, pieces=None):
        self.shape, self.dtype, self.block, self.imap, self.pieces = tuple(shape), dtype, tuple(block), imap, pieces

    def spec(self):
        return pl.BlockSpec(self.block, self.imap)

    def sds(self):
        return jax.ShapeDtypeStruct(self.shape, self.dtype)


def _lanes(ref, s, w):
    return (slice(None),) * (len(ref.shape) - 1) + (slice(s, s + w),)


def _load(ref, pieces):
    if pieces is None:
        return ref[...]
    return tuple(ref[_lanes(ref, s, w)] for s, w in pieces)


def _store(ref, val, pieces, accumulate=False):
    if pieces is None:
        if accumulate:
            ref[...] += val.astype(ref.dtype)
        else:
            ref[...] = val.astype(ref.dtype)
        return
    if not accumulate:
        covered = sum(w for _, w in pieces)
        if covered != ref.shape[-1]:
            ref[...] = jnp.zeros(ref.shape, ref.dtype)
    for (s, w), v in zip(pieces, val):
        if accumulate:
            ref[_lanes(ref, s, w)] += v.astype(ref.dtype)
        else:
            ref[_lanes(ref, s, w)] = v.astype(ref.dtype)


def _params(n_grid):
    return pltpu.CompilerParams(dimension_semantics=("arbitrary",) * n_grid, vmem_limit_bytes=VMEM_LIMIT)


def vfwd(name, f, grid, args, outs):
    n_in = len(args)

    def body(*refs):
        ids = tuple(pl.program_id(i) for i in range(len(grid)))
        vals = [_load(r, a.pieces) for r, a in zip(refs[:n_in], args)]
        res = f(ids, *vals)
        for r, o, spec in zip(refs[n_in:], res, outs):
            _store(r, o, spec.pieces)

    return pl.pallas_call(
        body, name=name, grid=grid,
        in_specs=[a.spec() for a in args], out_specs=[o.spec() for o in outs],
        out_shape=[o.sds() for o in outs], compiler_params=_params(len(grid)),
    )(*[a.x for a in args])


def vbwd(name, f, grid, args, outs, cts):
    n_in = len(args)
    diff_idx = [i for i, a in enumerate(args) if a.diff]
    ct_flat = [c for per_out in cts for c in per_out]
    ct_specs = [o.spec() for o, per_out in zip(outs, cts) for _ in per_out]
    n_ct = len(ct_flat)

    def body(*refs):
        ids = tuple(pl.program_id(i) for i in range(len(grid)))
        vals = [_load(r, a.pieces) for r, a in zip(refs[:n_in], args)]
        ct_refs = refs[n_in:n_in + n_ct]
        g_refs = refs[n_in + n_ct:]

        def g(*dvals):
            full = list(vals)
            for i, v in zip(diff_idx, dvals):
                full[i] = v
            return tuple(f(ids, *full))

        _, vjp = jax.vjp(g, *[vals[i] for i in diff_idx])
        ct_vals, k = [], 0
        for o, per_out in zip(outs, cts):
            tot = None
            for _ in per_out:
                v = _load(ct_refs[k], o.pieces)
                k += 1
                v = jax.tree.map(lambda t: t.astype(F32), v)
                tot = v if tot is None else jax.tree.map(jnp.add, tot, v)
            ct_vals.append(tot)
        grads = vjp(tuple(ct_vals))
        for gr, gv, i in zip(g_refs, grads, diff_idx):
            a = args[i]
            if a.first is None:
                _store(gr, gv, a.pieces)
            else:
                is_first = a.first(ids)

                @pl.when(is_first)
                def _():
                    _store(gr, gv, a.pieces)

                @pl.when(jnp.logical_not(is_first))
                def _():
                    _store(gr, gv, a.pieces, accumulate=True)

    g_specs = [args[i].gspec() for i in diff_idx]
    g_shapes = [jax.ShapeDtypeStruct(args[i].gshape, args[i].gdtype) for i in diff_idx]
    return pl.pallas_call(
        body, name=name, grid=grid,
        in_specs=[a.spec() for a in args] + ct_specs, out_specs=g_specs, out_shape=g_shapes,
        compiler_params=_params(len(grid)),
    )(*[a.x for a in args], *ct_flat)


def matmul(name, a, b, trans_a=False, trans_b=False, epilogue=None, extras=(), out_dtypes=(F32,)):
    if trans_a:
        kdim, m = a.shape
    else:
        m, kdim = a.shape
    n = b.shape[0] if trans_b else b.shape[1]
    assert b.shape[1 if trans_b else 0] == kdim
    tm = _pick(m, (1024, 512, 256, 192, 128, 64))
    tn = _pick(n, (1024, 512, 256, 128))
    tk = _pick(kdim, (1024, 512, 256, 192, 128, 64))
    nk = kdim // tk
    n_x = len(extras)
    assert epilogue is None or nk == 1
    assert epilogue is not None or tuple(out_dtypes) == (F32,)

    def body(a_ref, b_ref, *rest):
        o_ref = rest[n_x]
        part = _dg(a_ref[...], b_ref[...], 0 if trans_a else 1, 1 if trans_b else 0)
        if epilogue is not None:
            for r, val in zip(rest[n_x:], epilogue(part, *[x[...] for x in rest[:n_x]])):
                r[...] = val.astype(r.dtype)
        elif nk == 1:
            o_ref[...] = part
        else:
            k = pl.program_id(2)

            @pl.when(k == 0)
            def _():
                o_ref[...] = part

            @pl.when(k > 0)
            def _():
                o_ref[...] += part

    a_spec = pl.BlockSpec((tk, tm), lambda i, j, k: (k, i)) if trans_a else pl.BlockSpec((tm, tk), lambda i, j, k: (i, k))
    b_spec = pl.BlockSpec((tn, tk), lambda i, j, k: (j, k)) if trans_b else pl.BlockSpec((tk, tn), lambda i, j, k: (k, j))
    mn_spec = pl.BlockSpec((tm, tn), lambda i, j, k: (i, j))
    res = pl.pallas_call(
        body, name=name, grid=(m // tm, n // tn, nk),
        in_specs=[a_spec, b_spec] + [mn_spec] * n_x,
        out_specs=[mn_spec] * len(out_dtypes),
        out_shape=[jax.ShapeDtypeStruct((m, n), dt) for dt in out_dtypes],
        compiler_params=pltpu.CompilerParams(dimension_semantics=("parallel", "parallel", "arbitrary"),
                                             vmem_limit_bytes=VMEM_LIMIT),
    )(a, b, *extras)
    return res if epilogue is not None else res[0]


def _rms(x, gain):
    return x * lax.rsqrt(jnp.mean(x * x, axis=-1, keepdims=True) + EPS) * gain


def _silu(x):
    return x * jax.nn.sigmoid(x)


def f_modnorm(ids, x, gain, mod):
    return (_rms(x, gain) * (1.0 + mod[1:2]) + mod[0:1],)


def f_resnorm(ids, x, y, g_res, g_next, mod):
    x_new = x + mod[0:1] * _rms(y, g_res)
    return x_new, _rms(x_new, g_next) * (1.0 + mod[2:3]) + mod[1:2]


def f_resid(ids, x, y, g_res, mod):
    return (x + mod[0:1] * _rms(y, g_res),)


def f_act(ids, a):
    r = jnp.maximum(a, 0.0)
    return (r * r,)


def _head_consts(width, head):
    i = lax.broadcasted_iota(jnp.int32, (width, width), 0)
    j = lax.broadcasted_iota(jnp.int32, (width, width), 1)
    shift = int(math.log2(head))
    same = (i >> shift) == (j >> shift)
    group = jnp.where(same, 1.0 / head, 0.0).astype(F32)
    half = head // 2
    ii, jj = i & (head - 1), j & (head - 1)
    rot = jnp.where(same & (ii == jj + half) & (jj < half), -1.0, 0.0) + jnp.where(same & (ii + half == jj) & (jj >= half), 1.0, 0.0)
    ti = lax.broadcasted_iota(jnp.int32, (head, width), 0)
    tj = lax.broadcasted_iota(jnp.int32, (head, width), 1)
    tile = jnp.where(ti == (tj & (head - 1)), 1.0, 0.0).astype(F32)
    return group, rot.astype(F32), tile


def _rope(x, cos, sin, rot):
    return x * cos + _mm_exact(x, rot) * sin


def _softplus(x):
    return jnp.maximum(x, 0.0) + jnp.log(1.0 + jnp.exp(-jnp.abs(x)))


def f_prep(ids, zq, zk, rq, rk, zn, qgain, kgain, alog, dtb, cos, sin):
    grp_q, rot_q, tile_q = _head_consts(ATT_HEADS * HEAD64, HEAD64)
    grp_k, rot_k, tile_k = _head_consts(ATT_KV_HEADS * HEAD64, HEAD64)
    grp_r, rot_r, _ = _head_consts(N_HEAD4 * HEAD64, HEAD64)
    wq, wk, wr = zq.shape[-1], zk.shape[-1], rq.shape[-1]
    qn = zq * lax.rsqrt(_mm_exact(zq * zq, grp_q) + EPS) * _mm_exact(qgain, tile_q)
    kn = zk * lax.rsqrt(_mm_exact(zk * zk, grp_k) + EPS) * _mm_exact(kgain, tile_k)
    q_att = _rope(qn, cos[:, :wq], sin[:, :wq], rot_q) * (HEAD64 ** -0.5)
    k_att = _rope(kn, cos[:, :wk], sin[:, :wk], rot_k)
    q_ret = _rope(rq, cos[:, :wr], sin[:, :wr], rot_r)
    k_ret = _rope(rk * (HEAD64 ** -0.5), cos[:, :wr], sin[:, :wr], rot_r)
    lane = lax.broadcasted_iota(jnp.int32, zn.shape, 1)
    log_a = -jnp.exp(alog) * _softplus(zn + dtb)
    gates = jnp.where(lane < 8, log_a, jnp.where(lane < 16, jax.nn.sigmoid(zn), 0.0))
    return q_att, k_att, q_ret, k_ret, gates


def make_f_attn(tq):
    def f_attn(ids, q, k, v):
        outs = []
        per = ATT_HEADS // ATT_KV_HEADS
        for g in range(ATT_KV_HEADS):
            qg = jnp.concatenate(q[g * per:(g + 1) * per], axis=0)
            s = _mm(qg, k[g], 1, 1)
            e = jnp.exp(s - lax.stop_gradient(jnp.max(s, axis=-1, keepdims=True)))
            o = _mm(e, v[g], 1, 0) * (1.0 / jnp.sum(e, axis=-1, keepdims=True))
            outs += [o[i * tq:(i + 1) * tq] for i in range(per)]
        return (tuple(outs),)
    return f_attn


def _roll_rows(x, shift):
    return pltpu.roll(x, shift, 0)


def make_shifts(t, lc):
    def _down(x):
        row = lax.broadcasted_iota(jnp.int32, x.shape, 0)
        return jnp.where((row == 0) | (row == lc), 0.0, _roll_rows(x, 1))

    def _up(x):
        row = lax.broadcasted_iota(jnp.int32, x.shape, 0)
        return jnp.where((row == lc - 1) | (row == t - 1), 0.0, _roll_rows(x, t - 1))

    @jax.custom_vjp
    def down(x):
        return _down(x)

    @jax.custom_vjp
    def up(x):
        return _up(x)

    down.defvjp(lambda x: (_down(x), None), lambda _, g: (up(g),))
    up.defvjp(lambda x: (_up(x), None), lambda _, g: (down(g),))
    return down, up


def make_conv3(t, lc):
    down, up = make_shifts(t, lc)

    def conv3(x, w):
        return w[0:1] * down(x) + w[1:2] * x + w[2:3] * up(x)
    return conv3


def make_f_dnprep(t, lc):
    conv3 = make_conv3(t, lc)

    def l2n(x):
        return x * lax.rsqrt(jnp.sum(x * x, axis=-1, keepdims=True) + EPS)

    def f_dnprep(ids, q, k, v, wq, wk, wv):
        qn = l2n(_silu(conv3(q, wq))) * (HEAD128 ** -0.5)
        kn = l2n(_silu(conv3(k, wk)))
        return qn, kn, _silu(conv3(v, wv))
    return f_dnprep


def make_f_shortconv(t, lc):
    conv3 = make_conv3(t, lc)

    def f_shortconv(ids, b, c, x, w):
        return (b * conv3(c * x, w),)
    return f_shortconv


def f_finish(ids, o_dn_f, o_dn_b, z_dn, o_rt_f, o_rt_b, g_rt, ngain):
    y_dn, y_rt = [], []
    for h in range(N_HEAD4):
        o = o_dn_f[h] + o_dn_b[h]
        y_dn.append(_rms(o, ngain) * _silu(z_dn[h]))
        r = o_rt_f[h] + o_rt_b[h]
        mu = jnp.mean(r, axis=-1, keepdims=True)
        var = jnp.mean(jnp.square(r - mu), axis=-1, keepdims=True)
        y_rt.append((r - mu) * lax.rsqrt(var + EPS) * _silu(g_rt[h]))
    return tuple(y_dn), tuple(y_rt)


def f_merge(ids, p0, p1, p2, p3, gates):
    u = jax.nn.sigmoid(gates[0]) * p0
    for g, p in zip(gates[1:], (p1, p2, p3)):
        u = u + jax.nn.sigmoid(g) * p
    return (u,)


def _stack_masks():
    n = N_HEAD4 * CHUNK
    i = lax.broadcasted_iota(jnp.int32, (n, n), 0)
    j = lax.broadcasted_iota(jnp.int32, (n, n), 1)
    same = (i >> 6) == (j >> 6)
    pi, pj = i & (CHUNK - 1), j & (CHUNK - 1)
    return same, pi, pj


def _inv_unit_lower(low):
    n = low.shape[0]
    eye = jnp.where(lax.broadcasted_iota(jnp.int32, (n, n), 0) == lax.broadcasted_iota(jnp.int32, (n, n), 1), 1.0, 0.0).astype(F32)
    m = -low
    p = eye + m
    for _ in range(int(math.log2(CHUNK)) - 1):
        m = _mm3(m, m, 1, 0)
        p = p + _mm3(p, m, 1, 0)
    return p


@jax.custom_vjp
def _tri_solve(low, rhs):
    return _mm3(_inv_unit_lower(low), rhs, 1, 0)


def _tri_solve_fwd(low, rhs):
    inv = _inv_unit_lower(low)
    x = _mm3(inv, rhs, 1, 0)
    return x, (inv, x)


def _tri_solve_bwd(res, g):
    inv, x = res
    d_rhs = _mm3(inv, g, 0, 0)
    return -_mm3(d_rhs, x, 1, 1), d_rhs


_tri_solve.defvjp(_tri_solve_fwd, _tri_solve_bwd)


def _heads3(x):
    return x.reshape(N_HEAD4, CHUNK, x.shape[-1])


def _time_masks(direction):
    same, pi, pj = _stack_masks()
    if direction == 0:
        return same, same & (pi >= pj), same & (pi > pj), same & (pj == CHUNK - 1), pi - pj
    return same, same & (pi <= pj), same & (pi < pj), same & (pj == 0), pj - pi


def make_dn_chunk(direction):
    def dn_chunk(s, toks, params):
        q, k, v, la, beta = toks
        qs, ks, vs = (jnp.concatenate(t, axis=0) for t in (q, k, v))
        la, beta = jnp.concatenate(la, axis=0), jnp.concatenate(beta, axis=0)
        n = N_HEAD4 * CHUNK
        same, incl, strict, last, _ = _time_masks(direction)
        g = jnp.sum(jnp.where(incl, jnp.broadcast_to(la, (n, n)).T, 0.0), axis=1, keepdims=True)
        gb = jnp.broadcast_to(g, (n, n))
        gbt = gb.T
        dec_incl = jnp.where(incl, jnp.exp(jnp.where(incl, gb - gbt, 0.0)), 0.0)
        dec_strict = jnp.where(strict, dec_incl, 0.0)
        low = beta * _mm(ks, ks, 1, 1) * dec_strict
        eg = jnp.exp(g)
        sol = _tri_solve(low, jnp.concatenate([beta * vs, (beta * eg) * ks], axis=1))
        w_v, w_k = sol[:, :HEAD128], sol[:, HEAD128:]
        a_qk = _mm(qs, ks, 1, 1) * dec_incl
        g_last = jnp.sum(jnp.where(last, gbt, 0.0), axis=1, keepdims=True)
        k_dec = ks * jnp.exp(g_last - g)
        u = w_v - _bmm(_heads3(w_k), s, 2, 1).reshape(n, HEAD128)
        o = _bmm(_heads3(qs * eg), s, 2, 1).reshape(n, HEAD128) + _mm(a_qk, u, 1, 0)
        decay = jnp.exp(jnp.mean(_heads3(g_last), axis=1, keepdims=True))
        s_new = s * decay + _bmm(_heads3(k_dec), _heads3(u), 1, 1)
        return (tuple(o[h * CHUNK:(h + 1) * CHUNK] for h in range(N_HEAD4)),), s_new
    return dn_chunk


def make_ret_chunk(direction):
    def ret_chunk(s, toks, params):
        q, k, v = toks
        (decay_log,) = params
        qs, ks, vs = (jnp.concatenate(t, axis=0) for t in (q, k, v))
        n = N_HEAD4 * CHUNK
        _, incl, _, _, rel = _time_masks(direction)
        row_head = lax.broadcasted_iota(jnp.int32, (n, 1), 0) >> 6
        lg = jnp.zeros((n, 1), F32)
        for h in range(N_HEAD4):
            lg = jnp.where(row_head == h, -jnp.exp(decay_log[direction * N_HEAD4 + h]), lg)
        row = (lax.broadcasted_iota(jnp.int32, (n, 1), 0) & (CHUNK - 1)).astype(F32)
        pos = row if direction == 0 else CHUNK - 1.0 - row
        dmask = jnp.where(incl, jnp.exp(jnp.where(incl, rel.astype(F32) * lg, 0.0)), 0.0)
        o = _mm(_mm(qs, ks, 1, 1) * dmask, vs, 1, 0)
        k_dec = ks * jnp.exp((CHUNK - 1.0 - pos) * lg)
        kv = _bmm(_heads3(k_dec), _heads3(vs), 1, 1)
        o = o + _bmm(_heads3(qs * jnp.exp((pos + 1.0) * lg)), s, 2, 1).reshape(n, HEAD128)
        decay = jnp.exp(CHUNK * jnp.mean(_heads3(lg), axis=1, keepdims=True))
        s_new = s * decay + kv
        return (tuple(o[h * CHUNK:(h + 1) * CHUNK] for h in range(N_HEAD4)),), s_new
    return ret_chunk


def _lane_pieces(width, n=N_HEAD4):
    return [(h * width, width) for h in range(n)]


class Tok:
    def __init__(self, x, pieces, width=None, col=0):
        self.x, self.pieces, self.col = x, pieces, col
        self.width = x.shape[-1] if width is None else width


def _chunk_of(direction, step, nc, nctx):
    if direction == 0:
        return step
    return jnp.where(step < nctx, nctx - 1 - step, nc + nctx - 1 - step)


def scan_fwd(name, dirs, params, state_shape, outs, lc):
    bsz, t = dirs[0][1][0].x.shape[:2]
    nc, nctx = t // CHUNK, lc // CHUNK
    nd, n_t, n_p, n_o = len(dirs), len(dirs[0][1]), len(params), len(outs)

    def body(*refs):
        tok_refs, par_refs = refs[:nd * n_t], refs[nd * n_t:nd * n_t + n_p]
        out_refs = refs[nd * n_t + n_p:-nd]
        s_refs = refs[-nd:]
        pv = [_load(r, p) for r, (_, p) in zip(par_refs, params)]
        for d, (fn, toks) in enumerate(dirs):
            s_ref = s_refs[d]

            @pl.when(pl.program_id(1) == 0)
            def _():
                s_ref[...] = jnp.zeros(s_ref.shape, F32)

            o_refs = out_refs[d * (n_o + 1):(d + 1) * (n_o + 1)]
            s = s_ref[...]
            o_refs[-1][...] = s
            tv = [_load(r, tk.pieces) for r, tk in zip(tok_refs[d * n_t:(d + 1) * n_t], toks)]
            res, s_new = fn(s, tv, pv)
            s_ref[...] = s_new
            for r, o, (_, _, pieces) in zip(o_refs, res, outs):
                _store(r, o, pieces)

    nstate = len(state_shape)
    tok_specs, out_specs, out_shapes, operands = [], [], [], []
    for d, (_, toks) in enumerate(dirs):
        for tk in toks:
            tok_specs.append(pl.BlockSpec((None, CHUNK, tk.width), lambda b, c, d=d, col=tk.col: (b, _chunk_of(d, c, nc, nctx), col)))
            operands.append(tk.x)
        for w, dt, _ in outs:
            out_specs.append(pl.BlockSpec((None, CHUNK, w), lambda b, c, d=d: (b, _chunk_of(d, c, nc, nctx), 0)))
            out_shapes.append(jax.ShapeDtypeStruct((bsz, t, w), dt))
        out_specs.append(pl.BlockSpec((None, None) + state_shape, lambda b, c, d=d: (b, _chunk_of(d, c, nc, nctx)) + (0,) * nstate))
        out_shapes.append(jax.ShapeDtypeStruct((bsz, nc) + state_shape, F32))
    par_specs = [pl.BlockSpec(x.shape, lambda b, c: (0, 0)) for x, _ in params]
    return pl.pallas_call(
        body, name=name, grid=(bsz, nc),
        in_specs=tok_specs + par_specs, out_specs=out_specs, out_shape=out_shapes,
        scratch_shapes=[pltpu.VMEM(state_shape, F32)] * nd, compiler_params=_params(2),
    )(*operands, *[x for x, _ in params])


def scan_bwd(name, dirs, params, state_shape, outs, lc, sprevs, cts):
    bsz, t = dirs[0][1][0].x.shape[:2]
    nc, nctx = t // CHUNK, lc // CHUNK
    nd, n_t, n_p, n_o = len(dirs), len(dirs[0][1]), len(params), len(outs)
    per_in = n_t + 1 + n_o

    def body(*refs):
        par_refs = refs[nd * per_in:nd * per_in + n_p]
        g_refs = refs[nd * per_in + n_p:-nd]
        ds_refs = refs[-nd:]
        first = pl.program_id(1) == 0
        pv = [_load(r, p) for r, (_, p) in zip(par_refs, params)]
        d_par = None
        for d, (fn, toks) in enumerate(dirs):
            ins = refs[d * per_in:(d + 1) * per_in]
            tok_refs, sprev_ref, ct_refs = ins[:n_t], ins[n_t], ins[n_t + 1:]
            ds_ref = ds_refs[d]

            @pl.when(first)
            def _():
                ds_ref[...] = jnp.zeros(ds_ref.shape, F32)

            tv = [_load(r, tk.pieces) for r, tk in zip(tok_refs, toks)]
            _, vjp = jax.vjp(fn, sprev_ref[...], tv, pv)
            ct = tuple(_load(r, pieces) for r, (_, _, pieces) in zip(ct_refs, outs))
            d_s, d_tv, d_pv = vjp((ct, ds_ref[...]))
            ds_ref[...] = d_s
            for r, gv, tk in zip(g_refs[d * n_t:(d + 1) * n_t], d_tv, toks):
                _store(r, gv, tk.pieces)
            d_par = d_pv if d_par is None else jax.tree.map(jnp.add, d_par, d_pv)
        very_first = first & (pl.program_id(0) == 0)
        for r, gv, (_, p) in zip(g_refs[nd * n_t:], d_par, params):
            @pl.when(very_first)
            def _():
                _store(r, gv, p)

            @pl.when(jnp.logical_not(very_first))
            def _():
                _store(r, gv, p, accumulate=True)

    nstate = len(state_shape)
    in_specs, operands, g_specs, g_shapes = [], [], [], []
    for d, (_, toks) in enumerate(dirs):
        chunk = lambda c, d=d: _chunk_of(d, nc - 1 - c, nc, nctx)
        for tk in toks:
            in_specs.append(pl.BlockSpec((None, CHUNK, tk.width), lambda b, c, f=chunk, col=tk.col: (b, f(c), col)))
            operands.append(tk.x)
            g_specs.append(pl.BlockSpec((None, CHUNK, tk.width), lambda b, c, f=chunk: (b, f(c), 0)))
            g_shapes.append(jax.ShapeDtypeStruct((bsz, t, tk.width), F32))
        in_specs.append(pl.BlockSpec((None, None) + state_shape, lambda b, c, f=chunk: (b, f(c)) + (0,) * nstate))
        operands.append(sprevs[d])
        for (w, _, _), ct in zip(outs, cts[d]):
            in_specs.append(pl.BlockSpec((None, CHUNK, w), lambda b, c, f=chunk: (b, f(c), 0)))
            operands.append(ct)
    par_specs = [pl.BlockSpec(x.shape, lambda b, c: (0, 0)) for x, _ in params]
    return pl.pallas_call(
        body, name=name, grid=(bsz, nc),
        in_specs=in_specs + par_specs, out_specs=g_specs + par_specs,
        out_shape=g_shapes + [jax.ShapeDtypeStruct(x.shape, F32) for x, _ in params],
        scratch_shapes=[pltpu.VMEM(state_shape, F32)] * nd, compiler_params=_params(2),
    )(*operands, *[x for x, _ in params])


ROWS = N_HEAD4 * CHUNK


def _bmm3(a, b, ca, cb):
    ah, al = _split_bf16(a)
    bh, bl = _split_bf16(b)
    dn = (((ca,), (cb,)), ((0,), (0,)))
    d = lambda u, v: lax.dot_general(u, v, dn, preferred_element_type=F32)
    return d(ah, bh) + (d(ah, bl) + d(al, bh))


def _inv_unit_tri_b(low):
    n = low.shape[-1]
    eye = (lax.broadcasted_iota(jnp.int32, (1, n, n), 1) == lax.broadcasted_iota(jnp.int32, (1, n, n), 2)).astype(F32)
    m = -low
    p = eye + m
    for _ in range(int(math.log2(CHUNK)) - 1):
        m = _bmm3(m, m, 2, 1)
        p = p + _bmm3(p, m, 2, 1)
    return p


@jax.custom_vjp
def _tri_solve_b(low, rhs, inv):
    return _bmm3(inv, rhs, 2, 1)


def _tri_solve_b_fwd(low, rhs, inv):
    x = _bmm3(inv, rhs, 2, 1)
    return x, (inv, x)


def _tri_solve_b_bwd(res, g):
    inv, x = res
    d_rhs = _bmm3(inv, g, 1, 1)
    return -_bmm3(d_rhs, x, 2, 2), d_rhs, jnp.zeros_like(inv)


_tri_solve_b.defvjp(_tri_solve_b_fwd, _tri_solve_b_bwd)


def _stack_dirs(toks, i):
    return jnp.concatenate([jnp.concatenate(t[i], axis=1) for t in toks], axis=0)


def _problem_masks(p, nb):
    shape = (p, ROWS, ROWS)
    up = lax.broadcasted_iota(jnp.int32, shape, 0) >= nb
    i = lax.broadcasted_iota(jnp.int32, shape, 1)
    j = lax.broadcasted_iota(jnp.int32, shape, 2)
    same = (i >> 6) == (j >> 6)
    pi, pj = i & (CHUNK - 1), j & (CHUNK - 1)
    rel = jnp.where(up, pj - pi, pi - pj)
    last = same & (pj == jnp.where(up, 0, CHUNK - 1))
    return same & (rel >= 0), same & (rel > 0), last, rel


def _split_states(s_new, nd, nb):
    return [s_new[d * nb * N_HEAD4:(d + 1) * nb * N_HEAD4].reshape((nb, N_HEAD4) + s_new.shape[1:]) for d in range(nd)]


def _split_outs(o, nd, nb):
    return [(tuple(o[d * nb:(d + 1) * nb, h * CHUNK:(h + 1) * CHUNK] for h in range(N_HEAD4)),) for d in range(nd)]


def dn_chunks(states, toks, params, aux=None):
    nd, nb = len(toks), states[0].shape[0]
    p = nd * nb
    qs, ks, vs, la, beta = (_stack_dirs(toks, i) for i in range(5))
    incl, strict, last, _ = _problem_masks(p, nb)
    sq = (p, ROWS, ROWS)
    g = jnp.sum(jnp.where(incl, jnp.swapaxes(jnp.broadcast_to(la, sq), 1, 2), 0.0), axis=2, keepdims=True)
    gb = jnp.broadcast_to(g, sq)
    gbt = jnp.swapaxes(gb, 1, 2)
    dec_incl = jnp.where(incl, jnp.exp(jnp.where(incl, gb - gbt, 0.0)), 0.0)
    dec_strict = jnp.where(strict, dec_incl, 0.0)
    low = beta * _bmm(ks, ks, 2, 2) * dec_strict
    eg = jnp.exp(g)
    inv = _inv_unit_tri_b(low) if aux is None else aux[0]
    sol = _tri_solve_b(low, jnp.concatenate([beta * vs, (beta * eg) * ks], axis=2), inv)
    w_v, w_k = sol[:, :, :HEAD128], sol[:, :, HEAD128:]
    a_qk = _bmm(qs, ks, 2, 2) * dec_incl
    g_last = jnp.sum(jnp.where(last, gbt, 0.0), axis=2, keepdims=True)
    k_dec = ks * jnp.exp(g_last - g)
    s = jnp.concatenate(states, axis=0).reshape(p * N_HEAD4, HEAD128, HEAD128)
    h3 = lambda x: x.reshape(p * N_HEAD4, CHUNK, x.shape[-1])
    u = w_v - _bmm(h3(w_k), s, 2, 1).reshape(p, ROWS, HEAD128)
    o = _bmm(h3(qs * eg), s, 2, 1).reshape(p, ROWS, HEAD128) + _bmm(a_qk, u, 2, 1)
    decay = jnp.exp(jnp.mean(h3(g_last), axis=1, keepdims=True))
    s_new = s * decay + _bmm(h3(k_dec), h3(u), 1, 1)
    return _split_outs(o, nd, nb), _split_states(s_new, nd, nb), [inv]


def ret_chunks(states, toks, params, aux=None):
    nd, nb = len(toks), states[0].shape[0]
    p = nd * nb
    (decay_log,) = params
    qs, ks, vs = (_stack_dirs(toks, i) for i in range(3))
    incl, _, _, rel = _problem_masks(p, nb)
    col = (p, ROWS, 1)
    up = lax.broadcasted_iota(jnp.int32, col, 0) >= nb
    row = lax.broadcasted_iota(jnp.int32, col, 1)
    head = row >> 6
    lg = jnp.zeros(col, F32)
    for h in range(N_HEAD4):
        rate = jnp.where(up, -jnp.exp(decay_log[N_HEAD4 + h]), -jnp.exp(decay_log[h]))
        lg = jnp.where(head == h, rate, lg)
    place = row & (CHUNK - 1)
    pos = jnp.where(up, CHUNK - 1 - place, place).astype(F32)
    dmask = jnp.where(incl, jnp.exp(jnp.where(incl, rel.astype(F32) * lg, 0.0)), 0.0)
    o = _bmm(_bmm(qs, ks, 2, 2) * dmask, vs, 2, 1)
    s = jnp.concatenate(states, axis=0).reshape(p * N_HEAD4, HEAD64, HEAD128)
    h3 = lambda x: x.reshape(p * N_HEAD4, CHUNK, x.shape[-1])
    kv = _bmm(h3(ks * jnp.exp((CHUNK - 1.0 - pos) * lg)), h3(vs), 1, 1)
    o = o + _bmm(h3(qs * jnp.exp((pos + 1.0) * lg)), s, 2, 1).reshape(p, ROWS, HEAD128)
    decay = jnp.exp(CHUNK * jnp.mean(h3(lg), axis=1, keepdims=True))
    s_new = s * decay + kv
    return _split_outs(o, nd, nb), _split_states(s_new, nd, nb), []


def _scan_nb(bsz):
    return next(n for n in (2, 1) if bsz % n == 0)


def _carried(refs, n_in, n_out, comm):
    if comm is None:
        return refs[:n_in], refs[n_in:n_in + n_out], refs[n_in + n_out:], None
    n = len(comm[1])
    ins, srcs = refs[:n_in], refs[n_in:n_in + n]
    outs, dsts = refs[n_in + n:n_in + n + n_out], refs[n_in + n + n_out:n_in + 2 * n + n_out]
    scratch, sems = refs[n_in + 2 * n + n_out:-3], refs[-3:]
    return ins, outs, scratch, (srcs, dsts) + tuple(sems)


def _carry_start(xrefs, comm, grid):
    if comm is None:
        return
    first = functools.reduce(jnp.logical_and, [pl.program_id(i) == 0 for i in range(len(grid))])

    @pl.when(first)
    def _():
        _exchange_start(xrefs, comm[0])


def _carry_wait(xrefs, comm, grid):
    if comm is None:
        return
    last = functools.reduce(jnp.logical_and, [pl.program_id(i) == n - 1 for i, n in enumerate(grid)])

    @pl.when(last)
    def _():
        _exchange_wait(xrefs, comm[0])


def scan2_fwd(name, chunks_fn, dirs, params, state_shape, outs, lc, comm=None, aux_shapes=()):
    bsz, t = dirs[0][0].x.shape[:2]
    nb = _scan_nb(bsz)
    nc, nctx = t // CHUNK, lc // CHUNK
    nd, n_t, n_p, n_o, n_a = len(dirs), len(dirs[0]), len(params), len(outs), len(aux_shapes)
    grid = (bsz // nb, nc)

    def body(*refs):
        ins, out_refs, s_refs, xrefs = _carried(refs, nd * n_t + n_p, nd * (n_o + 1) + n_a, comm)
        tok_refs, par_refs = ins[:nd * n_t], ins[nd * n_t:]
        _carry_start(xrefs, comm, grid)

        @pl.when(pl.program_id(1) == 0)
        def _():
            for s_ref in s_refs:
                s_ref[...] = jnp.zeros(s_ref.shape, F32)

        pv = [_load(r, p) for r, (_, p) in zip(par_refs, params)]
        states = [s_ref[...] for s_ref in s_refs]
        tv = [[_load(r, tk.pieces) for r, tk in zip(tok_refs[d * n_t:(d + 1) * n_t], dirs[d])] for d in range(nd)]
        res, s_new, aux = chunks_fn(states, tv, pv)
        for d in range(nd):
            o_refs = out_refs[d * (n_o + 1):(d + 1) * (n_o + 1)]
            o_refs[-1][...] = states[d]
            s_refs[d][...] = s_new[d]
            for r, o, (_, _, pieces) in zip(o_refs, res[d], outs):
                _store(r, o, pieces)
        for r, a in zip(out_refs[nd * (n_o + 1):], aux):
            r[...] = a
        _carry_wait(xrefs, comm, grid)

    nstate = len(state_shape)
    tok_specs, out_specs, out_shapes, operands = [], [], [], []
    for d, toks in enumerate(dirs):
        chunk = lambda c, d=d: _chunk_of(d, c, nc, nctx)
        for tk in toks:
            tok_specs.append(pl.BlockSpec((nb, CHUNK, tk.width), lambda b, c, f=chunk, col=tk.col: (b, f(c), col)))
            operands.append(tk.x)
        for w, dt, _ in outs:
            out_specs.append(pl.BlockSpec((nb, CHUNK, w), lambda b, c, f=chunk: (b, f(c), 0)))
            out_shapes.append(jax.ShapeDtypeStruct((bsz, t, w), dt))
        out_specs.append(pl.BlockSpec((nb, None) + state_shape, lambda b, c, f=chunk: (b, f(c)) + (0,) * nstate))
        out_shapes.append(jax.ShapeDtypeStruct((bsz, nc) + state_shape, F32))
    for shape in aux_shapes:
        out_specs.append(pl.BlockSpec((None, None) + tuple(shape), lambda b, c, n=len(shape): (b, c) + (0,) * n))
        out_shapes.append(jax.ShapeDtypeStruct(grid + tuple(shape), F32))
    par_specs = [pl.BlockSpec(x.shape, lambda b, c: (0, 0)) for x, _ in params]
    operands += [x for x, _ in params]
    return _carrier_call(body, name, grid, tok_specs + par_specs, operands, out_specs, out_shapes,
                         [pltpu.VMEM((nb,) + state_shape, F32)] * nd, comm)


def _carrier_call(body, name, grid, in_specs, operands, out_specs, out_shapes, scratch, comm):
    if comm is not None:
        kind, arrays = comm
        in_specs, operands = in_specs + [HBM_SPEC] * len(arrays), operands + list(arrays)
        out_specs, out_shapes = out_specs + [HBM_SPEC] * len(arrays), out_shapes + [_exchange_shape(kind, a) for a in arrays]
        scratch = scratch + _exchange_sems(len(arrays))
    return pl.pallas_call(body, name=name, grid=grid, in_specs=in_specs, out_specs=out_specs, out_shape=out_shapes,
                          scratch_shapes=scratch, compiler_params=_params(len(grid)))(*operands)


def scan2_bwd(name, chunks_fn, dirs, params, state_shape, outs, lc, sprevs, cts, comm=None, aux=()):
    bsz, t = dirs[0][0].x.shape[:2]
    nb = _scan_nb(bsz)
    nc, nctx = t // CHUNK, lc // CHUNK
    nd, n_t, n_p, n_o, n_a = len(dirs), len(dirs[0]), len(params), len(outs), len(aux)
    per_in = n_t + 1 + n_o
    grid = (bsz // nb, nc)

    def body(*refs):
        refs, g_refs, ds_refs, xrefs = _carried(refs, nd * per_in + n_p + n_a, nd * n_t + n_p, comm)
        par_refs = refs[nd * per_in:nd * per_in + n_p]
        saved = [r[...] for r in refs[nd * per_in + n_p:]]
        fn = chunks_fn if not saved else (lambda s, tk, pr: chunks_fn(s, tk, pr, saved))
        _carry_start(xrefs, comm, grid)
        first = pl.program_id(1) == 0

        @pl.when(first)
        def _():
            for ds_ref in ds_refs:
                ds_ref[...] = jnp.zeros(ds_ref.shape, F32)

        pv = [_load(r, p) for r, (_, p) in zip(par_refs, params)]
        ins = [refs[d * per_in:(d + 1) * per_in] for d in range(nd)]
        tv = [[_load(r, tk.pieces) for r, tk in zip(ins[d][:n_t], dirs[d])] for d in range(nd)]
        states = [ins[d][n_t][...] for d in range(nd)]
        _, vjp = jax.vjp(lambda s, tk, pr: fn(s, tk, pr)[:2], states, tv, pv)
        ct = [tuple(_load(r, pieces) for r, (_, _, pieces) in zip(ins[d][n_t + 1:], outs)) for d in range(nd)]
        d_s, d_tv, d_pv = vjp((ct, [ds_ref[...] for ds_ref in ds_refs]))
        for d in range(nd):
            ds_refs[d][...] = d_s[d]
            for r, gv, tk in zip(g_refs[d * n_t:(d + 1) * n_t], d_tv[d], dirs[d]):
                _store(r, gv, tk.pieces)
        very_first = first & (pl.program_id(0) == 0)
        for r, gv, (_, p) in zip(g_refs[nd * n_t:], d_pv, params):
            @pl.when(very_first)
            def _():
                _store(r, gv, p)

            @pl.when(jnp.logical_not(very_first))
            def _():
                _store(r, gv, p, accumulate=True)
        _carry_wait(xrefs, comm, grid)

    nstate = len(state_shape)
    in_specs, operands, g_specs, g_shapes = [], [], [], []
    for d, toks in enumerate(dirs):
        chunk = lambda c, d=d: _chunk_of(d, nc - 1 - c, nc, nctx)
        for tk in toks:
            in_specs.append(pl.BlockSpec((nb, CHUNK, tk.width), lambda b, c, f=chunk, col=tk.col: (b, f(c), col)))
            operands.append(tk.x)
            g_specs.append(pl.BlockSpec((nb, CHUNK, tk.width), lambda b, c, f=chunk: (b, f(c), 0)))
            g_shapes.append(jax.ShapeDtypeStruct((bsz, t, tk.width), F32))
        in_specs.append(pl.BlockSpec((nb, None) + state_shape, lambda b, c, f=chunk: (b, f(c)) + (0,) * nstate))
        operands.append(sprevs[d])
        for (w, _, _), ct in zip(outs, cts[d]):
            in_specs.append(pl.BlockSpec((nb, CHUNK, w), lambda b, c, f=chunk: (b, f(c), 0)))
            operands.append(ct)
    par_specs = [pl.BlockSpec(x.shape, lambda b, c: (0, 0)) for x, _ in params]
    aux_specs = [pl.BlockSpec((None, None) + a.shape[2:], lambda b, c, n=a.ndim - 2: (b, nc - 1 - c) + (0,) * n) for a in aux]
    operands += [x for x, _ in params] + list(aux)
    return _carrier_call(body, name, grid, in_specs + par_specs + aux_specs, operands, g_specs + par_specs,
                         g_shapes + [jax.ShapeDtypeStruct(x.shape, F32) for x, _ in params],
                         [pltpu.VMEM((nb,) + state_shape, F32)] * nd, comm)


class Geo:
    def __init__(self, bsz, t, lc):
        self.b, self.t, self.lc, self.m = bsz, t, lc, bsz * t
        self.tm = _pick(lc, (256, 128, 64))
        self.tq = _pick(lc, (128, 64))
        assert t % self.tm == 0 and t % CHUNK == 0 and lc % CHUNK == 0
        self.nctx, self.nctx_q = lc // self.tm, lc // self.tq
        self.grid = (bsz, t // self.tm)


def a_tok(g, x, tm=None, **kw):
    tm = tm or g.tm
    return Arg(x, (None, tm, x.shape[-1]), lambda b, j: (b, j, 0), **kw)


def a_ztok(g, z, name, width, tm=None, **kw):
    tm = tm or g.tm
    col = ZOFF[name] // width
    return Arg(z, (None, tm, width), lambda b, j: (b, j, col),
               gshape=(g.b, g.t, width), gimap=lambda b, j: (b, j, 0), **kw)


def a_par(x, **kw):
    return Arg(x, x.shape, lambda b, j: (0, 0), first=lambda ids: (ids[0] == 0) & (ids[1] == 0), **kw)


def a_mod(g, x):
    n = g.nctx
    return Arg(x, (None, None, x.shape[2], D_MODEL), lambda b, j: (b, jnp.where(j >= n, 1, 0), 0, 0),
               first=lambda ids: (ids[1] == 0) | (ids[1] == n))


def a_tab(g, x):
    return Arg(x, (g.tm, x.shape[-1]), lambda b, j: (j, 0), diff=False)


def o_tok(g, w, dtype, tm=None, pieces=None):
    tm = tm or g.tm
    return Out((g.b, g.t, w), dtype, (None, tm, w), lambda b, j: (b, j, 0), pieces)


def addn(name, xs, dtype):
    bsz, t, w = xs[0].shape
    tm = _pick(t, (256, 192, 128, 64))

    def body(*refs):
        tot = refs[0][...].astype(F32)
        for r in refs[1:-1]:
            tot = tot + r[...].astype(F32)
        refs[-1][...] = tot.astype(dtype)

    spec = pl.BlockSpec((None, tm, w), lambda b, j: (b, j, 0))
    return pl.pallas_call(body, name=name, grid=(bsz, t // tm), in_specs=[spec] * len(xs), out_specs=spec,
                          out_shape=jax.ShapeDtypeStruct((bsz, t, w), dtype), compiler_params=_params(2))(*xs)


P64x8 = _lane_pieces(HEAD64, ATT_HEADS)
P64x4 = _lane_pieces(HEAD64, N_HEAD4)
P64x2 = _lane_pieces(HEAD64, ATT_KV_HEADS)
P128x4 = _lane_pieces(HEAD128, N_HEAD4)
P1024x4 = _lane_pieces(D_MODEL, N_BRANCH)


def _gate_pieces(direction):
    la = [(direction * N_HEAD4 + h, 1) for h in range(N_HEAD4)]
    beta = [(8 + direction * N_HEAD4 + h, 1) for h in range(N_HEAD4)]
    return la, beta


RET_PIECES = [(i, 1) for i in range(2 * N_HEAD4)]


def _prep_io(g, z, sp, tabs):
    args = [a_ztok(g, z, "att_q", 512, gdtype=BF16), a_ztok(g, z, "att_k", 128, gdtype=BF16),
            a_ztok(g, z, "ret_q", 256, gdtype=BF16), a_ztok(g, z, "ret_k", 256, gdtype=BF16),
            a_ztok(g, z, "narrow", 128, gdtype=BF16),
            a_par(sp["qgain"]), a_par(sp["kgain"]), a_par(sp["alog"]), a_par(sp["dtb"]),
            a_tab(g, tabs[0]), a_tab(g, tabs[1])]
    outs = [o_tok(g, 512, BF16), o_tok(g, 128, BF16), o_tok(g, 256, F32), o_tok(g, 256, F32), o_tok(g, 128, F32)]
    return args, outs


def _attn_io(g, q_att, k_att, z, latent):
    col = ZOFF["att_v"] // 128
    first = lambda ids: ids[1] == 0
    if latent:
        rows, off, nq = g.t, g.nctx_q, (g.t - g.lc) // g.tq
    else:
        rows, off, nq = g.lc, 0, g.nctx_q
    args = [Arg(q_att, (None, g.tq, 512), lambda b, j: (b, j + off, 0), pieces=P64x8,
                gshape=(g.b, nq * g.tq, 512), gimap=lambda b, j: (b, j, 0)),
            Arg(k_att, (None, rows, 128), lambda b, j: (b, 0, 0), first=first, pieces=P64x2, gshape=(g.b, rows, 128)),
            Arg(z, (None, rows, 128), lambda b, j: (b, 0, col), first=first, pieces=P64x2,
                gshape=(g.b, rows, 128), gimap=lambda b, j: (b, 0, 0))]
    outs = [Out((g.b, nq * g.tq, 512), BF16, (None, g.tq, 512), lambda b, j: (b, j, 0), P64x8)]
    return (g.b, nq), args, outs


def _pad_rows(x, t):
    return jnp.pad(x, ((0, 0), (0, t - x.shape[1]), (0, 0)))


def _colgrid_arg(g, z, name, width_total, **kw):
    col = ZOFF[name] // 128
    return Arg(z, (None, g.t, 128), lambda h, b: (b, 0, col + h),
               gshape=(g.b, g.t, width_total), gimap=lambda h, b: (b, 0, h), **kw)


def _dnprep_io(g, z, sp):
    wfirst = lambda ids: ids[1] == 0
    args = [_colgrid_arg(g, z, "dn_q", 512, gdtype=BF16), _colgrid_arg(g, z, "dn_k", 512, gdtype=BF16),
            _colgrid_arg(g, z, "dn_v", 512, gdtype=BF16)]
    for i in range(3):
        args.append(Arg(sp["dn_conv"][i], (3, 128), lambda h, b: (0, h), first=wfirst))
    outs = [Out((g.b, g.t, 512), F32, (None, g.t, 128), lambda h, b: (b, 0, h)) for _ in range(3)]
    return (N_HEAD4, g.b), args, outs


def _shortconv_io(g, z, sp):
    args = [_colgrid_arg(g, z, "sc_b", 512, gdtype=BF16), _colgrid_arg(g, z, "sc_c", 512, gdtype=BF16),
            _colgrid_arg(g, z, "sc_x", 512, gdtype=BF16),
            Arg(sp["sc_conv"], (3, 128), lambda h, b: (0, h), first=lambda ids: ids[1] == 0)]
    outs = [Out((g.b, g.t, 512), BF16, (None, g.t, 128), lambda h, b: (b, 0, h))]
    return (BRANCH_W // 128, g.b), args, outs


def _finish_io(g, o_dn_f, o_dn_b, z, o_rt_f, o_rt_b, sp):
    args = [a_tok(g, o_dn_f, pieces=P128x4), a_tok(g, o_dn_b, pieces=P128x4),
            a_ztok(g, z, "dn_z", 512, gdtype=BF16, pieces=P128x4),
            a_tok(g, o_rt_f, pieces=P128x4), a_tok(g, o_rt_b, pieces=P128x4),
            a_ztok(g, z, "ret_g", 512, gdtype=BF16, pieces=P128x4), a_par(sp["ngain"])]
    outs = [o_tok(g, 512, BF16, pieces=P128x4), o_tok(g, 512, BF16, pieces=P128x4)]
    return args, outs


def _merge_io(g, ps, z):
    args = [a_tok(g, p, gdtype=BF16) for p in ps] + [a_ztok(g, z, "gates", 4096, gdtype=BF16, pieces=P1024x4)]
    return args, [o_tok(g, D_MODEL, BF16)]


def _dn_dirs(qn, kn, vn, gates):
    dirs = []
    for d in range(2):
        la, beta = _gate_pieces(d)
        dirs.append([Tok(qn, P128x4), Tok(kn, P128x4), Tok(vn, P128x4), Tok(gates, la), Tok(gates, beta)])
    return dirs


def _ret_dirs(q_ret, k_ret, z):
    col = ZOFF["ret_v"] // 512
    return [[Tok(q_ret, P64x4), Tok(k_ret, P64x4), Tok(z, P128x4, 512, col)] for _ in range(2)]


DN_STATE = (N_HEAD4, HEAD128, HEAD128)
RET_STATE = (N_HEAD4, HEAD64, HEAD128)
SCAN_OUT = [(512, F32, P128x4)]


def layer_fwd(g, x, h, w, sp, mod_a, mod_b, tabs, g_next, comm=None):
    tp = {"x": x, "h": h}
    m, lc = g.m, g.lc
    z = matmul("win", h.reshape(m, D_MODEL), w["win"]).reshape(g.b, g.t, NZ)
    tp["z"] = z
    args, outs = _prep_io(g, z, sp, tabs)
    q_att, k_att, q_ret, k_ret, gates = vfwd("prep", f_prep, g.grid, args, outs)
    tp.update(q_att=q_att, k_att=k_att, q_ret=q_ret, k_ret=k_ret, gates=gates)
    y_att = []
    for latent in (False, True):
        agrid, args, outs = _attn_io(g, q_att, k_att, z, latent)
        y_att += vfwd("attn", make_f_attn(g.tq), agrid, args, outs)
    y_att = jnp.concatenate(y_att, axis=1)
    dgrid, args, outs = _dnprep_io(g, z, sp)
    qn, kn, vn = vfwd("dnprep", make_f_dnprep(g.t, lc), dgrid, args, outs)
    tp.update(qn=qn, kn=kn, vn=vn)
    inv_shape = (2 * _scan_nb(g.b), ROWS, ROWS)
    o_f, s_f, o_b, s_b, tp["dn_inv"], *carried = scan2_fwd("dnscan", dn_chunks, _dn_dirs(qn, kn, vn, gates), [], DN_STATE,
                                                            SCAN_OUT, lc, comm, [inv_shape])
    tp["carried"] = carried if carried else None
    o_dn, tp["dn_s"] = [o_f, o_b], [s_f, s_b]
    o_f, s_f, o_b, s_b = scan2_fwd("retscan", ret_chunks, _ret_dirs(q_ret, k_ret, z), [(sp["ret"], RET_PIECES)], RET_STATE, SCAN_OUT, lc)
    o_rt, tp["rt_s"] = [o_f, o_b], [s_f, s_b]
    tp.update(o_dn=o_dn, o_rt=o_rt)
    args, outs = _finish_io(g, o_dn[0], o_dn[1], z, o_rt[0], o_rt[1], sp)
    y_dn, y_rt = vfwd("finish", f_finish, g.grid, args, outs)
    sgrid, args, outs = _shortconv_io(g, z, sp)
    (y_sc,) = vfwd("shortconv", make_f_shortconv(g.t, lc), sgrid, args, outs)
    ys = [y_att, y_dn, y_rt, y_sc]
    tp["ys"] = ys
    ps = [matmul("wbranch", y.reshape(m, BRANCH_W), w["wbr"][i], epilogue=lambda p: (p,), out_dtypes=(BF16,))[0]
          .reshape(g.b, g.t, D_MODEL) for i, y in enumerate(ys)]
    tp["ps"] = ps
    args, outs = _merge_io(g, ps, z)
    (u,) = vfwd("merge", f_merge, g.grid, args, outs)
    tp["u"] = u
    y = matmul("wout", u.reshape(m, D_MODEL), w["wout"]).reshape(g.b, g.t, D_MODEL)
    tp["y"] = y
    args = [a_tok(g, x), a_tok(g, y), a_par(sp["g1"]), a_par(sp["g2"]), a_mod(g, mod_a)]
    x1, h2 = vfwd("resnorm", f_resnorm, g.grid, args, [o_tok(g, D_MODEL, F32), o_tok(g, D_MODEL, BF16)])
    tp.update(x1=x1, h2=h2)
    tp["a"], tp["r"] = matmul("wmlp1", h2.reshape(m, D_MODEL), w["w1"], epilogue=lambda p: (p, jnp.square(jnp.maximum(p, 0.0))),
                              out_dtypes=(F32, BF16))
    mo = matmul("wmlp2", tp["r"], w["w2"]).reshape(g.b, g.t, D_MODEL)
    tp["mo"] = mo
    if g_next is None:
        args = [a_tok(g, x1), a_tok(g, mo), a_par(sp["g3"]), a_mod(g, mod_b)]
        (x2,) = vfwd("resid", f_resid, g.grid, args, [o_tok(g, D_MODEL, F32)])
        return x2, None, tp
    args = [a_tok(g, x1), a_tok(g, mo), a_par(sp["g3"]), a_par(g_next), a_mod(g, mod_b)]
    x2, h_next = vfwd("resnorm", f_resnorm, g.grid, args, [o_tok(g, D_MODEL, F32), o_tok(g, D_MODEL, BF16)])
    return x2, h_next, tp


def layer_bwd(g, tp, w, sp, mod_a, mod_b, tabs, g_next, dx2, dh_next, comm=None):
    m, lc = g.m, g.lc
    gw, gs = {}, {}
    x, z = tp["x"], tp["z"]
    if g_next is None:
        args = [a_tok(g, tp["x1"]), a_tok(g, tp["mo"]), a_par(sp["g3"]), a_mod(g, mod_b)]
        dx1, dmo, gs["g3"], dmod_b = vbwd("resid_b", f_resid, g.grid, args, [o_tok(g, D_MODEL, F32)], [[dx2]])
    else:
        args = [a_tok(g, tp["x1"]), a_tok(g, tp["mo"]), a_par(sp["g3"]), a_par(g_next), a_mod(g, mod_b)]
        dx1, dmo, gs["g3"], gs["g0_next"], dmod_b = vbwd(
            "resnorm_b", f_resnorm, g.grid, args, [o_tok(g, D_MODEL, F32), o_tok(g, D_MODEL, BF16)], [[dx2], [dh_next]])
    dmo2 = dmo.reshape(m, D_MODEL)
    (da2,) = matmul("wmlp2_dx", dmo2, w["w2"], trans_b=True, extras=[tp["a"]],
                    epilogue=lambda p, a: (p * (2.0 * jnp.maximum(a, 0.0)),), out_dtypes=(BF16,))
    gw["w2"] = matmul("wmlp2_dw", tp["r"], dmo2, trans_a=True)
    dh2 = matmul("wmlp1_dx", da2, w["w1"], trans_b=True).reshape(g.b, g.t, D_MODEL)
    gw["w1"] = matmul("wmlp1_dw", tp["h2"].reshape(m, D_MODEL), da2, trans_a=True)
    args = [a_tok(g, x), a_tok(g, tp["y"]), a_par(sp["g1"]), a_par(sp["g2"]), a_mod(g, mod_a)]
    dx, dy, gs["g1"], gs["g2"], dmod_a = vbwd(
        "resnorm_b", f_resnorm, g.grid, args, [o_tok(g, D_MODEL, F32), o_tok(g, D_MODEL, BF16)], [[dx1], [dh2]])
    dy2 = dy.reshape(m, D_MODEL)
    du = matmul("wout_dx", dy2, w["wout"], trans_b=True).reshape(g.b, g.t, D_MODEL)
    gw["wout"] = matmul("wout_dw", tp["u"].reshape(m, D_MODEL), dy2, trans_a=True)
    args, outs = _merge_io(g, tp["ps"], z)
    *dps, dz_gates = vbwd("merge_b", f_merge, g.grid, args, outs, [[du]])
    dys, gwbr = [], []
    for i in range(N_BRANCH):
        dp2 = dps[i].reshape(m, D_MODEL)
        dys.append(matmul("wbranch_dx", dp2, w["wbr"][i], trans_b=True).reshape(g.b, g.t, BRANCH_W))
        gwbr.append(matmul("wbranch_dw", tp["ys"][i].reshape(m, BRANCH_W), dp2, trans_a=True))
    gw["wbr"] = jnp.stack(gwbr)
    dy_att, dy_dn, dy_rt, dy_sc = dys
    o_dn, o_rt = tp["o_dn"], tp["o_rt"]
    args, outs = _finish_io(g, o_dn[0], o_dn[1], z, o_rt[0], o_rt[1], sp)
    do_dn_f, do_dn_b, dz_dnz, do_rt_f, do_rt_b, dz_retg, gs["ngain"] = vbwd(
        "finish_b", f_finish, g.grid, args, outs, [[dy_dn], [dy_rt]])
    res = scan2_bwd("dnscan_b", dn_chunks, _dn_dirs(tp["qn"], tp["kn"], tp["vn"], tp["gates"]), [], DN_STATE, SCAN_OUT, lc,
                   tp["dn_s"], [[do_dn_f], [do_dn_b]], comm, [tp["dn_inv"]])
    dqn, dkn, dvn, dgates = [res[0], res[5]], [res[1], res[6]], [res[2], res[7]], [res[3], res[4], res[8], res[9]]
    received = list(res[10:]) if comm is not None else None
    swap = ("swap", received) if comm is not None else None
    res = scan2_bwd("retscan_b", ret_chunks, _ret_dirs(tp["q_ret"], tp["k_ret"], z), [(sp["ret"], RET_PIECES)], RET_STATE, SCAN_OUT, lc,
                   tp["rt_s"], [[do_rt_f], [do_rt_b]], swap)
    dq_ret, dk_ret, gs["ret"] = [res[0], res[3]], [res[1], res[4]], res[6]
    carried = (received, list(res[7:])) if comm is not None else None
    dz_retv = addn("sum_retv", [res[2], res[5]], BF16)
    sgrid, args, outs = _shortconv_io(g, z, sp)
    dz_scb, dz_scc, dz_scx, gs["sc_conv"] = vbwd("shortconv_b", make_f_shortconv(g.t, lc), sgrid, args, outs, [[dy_sc]])
    dq_att, dk_att, dv_att = [], [], []
    for latent, dy in ((False, dy_att[:, :lc]), (True, dy_att[:, lc:])):
        agrid, args, outs = _attn_io(g, tp["q_att"], tp["k_att"], z, latent)
        dq, dk, dv = vbwd("attn_b", make_f_attn(g.tq), agrid, args, outs, [[dy]])
        dq_att.append(dq); dk_att.append(_pad_rows(dk, g.t)); dv_att.append(_pad_rows(dv, g.t))
    dq_att = jnp.concatenate(dq_att, axis=1)
    dz_attv = addn("sum_attv", dv_att, BF16)
    dgrid, args, outs = _dnprep_io(g, z, sp)
    dz_dnq, dz_dnk, dz_dnv, gc_q, gc_k, gc_v = vbwd("dnprep_b", make_f_dnprep(g.t, lc), dgrid, args, outs, [dqn, dkn, dvn])
    gs["dn_conv"] = [gc_q, gc_k, gc_v]
    args, outs = _prep_io(g, z, sp, tabs)
    dz_attq, dz_attk, dz_retq, dz_retk, dz_nar, gs["qgain"], gs["kgain"], gs["alog"], gs["dtb"] = vbwd(
        "prep_b", f_prep, g.grid, args, outs, [[dq_att], dk_att, dq_ret, dk_ret, dgates])
    pad = jnp.zeros((g.b, g.t, NZ - (ZOFF["narrow"] + 128)), BF16)
    dz = jnp.concatenate([dz_gates, dz_attq, dz_dnq, dz_dnk, dz_dnv, dz_dnz, dz_retv, dz_retg, dz_scb, dz_scc, dz_scx,
                          dz_retq, dz_retk, dz_attk, dz_attv, dz_nar, pad], axis=-1)
    dz2 = dz.reshape(m, NZ)
    dh = matmul("win_dx", dz2, w["win"], trans_b=True).reshape(g.b, g.t, D_MODEL)
    gw["win"] = matmul("win_dw", tp["h"].reshape(m, D_MODEL), dz2, trans_a=True)
    return dx, dh, gw, gs, dmod_a, dmod_b, carried


def loss_call(g, xa, tgt):
    n = g.nctx
    inv_d = 1.0 / D_MODEL

    def body(x_ref, t_ref, loss_ref, dx_ref):
        j = pl.program_id(1)

        @pl.when((pl.program_id(0) == 0) & (j == 0))
        def _():
            loss_ref[...] = jnp.zeros(loss_ref.shape, F32)

        @pl.when(j < n)
        def _():
            dx_ref[...] = jnp.zeros(dx_ref.shape, F32)

        @pl.when(j >= n)
        def _():
            e = x_ref[...] - t_ref[...]
            dx_ref[...] = e * inv_d
            s = jnp.sum(jnp.sum(e * e, axis=1, keepdims=True), axis=0, keepdims=True)
            loss_ref[...] += jnp.broadcast_to(0.5 * inv_d * s, loss_ref.shape)

    tok = pl.BlockSpec((None, g.tm, D_MODEL), lambda b, j: (b, j, 0))
    return pl.pallas_call(
        body, name="loss", grid=g.grid,
        in_specs=[tok, pl.BlockSpec((None, g.tm, D_MODEL), lambda b, j: (b, jnp.maximum(j - n, 0), 0))],
        out_specs=[pl.BlockSpec((1, LANES), lambda b, j: (0, 0)), tok],
        out_shape=[jax.ShapeDtypeStruct((1, LANES), F32), jax.ShapeDtypeStruct((g.b, g.t, D_MODEL), F32)],
        compiler_params=_params(2),
    )(xa, tgt)


def rope_tables(g):
    seq = g.t - g.lc
    rows = seq // GRID_W
    r, col = jnp.meshgrid(jnp.arange(rows), jnp.arange(GRID_W), indexing="ij")
    quarter = HEAD64 // 4
    inv_freq = ROPE_THETA ** (-jnp.arange(quarter, dtype=F32) / quarter)
    ang = jnp.concatenate([r.reshape(-1, 1).astype(F32) * inv_freq, col.reshape(-1, 1).astype(F32) * inv_freq], axis=-1)
    cos, sin = jnp.cos(ang), jnp.sin(ang)
    cos = jnp.concatenate([jnp.ones((g.lc, HEAD64 // 2), F32), cos], axis=0)
    sin = jnp.concatenate([jnp.zeros((g.lc, HEAD64 // 2), F32), sin], axis=0)
    reps = 512 // (HEAD64 // 2)
    return jnp.tile(cos, (1, reps)), jnp.tile(sin, (1, reps))


def _row128(v):
    v = v.reshape(1, -1).astype(F32)
    return jnp.pad(v, ((0, 0), (0, LANES - v.shape[1])))


def layer_small(small, l):
    gn = small["g_norm"][l]
    return {
        "g0": gn[0:1], "g1": gn[1:2], "g2": gn[2:3], "g3": gn[3:4],
        "qgain": small["att_q_gain"][l][None], "kgain": small["att_k_gain"][l][None],
        "dn_conv": [small["dn_conv"][l][:, i * 512:(i + 1) * 512] for i in range(3)], "alog": _row128(small["dn_a_log"][l]), "dtb": _row128(small["dn_dt_bias"][l]),
        "ngain": small["dn_norm_gain"][l][None], "ret": _row128(small["ret_decay"][l]), "sc_conv": small["sc_conv"][l],
    }


def permute_win(w):
    parts = [w[..., off:off + width] for _, off, width in _SEGS]
    parts.append(jnp.zeros(w.shape[:-1] + (NZ - N_IN,), w.dtype))
    return jnp.concatenate(parts, axis=-1)


def unpermute_win(gw):
    order = sorted(_SEGS, key=lambda s: s[1])
    return jnp.concatenate([gw[..., ZOFF[name]:ZOFF[name] + width] for name, _, width in order], axis=-1)


def layer_weights(w_in, w_branch, w_out, w_mlp_in, w_mlp_out):
    wbr, wout, w1, w2 = (t.astype(BF16) for t in (w_branch, w_out, w_mlp_in, w_mlp_out))
    return {"win": permute_win(w_in.astype(BF16)), "wbr": wbr, "wout": wout, "w1": w1, "w2": w2}


BIG = (("w_in", -1), ("w_branch", -1), ("w_out", -2), ("w_mlp_in", -1), ("w_mlp_out", -2))


def weights_of_slots(slots):
    return layer_weights(*[jnp.concatenate([s[p] for p in range(N_XY)], axis=axis) for s, (_, axis) in zip(slots, BIG)])


def grad_slots(gw):
    full = [unpermute_win(gw["win"]), gw["wbr"], gw["wout"], gw["w1"], gw["w2"]]
    return [jnp.stack(jnp.split(t, N_XY, axis=axis)).astype(BF16) for t, (_, axis) in zip(full, BIG)]


def model_step(g, xa, tgt, mod, shards, small, local=False):
    depth = len(shards)
    tabs = rope_tables(g)
    sps = [layer_small(small, l) for l in range(depth)]
    mods_a = [mod[l][:, :, 2:5] for l in range(depth)]
    mods_b = [jnp.concatenate([mod[l][:, :, 5:6], mod[l + 1][:, :, 0:2]], axis=2) if l + 1 < depth else mod[l][:, :, 5:6]
              for l in range(depth)]
    mod0 = mod[0][:, :, 0:2]
    args0 = [a_tok(g, xa), a_par(sps[0]["g0"]), a_mod(g, mod0)]
    (h,) = vfwd("modnorm", f_modnorm, g.grid, args0, [o_tok(g, D_MODEL, BF16)])
    x, tapes, wl = xa, [], []
    slots = shards[0] if local else exchange("gather_w", "gather", shards[0])
    for l in range(depth):
        g_next = sps[l + 1]["g0"] if l + 1 < depth else None
        wl.append(weights_of_slots(slots))
        comm = ("gather", shards[l + 1]) if l + 1 < depth and not local else None
        x, h, tp = layer_fwd(g, x, h, wl[l], sps[l], mods_a[l], mods_b[l], tabs, g_next, comm)
        slots = shards[l + 1] if local and l + 1 < depth else tp["carried"]
        tapes.append(tp)
    loss_row, dx = loss_call(g, x, tgt)
    dh, gss, dmods = None, [None] * depth, [None] * depth
    mine, theirs = [None] * depth, [None] * depth
    gslots = None
    for l in reversed(range(depth)):
        g_next = sps[l + 1]["g0"] if l + 1 < depth else None
        comm = ("scatter", gslots) if gslots is not None and not local else None
        dx, dh, gw, gss[l], dma, dmb, carried = layer_bwd(g, tapes[l], wl[l], sps[l], mods_a[l], mods_b[l], tabs, g_next,
                                                          dx, dh, comm)
        if l + 1 < depth:
            mine[l + 1], theirs[l + 1] = (gslots, None) if local else carried
        gslots = grad_slots(gw)
        dmods[l] = (dma, dmb)
    if local:
        mine[0] = gslots
    else:
        mine[0] = exchange("scatter_w", "scatter", gslots)
        theirs[0] = exchange("swap_w", "swap", mine[0])
    dxh, g0_first, dmod0 = vbwd("modnorm_b", f_modnorm, g.grid, args0, [o_tok(g, D_MODEL, BF16)], [[dh]])
    dxa = addn("sum_dx", [dx, dxh], F32)
    dmod = []
    for l in range(depth):
        first2 = dmod0 if l == 0 else dmods[l - 1][1][:, :, 1:3]
        dmod.append(jnp.concatenate([first2, dmods[l][0], dmods[l][1][:, :, 0:1]], axis=2))
    dmod = jnp.stack(dmod)
    def rows(key, n):
        return jnp.stack([gs[key][0, :n] for gs in gss])
    g_norm = jnp.stack([jnp.concatenate([g0_first if l == 0 else gss[l - 1]["g0_next"], gss[l]["g1"], gss[l]["g2"], gss[l]["g3"]], axis=0)
                        for l in range(depth)])
    gsmall = {
        "g_norm": g_norm,
        "att_q_gain": jnp.stack([gs["qgain"][0] for gs in gss]), "att_k_gain": jnp.stack([gs["kgain"][0] for gs in gss]),
        "dn_conv": jnp.stack([jnp.concatenate(gs["dn_conv"], axis=1) for gs in gss]),
        "dn_a_log": rows("alog", 8).reshape(depth, 2, N_HEAD4), "dn_dt_bias": rows("dtb", 8).reshape(depth, 2, N_HEAD4),
        "dn_norm_gain": jnp.stack([gs["ngain"][0] for gs in gss]),
        "ret_decay": rows("ret", 8).reshape(depth, 2, N_HEAD4),
        "sc_conv": jnp.stack([gs["sc_conv"] for gs in gss]),
    }
    return loss_row[0, 0], dxa, mine, theirs, gsmall, dmod


N_DEV = 8
N_XY = 4
HBM_SPEC = pl.BlockSpec(memory_space=pltpu.HBM)
VMEM_SPEC = pl.BlockSpec(memory_space=pltpu.VMEM)


def _coords():
    return lax.axis_index("x"), lax.axis_index("y"), lax.axis_index("c")


def _flip(coords, k):
    x, y, c = coords
    return (1 - x if k & 4 else x, 1 - y if k & 2 else y, 1 - c if k & 1 else c)


def allgather8(name, v):
    def body(v_ref, out_ref, send_sems, recv_sems, local_sem):
        me3 = _coords()
        me = 4 * me3[0] + 2 * me3[1] + me3[2]
        mine = pltpu.make_async_copy(v_ref, out_ref.at[me], local_sem)
        mine.start()
        sends = []
        for k in range(1, N_DEV):
            cp = pltpu.make_async_remote_copy(src_ref=v_ref, dst_ref=out_ref.at[me], send_sem=send_sems.at[k - 1],
                                              recv_sem=recv_sems.at[k - 1], device_id=_flip(me3, k), device_id_type=MESH)
            cp.start()
            sends.append(cp)
        for k in range(1, N_DEV):
            pltpu.make_async_remote_copy(src_ref=v_ref, dst_ref=out_ref.at[jnp.bitwise_xor(me, k)], send_sem=send_sems.at[k - 1],
                                         recv_sem=recv_sems.at[k - 1], device_id=_flip(me3, k), device_id_type=MESH).wait_recv()
        for cp in sends:
            cp.wait_send()
        mine.wait()

    return pl.pallas_call(
        body, name=name, out_shape=jax.ShapeDtypeStruct((N_DEV,) + v.shape, v.dtype),
        in_specs=[VMEM_SPEC], out_specs=VMEM_SPEC,
        scratch_shapes=[pltpu.SemaphoreType.DMA((N_DEV - 1,)), pltpu.SemaphoreType.DMA((N_DEV - 1,)), pltpu.SemaphoreType.DMA],
        compiler_params=pltpu.CompilerParams(vmem_limit_bytes=VMEM_LIMIT),
    )(v)


def exchange(name, kind, arrays):
    n = len(arrays)

    def body(*refs):
        xrefs = (refs[:n], refs[n:2 * n]) + tuple(refs[2 * n:])
        _exchange_start(xrefs, kind)
        _exchange_wait(xrefs, kind)

    return pl.pallas_call(
        body, name=name, out_shape=[_exchange_shape(kind, a) for a in arrays], in_specs=[HBM_SPEC] * n,
        out_specs=[HBM_SPEC] * n, scratch_shapes=_exchange_sems(n),
    )(*arrays)


def _exchange_shape(kind, a):
    return jax.ShapeDtypeStruct((N_XY,) + a.shape if kind == "gather" else a.shape, a.dtype)


def _exchange_sems(n):
    return [pltpu.SemaphoreType.DMA((n, N_XY - 1)), pltpu.SemaphoreType.DMA((n, N_XY - 1)), pltpu.SemaphoreType.DMA((n,))]


def _exchange_copies(xrefs, kind):
    srcs, dsts, send_sems, recv_sems, local_sems = xrefs
    me3 = _coords()
    me = 2 * me3[0] + me3[1]
    locals_, sends, recvs = [], [], []
    for i, (v_ref, out_ref) in enumerate(zip(srcs, dsts)):
        if kind == "swap":
            cp = pltpu.make_async_remote_copy(src_ref=v_ref, dst_ref=out_ref, send_sem=send_sems.at[i, 0],
                                              recv_sem=recv_sems.at[i, 0], device_id=_flip(me3, 1), device_id_type=MESH)
            sends.append(cp)
            recvs.append(cp)
            continue
        src = (lambda s, r=v_ref: r) if kind == "gather" else (lambda s, r=v_ref: r.at[s])

        def remote(k, src_slot, dst_slot):
            return pltpu.make_async_remote_copy(src_ref=src(src_slot), dst_ref=out_ref.at[dst_slot], send_sem=send_sems.at[i, k - 1],
                                                recv_sem=recv_sems.at[i, k - 1], device_id=_flip(me3, 2 * k), device_id_type=MESH)

        locals_.append(pltpu.make_async_copy(src(me), out_ref.at[me], local_sems.at[i]))
        sends += [remote(k, jnp.bitwise_xor(me, k), me) for k in range(1, N_XY)]
        recvs += [remote(k, me, jnp.bitwise_xor(me, k)) for k in range(1, N_XY)]
    return locals_, sends, recvs


def _exchange_start(xrefs, kind):
    locals_, sends, _ = _exchange_copies(xrefs, kind)
    for cp in locals_ + sends:
        cp.start()


def _exchange_wait(xrefs, kind):
    locals_, sends, recvs = _exchange_copies(xrefs, kind)
    for cp in recvs:
        cp.wait_recv()
    for cp in sends:
        cp.wait_send()
    for cp in locals_:
        cp.wait()


BLOCK_BYTES = 1 << 20


def _rows_block(rows, cols, limit=BLOCK_BYTES):
    for tr in (1024, 512, 256, 128, 64, 32, 16, 8):
        if rows % tr == 0 and tr * cols * 4 <= limit:
            return tr
    return rows


def sum_slots(name, r):
    s, rows, cols = r.shape
    tr = _rows_block(rows, cols)

    def body(r_ref, o_ref):
        tot = r_ref[0].astype(F32)
        for i in range(1, s):
            tot = tot + r_ref[i].astype(F32)
        o_ref[...] = tot

    return pl.pallas_call(
        body, name=name, grid=(rows // tr,), in_specs=[pl.BlockSpec((s, tr, cols), lambda i: (0, i, 0))],
        out_specs=pl.BlockSpec((tr, cols), lambda i: (i, 0)), out_shape=jax.ShapeDtypeStruct((rows, cols), F32),
        compiler_params=_params(1),
    )(r)


def adamw(name, w, m, v, parts):
    rows, cols = w.shape
    tr = _rows_block(rows, cols)
    n = len(parts)
    c1 = 1.0 - ADAM_B1 ** ADAM_STEP
    c2 = 1.0 - ADAM_B2 ** ADAM_STEP

    def body(*refs):
        w_ref, m_ref, v_ref = refs[:3]
        g_ref, d_ref, nm_ref, nv_ref = refs[3 + n:]
        g = refs[3][...]
        for r in refs[4:3 + n]:
            g = g + r[...]
        nm = ADAM_B1 * m_ref[...] + (1.0 - ADAM_B1) * g
        nv = ADAM_B2 * v_ref[...] + (1.0 - ADAM_B2) * jnp.square(g)
        d_ref[...] = -ADAM_LR * ((nm / c1) / (jnp.sqrt(nv / c2) + ADAM_EPS) + ADAM_WD * w_ref[...])
        g_ref[...], nm_ref[...], nv_ref[...] = g, nm, nv

    spec = pl.BlockSpec((tr, cols), lambda i: (i, 0))
    sds = jax.ShapeDtypeStruct((rows, cols), F32)
    return pl.pallas_call(
        body, name=name, grid=(rows // tr,), in_specs=[spec] * (3 + n), out_specs=[spec] * 4, out_shape=[sds] * 4,
        compiler_params=_params(1),
    )(w, m, v, *parts)


def adamw_slots(name, w, m, v, mine, theirs):
    depth = len(mine)
    rows, cols = mine[0].shape[1:]
    tr = _rows_block(rows, cols, BLOCK_BYTES // 2)
    nblk = rows // tr
    c1 = 1.0 - ADAM_B1 ** ADAM_STEP
    c2 = 1.0 - ADAM_B2 ** ADAM_STEP

    def body(*refs):
        w_ref, m_ref, v_ref = refs[:3]
        slot_refs = refs[3:3 + 2 * depth]
        g_ref, d_ref, nm_ref, nv_ref = refs[3 + 2 * depth:]
        for j in range(depth):
            @pl.when(pl.program_id(0) == j)
            def _():
                def total(ref):
                    tot = ref[0].astype(F32)
                    for s in range(1, N_XY):
                        tot = tot + ref[s].astype(F32)
                    return tot
                g = total(slot_refs[j]) + total(slot_refs[depth + j])
                nm = ADAM_B1 * m_ref[...] + (1.0 - ADAM_B1) * g
                nv = ADAM_B2 * v_ref[...] + (1.0 - ADAM_B2) * jnp.square(g)
                d_ref[...] = -ADAM_LR * ((nm / c1) / (jnp.sqrt(nv / c2) + ADAM_EPS) + ADAM_WD * w_ref[...])
                g_ref[...], nm_ref[...], nv_ref[...] = g, nm, nv

    spec = pl.BlockSpec((tr, cols), lambda l, i: (l * nblk + i, 0))
    slot_specs = [pl.BlockSpec((N_XY, tr, cols), lambda l, i, j=j: (0, jnp.where(l == j, i, 0), 0)) for j in range(depth)] * 2
    sds = jax.ShapeDtypeStruct((depth * rows, cols), F32)
    return pl.pallas_call(
        body, name=name, grid=(depth, nblk), in_specs=[spec] * 3 + slot_specs, out_specs=[spec] * 4, out_shape=[sds] * 4,
        compiler_params=_params(2),
    )(w, m, v, *mine, *theirs)


MOD_COLS = 512


def mod_fwd(call, w_mod, b_sh):
    depth, _, cols = w_mod.shape
    nr = call.shape[0]

    def body(c_ref, w_ref, b_ref, o_ref):
        o_ref[...] = _dg(_silu(c_ref[...]), w_ref[...], 1, 0) + b_ref[...]

    return pl.pallas_call(
        body, name="mod_fwd", grid=(depth, cols // MOD_COLS),
        in_specs=[pl.BlockSpec((nr, D_MODEL), lambda l, j: (0, 0)), pl.BlockSpec((None, D_MODEL, MOD_COLS), lambda l, j: (l, 0, j)),
                  pl.BlockSpec((None, 1, MOD_COLS), lambda l, j: (l, 0, j))],
        out_specs=pl.BlockSpec((None, nr, MOD_COLS), lambda l, j: (l, 0, j)),
        out_shape=jax.ShapeDtypeStruct((depth, nr, cols), F32), compiler_params=_params(2),
    )(call, w_mod, b_sh)


def mod_bwd(call, c_ctx, d_lat, d_ctx, w_mod, ctx_row):
    depth, _, cols = w_mod.shape
    nr, ns = call.shape[0], d_ctx.shape[1]

    def body(c_ref, cc_ref, dl_ref, dc_ref, w_ref, gw_ref, gc_ref):
        crow = jnp.sum(dc_ref[...], axis=0, keepdims=True)
        row = lax.broadcasted_iota(jnp.int32, (nr, 1), 0)
        dm = jnp.where(row == ctx_row, crow, dl_ref[...])
        gw_ref[...] = _dg(_silu(c_ref[...]), dm, 0, 0)
        ds = jnp.sum(_dg(jnp.broadcast_to(crow, (8, MOD_COLS)), w_ref[...], 1, 1), axis=0, keepdims=True) * 0.125
        _, vjp = jax.vjp(_silu, cc_ref[...])
        (part,) = vjp(ds)
        first = (pl.program_id(0) == 0) & (pl.program_id(1) == 0)

        @pl.when(first)
        def _():
            gc_ref[...] = part

        @pl.when(jnp.logical_not(first))
        def _():
            gc_ref[...] += part

    return pl.pallas_call(
        body, name="mod_bwd", grid=(depth, cols // MOD_COLS),
        in_specs=[pl.BlockSpec((nr, D_MODEL), lambda l, j: (0, 0)), pl.BlockSpec((1, D_MODEL), lambda l, j: (0, 0)),
                  pl.BlockSpec((None, nr, MOD_COLS), lambda l, j: (l, 0, j)), pl.BlockSpec((None, ns, MOD_COLS), lambda l, j: (l, 0, j)),
                  pl.BlockSpec((None, D_MODEL, MOD_COLS), lambda l, j: (l, 0, j))],
        out_specs=[pl.BlockSpec((None, D_MODEL, MOD_COLS), lambda l, j: (l, 0, j)), pl.BlockSpec((1, D_MODEL), lambda l, j: (0, 0))],
        out_shape=[jax.ShapeDtypeStruct((depth, D_MODEL, cols), F32), jax.ShapeDtypeStruct((1, D_MODEL), F32)],
        compiler_params=_params(2),
    )(call, c_ctx, d_lat, d_ctx, w_mod)


def bmod_grad(dm_all):
    ndev, depth, ns, cols = dm_all.shape

    def body(d_ref, o_ref):
        tot = d_ref[0]
        for i in range(1, ndev):
            tot = tot + d_ref[i]
        o_ref[...] = jnp.sum(tot, axis=0, keepdims=True)

    return pl.pallas_call(
        body, name="bmod_grad", grid=(depth,), in_specs=[pl.BlockSpec((ndev, None, ns, cols), lambda l: (0, l, 0, 0))],
        out_specs=pl.BlockSpec((None, 1, cols), lambda l: (l, 0, 0)), out_shape=jax.ShapeDtypeStruct((depth, 1, cols), F32),
        compiler_params=_params(1),
    )(dm_all)


def small_reduce(gathered, rows_all):
    ndev, rows, lanes = gathered.shape

    def body(g_ref, o_ref):
        tot = g_ref[0, 0:rows_all]
        for i in range(1, ndev):
            tot = tot + g_ref[i, 0:rows_all]
        o_ref[0:rows_all] = tot
        part = g_ref[0, rows_all:rows]
        for i in range(2, ndev, 2):
            part = part + g_ref[i, rows_all:rows]
        o_ref[rows_all:rows] = part

    return pl.pallas_call(body, name="small_reduce", out_shape=jax.ShapeDtypeStruct((rows, lanes), F32),
                          in_specs=[VMEM_SPEC], out_specs=VMEM_SPEC)(gathered)


def pack_rows(arrays, row_multiple=8):
    flat = jnp.concatenate([a.reshape(-1).astype(F32) for a in arrays])
    per = LANES * row_multiple
    padded = -(-flat.shape[0] // per) * per
    return jnp.pad(flat, (0, padded - flat.shape[0])).reshape(-1, LANES)


def unpack_rows(buf, shapes):
    flat, out, off = buf.reshape(-1), [], 0
    for s in shapes:
        n = int(np.prod(s))
        out.append(flat[off:off + n].reshape(s))
        off += n
    return out


SMALL_SHARDED = ("g_norm", "dn_conv", "sc_conv")
SMALL_ORDER = ("g_norm", "att_q_gain", "att_k_gain", "dn_conv", "dn_a_log", "dn_dt_bias", "dn_norm_gain", "ret_decay", "sc_conv")


def kernel(x, c, ctx, c_ctx, w_mod, b_mod, g_norm, w_in, att_q_gain, att_k_gain, dn_conv, dn_a_log, dn_dt_bias, dn_norm_gain, ret_decay, sc_conv, w_branch, w_out, w_mlp_in, w_mlp_out, loss_target, m_c_ctx, m_w_mod, m_b_mod, m_g_norm, m_w_in, m_att_q_gain, m_att_k_gain, m_dn_conv, m_dn_a_log, m_dn_dt_bias, m_dn_norm_gain, m_ret_decay, m_sc_conv, m_w_branch, m_w_out, m_w_mlp_in, m_w_mlp_out, v_c_ctx, v_w_mod, v_b_mod, v_g_norm, v_w_in, v_att_q_gain, v_att_k_gain, v_dn_conv, v_dn_a_log, v_dn_dt_bias, v_dn_norm_gain, v_ret_decay, v_sc_conv, v_w_branch, v_w_out, v_w_mlp_in, v_w_mlp_out):
    wts = dict(c_ctx=c_ctx, w_mod=w_mod, b_mod=b_mod, g_norm=g_norm, w_in=w_in, att_q_gain=att_q_gain, att_k_gain=att_k_gain,
               dn_conv=dn_conv, dn_a_log=dn_a_log, dn_dt_bias=dn_dt_bias, dn_norm_gain=dn_norm_gain, ret_decay=ret_decay,
               sc_conv=sc_conv, w_branch=w_branch, w_out=w_out, w_mlp_in=w_mlp_in, w_mlp_out=w_mlp_out)
    mom = dict(c_ctx=m_c_ctx, w_mod=m_w_mod, b_mod=m_b_mod, g_norm=m_g_norm, w_in=m_w_in, att_q_gain=m_att_q_gain,
               att_k_gain=m_att_k_gain, dn_conv=m_dn_conv, dn_a_log=m_dn_a_log, dn_dt_bias=m_dn_dt_bias,
               dn_norm_gain=m_dn_norm_gain, ret_decay=m_ret_decay, sc_conv=m_sc_conv, w_branch=m_w_branch, w_out=m_w_out,
               w_mlp_in=m_w_mlp_in, w_mlp_out=m_w_mlp_out)
    var = dict(c_ctx=v_c_ctx, w_mod=v_w_mod, b_mod=v_b_mod, g_norm=v_g_norm, w_in=v_w_in, att_q_gain=v_att_q_gain,
               att_k_gain=v_att_k_gain, dn_conv=v_dn_conv, dn_a_log=v_dn_a_log, dn_dt_bias=v_dn_dt_bias,
               dn_norm_gain=v_dn_norm_gain, ret_decay=v_ret_decay, sc_conv=v_sc_conv, w_branch=v_w_branch, w_out=v_w_out,
               w_mlp_in=v_w_mlp_in, w_mlp_out=v_w_mlp_out)
    names = list(wts)
    depth, bsz, seq, lc = w_mod.shape[0], x.shape[0], x.shape[1], ctx.shape[1]
    g = Geo(bsz, lc + seq, lc)
    xi, yi, ci = _coords()
    dev, xy = 4 * xi + 2 * yi + ci, 2 * xi + yi
    n_batch = N_DEV * bsz
    nr = -(-(n_batch + 1) // 16) * 16
    mod_cols = w_mod.shape[2]

    c_all = allgather8("gather_c", c).reshape(n_batch, D_MODEL)
    call = jnp.concatenate([c_all, c_ctx[None], jnp.zeros((nr - n_batch - 1, D_MODEL), F32)], axis=0)
    b_sh = lax.dynamic_slice_in_dim(b_mod, xy * mod_cols, mod_cols, axis=1)[:, None, :]
    mod_sh = mod_fwd(call, w_mod, b_sh)
    mod_g = allgather8("gather_mod", mod_sh.reshape(depth * nr, mod_cols)).reshape(N_XY, 2, depth, nr, mod_cols)[:, 0]
    mod_all = mod_g.transpose(1, 2, 0, 3).reshape(depth, nr, N_XY * mod_cols)
    mod_lat = lax.dynamic_slice_in_dim(mod_all, dev * bsz, bsz, axis=1).reshape(depth, bsz, 6, D_MODEL)
    mod_ctx = jnp.broadcast_to(mod_all[:, n_batch].reshape(depth, 1, 6, D_MODEL), (depth, bsz, 6, D_MODEL))
    mod = jnp.stack([mod_ctx, mod_lat], axis=2)

    sm_shapes = [wts[k].shape for k in SMALL_SHARDED]
    sm_g = allgather8("gather_small", pack_rows([wts[k] for k in SMALL_SHARDED])).reshape(N_XY, 2, -1)[:, 0]
    small = {k: wts[k] for k in SMALL_ORDER}
    for k, parts in zip(SMALL_SHARDED, zip(*[unpack_rows(sm_g[p], sm_shapes) for p in range(N_XY)])):
        small[k] = jnp.concatenate(parts, axis=-1)

    shards = [[wts[k][l].astype(BF16) for k, _ in BIG] for l in range(depth)]

    xa = jnp.concatenate([ctx, x], axis=1)
    loss_part, dxa, mine, theirs, gsmall, dmod = model_step(g, xa, loss_target, mod, shards, small)
    loss = lax.psum(loss_part, ("x", "y", "c"))
    grad_x = dxa[:, lc:]

    grads, deltas, new_m, new_v = {}, {}, {}, {}

    def update(k, parts, shape2d):
        res = adamw("adamw_" + k, wts[k].reshape(shape2d), mom[k].reshape(shape2d), var[k].reshape(shape2d), parts)
        grads[k], deltas[k], new_m[k], new_v[k] = (r.reshape(wts[k].shape) for r in res)

    for i, (k, _) in enumerate(BIG):
        cols = wts[k].shape[-1]
        slots = lambda per_layer: [s[i].reshape(N_XY, -1, cols) for s in per_layer]
        res = adamw_slots("adamw_" + k, wts[k].reshape(-1, cols), mom[k].reshape(-1, cols), var[k].reshape(-1, cols),
                          slots(mine), slots(theirs))
        grads[k], deltas[k], new_m[k], new_v[k] = (r.reshape(wts[k].shape) for r in res)

    dm_mine = jnp.concatenate([dmod[:, :, 1], dmod[:, :, 0]], axis=1).reshape(depth * 2 * bsz, 6 * D_MODEL)
    dm_all = allgather8("gather_dmod", dm_mine).reshape(N_DEV, depth, 2 * bsz, 6 * D_MODEL)
    gb = bmod_grad(dm_all).reshape(depth, 6 * D_MODEL)
    dm_cols = lax.dynamic_slice_in_dim(dm_all, xy * mod_cols, mod_cols, axis=3)
    d_lat = dm_cols[:, :, :bsz].transpose(1, 0, 2, 3).reshape(depth, n_batch, mod_cols)
    d_lat = jnp.pad(d_lat, ((0, 0), (0, nr - n_batch), (0, 0)))
    d_ctx = dm_cols[:, :, bsz:].transpose(1, 0, 2, 3).reshape(depth, n_batch, mod_cols)
    gw_mod, gc_part = mod_bwd(call, c_ctx[None], d_lat, d_ctx, w_mod, n_batch)
    update("w_mod", [gw_mod.reshape(depth * D_MODEL, mod_cols)], (depth * D_MODEL, mod_cols))
    update("b_mod", [gb], b_mod.shape)

    pack_all = pack_rows([gsmall[k] for k in SMALL_ORDER])
    pack_xy = pack_rows([gc_part])
    rows_all = pack_all.shape[0]
    tot = small_reduce(allgather8("gather_gsmall", jnp.concatenate([pack_all, pack_xy], axis=0)), rows_all)
    gtot = dict(zip(SMALL_ORDER, unpack_rows(tot[:rows_all], [gsmall[k].shape for k in SMALL_ORDER])))
    gtot["c_ctx"] = unpack_rows(tot[rows_all:], [c_ctx.shape])[0]
    for k in SMALL_SHARDED:
        width = wts[k].shape[-1]
        gtot[k] = lax.dynamic_slice_in_dim(gtot[k], xy * width, width, axis=gtot[k].ndim - 1)
    sm_names = ("c_ctx",) + SMALL_ORDER
    sm_shapes = [wts[k].shape for k in sm_names]
    res = adamw("adamw_small", pack_rows([wts[k] for k in sm_names]), pack_rows([mom[k] for k in sm_names]),
                pack_rows([var[k] for k in sm_names]), [pack_rows([gtot[k] for k in sm_names])])
    for dst, buf in zip((grads, deltas, new_m, new_v), res):
        dst.update(zip(sm_names, unpack_rows(buf, sm_shapes)))

    return (loss, grad_x, *[grads[k] for k in names], *[deltas[k] for k in names], *[new_m[k] for k in names],
            *[new_v[k] for k in names])
```

```python
import functools
import math

import numpy as np
import jax
import jax.numpy as jnp
from jax import lax
from jax.experimental import pallas as pl
from jax.experimental.pallas import tpu as pltpu

F32, BF16 = jnp.float32, jnp.bfloat16
HIGHEST = lax.Precision.HIGHEST
MESH = pl.DeviceIdType.MESH

D_MODEL = 1024
GRID_W = 64
N_BRANCH = 4
BRANCH_W = 512
HEAD64 = 64
HEAD128 = 128
N_HEAD4 = 4
ATT_HEADS = 8
ATT_KV_HEADS = 2
CHUNK = 64
MLP_HIDDEN = 4 * D_MODEL
ROPE_THETA = 10000.0
EPS = 1e-6
N_IN = 10000
ADAM_LR, ADAM_B1, ADAM_B2, ADAM_EPS, ADAM_WD, ADAM_STEP = 0.001, 0.9, 0.999, 1e-08, 0.01, 10

LANES = 128
VMEM_LIMIT = 56 * 1024 * 1024

_SEGS = (
    ("gates", 5904, 4096),
    ("att_q", 0, 512), ("dn_q", 768, 512), ("dn_k", 1280, 512), ("dn_v", 1792, 512), ("dn_z", 2304, 512),
    ("ret_v", 3344, 512), ("ret_g", 3856, 512), ("sc_b", 4368, 512), ("sc_c", 4880, 512), ("sc_x", 5392, 512),
    ("ret_q", 2832, 256), ("ret_k", 3088, 256),
    ("att_k", 512, 128), ("att_v", 640, 128),
    ("narrow", 2816, 16),
)
NZ = 10240


def _seg_offsets():
    off, out = 0, {}
    for name, _, width in _SEGS:
        out[name] = off
        off += width
    return out


ZOFF = _seg_offsets()


def _pick(n, cands):
    for c in cands:
        if n % c == 0:
            return c
    return n


def _dg(a, b, ca, cb, batch=False):
    dn = (((ca,), (cb,)), ((0,), (0,))) if batch else (((ca,), (cb,)), ((), ()))
    return lax.dot_general(a.astype(BF16), b.astype(BF16), dn, preferred_element_type=F32)


@functools.partial(jax.custom_vjp, nondiff_argnums=(2, 3))
def _mm(a, b, ca, cb):
    return _dg(a, b, ca, cb)


def _mm_fwd(a, b, ca, cb):
    return _dg(a, b, ca, cb), (a, b)


def _mm_bwd(ca, cb, res, g):
    a, b = res
    if ca == 1:
        da = _mm(g, b, 1, 1) if cb == 0 else _mm(g, b, 1, 0)
    else:
        da = _mm(b, g, 1, 1) if cb == 0 else _mm(b, g, 0, 1)
    if cb == 0:
        db = _mm(a, g, 0, 0) if ca == 1 else _mm(a, g, 1, 0)
    else:
        db = _mm(g, a, 0, 0) if ca == 1 else _mm(g, a, 0, 1)
    return da.astype(a.dtype), db.astype(b.dtype)


_mm.defvjp(_mm_fwd, _mm_bwd)


@functools.partial(jax.custom_vjp, nondiff_argnums=(2, 3))
def _bmm(a, b, ca, cb):
    return _dg(a, b, ca, cb, True)


def _bmm_fwd(a, b, ca, cb):
    return _dg(a, b, ca, cb, True), (a, b)


def _bmm_bwd(ca, cb, res, g):
    a, b = res
    if ca == 2:
        da = _bmm(g, b, 2, 2) if cb == 1 else _bmm(g, b, 2, 1)
    else:
        da = _bmm(b, g, 2, 2) if cb == 1 else _bmm(b, g, 1, 2)
    if cb == 1:
        db = _bmm(a, g, 1, 1) if ca == 2 else _bmm(a, g, 2, 1)
    else:
        db = _bmm(g, a, 1, 1) if ca == 2 else _bmm(g, a, 1, 2)
    return da.astype(a.dtype), db.astype(b.dtype)


_bmm.defvjp(_bmm_fwd, _bmm_bwd)


def _split_bf16(x):
    hi = x.astype(BF16)
    lo = (x - hi.astype(F32)).astype(BF16)
    return hi, lo


def _mm3(a, b, ca, cb):
    ah, al = _split_bf16(a)
    bh, bl = _split_bf16(b)
    dn = (((ca,), (cb,)), ((), ()))
    d = lambda u, v: lax.dot_general(u, v, dn, preferred_element_type=F32)
    return d(ah, bh) + (d(ah, bl) + d(al, bh))


def _mm_exact(a, b):
    return jnp.dot(a, b, precision=HIGHEST, preferred_element_type=F32)


class Arg:
    def __init__(self, x, block, imap, diff=True, first=None, gdtype=F32, pieces=None, gshape=None, gimap=None):
        self.x, self.block, self.imap = x, tuple(block), imap
        self.diff, self.first, self.gdtype, self.pieces = diff, first, gdtype, pieces
        self.upcast = x.dtype == BF16 and diff
        self.gshape = tuple(x.shape) if gshape is None else tuple(gshape)
        self.gimap = imap if gimap is None else gimap

    def spec(self):
        return pl.BlockSpec(self.block, self.imap)

    def gspec(self):
        return pl.BlockSpec(self.block, self.gimap)


class Out:
    def __init__(self, shape, dtype, block, imap, pieces=None):
        self.shape, self.dtype, self.block, self.imap, self.pieces = tuple(shape), dtype, tuple(block), imap, pieces

    def spec(self):
        return pl.BlockSpec(self.block, self.imap)

    def sds(self):
        return jax.ShapeDtypeStruct(self.shape, self.dtype)


def _lanes(ref, s, w):
    return (slice(None),) * (len(ref.shape) - 1) + (slice(s, s + w),)


def _load(ref, pieces):
    if pieces is None:
        return ref[...]
    return tuple(ref[_lanes(ref, s, w)] for s, w in pieces)


def _store(ref, val, pieces, accumulate=False):
    if pieces is None:
        if accumulate:
            ref[...] += val.astype(ref.dtype)
        else:
            ref[...] = val.astype(ref.dtype)
        return
    if not accumulate:
        covered = sum(w for _, w in pieces)
        if covered != ref.shape[-1]:
            ref[...] = jnp.zeros(ref.shape, ref.dtype)
    for (s, w), v in zip(pieces, val):
        if accumulate:
            ref[_lanes(ref, s, w)] += v.astype(ref.dtype)
        else:
            ref[_lanes(ref, s, w)] = v.astype(ref.dtype)


def _params(n_grid):
    return pltpu.CompilerParams(dimension_semantics=("arbitrary",) * n_grid, vmem_limit_bytes=VMEM_LIMIT)


def _load_arg(ref, a):
    val = _load(ref, a.pieces)
    return jax.tree.map(lambda t: t.astype(F32), val) if a.upcast else val


def vfwd(name, f, grid, args, outs, comm=None):
    n_in = len(args)

    def body(*refs):
        in_refs, out_refs, _, xrefs = _carried(refs, n_in, len(outs), comm)
        _carry_start(xrefs, comm, grid)
        ids = tuple(pl.program_id(i) for i in range(len(grid)))
        vals = [_load_arg(r, a) for r, a in zip(in_refs, args)]
        res = f(ids, *vals)
        for r, o, spec in zip(out_refs, res, outs):
            _store(r, o, spec.pieces)
        _carry_wait(xrefs, comm, grid)

    return _carrier_call(body, name, grid, [a.spec() for a in args], [a.x for a in args], [o.spec() for o in outs],
                         [o.sds() for o in outs], [], comm)


def vbwd(name, f, grid, args, outs, cts):
    n_in = len(args)
    diff_idx = [i for i, a in enumerate(args) if a.diff]
    ct_flat = [c for per_out in cts for c in per_out]
    ct_specs = [o.spec() for o, per_out in zip(outs, cts) for _ in per_out]
    n_ct = len(ct_flat)

    def body(*refs):
        ids = tuple(pl.program_id(i) for i in range(len(grid)))
        vals = [_load_arg(r, a) for r, a in zip(refs[:n_in], args)]
        ct_refs = refs[n_in:n_in + n_ct]
        g_refs = refs[n_in + n_ct:]

        def g(*dvals):
            full = list(vals)
            for i, v in zip(diff_idx, dvals):
                full[i] = v
            return tuple(f(ids, *full))

        _, vjp = jax.vjp(g, *[vals[i] for i in diff_idx])
        ct_vals, k = [], 0
        for o, per_out in zip(outs, cts):
            tot = None
            for _ in per_out:
                v = _load(ct_refs[k], o.pieces)
                k += 1
                v = jax.tree.map(lambda t: t.astype(F32), v)
                tot = v if tot is None else jax.tree.map(jnp.add, tot, v)
            ct_vals.append(tot)
        grads = vjp(tuple(ct_vals))
        for gr, gv, i in zip(g_refs, grads, diff_idx):
            a = args[i]
            if a.first is None:
                _store(gr, gv, a.pieces)
            else:
                is_first = a.first(ids)

                @pl.when(is_first)
                def _():
                    _store(gr, gv, a.pieces)

                @pl.when(jnp.logical_not(is_first))
                def _():
                    _store(gr, gv, a.pieces, accumulate=True)

    g_specs = [args[i].gspec() for i in diff_idx]
    g_shapes = [jax.ShapeDtypeStruct(args[i].gshape, args[i].gdtype) for i in diff_idx]
    return pl.pallas_call(
        body, name=name, grid=grid,
        in_specs=[a.spec() for a in args] + ct_specs, out_specs=g_specs, out_shape=g_shapes,
        compiler_params=_params(len(grid)),
    )(*[a.x for a in args], *ct_flat)


def matmul(name, a, b, trans_a=False, trans_b=False, epilogue=None, extras=(), out_dtypes=(F32,)):
    if trans_a:
        kdim, m = a.shape
    else:
        m, kdim = a.shape
    n = b.shape[0] if trans_b else b.shape[1]
    assert b.shape[1 if trans_b else 0] == kdim
    tm = _pick(m, (1024, 512, 256, 192, 128, 64))
    tn = _pick(n, (1024, 512, 256, 128))
    tk = _pick(kdim, (1024, 512, 256, 192, 128, 64))
    nk = kdim // tk
    n_x = len(extras)
    assert epilogue is None or nk == 1
    assert epilogue is not None or tuple(out_dtypes) == (F32,)

    def body(a_ref, b_ref, *rest):
        o_ref = rest[n_x]
        part = _dg(a_ref[...], b_ref[...], 0 if trans_a else 1, 1 if trans_b else 0)
        if epilogue is not None:
            for r, val in zip(rest[n_x:], epilogue(part, *[x[...] for x in rest[:n_x]])):
                r[...] = val.astype(r.dtype)
        elif nk == 1:
            o_ref[...] = part
        else:
            k = pl.program_id(2)

            @pl.when(k == 0)
            def _():
                o_ref[...] = part

            @pl.when(k > 0)
            def _():
                o_ref[...] += part

    a_spec = pl.BlockSpec((tk, tm), lambda i, j, k: (k, i)) if trans_a else pl.BlockSpec((tm, tk), lambda i, j, k: (i, k))
    b_spec = pl.BlockSpec((tn, tk), lambda i, j, k: (j, k)) if trans_b else pl.BlockSpec((tk, tn), lambda i, j, k: (k, j))
    mn_spec = pl.BlockSpec((tm, tn), lambda i, j, k: (i, j))
    res = pl.pallas_call(
        body, name=name, grid=(m // tm, n // tn, nk),
        in_specs=[a_spec, b_spec] + [mn_spec] * n_x,
        out_specs=[mn_spec] * len(out_dtypes),
        out_shape=[jax.ShapeDtypeStruct((m, n), dt) for dt in out_dtypes],
        compiler_params=pltpu.CompilerParams(dimension_semantics=("parallel", "parallel", "arbitrary"),
                                             vmem_limit_bytes=VMEM_LIMIT),
    )(a, b, *extras)
    return res if epilogue is not None else res[0]


def _rms(x, gain):
    return x * lax.rsqrt(jnp.mean(x * x, axis=-1, keepdims=True) + EPS) * gain


def _silu(x):
    return x * jax.nn.sigmoid(x)


def f_modnorm(ids, x, gain, mod):
    return (_rms(x, gain) * (1.0 + mod[1:2]) + mod[0:1],)


def f_resnorm(ids, x, y, g_res, g_next, mod):
    x_new = x + mod[0:1] * _rms(y, g_res)
    return x_new, _rms(x_new, g_next) * (1.0 + mod[2:3]) + mod[1:2]


def f_resid(ids, x, y, g_res, mod):
    return (x + mod[0:1] * _rms(y, g_res),)


def f_act(ids, a):
    r = jnp.maximum(a, 0.0)
    return (r * r,)


def _head_consts(width, head):
    i = lax.broadcasted_iota(jnp.int32, (width, width), 0)
    j = lax.broadcasted_iota(jnp.int32, (width, width), 1)
    shift = int(math.log2(head))
    same = (i >> shift) == (j >> shift)
    group = jnp.where(same, 1.0 / head, 0.0).astype(F32)
    half = head // 2
    ii, jj = i & (head - 1), j & (head - 1)
    rot = jnp.where(same & (ii == jj + half) & (jj < half), -1.0, 0.0) + jnp.where(same & (ii + half == jj) & (jj >= half), 1.0, 0.0)
    ti = lax.broadcasted_iota(jnp.int32, (head, width), 0)
    tj = lax.broadcasted_iota(jnp.int32, (head, width), 1)
    tile = jnp.where(ti == (tj & (head - 1)), 1.0, 0.0).astype(F32)
    return group, rot.astype(F32), tile


def _rope(x, cos, sin, rot):
    return x * cos + _mm_exact(x, rot) * sin


def _softplus(x):
    return jnp.maximum(x, 0.0) + jnp.log(1.0 + jnp.exp(-jnp.abs(x)))


def f_prep(ids, zq, zk, rq, rk, zn, qgain, kgain, alog, dtb, cos, sin):
    grp_q, rot_q, tile_q = _head_consts(ATT_HEADS * HEAD64, HEAD64)
    grp_k, rot_k, tile_k = _head_consts(ATT_KV_HEADS * HEAD64, HEAD64)
    grp_r, rot_r, _ = _head_consts(N_HEAD4 * HEAD64, HEAD64)
    wq, wk, wr = zq.shape[-1], zk.shape[-1], rq.shape[-1]
    qn = zq * lax.rsqrt(_mm_exact(zq * zq, grp_q) + EPS) * _mm_exact(qgain, tile_q)
    kn = zk * lax.rsqrt(_mm_exact(zk * zk, grp_k) + EPS) * _mm_exact(kgain, tile_k)
    q_att = _rope(qn, cos[:, :wq], sin[:, :wq], rot_q) * (HEAD64 ** -0.5)
    k_att = _rope(kn, cos[:, :wk], sin[:, :wk], rot_k)
    q_ret = _rope(rq, cos[:, :wr], sin[:, :wr], rot_r)
    k_ret = _rope(rk * (HEAD64 ** -0.5), cos[:, :wr], sin[:, :wr], rot_r)
    lane = lax.broadcasted_iota(jnp.int32, zn.shape, 1)
    log_a = -jnp.exp(alog) * _softplus(zn + dtb)
    gates = jnp.where(lane < 8, log_a, jnp.where(lane < 16, jax.nn.sigmoid(zn), 0.0))
    return q_att, k_att, q_ret, k_ret, gates


def make_f_attn(tq):
    def f_attn(ids, q, k, v):
        outs = []
        per = ATT_HEADS // ATT_KV_HEADS
        for g in range(ATT_KV_HEADS):
            qg = jnp.concatenate(q[g * per:(g + 1) * per], axis=0)
            s = _mm(qg, k[g], 1, 1)
            e = jnp.exp(s - lax.stop_gradient(jnp.max(s, axis=-1, keepdims=True)))
            o = _mm(e, v[g], 1, 0) * (1.0 / jnp.sum(e, axis=-1, keepdims=True))
            outs += [o[i * tq:(i + 1) * tq] for i in range(per)]
        return (tuple(outs),)
    return f_attn


def _roll_rows(x, shift):
    return pltpu.roll(x, shift, 0)


def make_shifts(t, lc):
    def _down(x):
        row = lax.broadcasted_iota(jnp.int32, x.shape, 0)
        return jnp.where((row == 0) | (row == lc), 0.0, _roll_rows(x, 1))

    def _up(x):
        row = lax.broadcasted_iota(jnp.int32, x.shape, 0)
        return jnp.where((row == lc - 1) | (row == t - 1), 0.0, _roll_rows(x, t - 1))

    @jax.custom_vjp
    def down(x):
        return _down(x)

    @jax.custom_vjp
    def up(x):
        return _up(x)

    down.defvjp(lambda x: (_down(x), None), lambda _, g: (up(g),))
    up.defvjp(lambda x: (_up(x), None), lambda _, g: (down(g),))
    return down, up


def make_conv3(t, lc):
    down, up = make_shifts(t, lc)

    def conv3(x, w):
        return w[0:1] * down(x) + w[1:2] * x + w[2:3] * up(x)
    return conv3


def make_f_dnprep(t, lc):
    conv3 = make_conv3(t, lc)

    def l2n(x):
        return x * lax.rsqrt(jnp.sum(x * x, axis=-1, keepdims=True) + EPS)

    def f_dnprep(ids, q, k, v, wq, wk, wv):
        qn = l2n(_silu(conv3(q, wq))) * (HEAD128 ** -0.5)
        kn = l2n(_silu(conv3(k, wk)))
        return qn, kn, _silu(conv3(v, wv))
    return f_dnprep


def make_f_shortconv(t, lc):
    conv3 = make_conv3(t, lc)

    def f_shortconv(ids, b, c, x, w):
        return (b * conv3(c * x, w),)
    return f_shortconv


def f_finish(ids, o_dn_f, o_dn_b, z_dn, o_rt_f, o_rt_b, g_rt, ngain):
    y_dn, y_rt = [], []
    for h in range(N_HEAD4):
        o = o_dn_f[h] + o_dn_b[h]
        y_dn.append(_rms(o, ngain) * _silu(z_dn[h]))
        r = o_rt_f[h] + o_rt_b[h]
        mu = jnp.mean(r, axis=-1, keepdims=True)
        var = jnp.mean(jnp.square(r - mu), axis=-1, keepdims=True)
        y_rt.append((r - mu) * lax.rsqrt(var + EPS) * _silu(g_rt[h]))
    return tuple(y_dn), tuple(y_rt)


def f_merge(ids, p0, p1, p2, p3, gates):
    u = jax.nn.sigmoid(gates[0]) * p0
    for g, p in zip(gates[1:], (p1, p2, p3)):
        u = u + jax.nn.sigmoid(g) * p
    return (u,)


def _stack_masks():
    n = N_HEAD4 * CHUNK
    i = lax.broadcasted_iota(jnp.int32, (n, n), 0)
    j = lax.broadcasted_iota(jnp.int32, (n, n), 1)
    same = (i >> 6) == (j >> 6)
    pi, pj = i & (CHUNK - 1), j & (CHUNK - 1)
    return same, pi, pj


def _inv_unit_lower(low):
    n = low.shape[0]
    eye = jnp.where(lax.broadcasted_iota(jnp.int32, (n, n), 0) == lax.broadcasted_iota(jnp.int32, (n, n), 1), 1.0, 0.0).astype(F32)
    m = -low
    p = eye + m
    for _ in range(int(math.log2(CHUNK)) - 1):
        m = _mm3(m, m, 1, 0)
        p = p + _mm3(p, m, 1, 0)
    return p


@jax.custom_vjp
def _tri_solve(low, rhs):
    return _mm3(_inv_unit_lower(low), rhs, 1, 0)


def _tri_solve_fwd(low, rhs):
    inv = _inv_unit_lower(low)
    x = _mm3(inv, rhs, 1, 0)
    return x, (inv, x)


def _tri_solve_bwd(res, g):
    inv, x = res
    d_rhs = _mm3(inv, g, 0, 0)
    return -_mm3(d_rhs, x, 1, 1), d_rhs


_tri_solve.defvjp(_tri_solve_fwd, _tri_solve_bwd)


def _heads3(x):
    return x.reshape(N_HEAD4, CHUNK, x.shape[-1])


def _time_masks(direction):
    same, pi, pj = _stack_masks()
    if direction == 0:
        return same, same & (pi >= pj), same & (pi > pj), same & (pj == CHUNK - 1), pi - pj
    return same, same & (pi <= pj), same & (pi < pj), same & (pj == 0), pj - pi


def make_dn_chunk(direction):
    def dn_chunk(s, toks, params):
        q, k, v, la, beta = toks
        qs, ks, vs = (jnp.concatenate(t, axis=0) for t in (q, k, v))
        la, beta = jnp.concatenate(la, axis=0), jnp.concatenate(beta, axis=0)
        n = N_HEAD4 * CHUNK
        same, incl, strict, last, _ = _time_masks(direction)
        g = jnp.sum(jnp.where(incl, jnp.broadcast_to(la, (n, n)).T, 0.0), axis=1, keepdims=True)
        gb = jnp.broadcast_to(g, (n, n))
        gbt = gb.T
        dec_incl = jnp.where(incl, jnp.exp(jnp.where(incl, gb - gbt, 0.0)), 0.0)
        dec_strict = jnp.where(strict, dec_incl, 0.0)
        low = beta * _mm(ks, ks, 1, 1) * dec_strict
        eg = jnp.exp(g)
        sol = _tri_solve(low, jnp.concatenate([beta * vs, (beta * eg) * ks], axis=1))
        w_v, w_k = sol[:, :HEAD128], sol[:, HEAD128:]
        a_qk = _mm(qs, ks, 1, 1) * dec_incl
        g_last = jnp.sum(jnp.where(last, gbt, 0.0), axis=1, keepdims=True)
        k_dec = ks * jnp.exp(g_last - g)
        u = w_v - _bmm(_heads3(w_k), s, 2, 1).reshape(n, HEAD128)
        o = _bmm(_heads3(qs * eg), s, 2, 1).reshape(n, HEAD128) + _mm(a_qk, u, 1, 0)
        decay = jnp.exp(jnp.mean(_heads3(g_last), axis=1, keepdims=True))
        s_new = s * decay + _bmm(_heads3(k_dec), _heads3(u), 1, 1)
        return (tuple(o[h * CHUNK:(h + 1) * CHUNK] for h in range(N_HEAD4)),), s_new
    return dn_chunk


def make_ret_chunk(direction):
    def ret_chunk(s, toks, params):
        q, k, v = toks
        (decay_log,) = params
        qs, ks, vs = (jnp.concatenate(t, axis=0) for t in (q, k, v))
        n = N_HEAD4 * CHUNK
        _, incl, _, _, rel = _time_masks(direction)
        row_head = lax.broadcasted_iota(jnp.int32, (n, 1), 0) >> 6
        lg = jnp.zeros((n, 1), F32)
        for h in range(N_HEAD4):
            lg = jnp.where(row_head == h, -jnp.exp(decay_log[direction * N_HEAD4 + h]), lg)
        row = (lax.broadcasted_iota(jnp.int32, (n, 1), 0) & (CHUNK - 1)).astype(F32)
        pos = row if direction == 0 else CHUNK - 1.0 - row
        dmask = jnp.where(incl, jnp.exp(jnp.where(incl, rel.astype(F32) * lg, 0.0)), 0.0)
        o = _mm(_mm(qs, ks, 1, 1) * dmask, vs, 1, 0)
        k_dec = ks * jnp.exp((CHUNK - 1.0 - pos) * lg)
        kv = _bmm(_heads3(k_dec), _heads3(vs), 1, 1)
        o = o + _bmm(_heads3(qs * jnp.exp((pos + 1.0) * lg)), s, 2, 1).reshape(n, HEAD128)
        decay = jnp.exp(CHUNK * jnp.mean(_heads3(lg), axis=1, keepdims=True))
        s_new = s * decay + kv
        return (tuple(o[h * CHUNK:(h + 1) * CHUNK] for h in range(N_HEAD4)),), s_new
    return ret_chunk


def _lane_pieces(width, n=N_HEAD4):
    return [(h * width, width) for h in range(n)]


class Tok:
    def __init__(self, x, pieces, width=None, col=0):
        self.x, self.pieces, self.col = x, pieces, col
        self.width = x.shape[-1] if width is None else width
        self.upcast = x.dtype == BF16


def _chunk_of(direction, step, nc, nctx):
    if direction == 0:
        return step
    return jnp.where(step < nctx, nctx - 1 - step, nc + nctx - 1 - step)


def scan_fwd(name, dirs, params, state_shape, outs, lc):
    bsz, t = dirs[0][1][0].x.shape[:2]
    nc, nctx = t // CHUNK, lc // CHUNK
    nd, n_t, n_p, n_o = len(dirs), len(dirs[0][1]), len(params), len(outs)

    def body(*refs):
        tok_refs, par_refs = refs[:nd * n_t], refs[nd * n_t:nd * n_t + n_p]
        out_refs = refs[nd * n_t + n_p:-nd]
        s_refs = refs[-nd:]
        pv = [_load(r, p) for r, (_, p) in zip(par_refs, params)]
        for d, (fn, toks) in enumerate(dirs):
            s_ref = s_refs[d]

            @pl.when(pl.program_id(1) == 0)
            def _():
                s_ref[...] = jnp.zeros(s_ref.shape, F32)

            o_refs = out_refs[d * (n_o + 1):(d + 1) * (n_o + 1)]
            s = s_ref[...]
            o_refs[-1][...] = s
            tv = [_load(r, tk.pieces) for r, tk in zip(tok_refs[d * n_t:(d + 1) * n_t], toks)]
            res, s_new = fn(s, tv, pv)
            s_ref[...] = s_new
            for r, o, (_, _, pieces) in zip(o_refs, res, outs):
                _store(r, o, pieces)

    nstate = len(state_shape)
    tok_specs, out_specs, out_shapes, operands = [], [], [], []
    for d, (_, toks) in enumerate(dirs):
        for tk in toks:
            tok_specs.append(pl.BlockSpec((None, CHUNK, tk.width), lambda b, c, d=d, col=tk.col: (b, _chunk_of(d, c, nc, nctx), col)))
            operands.append(tk.x)
        for w, dt, _ in outs:
            out_specs.append(pl.BlockSpec((None, CHUNK, w), lambda b, c, d=d: (b, _chunk_of(d, c, nc, nctx), 0)))
            out_shapes.append(jax.ShapeDtypeStruct((bsz, t, w), dt))
        out_specs.append(pl.BlockSpec((None, None) + state_shape, lambda b, c, d=d: (b, _chunk_of(d, c, nc, nctx)) + (0,) * nstate))
        out_shapes.append(jax.ShapeDtypeStruct((bsz, nc) + state_shape, F32))
    par_specs = [pl.BlockSpec(x.shape, lambda b, c: (0, 0)) for x, _ in params]
    return pl.pallas_call(
        body, name=name, grid=(bsz, nc),
        in_specs=tok_specs + par_specs, out_specs=out_specs, out_shape=out_shapes,
        scratch_shapes=[pltpu.VMEM(state_shape, F32)] * nd, compiler_params=_params(2),
    )(*operands, *[x for x, _ in params])


def scan_bwd(name, dirs, params, state_shape, outs, lc, sprevs, cts):
    bsz, t = dirs[0][1][0].x.shape[:2]
    nc, nctx = t // CHUNK, lc // CHUNK
    nd, n_t, n_p, n_o = len(dirs), len(dirs[0][1]), len(params), len(outs)
    per_in = n_t + 1 + n_o

    def body(*refs):
        par_refs = refs[nd * per_in:nd * per_in + n_p]
        g_refs = refs[nd * per_in + n_p:-nd]
        ds_refs = refs[-nd:]
        first = pl.program_id(1) == 0
        pv = [_load(r, p) for r, (_, p) in zip(par_refs, params)]
        d_par = None
        for d, (fn, toks) in enumerate(dirs):
            ins = refs[d * per_in:(d + 1) * per_in]
            tok_refs, sprev_ref, ct_refs = ins[:n_t], ins[n_t], ins[n_t + 1:]
            ds_ref = ds_refs[d]

            @pl.when(first)
            def _():
                ds_ref[...] = jnp.zeros(ds_ref.shape, F32)

            tv = [_load(r, tk.pieces) for r, tk in zip(tok_refs, toks)]
            _, vjp = jax.vjp(fn, sprev_ref[...], tv, pv)
            ct = tuple(_load(r, pieces) for r, (_, _, pieces) in zip(ct_refs, outs))
            d_s, d_tv, d_pv = vjp((ct, ds_ref[...]))
            ds_ref[...] = d_s
            for r, gv, tk in zip(g_refs[d * n_t:(d + 1) * n_t], d_tv, toks):
                _store(r, gv, tk.pieces)
            d_par = d_pv if d_par is None else jax.tree.map(jnp.add, d_par, d_pv)
        very_first = first & (pl.program_id(0) == 0)
        for r, gv, (_, p) in zip(g_refs[nd * n_t:], d_par, params):
            @pl.when(very_first)
            def _():
                _store(r, gv, p)

            @pl.when(jnp.logical_not(very_first))
            def _():
                _store(r, gv, p, accumulate=True)

    nstate = len(state_shape)
    in_specs, operands, g_specs, g_shapes = [], [], [], []
    for d, (_, toks) in enumerate(dirs):
        chunk = lambda c, d=d: _chunk_of(d, nc - 1 - c, nc, nctx)
        for tk in toks:
            in_specs.append(pl.BlockSpec((None, CHUNK, tk.width), lambda b, c, f=chunk, col=tk.col: (b, f(c), col)))
            operands.append(tk.x)
            g_specs.append(pl.BlockSpec((None, CHUNK, tk.width), lambda b, c, f=chunk: (b, f(c), 0)))
            g_shapes.append(jax.ShapeDtypeStruct((bsz, t, tk.width), F32))
        in_specs.append(pl.BlockSpec((None, None) + state_shape, lambda b, c, f=chunk: (b, f(c)) + (0,) * nstate))
        operands.append(sprevs[d])
        for (w, _, _), ct in zip(outs, cts[d]):
            in_specs.append(pl.BlockSpec((None, CHUNK, w), lambda b, c, f=chunk: (b, f(c), 0)))
            operands.append(ct)
    par_specs = [pl.BlockSpec(x.shape, lambda b, c: (0, 0)) for x, _ in params]
    return pl.pallas_call(
        body, name=name, grid=(bsz, nc),
        in_specs=in_specs + par_specs, out_specs=g_specs + par_specs,
        out_shape=g_shapes + [jax.ShapeDtypeStruct(x.shape, F32) for x, _ in params],
        scratch_shapes=[pltpu.VMEM(state_shape, F32)] * nd, compiler_params=_params(2),
    )(*operands, *[x for x, _ in params])


ROWS = N_HEAD4 * CHUNK


def _bmm3(a, b, ca, cb):
    ah, al = _split_bf16(a)
    bh, bl = _split_bf16(b)
    dn = (((ca,), (cb,)), ((0,), (0,)))
    d = lambda u, v: lax.dot_general(u, v, dn, preferred_element_type=F32)
    return d(ah, bh) + (d(ah, bl) + d(al, bh))


def _inv_unit_tri_b(low):
    n = low.shape[-1]
    eye = (lax.broadcasted_iota(jnp.int32, (1, n, n), 1) == lax.broadcasted_iota(jnp.int32, (1, n, n), 2)).astype(F32)
    m = -low
    p = eye + m
    for _ in range(int(math.log2(CHUNK)) - 1):
        m = _bmm3(m, m, 2, 1)
        p = p + _bmm3(p, m, 2, 1)
    return p


@jax.custom_vjp
def _tri_solve_b(low, rhs, inv):
    return _bmm3(inv, rhs, 2, 1)


def _tri_solve_b_fwd(low, rhs, inv):
    x = _bmm3(inv, rhs, 2, 1)
    return x, (inv, x)


def _tri_solve_b_bwd(res, g):
    inv, x = res
    d_rhs = _bmm3(inv, g, 1, 1)
    return -_bmm3(d_rhs, x, 2, 2), d_rhs, jnp.zeros_like(inv)


_tri_solve_b.defvjp(_tri_solve_b_fwd, _tri_solve_b_bwd)


def _stack_dirs(toks, i):
    return jnp.concatenate([jnp.concatenate(t[i], axis=1) for t in toks], axis=0)


def _problem_masks(p, nb):
    shape = (p, ROWS, ROWS)
    up = lax.broadcasted_iota(jnp.int32, shape, 0) >= nb
    i = lax.broadcasted_iota(jnp.int32, shape, 1)
    j = lax.broadcasted_iota(jnp.int32, shape, 2)
    same = (i >> 6) == (j >> 6)
    pi, pj = i & (CHUNK - 1), j & (CHUNK - 1)
    rel = jnp.where(up, pj - pi, pi - pj)
    last = same & (pj == jnp.where(up, 0, CHUNK - 1))
    return same & (rel >= 0), same & (rel > 0), last, rel


def _split_states(s_new, nd, nb):
    return [s_new[d * nb * N_HEAD4:(d + 1) * nb * N_HEAD4].reshape((nb, N_HEAD4) + s_new.shape[1:]) for d in range(nd)]


def _split_outs(o, nd, nb):
    return [(tuple(o[d * nb:(d + 1) * nb, h * CHUNK:(h + 1) * CHUNK] for h in range(N_HEAD4)),) for d in range(nd)]


def dn_chunks(states, toks, params, aux=None):
    nd, nb = len(toks), states[0].shape[0]
    p = nd * nb
    qs, ks, vs, la, beta = (_stack_dirs(toks, i) for i in range(5))
    incl, strict, last, _ = _problem_masks(p, nb)
    sq = (p, ROWS, ROWS)
    g = jnp.sum(jnp.where(incl, jnp.swapaxes(jnp.broadcast_to(la, sq), 1, 2), 0.0), axis=2, keepdims=True)
    gb = jnp.broadcast_to(g, sq)
    gbt = jnp.swapaxes(gb, 1, 2)
    dec_incl = jnp.where(incl, jnp.exp(jnp.where(incl, gb - gbt, 0.0)), 0.0)
    dec_strict = jnp.where(strict, dec_incl, 0.0)
    low = beta * _bmm(ks, ks, 2, 2) * dec_strict
    eg = jnp.exp(g)
    inv = _inv_unit_tri_b(low) if aux is None else aux[0]
    sol = _tri_solve_b(low, jnp.concatenate([beta * vs, (beta * eg) * ks], axis=2), inv)
    w_v, w_k = sol[:, :, :HEAD128], sol[:, :, HEAD128:]
    a_qk = _bmm(qs, ks, 2, 2) * dec_incl
    g_last = jnp.sum(jnp.where(last, gbt, 0.0), axis=2, keepdims=True)
    k_dec = ks * jnp.exp(g_last - g)
    s = jnp.concatenate(states, axis=0).reshape(p * N_HEAD4, HEAD128, HEAD128)
    h3 = lambda x: x.reshape(p * N_HEAD4, CHUNK, x.shape[-1])
    u = w_v - _bmm(h3(w_k), s, 2, 1).reshape(p, ROWS, HEAD128)
    o = _bmm(h3(qs * eg), s, 2, 1).reshape(p, ROWS, HEAD128) + _bmm(a_qk, u, 2, 1)
    decay = jnp.exp(jnp.mean(h3(g_last), axis=1, keepdims=True))
    s_new = s * decay + _bmm(h3(k_dec), h3(u), 1, 1)
    return _split_outs(o, nd, nb), _split_states(s_new, nd, nb), [inv]


def ret_chunks(states, toks, params, aux=None):
    nd, nb = len(toks), states[0].shape[0]
    p = nd * nb
    (decay_log,) = params
    qs, ks, vs = (_stack_dirs(toks, i) for i in range(3))
    incl, _, _, rel = _problem_masks(p, nb)
    col = (p, ROWS, 1)
    up = lax.broadcasted_iota(jnp.int32, col, 0) >= nb
    row = lax.broadcasted_iota(jnp.int32, col, 1)
    head = row >> 6
    lg = jnp.zeros(col, F32)
    for h in range(N_HEAD4):
        rate = jnp.where(up, -jnp.exp(decay_log[N_HEAD4 + h]), -jnp.exp(decay_log[h]))
        lg = jnp.where(head == h, rate, lg)
    place = row & (CHUNK - 1)
    pos = jnp.where(up, CHUNK - 1 - place, place).astype(F32)
    dmask = jnp.where(incl, jnp.exp(jnp.where(incl, rel.astype(F32) * lg, 0.0)), 0.0)
    o = _bmm(_bmm(qs, ks, 2, 2) * dmask, vs, 2, 1)
    s = jnp.concatenate(states, axis=0).reshape(p * N_HEAD4, HEAD64, HEAD128)
    h3 = lambda x: x.reshape(p * N_HEAD4, CHUNK, x.shape[-1])
    kv = _bmm(h3(ks * jnp.exp((CHUNK - 1.0 - pos) * lg)), h3(vs), 1, 1)
    o = o + _bmm(h3(qs * jnp.exp((pos + 1.0) * lg)), s, 2, 1).reshape(p, ROWS, HEAD128)
    decay = jnp.exp(CHUNK * jnp.mean(h3(lg), axis=1, keepdims=True))
    s_new = s * decay + kv
    return _split_outs(o, nd, nb), _split_states(s_new, nd, nb), []


def _scan_nb(bsz):
    return next(n for n in (2, 1) if bsz % n == 0)


def _carried(refs, n_in, n_out, comm):
    if comm is None:
        return refs[:n_in], refs[n_in:n_in + n_out], refs[n_in + n_out:], None
    n = len(comm[1])
    ins, srcs = refs[:n_in], refs[n_in:n_in + n]
    outs, dsts = refs[n_in + n:n_in + n + n_out], refs[n_in + n + n_out:n_in + 2 * n + n_out]
    scratch, sems = refs[n_in + 2 * n + n_out:-3], refs[-3:]
    return ins, outs, scratch, (srcs, dsts) + tuple(sems)


def _carry_start(xrefs, comm, grid):
    if comm is None:
        return
    first = functools.reduce(jnp.logical_and, [pl.program_id(i) == 0 for i in range(len(grid))])

    @pl.when(first)
    def _():
        _exchange_start(xrefs, comm[0])


def _carry_wait(xrefs, comm, grid):
    if comm is None:
        return
    last = functools.reduce(jnp.logical_and, [pl.program_id(i) == n - 1 for i, n in enumerate(grid)])

    @pl.when(last)
    def _():
        _exchange_wait(xrefs, comm[0])


def scan2_fwd(name, chunks_fn, dirs, params, state_shape, outs, lc, comm=None, aux_shapes=()):
    bsz, t = dirs[0][0].x.shape[:2]
    nb = _scan_nb(bsz)
    nc, nctx = t // CHUNK, lc // CHUNK
    nd, n_t, n_p, n_o, n_a = len(dirs), len(dirs[0]), len(params), len(outs), len(aux_shapes)
    grid = (bsz // nb, nc)

    def body(*refs):
        ins, out_refs, s_refs, xrefs = _carried(refs, nd * n_t + n_p, nd * (n_o + 1) + n_a, comm)
        tok_refs, par_refs = ins[:nd * n_t], ins[nd * n_t:]
        _carry_start(xrefs, comm, grid)

        @pl.when(pl.program_id(1) == 0)
        def _():
            for s_ref in s_refs:
                s_ref[...] = jnp.zeros(s_ref.shape, F32)

        pv = [_load(r, p) for r, (_, p) in zip(par_refs, params)]
        states = [s_ref[...] for s_ref in s_refs]
        tv = [[_load_arg(r, tk) for r, tk in zip(tok_refs[d * n_t:(d + 1) * n_t], dirs[d])] for d in range(nd)]
        res, s_new, aux = chunks_fn(states, tv, pv)
        for d in range(nd):
            o_refs = out_refs[d * (n_o + 1):(d + 1) * (n_o + 1)]
            o_refs[-1][...] = states[d]
            s_refs[d][...] = s_new[d]
            for r, o, (_, _, pieces) in zip(o_refs, res[d], outs):
                _store(r, o, pieces)
        for r, a in zip(out_refs[nd * (n_o + 1):], aux):
            r[...] = a
        _carry_wait(xrefs, comm, grid)

    nstate = len(state_shape)
    tok_specs, out_specs, out_shapes, operands = [], [], [], []
    for d, toks in enumerate(dirs):
        chunk = lambda c, d=d: _chunk_of(d, c, nc, nctx)
        for tk in toks:
            tok_specs.append(pl.BlockSpec((nb, CHUNK, tk.width), lambda b, c, f=chunk, col=tk.col: (b, f(c), col)))
            operands.append(tk.x)
        for w, dt, _ in outs:
            out_specs.append(pl.BlockSpec((nb, CHUNK, w), lambda b, c, f=chunk: (b, f(c), 0)))
            out_shapes.append(jax.ShapeDtypeStruct((bsz, t, w), dt))
        out_specs.append(pl.BlockSpec((nb, None) + state_shape, lambda b, c, f=chunk: (b, f(c)) + (0,) * nstate))
        out_shapes.append(jax.ShapeDtypeStruct((bsz, nc) + state_shape, F32))
    for shape in aux_shapes:
        out_specs.append(pl.BlockSpec((None, None) + tuple(shape), lambda b, c, n=len(shape): (b, c) + (0,) * n))
        out_shapes.append(jax.ShapeDtypeStruct(grid + tuple(shape), F32))
    par_specs = [pl.BlockSpec(x.shape, lambda b, c: (0, 0)) for x, _ in params]
    operands += [x for x, _ in params]
    return _carrier_call(body, name, grid, tok_specs + par_specs, operands, out_specs, out_shapes,
                         [pltpu.VMEM((nb,) + state_shape, F32)] * nd, comm)


def _carrier_call(body, name, grid, in_specs, operands, out_specs, out_shapes, scratch, comm):
    if comm is not None:
        kind, arrays = comm
        in_specs, operands = in_specs + [HBM_SPEC] * len(arrays), operands + list(arrays)
        out_specs, out_shapes = out_specs + [HBM_SPEC] * len(arrays), out_shapes + [_exchange_shape(kind, a) for a in arrays]
        scratch = scratch + _exchange_sems(len(arrays))
    return pl.pallas_call(body, name=name, grid=grid, in_specs=in_specs, out_specs=out_specs, out_shape=out_shapes,
                          scratch_shapes=scratch, compiler_params=_params(len(grid)))(*operands)


def scan2_bwd(name, chunks_fn, dirs, params, state_shape, outs, lc, sprevs, cts, comm=None, aux=()):
    bsz, t = dirs[0][0].x.shape[:2]
    nb = _scan_nb(bsz)
    nc, nctx = t // CHUNK, lc // CHUNK
    nd, n_t, n_p, n_o, n_a = len(dirs), len(dirs[0]), len(params), len(outs), len(aux)
    per_in = n_t + 1 + n_o
    grid = (bsz // nb, nc)

    def body(*refs):
        refs, g_refs, ds_refs, xrefs = _carried(refs, nd * per_in + n_p + n_a, nd * n_t + n_p, comm)
        par_refs = refs[nd * per_in:nd * per_in + n_p]
        saved = [r[...] for r in refs[nd * per_in + n_p:]]
        fn = chunks_fn if not saved else (lambda s, tk, pr: chunks_fn(s, tk, pr, saved))
        _carry_start(xrefs, comm, grid)
        first = pl.program_id(1) == 0

        @pl.when(first)
        def _():
            for ds_ref in ds_refs:
                ds_ref[...] = jnp.zeros(ds_ref.shape, F32)

        pv = [_load(r, p) for r, (_, p) in zip(par_refs, params)]
        ins = [refs[d * per_in:(d + 1) * per_in] for d in range(nd)]
        tv = [[_load_arg(r, tk) for r, tk in zip(ins[d][:n_t], dirs[d])] for d in range(nd)]
        states = [ins[d][n_t][...] for d in range(nd)]
        _, vjp = jax.vjp(lambda s, tk, pr: fn(s, tk, pr)[:2], states, tv, pv)
        ct = [tuple(_load(r, pieces) for r, (_, _, pieces) in zip(ins[d][n_t + 1:], outs)) for d in range(nd)]
        d_s, d_tv, d_pv = vjp((ct, [ds_ref[...] for ds_ref in ds_refs]))
        for d in range(nd):
            ds_refs[d][...] = d_s[d]
            for r, gv, tk in zip(g_refs[d * n_t:(d + 1) * n_t], d_tv[d], dirs[d]):
                _store(r, gv, tk.pieces)
        very_first = first & (pl.program_id(0) == 0)
        for r, gv, (_, p) in zip(g_refs[nd * n_t:], d_pv, params):
            @pl.when(very_first)
            def _():
                _store(r, gv, p)

            @pl.when(jnp.logical_not(very_first))
            def _():
                _store(r, gv, p, accumulate=True)
        _carry_wait(xrefs, comm, grid)

    nstate = len(state_shape)
    in_specs, operands, g_specs, g_shapes = [], [], [], []
    for d, toks in enumerate(dirs):
        chunk = lambda c, d=d: _chunk_of(d, nc - 1 - c, nc, nctx)
        for tk in toks:
            in_specs.append(pl.BlockSpec((nb, CHUNK, tk.width), lambda b, c, f=chunk, col=tk.col: (b, f(c), col)))
            operands.append(tk.x)
            g_specs.append(pl.BlockSpec((nb, CHUNK, tk.width), lambda b, c, f=chunk: (b, f(c), 0)))
            g_shapes.append(jax.ShapeDtypeStruct((bsz, t, tk.width), F32))
        in_specs.append(pl.BlockSpec((nb, None) + state_shape, lambda b, c, f=chunk: (b, f(c)) + (0,) * nstate))
        operands.append(sprevs[d])
        for (w, _, _), ct in zip(outs, cts[d]):
            in_specs.append(pl.BlockSpec((nb, CHUNK, w), lambda b, c, f=chunk: (b, f(c), 0)))
            operands.append(ct)
    par_specs = [pl.BlockSpec(x.shape, lambda b, c: (0, 0)) for x, _ in params]
    aux_specs = [pl.BlockSpec((None, None) + a.shape[2:], lambda b, c, n=a.ndim - 2: (b, nc - 1 - c) + (0,) * n) for a in aux]
    operands += [x for x, _ in params] + list(aux)
    return _carrier_call(body, name, grid, in_specs + par_specs + aux_specs, operands, g_specs + par_specs,
                         g_shapes + [jax.ShapeDtypeStruct(x.shape, F32) for x, _ in params],
                         [pltpu.VMEM((nb,) + state_shape, F32)] * nd, comm)


class Geo:
    def __init__(self, bsz, t, lc):
        self.b, self.t, self.lc, self.m = bsz, t, lc, bsz * t
        self.tm = _pick(lc, (256, 128, 64))
        self.tq = _pick(lc, (128, 64))
        assert t % self.tm == 0 and t % CHUNK == 0 and lc % CHUNK == 0
        self.nctx, self.nctx_q = lc // self.tm, lc // self.tq
        self.grid = (bsz, t // self.tm)


def a_tok(g, x, tm=None, **kw):
    tm = tm or g.tm
    return Arg(x, (None, tm, x.shape[-1]), lambda b, j: (b, j, 0), **kw)


def a_ztok(g, z, name, width, tm=None, **kw):
    tm = tm or g.tm
    col = ZOFF[name] // width
    return Arg(z, (None, tm, width), lambda b, j: (b, j, col),
               gshape=(g.b, g.t, width), gimap=lambda b, j: (b, j, 0), **kw)


def a_par(x, **kw):
    return Arg(x, x.shape, lambda b, j: (0, 0), first=lambda ids: (ids[0] == 0) & (ids[1] == 0), **kw)


def a_mod(g, x):
    n = g.nctx
    return Arg(x, (None, None, x.shape[2], D_MODEL), lambda b, j: (b, jnp.where(j >= n, 1, 0), 0, 0),
               first=lambda ids: (ids[1] == 0) | (ids[1] == n))


def a_tab(g, x):
    return Arg(x, (g.tm, x.shape[-1]), lambda b, j: (j, 0), diff=False)


def o_tok(g, w, dtype, tm=None, pieces=None):
    tm = tm or g.tm
    return Out((g.b, g.t, w), dtype, (None, tm, w), lambda b, j: (b, j, 0), pieces)


def addn(name, xs, dtype):
    bsz, t, w = xs[0].shape
    tm = _pick(t, (256, 192, 128, 64))

    def body(*refs):
        tot = refs[0][...].astype(F32)
        for r in refs[1:-1]:
            tot = tot + r[...].astype(F32)
        refs[-1][...] = tot.astype(dtype)

    spec = pl.BlockSpec((None, tm, w), lambda b, j: (b, j, 0))
    return pl.pallas_call(body, name=name, grid=(bsz, t // tm), in_specs=[spec] * len(xs), out_specs=spec,
                          out_shape=jax.ShapeDtypeStruct((bsz, t, w), dtype), compiler_params=_params(2))(*xs)


P64x8 = _lane_pieces(HEAD64, ATT_HEADS)
P64x4 = _lane_pieces(HEAD64, N_HEAD4)
P64x2 = _lane_pieces(HEAD64, ATT_KV_HEADS)
P128x4 = _lane_pieces(HEAD128, N_HEAD4)
P1024x4 = _lane_pieces(D_MODEL, N_BRANCH)


def _gate_pieces(direction):
    la = [(direction * N_HEAD4 + h, 1) for h in range(N_HEAD4)]
    beta = [(8 + direction * N_HEAD4 + h, 1) for h in range(N_HEAD4)]
    return la, beta


RET_PIECES = [(i, 1) for i in range(2 * N_HEAD4)]


def _prep_io(g, z, sp, tabs):
    args = [a_ztok(g, z, "att_q", 512, gdtype=BF16), a_ztok(g, z, "att_k", 128, gdtype=BF16),
            a_ztok(g, z, "ret_q", 256, gdtype=BF16), a_ztok(g, z, "ret_k", 256, gdtype=BF16),
            a_ztok(g, z, "narrow", 128, gdtype=BF16),
            a_par(sp["qgain"]), a_par(sp["kgain"]), a_par(sp["alog"]), a_par(sp["dtb"]),
            a_tab(g, tabs[0]), a_tab(g, tabs[1])]
    outs = [o_tok(g, 512, BF16), o_tok(g, 128, BF16), o_tok(g, 256, F32), o_tok(g, 256, F32), o_tok(g, 128, F32)]
    return args, outs


def _attn_io(g, q_att, k_att, z, latent):
    col = ZOFF["att_v"] // 128
    first = lambda ids: ids[1] == 0
    if latent:
        rows, off, nq = g.t, g.nctx_q, (g.t - g.lc) // g.tq
    else:
        rows, off, nq = g.lc, 0, g.nctx_q
    args = [Arg(q_att, (None, g.tq, 512), lambda b, j: (b, j + off, 0), pieces=P64x8,
                gshape=(g.b, nq * g.tq, 512), gimap=lambda b, j: (b, j, 0)),
            Arg(k_att, (None, rows, 128), lambda b, j: (b, 0, 0), first=first, pieces=P64x2, gshape=(g.b, rows, 128)),
            Arg(z, (None, rows, 128), lambda b, j: (b, 0, col), first=first, pieces=P64x2,
                gshape=(g.b, rows, 128), gimap=lambda b, j: (b, 0, 0))]
    outs = [Out((g.b, nq * g.tq, 512), BF16, (None, g.tq, 512), lambda b, j: (b, j, 0), P64x8)]
    return (g.b, nq), args, outs


def _pad_rows(x, t):
    return jnp.pad(x, ((0, 0), (0, t - x.shape[1]), (0, 0)))


def _colgrid_arg(g, z, name, width_total, **kw):
    col = ZOFF[name] // 128
    return Arg(z, (None, g.t, 128), lambda h, b: (b, 0, col + h),
               gshape=(g.b, g.t, width_total), gimap=lambda h, b: (b, 0, h), **kw)


def _dnprep_io(g, z, sp):
    wfirst = lambda ids: ids[1] == 0
    args = [_colgrid_arg(g, z, "dn_q", 512, gdtype=BF16), _colgrid_arg(g, z, "dn_k", 512, gdtype=BF16),
            _colgrid_arg(g, z, "dn_v", 512, gdtype=BF16)]
    for i in range(3):
        args.append(Arg(sp["dn_conv"][i], (3, 128), lambda h, b: (0, h), first=wfirst))
    outs = [Out((g.b, g.t, 512), F32, (None, g.t, 128), lambda h, b: (b, 0, h)) for _ in range(3)]
    return (N_HEAD4, g.b), args, outs


def _shortconv_io(g, z, sp):
    args = [_colgrid_arg(g, z, "sc_b", 512, gdtype=BF16), _colgrid_arg(g, z, "sc_c", 512, gdtype=BF16),
            _colgrid_arg(g, z, "sc_x", 512, gdtype=BF16),
            Arg(sp["sc_conv"], (3, 128), lambda h, b: (0, h), first=lambda ids: ids[1] == 0)]
    outs = [Out((g.b, g.t, 512), BF16, (None, g.t, 128), lambda h, b: (b, 0, h))]
    return (BRANCH_W // 128, g.b), args, outs


def _finish_io(g, o_dn_f, o_dn_b, z, o_rt_f, o_rt_b, sp):
    args = [a_tok(g, o_dn_f, pieces=P128x4), a_tok(g, o_dn_b, pieces=P128x4),
            a_ztok(g, z, "dn_z", 512, gdtype=BF16, pieces=P128x4),
            a_tok(g, o_rt_f, pieces=P128x4), a_tok(g, o_rt_b, pieces=P128x4),
            a_ztok(g, z, "ret_g", 512, gdtype=BF16, pieces=P128x4), a_par(sp["ngain"])]
    outs = [o_tok(g, 512, BF16, pieces=P128x4), o_tok(g, 512, BF16, pieces=P128x4)]
    return args, outs


def _merge_io(g, ps, z):
    args = [a_tok(g, p, gdtype=BF16) for p in ps] + [a_ztok(g, z, "gates", 4096, gdtype=BF16, pieces=P1024x4)]
    return args, [o_tok(g, D_MODEL, BF16)]


def _dn_dirs(qn, kn, vn, gates):
    dirs = []
    for d in range(2):
        la, beta = _gate_pieces(d)
        dirs.append([Tok(qn, P128x4), Tok(kn, P128x4), Tok(vn, P128x4), Tok(gates, la), Tok(gates, beta)])
    return dirs


def _ret_dirs(q_ret, k_ret, z):
    col = ZOFF["ret_v"] // 512
    return [[Tok(q_ret, P64x4), Tok(k_ret, P64x4), Tok(z, P128x4, 512, col)] for _ in range(2)]


DN_STATE = (N_HEAD4, HEAD128, HEAD128)
RET_STATE = (N_HEAD4, HEAD64, HEAD128)
SCAN_OUT = [(512, F32, P128x4)]


def layer_fwd(g, x, h, w, sp, mod_a, mod_b, tabs, g_next, comm=None, rest=None):
    tp = {"x": x, "h": h}
    m, lc = g.m, g.lc
    z = matmul("win", h.reshape(m, D_MODEL), w["win"], epilogue=lambda p: (p,), out_dtypes=(BF16,))[0].reshape(g.b, g.t, NZ)
    tp["z"] = z
    args, outs = _prep_io(g, z, sp, tabs)
    q_att, k_att, q_ret, k_ret, gates = vfwd("prep", f_prep, g.grid, args, outs)
    tp.update(q_att=q_att, k_att=k_att, q_ret=q_ret, k_ret=k_ret, gates=gates)
    y_att = []
    for latent in (False, True):
        agrid, args, outs = _attn_io(g, q_att, k_att, z, latent)
        ride = ("gather", rest) if latent and rest is not None else None
        y, *gathered = vfwd("attn", make_f_attn(g.tq), agrid, args, outs, ride)
        y_att.append(y)
        if gathered:
            w = {**w, **weights_of_slots(gathered, 1)}
    tp["w"] = w
    y_att = jnp.concatenate(y_att, axis=1)
    dgrid, args, outs = _dnprep_io(g, z, sp)
    qn, kn, vn = vfwd("dnprep", make_f_dnprep(g.t, lc), dgrid, args, outs)
    tp.update(qn=qn, kn=kn, vn=vn)
    inv_shape = (2 * _scan_nb(g.b), ROWS, ROWS)
    o_f, s_f, o_b, s_b, tp["dn_inv"], *carried = scan2_fwd("dnscan", dn_chunks, _dn_dirs(qn, kn, vn, gates), [], DN_STATE,
                                                            SCAN_OUT, lc, comm, [inv_shape])
    tp["carried"] = carried if carried else None
    o_dn, tp["dn_s"] = [o_f, o_b], [s_f, s_b]
    o_f, s_f, o_b, s_b = scan2_fwd("retscan", ret_chunks, _ret_dirs(q_ret, k_ret, z), [(sp["ret"], RET_PIECES)], RET_STATE, SCAN_OUT, lc)
    o_rt, tp["rt_s"] = [o_f, o_b], [s_f, s_b]
    tp.update(o_dn=o_dn, o_rt=o_rt)
    args, outs = _finish_io(g, o_dn[0], o_dn[1], z, o_rt[0], o_rt[1], sp)
    y_dn, y_rt = vfwd("finish", f_finish, g.grid, args, outs)
    sgrid, args, outs = _shortconv_io(g, z, sp)
    (y_sc,) = vfwd("shortconv", make_f_shortconv(g.t, lc), sgrid, args, outs)
    ys = [y_att, y_dn, y_rt, y_sc]
    tp["ys"] = ys
    ps = [matmul("wbranch", y.reshape(m, BRANCH_W), w["wbr"][i], epilogue=lambda p: (p,), out_dtypes=(BF16,))[0]
          .reshape(g.b, g.t, D_MODEL) for i, y in enumerate(ys)]
    tp["ps"] = ps
    args, outs = _merge_io(g, ps, z)
    (u,) = vfwd("merge", f_merge, g.grid, args, outs)
    tp["u"] = u
    y = matmul("wout", u.reshape(m, D_MODEL), w["wout"]).reshape(g.b, g.t, D_MODEL)
    tp["y"] = y
    args = [a_tok(g, x), a_tok(g, y), a_par(sp["g1"]), a_par(sp["g2"]), a_mod(g, mod_a)]
    x1, h2 = vfwd("resnorm", f_resnorm, g.grid, args, [o_tok(g, D_MODEL, F32), o_tok(g, D_MODEL, BF16)])
    tp.update(x1=x1, h2=h2)
    tp["a"], tp["r"] = matmul("wmlp1", h2.reshape(m, D_MODEL), w["w1"], epilogue=lambda p: (p, jnp.square(jnp.maximum(p, 0.0))),
                              out_dtypes=(F32, BF16))
    mo = matmul("wmlp2", tp["r"], w["w2"]).reshape(g.b, g.t, D_MODEL)
    tp["mo"] = mo
    if g_next is None:
        args = [a_tok(g, x1), a_tok(g, mo), a_par(sp["g3"]), a_mod(g, mod_b)]
        (x2,) = vfwd("resid", f_resid, g.grid, args, [o_tok(g, D_MODEL, F32)])
        return x2, None, tp
    args = [a_tok(g, x1), a_tok(g, mo), a_par(sp["g3"]), a_par(g_next), a_mod(g, mod_b)]
    x2, h_next = vfwd("resnorm", f_resnorm, g.grid, args, [o_tok(g, D_MODEL, F32), o_tok(g, D_MODEL, BF16)])
    return x2, h_next, tp


def layer_bwd(g, tp, sp, mod_a, mod_b, tabs, g_next, dx2, dh_next, send=None):
    m, lc = g.m, g.lc
    w = tp["w"]
    gw, gs = {}, {}
    x, z = tp["x"], tp["z"]
    if g_next is None:
        args = [a_tok(g, tp["x1"]), a_tok(g, tp["mo"]), a_par(sp["g3"]), a_mod(g, mod_b)]
        dx1, dmo, gs["g3"], dmod_b = vbwd("resid_b", f_resid, g.grid, args, [o_tok(g, D_MODEL, F32)], [[dx2]])
    else:
        args = [a_tok(g, tp["x1"]), a_tok(g, tp["mo"]), a_par(sp["g3"]), a_par(g_next), a_mod(g, mod_b)]
        dx1, dmo, gs["g3"], gs["g0_next"], dmod_b = vbwd(
            "resnorm_b", f_resnorm, g.grid, args, [o_tok(g, D_MODEL, F32), o_tok(g, D_MODEL, BF16)], [[dx2], [dh_next]])
    dmo2 = dmo.reshape(m, D_MODEL)
    (da2,) = matmul("wmlp2_dx", dmo2, w["w2"], trans_b=True, extras=[tp["a"]],
                    epilogue=lambda p, a: (p * (2.0 * jnp.maximum(a, 0.0)),), out_dtypes=(BF16,))
    gw["w2"] = matmul("wmlp2_dw", tp["r"], dmo2, trans_a=True)
    dh2 = matmul("wmlp1_dx", da2, w["w1"], trans_b=True).reshape(g.b, g.t, D_MODEL)
    gw["w1"] = matmul("wmlp1_dw", tp["h2"].reshape(m, D_MODEL), da2, trans_a=True)
    args = [a_tok(g, x), a_tok(g, tp["y"]), a_par(sp["g1"]), a_par(sp["g2"]), a_mod(g, mod_a)]
    dx, dy, gs["g1"], gs["g2"], dmod_a = vbwd(
        "resnorm_b", f_resnorm, g.grid, args, [o_tok(g, D_MODEL, F32), o_tok(g, D_MODEL, BF16)], [[dx1], [dh2]])
    dy2 = dy.reshape(m, D_MODEL)
    du = matmul("wout_dx", dy2, w["wout"], trans_b=True).reshape(g.b, g.t, D_MODEL)
    gw["wout"] = matmul("wout_dw", tp["u"].reshape(m, D_MODEL), dy2, trans_a=True)
    args, outs = _merge_io(g, tp["ps"], z)
    *dps, dz_gates = vbwd("merge_b", f_merge, g.grid, args, outs, [[du]])
    dys, gwbr = [], []
    for i in range(N_BRANCH):
        dp2 = dps[i].reshape(m, D_MODEL)
        dys.append(matmul("wbranch_dx", dp2, w["wbr"][i], trans_b=True).reshape(g.b, g.t, BRANCH_W))
        gwbr.append(matmul("wbranch_dw", tp["ys"][i].reshape(m, BRANCH_W), dp2, trans_a=True))
    gw["wbr"] = jnp.stack(gwbr)
    dy_att, dy_dn, dy_rt, dy_sc = dys
    o_dn, o_rt = tp["o_dn"], tp["o_rt"]
    args, outs = _finish_io(g, o_dn[0], o_dn[1], z, o_rt[0], o_rt[1], sp)
    do_dn_f, do_dn_b, dz_dnz, do_rt_f, do_rt_b, dz_retg, gs["ngain"] = vbwd(
        "finish_b", f_finish, g.grid, args, outs, [[dy_dn], [dy_rt]])
    comm = ("scatter", list(send) + grad_slots(gw, WKEYS[1:])) if send is not None else None
    res = scan2_bwd("dnscan_b", dn_chunks, _dn_dirs(tp["qn"], tp["kn"], tp["vn"], tp["gates"]), [], DN_STATE, SCAN_OUT, lc,
                   tp["dn_s"], [[do_dn_f], [do_dn_b]], comm, [tp["dn_inv"]])
    dqn, dkn, dvn, dgates = [res[0], res[5]], [res[1], res[6]], [res[2], res[7]], [res[3], res[4], res[8], res[9]]
    received = list(res[10:]) if comm is not None else None
    swap = ("swap", received) if comm is not None else None
    res = scan2_bwd("retscan_b", ret_chunks, _ret_dirs(tp["q_ret"], tp["k_ret"], z), [(sp["ret"], RET_PIECES)], RET_STATE, SCAN_OUT, lc,
                   tp["rt_s"], [[do_rt_f], [do_rt_b]], swap)
    dq_ret, dk_ret, gs["ret"] = [res[0], res[3]], [res[1], res[4]], res[6]
    carried = (received, list(res[7:])) if comm is not None else None
    dz_retv = addn("sum_retv", [res[2], res[5]], BF16)
    sgrid, args, outs = _shortconv_io(g, z, sp)
    dz_scb, dz_scc, dz_scx, gs["sc_conv"] = vbwd("shortconv_b", make_f_shortconv(g.t, lc), sgrid, args, outs, [[dy_sc]])
    dq_att, dk_att, dv_att = [], [], []
    for latent, dy in ((False, dy_att[:, :lc]), (True, dy_att[:, lc:])):
        agrid, args, outs = _attn_io(g, tp["q_att"], tp["k_att"], z, latent)
        dq, dk, dv = vbwd("attn_b", make_f_attn(g.tq), agrid, args, outs, [[dy]])
        dq_att.append(dq); dk_att.append(_pad_rows(dk, g.t)); dv_att.append(_pad_rows(dv, g.t))
    dq_att = jnp.concatenate(dq_att, axis=1)
    dz_attv = addn("sum_attv", dv_att, BF16)
    dgrid, args, outs = _dnprep_io(g, z, sp)
    dz_dnq, dz_dnk, dz_dnv, gc_q, gc_k, gc_v = vbwd("dnprep_b", make_f_dnprep(g.t, lc), dgrid, args, outs, [dqn, dkn, dvn])
    gs["dn_conv"] = [gc_q, gc_k, gc_v]
    args, outs = _prep_io(g, z, sp, tabs)
    dz_attq, dz_attk, dz_retq, dz_retk, dz_nar, gs["qgain"], gs["kgain"], gs["alog"], gs["dtb"] = vbwd(
        "prep_b", f_prep, g.grid, args, outs, [[dq_att], dk_att, dq_ret, dk_ret, dgates])
    pad = jnp.zeros((g.b, g.t, NZ - (ZOFF["narrow"] + 128)), BF16)
    dz = jnp.concatenate([dz_gates, dz_attq, dz_dnq, dz_dnk, dz_dnv, dz_dnz, dz_retv, dz_retg, dz_scb, dz_scc, dz_scx,
                          dz_retq, dz_retk, dz_attk, dz_attv, dz_nar, pad], axis=-1)
    dz2 = dz.reshape(m, NZ)
    dh = matmul("win_dx", dz2, w["win"], trans_b=True).reshape(g.b, g.t, D_MODEL)
    gw["win"] = matmul("win_dw", tp["h"].reshape(m, D_MODEL), dz2, trans_a=True)
    return dx, dh, gw, gs, dmod_a, dmod_b, carried


def loss_call(g, xa, tgt):
    n = g.nctx
    inv_d = 1.0 / D_MODEL

    def body(x_ref, t_ref, loss_ref, dx_ref):
        j = pl.program_id(1)

        @pl.when((pl.program_id(0) == 0) & (j == 0))
        def _():
            loss_ref[...] = jnp.zeros(loss_ref.shape, F32)

        @pl.when(j < n)
        def _():
            dx_ref[...] = jnp.zeros(dx_ref.shape, F32)

        @pl.when(j >= n)
        def _():
            e = x_ref[...] - t_ref[...]
            dx_ref[...] = e * inv_d
            s = jnp.sum(jnp.sum(e * e, axis=1, keepdims=True), axis=0, keepdims=True)
            loss_ref[...] += jnp.broadcast_to(0.5 * inv_d * s, loss_ref.shape)

    tok = pl.BlockSpec((None, g.tm, D_MODEL), lambda b, j: (b, j, 0))
    return pl.pallas_call(
        body, name="loss", grid=g.grid,
        in_specs=[tok, pl.BlockSpec((None, g.tm, D_MODEL), lambda b, j: (b, jnp.maximum(j - n, 0), 0))],
        out_specs=[pl.BlockSpec((1, LANES), lambda b, j: (0, 0)), tok],
        out_shape=[jax.ShapeDtypeStruct((1, LANES), F32), jax.ShapeDtypeStruct((g.b, g.t, D_MODEL), F32)],
        compiler_params=_params(2),
    )(xa, tgt)


def rope_tables(g):
    seq = g.t - g.lc
    rows = seq // GRID_W
    r, col = jnp.meshgrid(jnp.arange(rows), jnp.arange(GRID_W), indexing="ij")
    quarter = HEAD64 // 4
    inv_freq = ROPE_THETA ** (-jnp.arange(quarter, dtype=F32) / quarter)
    ang = jnp.concatenate([r.reshape(-1, 1).astype(F32) * inv_freq, col.reshape(-1, 1).astype(F32) * inv_freq], axis=-1)
    cos, sin = jnp.cos(ang), jnp.sin(ang)
    cos = jnp.concatenate([jnp.ones((g.lc, HEAD64 // 2), F32), cos], axis=0)
    sin = jnp.concatenate([jnp.zeros((g.lc, HEAD64 // 2), F32), sin], axis=0)
    reps = 512 // (HEAD64 // 2)
    return jnp.tile(cos, (1, reps)), jnp.tile(sin, (1, reps))


def _row128(v):
    v = v.reshape(1, -1).astype(F32)
    return jnp.pad(v, ((0, 0), (0, LANES - v.shape[1])))


def layer_small(small, l):
    gn = small["g_norm"][l]
    return {
        "g0": gn[0:1], "g1": gn[1:2], "g2": gn[2:3], "g3": gn[3:4],
        "qgain": small["att_q_gain"][l][None], "kgain": small["att_k_gain"][l][None],
        "dn_conv": [small["dn_conv"][l][:, i * 512:(i + 1) * 512] for i in range(3)], "alog": _row128(small["dn_a_log"][l]), "dtb": _row128(small["dn_dt_bias"][l]),
        "ngain": small["dn_norm_gain"][l][None], "ret": _row128(small["ret_decay"][l]), "sc_conv": small["sc_conv"][l],
    }


def permute_win(w):
    parts = [w[..., off:off + width] for _, off, width in _SEGS]
    parts.append(jnp.zeros(w.shape[:-1] + (NZ - N_IN,), w.dtype))
    return jnp.concatenate(parts, axis=-1)


def unpermute_win(gw):
    order = sorted(_SEGS, key=lambda s: s[1])
    return jnp.concatenate([gw[..., ZOFF[name]:ZOFF[name] + width] for name, _, width in order], axis=-1)


BIG = (("w_in", -1), ("w_branch", -1), ("w_out", -2), ("w_mlp_in", -1), ("w_mlp_out", -2))
WKEYS = ("win", "wbr", "wout", "w1", "w2")


def weights_of_slots(slots, first=0):
    out = {}
    for s, key, (_, axis) in zip(slots, WKEYS[first:], BIG[first:]):
        full = jnp.concatenate([s[p] for p in range(N_XY)], axis=axis).astype(BF16)
        out[key] = permute_win(full) if key == "win" else full
    return out


def grad_slots(gw, keys):
    out = []
    for key in keys:
        full = unpermute_win(gw[key]) if key == "win" else gw[key]
        out.append(jnp.stack(jnp.split(full, N_XY, axis=BIG[WKEYS.index(key)][1])).astype(BF16))
    return out


def model_step(g, xa, tgt, mod, shards, small, local=False):
    depth = len(shards)
    tabs = rope_tables(g)
    sps = [layer_small(small, l) for l in range(depth)]
    mods_a = [mod[l][:, :, 2:5] for l in range(depth)]
    mods_b = [jnp.concatenate([mod[l][:, :, 5:6], mod[l + 1][:, :, 0:2]], axis=2) if l + 1 < depth else mod[l][:, :, 5:6]
              for l in range(depth)]
    mod0 = mod[0][:, :, 0:2]
    args0 = [a_tok(g, xa), a_par(sps[0]["g0"]), a_mod(g, mod0)]
    (h,) = vfwd("modnorm", f_modnorm, g.grid, args0, [o_tok(g, D_MODEL, BF16)])
    x, tapes = xa, []
    if local:
        w, rest = weights_of_slots(shards[0]), None
    else:
        w, rest = weights_of_slots(exchange("gather_win", "gather", shards[0][:1])), shards[0][1:]
    for l in range(depth):
        g_next = sps[l + 1]["g0"] if l + 1 < depth else None
        comm = ("gather", shards[l + 1]) if l + 1 < depth and not local else None
        x, h, tp = layer_fwd(g, x, h, w, sps[l], mods_a[l], mods_b[l], tabs, g_next, comm, rest)
        if l + 1 < depth:
            w, rest = weights_of_slots(shards[l + 1] if local else tp["carried"]), None
        tapes.append(tp)
    loss_row, dx = loss_call(g, x, tgt)
    dh, gss, dmods = None, [None] * depth, [None] * depth
    mine, theirs = [[None] * len(BIG) for _ in range(depth)], [[None] * len(BIG) for _ in range(depth)]
    win_slots = None
    for l in reversed(range(depth)):
        g_next = sps[l + 1]["g0"] if l + 1 < depth else None
        send = None if local else ([win_slots] if win_slots is not None else [])
        dx, dh, gw, gss[l], dma, dmb, carried = layer_bwd(g, tapes[l], sps[l], mods_a[l], mods_b[l], tabs, g_next, dx, dh, send)
        (win_new,) = grad_slots(gw, WKEYS[:1])
        if local:
            mine[l] = [win_new] + grad_slots(gw, WKEYS[1:])
        else:
            received, swapped = carried
            if win_slots is not None:
                mine[l + 1][0], theirs[l + 1][0] = received[0], swapped[0]
            mine[l][1:], theirs[l][1:] = received[-4:], swapped[-4:]
        win_slots = win_new
        dmods[l] = (dma, dmb)
    if not local:
        mine[0][:1] = exchange("scatter_win", "scatter", [win_slots])
        theirs[0][:1] = exchange("swap_win", "swap", mine[0][:1])
    dxh, g0_first, dmod0 = vbwd("modnorm_b", f_modnorm, g.grid, args0, [o_tok(g, D_MODEL, BF16)], [[dh]])
    dxa = addn("sum_dx", [dx, dxh], F32)
    dmod = []
    for l in range(depth):
        first2 = dmod0 if l == 0 else dmods[l - 1][1][:, :, 1:3]
        dmod.append(jnp.concatenate([first2, dmods[l][0], dmods[l][1][:, :, 0:1]], axis=2))
    dmod = jnp.stack(dmod)
    def rows(key, n):
        return jnp.stack([gs[key][0, :n] for gs in gss])
    g_norm = jnp.stack([jnp.concatenate([g0_first if l == 0 else gss[l - 1]["g0_next"], gss[l]["g1"], gss[l]["g2"], gss[l]["g3"]], axis=0)
                        for l in range(depth)])
    gsmall = {
        "g_norm": g_norm,
        "att_q_gain": jnp.stack([gs["qgain"][0] for gs in gss]), "att_k_gain": jnp.stack([gs["kgain"][0] for gs in gss]),
        "dn_conv": jnp.stack([jnp.concatenate(gs["dn_conv"], axis=1) for gs in gss]),
        "dn_a_log": rows("alog", 8).reshape(depth, 2, N_HEAD4), "dn_dt_bias": rows("dtb", 8).reshape(depth, 2, N_HEAD4),
        "dn_norm_gain": jnp.stack([gs["ngain"][0] for gs in gss]),
        "ret_decay": rows("ret", 8).reshape(depth, 2, N_HEAD4),
        "sc_conv": jnp.stack([gs["sc_conv"] for gs in gss]),
    }
    return loss_row[0, 0], dxa, mine, theirs, gsmall, dmod


N_DEV = 8
N_XY = 4
HBM_SPEC = pl.BlockSpec(memory_space=pltpu.HBM)
VMEM_SPEC = pl.BlockSpec(memory_space=pltpu.VMEM)


def _coords():
    return lax.axis_index("x"), lax.axis_index("y"), lax.axis_index("c")


def _flip(coords, k):
    x, y, c = coords
    return (1 - x if k & 4 else x, 1 - y if k & 2 else y, 1 - c if k & 1 else c)


def allgather8(name, v):
    def body(v_ref, out_ref, send_sems, recv_sems, local_sem):
        me3 = _coords()
        me = 4 * me3[0] + 2 * me3[1] + me3[2]
        mine = pltpu.make_async_copy(v_ref, out_ref.at[me], local_sem)
        mine.start()
        sends = []
        for k in range(1, N_DEV):
            cp = pltpu.make_async_remote_copy(src_ref=v_ref, dst_ref=out_ref.at[me], send_sem=send_sems.at[k - 1],
                                              recv_sem=recv_sems.at[k - 1], device_id=_flip(me3, k), device_id_type=MESH)
            cp.start()
            sends.append(cp)
        for k in range(1, N_DEV):
            pltpu.make_async_remote_copy(src_ref=v_ref, dst_ref=out_ref.at[jnp.bitwise_xor(me, k)], send_sem=send_sems.at[k - 1],
                                         recv_sem=recv_sems.at[k - 1], device_id=_flip(me3, k), device_id_type=MESH).wait_recv()
        for cp in sends:
            cp.wait_send()
        mine.wait()

    return pl.pallas_call(
        body, name=name, out_shape=jax.ShapeDtypeStruct((N_DEV,) + v.shape, v.dtype),
        in_specs=[VMEM_SPEC], out_specs=VMEM_SPEC,
        scratch_shapes=[pltpu.SemaphoreType.DMA((N_DEV - 1,)), pltpu.SemaphoreType.DMA((N_DEV - 1,)), pltpu.SemaphoreType.DMA],
        compiler_params=pltpu.CompilerParams(vmem_limit_bytes=VMEM_LIMIT),
    )(v)


def exchange(name, kind, arrays):
    n = len(arrays)

    def body(*refs):
        xrefs = (refs[:n], refs[n:2 * n]) + tuple(refs[2 * n:])
        _exchange_start(xrefs, kind)
        _exchange_wait(xrefs, kind)

    return pl.pallas_call(
        body, name=name, out_shape=[_exchange_shape(kind, a) for a in arrays], in_specs=[HBM_SPEC] * n,
        out_specs=[HBM_SPEC] * n, scratch_shapes=_exchange_sems(n),
    )(*arrays)


def _exchange_shape(kind, a):
    return jax.ShapeDtypeStruct((N_XY,) + a.shape if kind == "gather" else a.shape, a.dtype)


def _exchange_sems(n):
    return [pltpu.SemaphoreType.DMA((n, N_XY - 1)), pltpu.SemaphoreType.DMA((n, N_XY - 1)), pltpu.SemaphoreType.DMA((n,))]


def _exchange_copies(xrefs, kind):
    srcs, dsts, send_sems, recv_sems, local_sems = xrefs
    me3 = _coords()
    me = 2 * me3[0] + me3[1]
    locals_, sends, recvs = [], [], []
    for i, (v_ref, out_ref) in enumerate(zip(srcs, dsts)):
        if kind == "swap":
            cp = pltpu.make_async_remote_copy(src_ref=v_ref, dst_ref=out_ref, send_sem=send_sems.at[i, 0],
                                              recv_sem=recv_sems.at[i, 0], device_id=_flip(me3, 1), device_id_type=MESH)
            sends.append(cp)
            recvs.append(cp)
            continue
        src = (lambda s, r=v_ref: r) if kind == "gather" else (lambda s, r=v_ref: r.at[s])

        def remote(k, src_slot, dst_slot):
            return pltpu.make_async_remote_copy(src_ref=src(src_slot), dst_ref=out_ref.at[dst_slot], send_sem=send_sems.at[i, k - 1],
                                                recv_sem=recv_sems.at[i, k - 1], device_id=_flip(me3, 2 * k), device_id_type=MESH)

        locals_.append(pltpu.make_async_copy(src(me), out_ref.at[me], local_sems.at[i]))
        sends += [remote(k, jnp.bitwise_xor(me, k), me) for k in range(1, N_XY)]
        recvs += [remote(k, me, jnp.bitwise_xor(me, k)) for k in range(1, N_XY)]
    return locals_, sends, recvs


def _exchange_start(xrefs, kind):
    locals_, sends, _ = _exchange_copies(xrefs, kind)
    for cp in locals_ + sends:
        cp.start()


def _exchange_wait(xrefs, kind):
    locals_, sends, recvs = _exchange_copies(xrefs, kind)
    for cp in recvs:
        cp.wait_recv()
    for cp in sends:
        cp.wait_send()
    for cp in locals_:
        cp.wait()


BLOCK_BYTES = 1 << 20


def _rows_block(rows, cols, limit=BLOCK_BYTES):
    for tr in (1024, 512, 256, 128, 64, 32, 16, 8):
        if rows % tr == 0 and tr * cols * 4 <= limit:
            return tr
    return rows


def sum_slots(name, r):
    s, rows, cols = r.shape
    tr = _rows_block(rows, cols)

    def body(r_ref, o_ref):
        tot = r_ref[0].astype(F32)
        for i in range(1, s):
            tot = tot + r_ref[i].astype(F32)
        o_ref[...] = tot

    return pl.pallas_call(
        body, name=name, grid=(rows // tr,), in_specs=[pl.BlockSpec((s, tr, cols), lambda i: (0, i, 0))],
        out_specs=pl.BlockSpec((tr, cols), lambda i: (i, 0)), out_shape=jax.ShapeDtypeStruct((rows, cols), F32),
        compiler_params=_params(1),
    )(r)


def adamw(name, w, m, v, parts):
    rows, cols = w.shape
    tr = _rows_block(rows, cols)
    n = len(parts)
    c1 = 1.0 - ADAM_B1 ** ADAM_STEP
    c2 = 1.0 - ADAM_B2 ** ADAM_STEP

    def body(*refs):
        w_ref, m_ref, v_ref = refs[:3]
        g_ref, d_ref, nm_ref, nv_ref = refs[3 + n:]
        g = refs[3][...]
        for r in refs[4:3 + n]:
            g = g + r[...]
        nm = ADAM_B1 * m_ref[...] + (1.0 - ADAM_B1) * g
        nv = ADAM_B2 * v_ref[...] + (1.0 - ADAM_B2) * jnp.square(g)
        d_ref[...] = -ADAM_LR * ((nm / c1) / (jnp.sqrt(nv / c2) + ADAM_EPS) + ADAM_WD * w_ref[...])
        g_ref[...], nm_ref[...], nv_ref[...] = g, nm, nv

    spec = pl.BlockSpec((tr, cols), lambda i: (i, 0))
    sds = jax.ShapeDtypeStruct((rows, cols), F32)
    return pl.pallas_call(
        body, name=name, grid=(rows // tr,), in_specs=[spec] * (3 + n), out_specs=[spec] * 4, out_shape=[sds] * 4,
        compiler_params=_params(1),
    )(w, m, v, *parts)


def adamw_slots(name, w, m, v, mine, theirs):
    depth = len(mine)
    rows, cols = mine[0].shape[1:]
    tr = _rows_block(rows, cols, BLOCK_BYTES // 2)
    nblk = rows // tr
    c1 = 1.0 - ADAM_B1 ** ADAM_STEP
    c2 = 1.0 - ADAM_B2 ** ADAM_STEP

    def body(*refs):
        w_ref, m_ref, v_ref = refs[:3]
        slot_refs = refs[3:3 + 2 * depth]
        g_ref, d_ref, nm_ref, nv_ref = refs[3 + 2 * depth:]
        for j in range(depth):
            @pl.when(pl.program_id(0) == j)
            def _():
                def total(ref):
                    tot = ref[0].astype(F32)
                    for s in range(1, N_XY):
                        tot = tot + ref[s].astype(F32)
                    return tot
                g = total(slot_refs[j]) + total(slot_refs[depth + j])
                nm = ADAM_B1 * m_ref[...] + (1.0 - ADAM_B1) * g
                nv = ADAM_B2 * v_ref[...] + (1.0 - ADAM_B2) * jnp.square(g)
                d_ref[...] = -ADAM_LR * ((nm / c1) / (jnp.sqrt(nv / c2) + ADAM_EPS) + ADAM_WD * w_ref[...])
                g_ref[...], nm_ref[...], nv_ref[...] = g, nm, nv

    spec = pl.BlockSpec((tr, cols), lambda l, i: (l * nblk + i, 0))
    slot_specs = [pl.BlockSpec((N_XY, tr, cols), lambda l, i, j=j: (0, jnp.where(l == j, i, 0), 0)) for j in range(depth)] * 2
    sds = jax.ShapeDtypeStruct((depth * rows, cols), F32)
    return pl.pallas_call(
        body, name=name, grid=(depth, nblk), in_specs=[spec] * 3 + slot_specs, out_specs=[spec] * 4, out_shape=[sds] * 4,
        compiler_params=_params(2),
    )(w, m, v, *mine, *theirs)


MOD_COLS = 512


def mod_fwd(call, w_mod, b_sh):
    depth, _, cols = w_mod.shape
    nr = call.shape[0]

    def body(c_ref, w_ref, b_ref, o_ref):
        o_ref[...] = _dg(_silu(c_ref[...]), w_ref[...], 1, 0) + b_ref[...]

    return pl.pallas_call(
        body, name="mod_fwd", grid=(depth, cols // MOD_COLS),
        in_specs=[pl.BlockSpec((nr, D_MODEL), lambda l, j: (0, 0)), pl.BlockSpec((None, D_MODEL, MOD_COLS), lambda l, j: (l, 0, j)),
                  pl.BlockSpec((None, 1, MOD_COLS), lambda l, j: (l, 0, j))],
        out_specs=pl.BlockSpec((None, nr, MOD_COLS), lambda l, j: (l, 0, j)),
        out_shape=jax.ShapeDtypeStruct((depth, nr, cols), F32), compiler_params=_params(2),
    )(call, w_mod, b_sh)


def mod_bwd(call, c_ctx, d_lat, d_ctx, w_mod, ctx_row):
    depth, _, cols = w_mod.shape
    nr, ns = call.shape[0], d_ctx.shape[1]

    def body(c_ref, cc_ref, dl_ref, dc_ref, w_ref, gw_ref, gc_ref):
        crow = jnp.sum(dc_ref[...], axis=0, keepdims=True)
        row = lax.broadcasted_iota(jnp.int32, (nr, 1), 0)
        dm = jnp.where(row == ctx_row, crow, dl_ref[...])
        gw_ref[...] = _dg(_silu(c_ref[...]), dm, 0, 0)
        ds = jnp.sum(_dg(jnp.broadcast_to(crow, (8, MOD_COLS)), w_ref[...], 1, 1), axis=0, keepdims=True) * 0.125
        _, vjp = jax.vjp(_silu, cc_ref[...])
        (part,) = vjp(ds)
        first = (pl.program_id(0) == 0) & (pl.program_id(1) == 0)

        @pl.when(first)
        def _():
            gc_ref[...] = part

        @pl.when(jnp.logical_not(first))
        def _():
            gc_ref[...] += part

    return pl.pallas_call(
        body, name="mod_bwd", grid=(depth, cols // MOD_COLS),
        in_specs=[pl.BlockSpec((nr, D_MODEL), lambda l, j: (0, 0)), pl.BlockSpec((1, D_MODEL), lambda l, j: (0, 0)),
                  pl.BlockSpec((None, nr, MOD_COLS), lambda l, j: (l, 0, j)), pl.BlockSpec((None, ns, MOD_COLS), lambda l, j: (l, 0, j)),
                  pl.BlockSpec((None, D_MODEL, MOD_COLS), lambda l, j: (l, 0, j))],
        out_specs=[pl.BlockSpec((None, D_MODEL, MOD_COLS), lambda l, j: (l, 0, j)), pl.BlockSpec((1, D_MODEL), lambda l, j: (0, 0))],
        out_shape=[jax.ShapeDtypeStruct((depth, D_MODEL, cols), F32), jax.ShapeDtypeStruct((1, D_MODEL), F32)],
        compiler_params=_params(2),
    )(call, c_ctx, d_lat, d_ctx, w_mod)


def bmod_grad(dm_all):
    ndev, depth, ns, cols = dm_all.shape

    def body(d_ref, o_ref):
        tot = d_ref[0]
        for i in range(1, ndev):
            tot = tot + d_ref[i]
        o_ref[...] = jnp.sum(tot, axis=0, keepdims=True)

    return pl.pallas_call(
        body, name="bmod_grad", grid=(depth,), in_specs=[pl.BlockSpec((ndev, None, ns, cols), lambda l: (0, l, 0, 0))],
        out_specs=pl.BlockSpec((None, 1, cols), lambda l: (l, 0, 0)), out_shape=jax.ShapeDtypeStruct((depth, 1, cols), F32),
        compiler_params=_params(1),
    )(dm_all)


def small_reduce(gathered, rows_all):
    ndev, rows, lanes = gathered.shape

    def body(g_ref, o_ref):
        tot = g_ref[0, 0:rows_all]
        for i in range(1, ndev):
            tot = tot + g_ref[i, 0:rows_all]
        o_ref[0:rows_all] = tot
        part = g_ref[0, rows_all:rows]
        for i in range(2, ndev, 2):
            part = part + g_ref[i, rows_all:rows]
        o_ref[rows_all:rows] = part

    return pl.pallas_call(body, name="small_reduce", out_shape=jax.ShapeDtypeStruct((rows, lanes), F32),
                          in_specs=[VMEM_SPEC], out_specs=VMEM_SPEC)(gathered)


def pack_rows(arrays, row_multiple=8):
    flat = jnp.concatenate([a.reshape(-1).astype(F32) for a in arrays])
    per = LANES * row_multiple
    padded = -(-flat.shape[0] // per) * per
    return jnp.pad(flat, (0, padded - flat.shape[0])).reshape(-1, LANES)


def unpack_rows(buf, shapes):
    flat, out, off = buf.reshape(-1), [], 0
    for s in shapes:
        n = int(np.prod(s))
        out.append(flat[off:off + n].reshape(s))
        off += n
    return out


SMALL_SHARDED = ("g_norm", "dn_conv", "sc_conv")
SMALL_ORDER = ("g_norm", "att_q_gain", "att_k_gain", "dn_conv", "dn_a_log", "dn_dt_bias", "dn_norm_gain", "ret_decay", "sc_conv")


def kernel(x, c, ctx, c_ctx, w_mod, b_mod, g_norm, w_in, att_q_gain, att_k_gain, dn_conv, dn_a_log, dn_dt_bias, dn_norm_gain, ret_decay, sc_conv, w_branch, w_out, w_mlp_in, w_mlp_out, loss_target, m_c_ctx, m_w_mod, m_b_mod, m_g_norm, m_w_in, m_att_q_gain, m_att_k_gain, m_dn_conv, m_dn_a_log, m_dn_dt_bias, m_dn_norm_gain, m_ret_decay, m_sc_conv, m_w_branch, m_w_out, m_w_mlp_in, m_w_mlp_out, v_c_ctx, v_w_mod, v_b_mod, v_g_norm, v_w_in, v_att_q_gain, v_att_k_gain, v_dn_conv, v_dn_a_log, v_dn_dt_bias, v_dn_norm_gain, v_ret_decay, v_sc_conv, v_w_branch, v_w_out, v_w_mlp_in, v_w_mlp_out):
    wts = dict(c_ctx=c_ctx, w_mod=w_mod, b_mod=b_mod, g_norm=g_norm, w_in=w_in, att_q_gain=att_q_gain, att_k_gain=att_k_gain,
               dn_conv=dn_conv, dn_a_log=dn_a_log, dn_dt_bias=dn_dt_bias, dn_norm_gain=dn_norm_gain, ret_decay=ret_decay,
               sc_conv=sc_conv, w_branch=w_branch, w_out=w_out, w_mlp_in=w_mlp_in, w_mlp_out=w_mlp_out)
    mom = dict(c_ctx=m_c_ctx, w_mod=m_w_mod, b_mod=m_b_mod, g_norm=m_g_norm, w_in=m_w_in, att_q_gain=m_att_q_gain,
               att_k_gain=m_att_k_gain, dn_conv=m_dn_conv, dn_a_log=m_dn_a_log, dn_dt_bias=m_dn_dt_bias,
               dn_norm_gain=m_dn_norm_gain, ret_decay=m_ret_decay, sc_conv=m_sc_conv, w_branch=m_w_branch, w_out=m_w_out,
               w_mlp_in=m_w_mlp_in, w_mlp_out=m_w_mlp_out)
    var = dict(c_ctx=v_c_ctx, w_mod=v_w_mod, b_mod=v_b_mod, g_norm=v_g_norm, w_in=v_w_in, att_q_gain=v_att_q_gain,
               att_k_gain=v_att_k_gain, dn_conv=v_dn_conv, dn_a_log=v_dn_a_log, dn_dt_bias=v_dn_dt_bias,
               dn_norm_gain=v_dn_norm_gain, ret_decay=v_ret_decay, sc_conv=v_sc_conv, w_branch=v_w_branch, w_out=v_w_out,
               w_mlp_in=v_w_mlp_in, w_mlp_out=v_w_mlp_out)
    names = list(wts)
    depth, bsz, seq, lc = w_mod.shape[0], x.shape[0], x.shape[1], ctx.shape[1]
    g = Geo(bsz, lc + seq, lc)
    xi, yi, ci = _coords()
    dev, xy = 4 * xi + 2 * yi + ci, 2 * xi + yi
    n_batch = N_DEV * bsz
    nr = -(-(n_batch + 1) // 16) * 16
    mod_cols = w_mod.shape[2]

    c_all = allgather8("gather_c", c).reshape(n_batch, D_MODEL)
    call = jnp.concatenate([c_all, c_ctx[None], jnp.zeros((nr - n_batch - 1, D_MODEL), F32)], axis=0)
    b_sh = lax.dynamic_slice_in_dim(b_mod, xy * mod_cols, mod_cols, axis=1)[:, None, :]
    mod_sh = mod_fwd(call, w_mod, b_sh)
    mod_g = allgather8("gather_mod", mod_sh.reshape(depth * nr, mod_cols)).reshape(N_XY, 2, depth, nr, mod_cols)[:, 0]
    mod_all = mod_g.transpose(1, 2, 0, 3).reshape(depth, nr, N_XY * mod_cols)
    mod_lat = lax.dynamic_slice_in_dim(mod_all, dev * bsz, bsz, axis=1).reshape(depth, bsz, 6, D_MODEL)
    mod_ctx = jnp.broadcast_to(mod_all[:, n_batch].reshape(depth, 1, 6, D_MODEL), (depth, bsz, 6, D_MODEL))
    mod = jnp.stack([mod_ctx, mod_lat], axis=2)

    sm_shapes = [wts[k].shape for k in SMALL_SHARDED]
    sm_g = allgather8("gather_small", pack_rows([wts[k] for k in SMALL_SHARDED])).reshape(N_XY, 2, -1)[:, 0]
    small = {k: wts[k] for k in SMALL_ORDER}
    for k, parts in zip(SMALL_SHARDED, zip(*[unpack_rows(sm_g[p], sm_shapes) for p in range(N_XY)])):
        small[k] = jnp.concatenate(parts, axis=-1)

    shards = [[wts[k][l].astype(BF16) for k, _ in BIG] for l in range(depth)]

    xa = jnp.concatenate([ctx, x], axis=1)
    loss_part, dxa, mine, theirs, gsmall, dmod = model_step(g, xa, loss_target, mod, shards, small)
    loss = lax.psum(loss_part, ("x", "y", "c"))
    grad_x = dxa[:, lc:]

    grads, deltas, new_m, new_v = {}, {}, {}, {}

    def update(k, parts, shape2d):
        res = adamw("adamw_" + k, wts[k].reshape(shape2d), mom[k].reshape(shape2d), var[k].reshape(shape2d), parts)
        grads[k], deltas[k], new_m[k], new_v[k] = (r.reshape(wts[k].shape) for r in res)

    for i, (k, _) in enumerate(BIG):
        cols = wts[k].shape[-1]
        slots = lambda per_layer: [s[i].reshape(N_XY, -1, cols) for s in per_layer]
        res = adamw_slots("adamw_" + k, wts[k].reshape(-1, cols), mom[k].reshape(-1, cols), var[k].reshape(-1, cols),
                          slots(mine), slots(theirs))
        grads[k], deltas[k], new_m[k], new_v[k] = (r.reshape(wts[k].shape) for r in res)

    dm_mine = jnp.concatenate([dmod[:, :, 1], dmod[:, :, 0]], axis=1).reshape(depth * 2 * bsz, 6 * D_MODEL)
    dm_all = allgather8("gather_dmod", dm_mine).reshape(N_DEV, depth, 2 * bsz, 6 * D_MODEL)
    gb = bmod_grad(dm_all).reshape(depth, 6 * D_MODEL)
    dm_cols = lax.dynamic_slice_in_dim(dm_all, xy * mod_cols, mod_cols, axis=3)
    d_lat = dm_cols[:, :, :bsz].transpose(1, 0, 2, 3).reshape(depth, n_batch, mod_cols)
    d_lat = jnp.pad(d_lat, ((0, 0), (0, nr - n_batch), (0, 0)))
    d_ctx = dm_cols[:, :, bsz:].transpose(1, 0, 2, 3).reshape(depth, n_batch, mod_cols)
    gw_mod, gc_part = mod_bwd(call, c_ctx[None], d_lat, d_ctx, w_mod, n_batch)
    update("w_mod", [gw_mod.reshape(depth * D_MODEL, mod_cols)], (depth * D_MODEL, mod_cols))
    update("b_mod", [gb], b_mod.shape)

    pack_all = pack_rows([gsmall[k] for k in SMALL_ORDER])
    pack_xy = pack_rows([gc_part])
    rows_all = pack_all.shape[0]
    tot = small_reduce(allgather8("gather_gsmall", jnp.concatenate([pack_all, pack_xy], axis=0)), rows_all)
    gtot = dict(zip(SMALL_ORDER, unpack_rows(tot[:rows_all], [gsmall[k].shape for k in SMALL_ORDER])))
    gtot["c_ctx"] = unpack_rows(tot[rows_all:], [c_ctx.shape])[0]
    for k in SMALL_SHARDED:
        width = wts[k].shape[-1]
        gtot[k] = lax.dynamic_slice_in_dim(gtot[k], xy * width, width, axis=gtot[k].ndim - 1)
    sm_names = ("c_ctx",) + SMALL_ORDER
    sm_shapes = [wts[k].shape for k in sm_names]
    res = adamw("adamw_small", pack_rows([wts[k] for k in sm_names]), pack_rows([mom[k] for k in sm_names]),
                pack_rows([var[k] for k in sm_names]), [pack_rows([gtot[k] for k in sm_names])])
    for dst, buf in zip((grads, deltas, new_m, new_v), res):
        dst.update(zip(sm_names, unpack_rows(buf, sm_shapes)))

    return (loss, grad_x, *[grads[k] for k in names], *[deltas[k] for k in names], *[new_m[k] for k in names],
            *[new_v[k] for k in names])
```

```python
import functools
import math

import numpy as np
import jax
import jax.numpy as jnp
from jax import lax
from jax.experimental import pallas as pl
from jax.experimental.pallas import tpu as pltpu

F32, BF16 = jnp.float32, jnp.bfloat16
HIGHEST = lax.Precision.HIGHEST
MESH = pl.DeviceIdType.MESH

D_MODEL = 1024
GRID_W = 64
N_BRANCH = 4
BRANCH_W = 512
HEAD64 = 64
HEAD128 = 128
N_HEAD4 = 4
ATT_HEADS = 8
ATT_KV_HEADS = 2
CHUNK = 64
MLP_HIDDEN = 4 * D_MODEL
ROPE_THETA = 10000.0
EPS = 1e-6
N_IN = 10000
ADAM_LR, ADAM_B1, ADAM_B2, ADAM_EPS, ADAM_WD, ADAM_STEP = 0.001, 0.9, 0.999, 1e-08, 0.01, 10

LANES = 128
VMEM_LIMIT = 56 * 1024 * 1024

_SEGS = (
    ("gates", 5904, 4096),
    ("att_q", 0, 512), ("dn_q", 768, 512), ("dn_k", 1280, 512), ("dn_v", 1792, 512), ("dn_z", 2304, 512),
    ("ret_v", 3344, 512), ("ret_g", 3856, 512), ("sc_b", 4368, 512), ("sc_c", 4880, 512), ("sc_x", 5392, 512),
    ("ret_q", 2832, 256), ("ret_k", 3088, 256),
    ("att_k", 512, 128), ("att_v", 640, 128),
    ("narrow", 2816, 16),
)
NZ = 10240


def _seg_offsets():
    off, out = 0, {}
    for name, _, width in _SEGS:
        out[name] = off
        off += width
    return out


ZOFF = _seg_offsets()


def _pick(n, cands):
    for c in cands:
        if n % c == 0:
            return c
    return n


def _dg(a, b, ca, cb, batch=False):
    dn = (((ca,), (cb,)), ((0,), (0,))) if batch else (((ca,), (cb,)), ((), ()))
    return lax.dot_general(a.astype(BF16), b.astype(BF16), dn, preferred_element_type=F32)


@functools.partial(jax.custom_vjp, nondiff_argnums=(2, 3))
def _mm(a, b, ca, cb):
    return _dg(a, b, ca, cb)


def _mm_fwd(a, b, ca, cb):
    return _dg(a, b, ca, cb), (a, b)


def _mm_bwd(ca, cb, res, g):
    a, b = res
    if ca == 1:
        da = _mm(g, b, 1, 1) if cb == 0 else _mm(g, b, 1, 0)
    else:
        da = _mm(b, g, 1, 1) if cb == 0 else _mm(b, g, 0, 1)
    if cb == 0:
        db = _mm(a, g, 0, 0) if ca == 1 else _mm(a, g, 1, 0)
    else:
        db = _mm(g, a, 0, 0) if ca == 1 else _mm(g, a, 0, 1)
    return da.astype(a.dtype), db.astype(b.dtype)


_mm.defvjp(_mm_fwd, _mm_bwd)


@functools.partial(jax.custom_vjp, nondiff_argnums=(2, 3))
def _bmm(a, b, ca, cb):
    return _dg(a, b, ca, cb, True)


def _bmm_fwd(a, b, ca, cb):
    return _dg(a, b, ca, cb, True), (a, b)


def _bmm_bwd(ca, cb, res, g):
    a, b = res
    if ca == 2:
        da = _bmm(g, b, 2, 2) if cb == 1 else _bmm(g, b, 2, 1)
    else:
        da = _bmm(b, g, 2, 2) if cb == 1 else _bmm(b, g, 1, 2)
    if cb == 1:
        db = _bmm(a, g, 1, 1) if ca == 2 else _bmm(a, g, 2, 1)
    else:
        db = _bmm(g, a, 1, 1) if ca == 2 else _bmm(g, a, 1, 2)
    return da.astype(a.dtype), db.astype(b.dtype)


_bmm.defvjp(_bmm_fwd, _bmm_bwd)


def _split_bf16(x):
    hi = x.astype(BF16)
    lo = (x - hi.astype(F32)).astype(BF16)
    return hi, lo


def _select3(a, b, cb):
    hi = a.astype(BF16)
    rest = a - hi.astype(F32)
    mid = rest.astype(BF16)
    lo = (rest - mid.astype(F32)).astype(BF16)
    bb = b.astype(BF16)
    d = lambda u: lax.dot_general(u, bb, (((1,), (cb,)), ((), ())), preferred_element_type=F32)
    return d(hi) + (d(mid) + d(lo))


@jax.custom_vjp
def _mm_exact(a, b):
    return _select3(a, b, 0)


_mm_exact.defvjp(lambda a, b: (_select3(a, b, 0), b), lambda b, g: (_select3(g, b, 1), jnp.zeros_like(b)))


class Arg:
    def __init__(self, x, block, imap, diff=True, first=None, gdtype=F32, pieces=None, gshape=None, gimap=None):
        self.x, self.block, self.imap = x, tuple(block), imap
        self.diff, self.first, self.gdtype, self.pieces = diff, first, gdtype, pieces
        self.upcast = x.dtype == BF16 and diff
        self.gshape = tuple(x.shape) if gshape is None else tuple(gshape)
        self.gimap = imap if gimap is None else gimap

    def spec(self):
        return pl.BlockSpec(self.block, self.imap)

    def gspec(self):
        return pl.BlockSpec(self.block, self.gimap)


class Out:
    def __init__(self, shape, dtype, block, imap, pieces=None):
        self.shape, self.dtype, self.block, self.imap, self.pieces = tuple(shape), dtype, tuple(block), imap, pieces

    def spec(self):
        return pl.BlockSpec(self.block, self.imap)

    def sds(self):
        return jax.ShapeDtypeStruct(self.shape, self.dtype)


def _lanes(ref, s, w):
    return (slice(None),) * (len(ref.shape) - 1) + (slice(s, s + w),)


def _load(ref, pieces):
    if pieces is None:
        return ref[...]
    return tuple(ref[_lanes(ref, s, w)] for s, w in pieces)


def _store(ref, val, pieces, accumulate=False):
    if pieces is None:
        if accumulate:
            ref[...] += val.astype(ref.dtype)
        else:
            ref[...] = val.astype(ref.dtype)
        return
    if not accumulate:
        covered = sum(w for _, w in pieces)
        if covered != ref.shape[-1]:
            ref[...] = jnp.zeros(ref.shape, ref.dtype)
    for (s, w), v in zip(pieces, val):
        if accumulate:
            ref[_lanes(ref, s, w)] += v.astype(ref.dtype)
        else:
            ref[_lanes(ref, s, w)] = v.astype(ref.dtype)


def _params(n_grid):
    return pltpu.CompilerParams(dimension_semantics=("arbitrary",) * n_grid, vmem_limit_bytes=VMEM_LIMIT)


def _load_arg(ref, a):
    val = _load(ref, a.pieces)
    return jax.tree.map(lambda t: t.astype(F32), val) if a.upcast else val


def vfwd(name, f, grid, args, outs, comm=None):
    n_in = len(args)

    def body(*refs):
        in_refs, out_refs, _, xrefs = _carried(refs, n_in, len(outs), comm)
        _carry_start(xrefs, comm, grid)
        ids = tuple(pl.program_id(i) for i in range(len(grid)))
        vals = [_load_arg(r, a) for r, a in zip(in_refs, args)]
        res = f(ids, *vals)
        for r, o, spec in zip(out_refs, res, outs):
            _store(r, o, spec.pieces)
        _carry_wait(xrefs, comm, grid)

    return _carrier_call(body, name, grid, [a.spec() for a in args], [a.x for a in args], [o.spec() for o in outs],
                         [o.sds() for o in outs], [], comm)


def vbwd(name, f, grid, args, outs, cts):
    n_in = len(args)
    diff_idx = [i for i, a in enumerate(args) if a.diff]
    ct_flat = [c for per_out in cts for c in per_out]
    ct_specs = [o.spec() for o, per_out in zip(outs, cts) for _ in per_out]
    n_ct = len(ct_flat)

    def body(*refs):
        ids = tuple(pl.program_id(i) for i in range(len(grid)))
        vals = [_load_arg(r, a) for r, a in zip(refs[:n_in], args)]
        ct_refs = refs[n_in:n_in + n_ct]
        g_refs = refs[n_in + n_ct:]

        def g(*dvals):
            full = list(vals)
            for i, v in zip(diff_idx, dvals):
                full[i] = v
            return tuple(f(ids, *full))

        _, vjp = jax.vjp(g, *[vals[i] for i in diff_idx])
        ct_vals, k = [], 0
        for o, per_out in zip(outs, cts):
            tot = None
            for _ in per_out:
                v = _load(ct_refs[k], o.pieces)
                k += 1
                v = jax.tree.map(lambda t: t.astype(F32), v)
                tot = v if tot is None else jax.tree.map(jnp.add, tot, v)
            ct_vals.append(tot)
        grads = vjp(tuple(ct_vals))
        for gr, gv, i in zip(g_refs, grads, diff_idx):
            a = args[i]
            if a.first is None:
                _store(gr, gv, a.pieces)
            else:
                is_first = a.first(ids)

                @pl.when(is_first)
                def _():
                    _store(gr, gv, a.pieces)

                @pl.when(jnp.logical_not(is_first))
                def _():
                    _store(gr, gv, a.pieces, accumulate=True)

    g_specs = [args[i].gspec() for i in diff_idx]
    g_shapes = [jax.ShapeDtypeStruct(args[i].gshape, args[i].gdtype) for i in diff_idx]
    return pl.pallas_call(
        body, name=name, grid=grid,
        in_specs=[a.spec() for a in args] + ct_specs, out_specs=g_specs, out_shape=g_shapes,
        compiler_params=_params(len(grid)),
    )(*[a.x for a in args], *ct_flat)


def matmul(name, a, b, trans_a=False, trans_b=False, epilogue=None, extras=(), out_dtypes=(F32,)):
    if trans_a:
        kdim, m = a.shape
    else:
        m, kdim = a.shape
    n = b.shape[0] if trans_b else b.shape[1]
    assert b.shape[1 if trans_b else 0] == kdim
    tm = _pick(m, (1024, 512, 256, 192, 128, 64))
    tn = _pick(n, (1024, 512, 256, 128))
    tk = _pick(kdim, (1024, 512, 256, 192, 128, 64))
    nk = kdim // tk
    n_x = len(extras)
    assert epilogue is None or nk == 1
    assert epilogue is not None or tuple(out_dtypes) == (F32,)

    def body(a_ref, b_ref, *rest):
        o_ref = rest[n_x]
        part = _dg(a_ref[...], b_ref[...], 0 if trans_a else 1, 1 if trans_b else 0)
        if epilogue is not None:
            for r, val in zip(rest[n_x:], epilogue(part, *[x[...] for x in rest[:n_x]])):
                r[...] = val.astype(r.dtype)
        elif nk == 1:
            o_ref[...] = part
        else:
            k = pl.program_id(2)

            @pl.when(k == 0)
            def _():
                o_ref[...] = part

            @pl.when(k > 0)
            def _():
                o_ref[...] += part

    a_spec = pl.BlockSpec((tk, tm), lambda i, j, k: (k, i)) if trans_a else pl.BlockSpec((tm, tk), lambda i, j, k: (i, k))
    b_spec = pl.BlockSpec((tn, tk), lambda i, j, k: (j, k)) if trans_b else pl.BlockSpec((tk, tn), lambda i, j, k: (k, j))
    mn_spec = pl.BlockSpec((tm, tn), lambda i, j, k: (i, j))
    res = pl.pallas_call(
        body, name=name, grid=(m // tm, n // tn, nk),
        in_specs=[a_spec, b_spec] + [mn_spec] * n_x,
        out_specs=[mn_spec] * len(out_dtypes),
        out_shape=[jax.ShapeDtypeStruct((m, n), dt) for dt in out_dtypes],
        compiler_params=pltpu.CompilerParams(dimension_semantics=("parallel", "parallel", "arbitrary"),
                                             vmem_limit_bytes=VMEM_LIMIT),
    )(a, b, *extras)
    return res if epilogue is not None else res[0]


def _rms(x, gain):
    return x * lax.rsqrt(jnp.mean(x * x, axis=-1, keepdims=True) + EPS) * gain


def _silu(x):
    return x * jax.nn.sigmoid(x)


def f_modnorm(ids, x, gain, mod):
    return (_rms(x, gain) * (1.0 + mod[1:2]) + mod[0:1],)


def f_resnorm(ids, x, y, g_res, g_next, mod):
    x_new = x + mod[0:1] * _rms(y, g_res)
    return x_new, _rms(x_new, g_next) * (1.0 + mod[2:3]) + mod[1:2]


def f_resid(ids, x, y, g_res, mod):
    return (x + mod[0:1] * _rms(y, g_res),)


def _head_consts(width, head):
    i = lax.broadcasted_iota(jnp.int32, (width, width), 0)
    j = lax.broadcasted_iota(jnp.int32, (width, width), 1)
    shift = int(math.log2(head))
    same = (i >> shift) == (j >> shift)
    group = jnp.where(same, 1.0 / head, 0.0).astype(F32)
    half = head // 2
    ii, jj = i & (head - 1), j & (head - 1)
    rot = jnp.where(same & (ii == jj + half) & (jj < half), -1.0, 0.0) + jnp.where(same & (ii + half == jj) & (jj >= half), 1.0, 0.0)
    ti = lax.broadcasted_iota(jnp.int32, (head, width), 0)
    tj = lax.broadcasted_iota(jnp.int32, (head, width), 1)
    tile = jnp.where(ti == (tj & (head - 1)), 1.0, 0.0).astype(F32)
    return group, rot.astype(F32), tile


def _rope(x, cos, sin, rot):
    return x * cos + _mm_exact(x, rot) * sin


def _softplus(x):
    return jnp.maximum(x, 0.0) + jnp.log(1.0 + jnp.exp(-jnp.abs(x)))


def f_prep(ids, zq, zk, rq, rk, zn, qgain, kgain, alog, dtb, cos, sin):
    grp_q, rot_q, tile_q = _head_consts(ATT_HEADS * HEAD64, HEAD64)
    grp_k, rot_k, tile_k = _head_consts(ATT_KV_HEADS * HEAD64, HEAD64)
    grp_r, rot_r, _ = _head_consts(N_HEAD4 * HEAD64, HEAD64)
    wq, wk, wr = zq.shape[-1], zk.shape[-1], rq.shape[-1]
    tile = lambda gain, sel: jnp.dot(gain, sel, precision=HIGHEST, preferred_element_type=F32)
    qn = zq * lax.rsqrt(_mm_exact(zq * zq, grp_q) + EPS) * tile(qgain, tile_q)
    kn = zk * lax.rsqrt(_mm_exact(zk * zk, grp_k) + EPS) * tile(kgain, tile_k)
    q_att = _rope(qn, cos[:, :wq], sin[:, :wq], rot_q) * (HEAD64 ** -0.5)
    k_att = _rope(kn, cos[:, :wk], sin[:, :wk], rot_k)
    q_ret = _rope(rq, cos[:, :wr], sin[:, :wr], rot_r)
    k_ret = _rope(rk * (HEAD64 ** -0.5), cos[:, :wr], sin[:, :wr], rot_r)
    lane = lax.broadcasted_iota(jnp.int32, zn.shape, 1)
    log_a = -jnp.exp(alog) * _softplus(zn + dtb)
    gates = jnp.where(lane < 8, log_a, jnp.where(lane < 16, jax.nn.sigmoid(zn), 0.0))
    return q_att, k_att, q_ret, k_ret, gates


def make_f_attn(tq):
    def f_attn(ids, q, k, v):
        outs = []
        per = ATT_HEADS // ATT_KV_HEADS
        for g in range(ATT_KV_HEADS):
            qg = jnp.concatenate(q[g * per:(g + 1) * per], axis=0)
            s = _mm(qg, k[g], 1, 1)
            e = jnp.exp(s - lax.stop_gradient(jnp.max(s, axis=-1, keepdims=True)))
            o = _mm(e, v[g], 1, 0) * (1.0 / jnp.sum(e, axis=-1, keepdims=True))
            outs += [o[i * tq:(i + 1) * tq] for i in range(per)]
        return (tuple(outs),)
    return f_attn


def _roll_rows(x, shift):
    return pltpu.roll(x, shift, 0)


def make_shifts(t, lc):
    def _down(x):
        row = lax.broadcasted_iota(jnp.int32, x.shape, 0)
        return jnp.where((row == 0) | (row == lc), 0.0, _roll_rows(x, 1))

    def _up(x):
        row = lax.broadcasted_iota(jnp.int32, x.shape, 0)
        return jnp.where((row == lc - 1) | (row == t - 1), 0.0, _roll_rows(x, t - 1))

    @jax.custom_vjp
    def down(x):
        return _down(x)

    @jax.custom_vjp
    def up(x):
        return _up(x)

    down.defvjp(lambda x: (_down(x), None), lambda _, g: (up(g),))
    up.defvjp(lambda x: (_up(x), None), lambda _, g: (down(g),))
    return down, up


def make_conv3(t, lc):
    down, up = make_shifts(t, lc)

    def conv3(x, w):
        return w[0:1] * down(x) + w[1:2] * x + w[2:3] * up(x)
    return conv3


def make_f_dnprep(t, lc):
    conv3 = make_conv3(t, lc)

    def l2n(x):
        return x * lax.rsqrt(jnp.sum(x * x, axis=-1, keepdims=True) + EPS)

    def f_dnprep(ids, q, k, v, wq, wk, wv):
        qn = l2n(_silu(conv3(q, wq))) * (HEAD128 ** -0.5)
        kn = l2n(_silu(conv3(k, wk)))
        return qn, kn, _silu(conv3(v, wv))
    return f_dnprep


def make_f_shortconv(t, lc):
    conv3 = make_conv3(t, lc)

    def f_shortconv(ids, b, c, x, w):
        return (b * conv3(c * x, w),)
    return f_shortconv


def f_finish(ids, o_dn_f, o_dn_b, z_dn, o_rt_f, o_rt_b, g_rt, ngain):
    y_dn, y_rt = [], []
    for h in range(N_HEAD4):
        o = o_dn_f[h] + o_dn_b[h]
        y_dn.append(_rms(o, ngain) * _silu(z_dn[h]))
        r = o_rt_f[h] + o_rt_b[h]
        mu = jnp.mean(r, axis=-1, keepdims=True)
        var = jnp.mean(jnp.square(r - mu), axis=-1, keepdims=True)
        y_rt.append((r - mu) * lax.rsqrt(var + EPS) * _silu(g_rt[h]))
    return tuple(y_dn), tuple(y_rt)


def f_merge(ids, p0, p1, p2, p3, gates):
    u = jax.nn.sigmoid(gates[0]) * p0
    for g, p in zip(gates[1:], (p1, p2, p3)):
        u = u + jax.nn.sigmoid(g) * p
    return (u,)


def _lane_pieces(width, n=N_HEAD4):
    return [(h * width, width) for h in range(n)]


class Tok:
    def __init__(self, x, pieces, width=None, col=0):
        self.x, self.pieces, self.col = x, pieces, col
        self.width = x.shape[-1] if width is None else width
        self.upcast = x.dtype == BF16


def _chunk_of(direction, step, nc, nctx):
    if direction == 0:
        return step
    return jnp.where(step < nctx, nctx - 1 - step, nc + nctx - 1 - step)


ROWS = N_HEAD4 * CHUNK


def _bmm3(a, b, ca, cb):
    ah, al = _split_bf16(a)
    bh, bl = _split_bf16(b)
    dn = (((ca,), (cb,)), ((0,), (0,)))
    d = lambda u, v: lax.dot_general(u, v, dn, preferred_element_type=F32)
    return d(ah, bh) + (d(ah, bl) + d(al, bh))


def _inv_unit_tri_b(low):
    n = low.shape[-1]
    eye = (lax.broadcasted_iota(jnp.int32, (1, n, n), 1) == lax.broadcasted_iota(jnp.int32, (1, n, n), 2)).astype(F32)
    m = -low
    p = eye + m
    for _ in range(int(math.log2(CHUNK)) - 1):
        m = _bmm3(m, m, 2, 1)
        p = p + _bmm3(p, m, 2, 1)
    return p


@jax.custom_vjp
def _tri_solve_b(low, rhs, inv):
    return _bmm3(inv, rhs, 2, 1)


def _tri_solve_b_fwd(low, rhs, inv):
    x = _bmm3(inv, rhs, 2, 1)
    return x, (inv, x)


def _tri_solve_b_bwd(res, g):
    inv, x = res
    d_rhs = _bmm3(inv, g, 1, 1)
    return -_bmm3(d_rhs, x, 2, 2), d_rhs, jnp.zeros_like(inv)


_tri_solve_b.defvjp(_tri_solve_b_fwd, _tri_solve_b_bwd)


def _stack_dirs(toks, i):
    return jnp.concatenate([jnp.concatenate(t[i], axis=1) for t in toks], axis=0)


def _problem_masks(p, nb):
    shape = (p, ROWS, ROWS)
    up = lax.broadcasted_iota(jnp.int32, shape, 0) >= nb
    i = lax.broadcasted_iota(jnp.int32, shape, 1)
    j = lax.broadcasted_iota(jnp.int32, shape, 2)
    same = (i >> 6) == (j >> 6)
    pi, pj = i & (CHUNK - 1), j & (CHUNK - 1)
    rel = jnp.where(up, pj - pi, pi - pj)
    last = same & (pj == jnp.where(up, 0, CHUNK - 1))
    return same & (rel >= 0), same & (rel > 0), last, rel


def _split_states(s_new, nd, nb):
    return [s_new[d * nb * N_HEAD4:(d + 1) * nb * N_HEAD4].reshape((nb, N_HEAD4) + s_new.shape[1:]) for d in range(nd)]


def _split_outs(o, nd, nb):
    return [(tuple(o[d * nb:(d + 1) * nb, h * CHUNK:(h + 1) * CHUNK] for h in range(N_HEAD4)),) for d in range(nd)]


def dn_chunks(states, toks, params, aux=None):
    nd, nb = len(toks), states[0].shape[0]
    p = nd * nb
    qs, ks, vs, la, beta = (_stack_dirs(toks, i) for i in range(5))
    incl, strict, last, _ = _problem_masks(p, nb)
    sq = (p, ROWS, ROWS)
    g = jnp.sum(jnp.where(incl, jnp.swapaxes(jnp.broadcast_to(la, sq), 1, 2), 0.0), axis=2, keepdims=True)
    gb = jnp.broadcast_to(g, sq)
    gbt = jnp.swapaxes(gb, 1, 2)
    dec_incl = jnp.where(incl, jnp.exp(jnp.where(incl, gb - gbt, 0.0)), 0.0)
    dec_strict = jnp.where(strict, dec_incl, 0.0)
    low = beta * _bmm(ks, ks, 2, 2) * dec_strict
    eg = jnp.exp(g)
    inv = _inv_unit_tri_b(low) if aux is None else aux[0]
    sol = _tri_solve_b(low, jnp.concatenate([beta * vs, (beta * eg) * ks], axis=2), inv)
    w_v, w_k = sol[:, :, :HEAD128], sol[:, :, HEAD128:]
    a_qk = _bmm(qs, ks, 2, 2) * dec_incl
    g_last = jnp.sum(jnp.where(last, gbt, 0.0), axis=2, keepdims=True)
    k_dec = ks * jnp.exp(g_last - g)
    s = jnp.concatenate(states, axis=0).reshape(p * N_HEAD4, HEAD128, HEAD128)
    h3 = lambda x: x.reshape(p * N_HEAD4, CHUNK, x.shape[-1])
    u = w_v - _bmm(h3(w_k), s, 2, 1).reshape(p, ROWS, HEAD128)
    o = _bmm(h3(qs * eg), s, 2, 1).reshape(p, ROWS, HEAD128) + _bmm(a_qk, u, 2, 1)
    decay = jnp.exp(jnp.mean(h3(g_last), axis=1, keepdims=True))
    s_new = s * decay + _bmm(h3(k_dec), h3(u), 1, 1)
    return _split_outs(o, nd, nb), _split_states(s_new, nd, nb), [inv]


def ret_chunks(states, toks, params, aux=None):
    nd, nb = len(toks), states[0].shape[0]
    p = nd * nb
    (decay_log,) = params
    qs, ks, vs = (_stack_dirs(toks, i) for i in range(3))
    incl, _, _, rel = _problem_masks(p, nb)
    col = (p, ROWS, 1)
    up = lax.broadcasted_iota(jnp.int32, col, 0) >= nb
    row = lax.broadcasted_iota(jnp.int32, col, 1)
    head = row >> 6
    lg = jnp.zeros(col, F32)
    for h in range(N_HEAD4):
        rate = jnp.where(up, -jnp.exp(decay_log[N_HEAD4 + h]), -jnp.exp(decay_log[h]))
        lg = jnp.where(head == h, rate, lg)
    place = row & (CHUNK - 1)
    pos = jnp.where(up, CHUNK - 1 - place, place).astype(F32)
    dmask = jnp.where(incl, jnp.exp(jnp.where(incl, rel.astype(F32) * lg, 0.0)), 0.0)
    o = _bmm(_bmm(qs, ks, 2, 2) * dmask, vs, 2, 1)
    s = jnp.concatenate(states, axis=0).reshape(p * N_HEAD4, HEAD64, HEAD128)
    h3 = lambda x: x.reshape(p * N_HEAD4, CHUNK, x.shape[-1])
    kv = _bmm(h3(ks * jnp.exp((CHUNK - 1.0 - pos) * lg)), h3(vs), 1, 1)
    o = o + _bmm(h3(qs * jnp.exp((pos + 1.0) * lg)), s, 2, 1).reshape(p, ROWS, HEAD128)
    decay = jnp.exp(CHUNK * jnp.mean(h3(lg), axis=1, keepdims=True))
    s_new = s * decay + kv
    return _split_outs(o, nd, nb), _split_states(s_new, nd, nb), []


def _scan_nb(bsz):
    return next(n for n in (2, 1) if bsz % n == 0)


def _carried(refs, n_in, n_out, comm):
    if comm is None:
        return refs[:n_in], refs[n_in:n_in + n_out], refs[n_in + n_out:], None
    n = len(comm[1])
    ins, srcs = refs[:n_in], refs[n_in:n_in + n]
    outs, dsts = refs[n_in + n:n_in + n + n_out], refs[n_in + n + n_out:n_in + 2 * n + n_out]
    scratch, sems = refs[n_in + 2 * n + n_out:-3], refs[-3:]
    return ins, outs, scratch, (srcs, dsts) + tuple(sems)


def _carry_start(xrefs, comm, grid):
    if comm is None:
        return
    first = functools.reduce(jnp.logical_and, [pl.program_id(i) == 0 for i in range(len(grid))])

    @pl.when(first)
    def _():
        _exchange_start(xrefs, comm[0])


def _carry_wait(xrefs, comm, grid):
    if comm is None:
        return
    last = functools.reduce(jnp.logical_and, [pl.program_id(i) == n - 1 for i, n in enumerate(grid)])

    @pl.when(last)
    def _():
        _exchange_wait(xrefs, comm[0])


def scan2_fwd(name, chunks_fn, dirs, params, state_shape, outs, lc, comm=None, aux_shapes=()):
    bsz, t = dirs[0][0].x.shape[:2]
    nb = _scan_nb(bsz)
    nc, nctx = t // CHUNK, lc // CHUNK
    nd, n_t, n_p, n_o, n_a = len(dirs), len(dirs[0]), len(params), len(outs), len(aux_shapes)
    grid = (bsz // nb, nc)

    def body(*refs):
        ins, out_refs, s_refs, xrefs = _carried(refs, nd * n_t + n_p, nd * (n_o + 1) + n_a, comm)
        tok_refs, par_refs = ins[:nd * n_t], ins[nd * n_t:]
        _carry_start(xrefs, comm, grid)

        @pl.when(pl.program_id(1) == 0)
        def _():
            for s_ref in s_refs:
                s_ref[...] = jnp.zeros(s_ref.shape, F32)

        pv = [_load(r, p) for r, (_, p) in zip(par_refs, params)]
        states = [s_ref[...] for s_ref in s_refs]
        tv = [[_load_arg(r, tk) for r, tk in zip(tok_refs[d * n_t:(d + 1) * n_t], dirs[d])] for d in range(nd)]
        res, s_new, aux = chunks_fn(states, tv, pv)
        for d in range(nd):
            o_refs = out_refs[d * (n_o + 1):(d + 1) * (n_o + 1)]
            o_refs[-1][...] = states[d]
            s_refs[d][...] = s_new[d]
            for r, o, (_, _, pieces) in zip(o_refs, res[d], outs):
                _store(r, o, pieces)
        for r, a in zip(out_refs[nd * (n_o + 1):], aux):
            r[...] = a
        _carry_wait(xrefs, comm, grid)

    nstate = len(state_shape)
    tok_specs, out_specs, out_shapes, operands = [], [], [], []
    for d, toks in enumerate(dirs):
        chunk = lambda c, d=d: _chunk_of(d, c, nc, nctx)
        for tk in toks:
            tok_specs.append(pl.BlockSpec((nb, CHUNK, tk.width), lambda b, c, f=chunk, col=tk.col: (b, f(c), col)))
            operands.append(tk.x)
        for w, dt, _ in outs:
            out_specs.append(pl.BlockSpec((nb, CHUNK, w), lambda b, c, f=chunk: (b, f(c), 0)))
            out_shapes.append(jax.ShapeDtypeStruct((bsz, t, w), dt))
        out_specs.append(pl.BlockSpec((nb, None) + state_shape, lambda b, c, f=chunk: (b, f(c)) + (0,) * nstate))
        out_shapes.append(jax.ShapeDtypeStruct((bsz, nc) + state_shape, F32))
    for shape in aux_shapes:
        out_specs.append(pl.BlockSpec((None, None) + tuple(shape), lambda b, c, n=len(shape): (b, c) + (0,) * n))
        out_shapes.append(jax.ShapeDtypeStruct(grid + tuple(shape), F32))
    par_specs = [pl.BlockSpec(x.shape, lambda b, c: (0, 0)) for x, _ in params]
    operands += [x for x, _ in params]
    return _carrier_call(body, name, grid, tok_specs + par_specs, operands, out_specs, out_shapes,
                         [pltpu.VMEM((nb,) + state_shape, F32)] * nd, comm)


def _carrier_call(body, name, grid, in_specs, operands, out_specs, out_shapes, scratch, comm):
    if comm is not None:
        kind, arrays = comm
        in_specs, operands = in_specs + [HBM_SPEC] * len(arrays), operands + list(arrays)
        out_specs, out_shapes = out_specs + [HBM_SPEC] * len(arrays), out_shapes + [_exchange_shape(kind, a) for a in arrays]
        scratch = scratch + _exchange_sems(len(arrays))
    return pl.pallas_call(body, name=name, grid=grid, in_specs=in_specs, out_specs=out_specs, out_shape=out_shapes,
                          scratch_shapes=scratch, compiler_params=_params(len(grid)))(*operands)


def scan2_bwd(name, chunks_fn, dirs, params, state_shape, outs, lc, sprevs, cts, comm=None, aux=()):
    bsz, t = dirs[0][0].x.shape[:2]
    nb = _scan_nb(bsz)
    nc, nctx = t // CHUNK, lc // CHUNK
    nd, n_t, n_p, n_o, n_a = len(dirs), len(dirs[0]), len(params), len(outs), len(aux)
    per_in = n_t + 1 + n_o
    grid = (bsz // nb, nc)

    def body(*refs):
        refs, g_refs, ds_refs, xrefs = _carried(refs, nd * per_in + n_p + n_a, nd * n_t + n_p, comm)
        par_refs = refs[nd * per_in:nd * per_in + n_p]
        saved = [r[...] for r in refs[nd * per_in + n_p:]]
        fn = chunks_fn if not saved else (lambda s, tk, pr: chunks_fn(s, tk, pr, saved))
        _carry_start(xrefs, comm, grid)
        first = pl.program_id(1) == 0

        @pl.when(first)
        def _():
            for ds_ref in ds_refs:
                ds_ref[...] = jnp.zeros(ds_ref.shape, F32)

        pv = [_load(r, p) for r, (_, p) in zip(par_refs, params)]
        ins = [refs[d * per_in:(d + 1) * per_in] for d in range(nd)]
        tv = [[_load_arg(r, tk) for r, tk in zip(ins[d][:n_t], dirs[d])] for d in range(nd)]
        states = [ins[d][n_t][...] for d in range(nd)]
        _, vjp = jax.vjp(lambda s, tk, pr: fn(s, tk, pr)[:2], states, tv, pv)
        ct = [tuple(_load(r, pieces) for r, (_, _, pieces) in zip(ins[d][n_t + 1:], outs)) for d in range(nd)]
        d_s, d_tv, d_pv = vjp((ct, [ds_ref[...] for ds_ref in ds_refs]))
        for d in range(nd):
            ds_refs[d][...] = d_s[d]
            for r, gv, tk in zip(g_refs[d * n_t:(d + 1) * n_t], d_tv[d], dirs[d]):
                _store(r, gv, tk.pieces)
        very_first = first & (pl.program_id(0) == 0)
        for r, gv, (_, p) in zip(g_refs[nd * n_t:], d_pv, params):
            @pl.when(very_first)
            def _():
                _store(r, gv, p)

            @pl.when(jnp.logical_not(very_first))
            def _():
                _store(r, gv, p, accumulate=True)
        _carry_wait(xrefs, comm, grid)

    nstate = len(state_shape)
    in_specs, operands, g_specs, g_shapes = [], [], [], []
    for d, toks in enumerate(dirs):
        chunk = lambda c, d=d: _chunk_of(d, nc - 1 - c, nc, nctx)
        for tk in toks:
            in_specs.append(pl.BlockSpec((nb, CHUNK, tk.width), lambda b, c, f=chunk, col=tk.col: (b, f(c), col)))
            operands.append(tk.x)
            g_specs.append(pl.BlockSpec((nb, CHUNK, tk.width), lambda b, c, f=chunk: (b, f(c), 0)))
            g_shapes.append(jax.ShapeDtypeStruct((bsz, t, tk.width), F32))
        in_specs.append(pl.BlockSpec((nb, None) + state_shape, lambda b, c, f=chunk: (b, f(c)) + (0,) * nstate))
        operands.append(sprevs[d])
        for (w, _, _), ct in zip(outs, cts[d]):
            in_specs.append(pl.BlockSpec((nb, CHUNK, w), lambda b, c, f=chunk: (b, f(c), 0)))
            operands.append(ct)
    par_specs = [pl.BlockSpec(x.shape, lambda b, c: (0, 0)) for x, _ in params]
    aux_specs = [pl.BlockSpec((None, None) + a.shape[2:], lambda b, c, n=a.ndim - 2: (b, nc - 1 - c) + (0,) * n) for a in aux]
    operands += [x for x, _ in params] + list(aux)
    return _carrier_call(body, name, grid, in_specs + par_specs + aux_specs, operands, g_specs + par_specs,
                         g_shapes + [jax.ShapeDtypeStruct(x.shape, F32) for x, _ in params],
                         [pltpu.VMEM((nb,) + state_shape, F32)] * nd, comm)


class Geo:
    def __init__(self, bsz, t, lc):
        self.b, self.t, self.lc, self.m = bsz, t, lc, bsz * t
        self.tm = _pick(lc, (256, 128, 64))
        self.tq = _pick(lc, (128, 64))
        assert t % self.tm == 0 and t % CHUNK == 0 and lc % CHUNK == 0
        self.nctx, self.nctx_q = lc // self.tm, lc // self.tq
        self.grid = (bsz, t // self.tm)


def a_tok(g, x, tm=None, **kw):
    tm = tm or g.tm
    return Arg(x, (None, tm, x.shape[-1]), lambda b, j: (b, j, 0), **kw)


def a_ztok(g, z, name, width, tm=None, **kw):
    tm = tm or g.tm
    col = ZOFF[name] // width
    return Arg(z, (None, tm, width), lambda b, j: (b, j, col),
               gshape=(g.b, g.t, width), gimap=lambda b, j: (b, j, 0), **kw)


def a_par(x, **kw):
    return Arg(x, x.shape, lambda b, j: (0, 0), first=lambda ids: (ids[0] == 0) & (ids[1] == 0), **kw)


def a_mod(g, x):
    n = g.nctx
    return Arg(x, (None, None, x.shape[2], D_MODEL), lambda b, j: (b, jnp.where(j >= n, 1, 0), 0, 0),
               first=lambda ids: (ids[1] == 0) | (ids[1] == n))


def a_tab(g, x):
    return Arg(x, (g.tm, x.shape[-1]), lambda b, j: (j, 0), diff=False)


def o_tok(g, w, dtype, tm=None, pieces=None):
    tm = tm or g.tm
    return Out((g.b, g.t, w), dtype, (None, tm, w), lambda b, j: (b, j, 0), pieces)


def addn(name, xs, dtype):
    bsz, t, w = xs[0].shape
    tm = _pick(t, (256, 192, 128, 64))

    def body(*refs):
        tot = refs[0][...].astype(F32)
        for r in refs[1:-1]:
            tot = tot + r[...].astype(F32)
        refs[-1][...] = tot.astype(dtype)

    spec = pl.BlockSpec((None, tm, w), lambda b, j: (b, j, 0))
    return pl.pallas_call(body, name=name, grid=(bsz, t // tm), in_specs=[spec] * len(xs), out_specs=spec,
                          out_shape=jax.ShapeDtypeStruct((bsz, t, w), dtype), compiler_params=_params(2))(*xs)


P64x8 = _lane_pieces(HEAD64, ATT_HEADS)
P64x4 = _lane_pieces(HEAD64, N_HEAD4)
P64x2 = _lane_pieces(HEAD64, ATT_KV_HEADS)
P128x4 = _lane_pieces(HEAD128, N_HEAD4)
P1024x4 = _lane_pieces(D_MODEL, N_BRANCH)


def _gate_pieces(direction):
    la = [(direction * N_HEAD4 + h, 1) for h in range(N_HEAD4)]
    beta = [(8 + direction * N_HEAD4 + h, 1) for h in range(N_HEAD4)]
    return la, beta


RET_PIECES = [(i, 1) for i in range(2 * N_HEAD4)]


def _prep_io(g, z, sp, tabs):
    args = [a_ztok(g, z, "att_q", 512, gdtype=BF16), a_ztok(g, z, "att_k", 128, gdtype=BF16),
            a_ztok(g, z, "ret_q", 256, gdtype=BF16), a_ztok(g, z, "ret_k", 256, gdtype=BF16),
            a_ztok(g, z, "narrow", 128, gdtype=BF16),
            a_par(sp["qgain"]), a_par(sp["kgain"]), a_par(sp["alog"]), a_par(sp["dtb"]),
            a_tab(g, tabs[0]), a_tab(g, tabs[1])]
    outs = [o_tok(g, 512, BF16), o_tok(g, 128, BF16), o_tok(g, 256, F32), o_tok(g, 256, F32), o_tok(g, 128, F32)]
    return args, outs


def _attn_io(g, q_att, k_att, z, latent):
    col = ZOFF["att_v"] // 128
    first = lambda ids: ids[1] == 0
    if latent:
        rows, off, nq = g.t, g.nctx_q, (g.t - g.lc) // g.tq
    else:
        rows, off, nq = g.lc, 0, g.nctx_q
    args = [Arg(q_att, (None, g.tq, 512), lambda b, j: (b, j + off, 0), pieces=P64x8,
                gshape=(g.b, nq * g.tq, 512), gimap=lambda b, j: (b, j, 0)),
            Arg(k_att, (None, rows, 128), lambda b, j: (b, 0, 0), first=first, pieces=P64x2, gshape=(g.b, rows, 128)),
            Arg(z, (None, rows, 128), lambda b, j: (b, 0, col), first=first, pieces=P64x2,
                gshape=(g.b, rows, 128), gimap=lambda b, j: (b, 0, 0))]
    outs = [Out((g.b, nq * g.tq, 512), BF16, (None, g.tq, 512), lambda b, j: (b, j, 0), P64x8)]
    return (g.b, nq), args, outs


def _pad_rows(x, t):
    return jnp.pad(x, ((0, 0), (0, t - x.shape[1]), (0, 0)))


def _colgrid_arg(g, z, name, width_total, **kw):
    col = ZOFF[name] // 128
    return Arg(z, (None, g.t, 128), lambda h, b: (b, 0, col + h),
               gshape=(g.b, g.t, width_total), gimap=lambda h, b: (b, 0, h), **kw)


def _dnprep_io(g, z, sp):
    wfirst = lambda ids: ids[1] == 0
    args = [_colgrid_arg(g, z, "dn_q", 512, gdtype=BF16), _colgrid_arg(g, z, "dn_k", 512, gdtype=BF16),
            _colgrid_arg(g, z, "dn_v", 512, gdtype=BF16)]
    for i in range(3):
        args.append(Arg(sp["dn_conv"][i], (3, 128), lambda h, b: (0, h), first=wfirst))
    outs = [Out((g.b, g.t, 512), F32, (None, g.t, 128), lambda h, b: (b, 0, h)) for _ in range(3)]
    return (N_HEAD4, g.b), args, outs


def _shortconv_io(g, z, sp):
    args = [_colgrid_arg(g, z, "sc_b", 512, gdtype=BF16), _colgrid_arg(g, z, "sc_c", 512, gdtype=BF16),
            _colgrid_arg(g, z, "sc_x", 512, gdtype=BF16),
            Arg(sp["sc_conv"], (3, 128), lambda h, b: (0, h), first=lambda ids: ids[1] == 0)]
    outs = [Out((g.b, g.t, 512), BF16, (None, g.t, 128), lambda h, b: (b, 0, h))]
    return (BRANCH_W // 128, g.b), args, outs


def _finish_io(g, o_dn_f, o_dn_b, z, o_rt_f, o_rt_b, sp):
    args = [a_tok(g, o_dn_f, pieces=P128x4), a_tok(g, o_dn_b, pieces=P128x4),
            a_ztok(g, z, "dn_z", 512, gdtype=BF16, pieces=P128x4),
            a_tok(g, o_rt_f, pieces=P128x4), a_tok(g, o_rt_b, pieces=P128x4),
            a_ztok(g, z, "ret_g", 512, gdtype=BF16, pieces=P128x4), a_par(sp["ngain"])]
    outs = [o_tok(g, 512, BF16, pieces=P128x4), o_tok(g, 512, BF16, pieces=P128x4)]
    return args, outs


def _merge_io(g, ps, z):
    args = [a_tok(g, p, gdtype=BF16) for p in ps] + [a_ztok(g, z, "gates", 4096, gdtype=BF16, pieces=P1024x4)]
    return args, [o_tok(g, D_MODEL, BF16)]


def _dn_dirs(qn, kn, vn, gates):
    dirs = []
    for d in range(2):
        la, beta = _gate_pieces(d)
        dirs.append([Tok(qn, P128x4), Tok(kn, P128x4), Tok(vn, P128x4), Tok(gates, la), Tok(gates, beta)])
    return dirs


def _ret_dirs(q_ret, k_ret, z):
    col = ZOFF["ret_v"] // 512
    return [[Tok(q_ret, P64x4), Tok(k_ret, P64x4), Tok(z, P128x4, 512, col)] for _ in range(2)]


DN_STATE = (N_HEAD4, HEAD128, HEAD128)
RET_STATE = (N_HEAD4, HEAD64, HEAD128)
SCAN_OUT = [(512, F32, P128x4)]


def layer_fwd(g, x, h, w, sp, mod_a, mod_b, tabs, g_next, comm=None, rest=None):
    tp = {"x": x, "h": h}
    m, lc = g.m, g.lc
    z = matmul("win", h.reshape(m, D_MODEL), w["win"], epilogue=lambda p: (p,), out_dtypes=(BF16,))[0].reshape(g.b, g.t, NZ)
    tp["z"] = z
    args, outs = _prep_io(g, z, sp, tabs)
    q_att, k_att, q_ret, k_ret, gates = vfwd("prep", f_prep, g.grid, args, outs)
    tp.update(q_att=q_att, k_att=k_att, q_ret=q_ret, k_ret=k_ret, gates=gates)
    y_att = []
    for latent in (False, True):
        agrid, args, outs = _attn_io(g, q_att, k_att, z, latent)
        ride = ("gather", rest) if latent and rest is not None else None
        y, *gathered = vfwd("attn", make_f_attn(g.tq), agrid, args, outs, ride)
        y_att.append(y)
        if gathered:
            w = {**w, **weights_of_slots(gathered, 1)}
    tp["w"] = w
    y_att = jnp.concatenate(y_att, axis=1)
    dgrid, args, outs = _dnprep_io(g, z, sp)
    qn, kn, vn = vfwd("dnprep", make_f_dnprep(g.t, lc), dgrid, args, outs)
    tp.update(qn=qn, kn=kn, vn=vn)
    inv_shape = (2 * _scan_nb(g.b), ROWS, ROWS)
    o_f, s_f, o_b, s_b, tp["dn_inv"], *carried = scan2_fwd("dnscan", dn_chunks, _dn_dirs(qn, kn, vn, gates), [], DN_STATE,
                                                            SCAN_OUT, lc, comm, [inv_shape])
    tp["carried"] = carried if carried else None
    o_dn, tp["dn_s"] = [o_f, o_b], [s_f, s_b]
    o_f, s_f, o_b, s_b = scan2_fwd("retscan", ret_chunks, _ret_dirs(q_ret, k_ret, z), [(sp["ret"], RET_PIECES)], RET_STATE, SCAN_OUT, lc)
    o_rt, tp["rt_s"] = [o_f, o_b], [s_f, s_b]
    tp.update(o_dn=o_dn, o_rt=o_rt)
    args, outs = _finish_io(g, o_dn[0], o_dn[1], z, o_rt[0], o_rt[1], sp)
    y_dn, y_rt = vfwd("finish", f_finish, g.grid, args, outs)
    sgrid, args, outs = _shortconv_io(g, z, sp)
    (y_sc,) = vfwd("shortconv", make_f_shortconv(g.t, lc), sgrid, args, outs)
    ys = [y_att, y_dn, y_rt, y_sc]
    tp["ys"] = ys
    ps = [matmul("wbranch", y.reshape(m, BRANCH_W), w["wbr"][i], epilogue=lambda p: (p,), out_dtypes=(BF16,))[0]
          .reshape(g.b, g.t, D_MODEL) for i, y in enumerate(ys)]
    tp["ps"] = ps
    args, outs = _merge_io(g, ps, z)
    (u,) = vfwd("merge", f_merge, g.grid, args, outs)
    tp["u"] = u
    y = matmul("wout", u.reshape(m, D_MODEL), w["wout"]).reshape(g.b, g.t, D_MODEL)
    tp["y"] = y
    args = [a_tok(g, x), a_tok(g, y), a_par(sp["g1"]), a_par(sp["g2"]), a_mod(g, mod_a)]
    x1, h2 = vfwd("resnorm", f_resnorm, g.grid, args, [o_tok(g, D_MODEL, F32), o_tok(g, D_MODEL, BF16)])
    tp.update(x1=x1, h2=h2)
    tp["a"], tp["r"] = matmul("wmlp1", h2.reshape(m, D_MODEL), w["w1"], epilogue=lambda p: (p, jnp.square(jnp.maximum(p, 0.0))),
                              out_dtypes=(F32, BF16))
    mo = matmul("wmlp2", tp["r"], w["w2"]).reshape(g.b, g.t, D_MODEL)
    tp["mo"] = mo
    if g_next is None:
        args = [a_tok(g, x1), a_tok(g, mo), a_par(sp["g3"]), a_mod(g, mod_b)]
        (x2,) = vfwd("resid", f_resid, g.grid, args, [o_tok(g, D_MODEL, F32)])
        return x2, None, tp
    args = [a_tok(g, x1), a_tok(g, mo), a_par(sp["g3"]), a_par(g_next), a_mod(g, mod_b)]
    x2, h_next = vfwd("resnorm", f_resnorm, g.grid, args, [o_tok(g, D_MODEL, F32), o_tok(g, D_MODEL, BF16)])
    return x2, h_next, tp


def layer_bwd(g, tp, sp, mod_a, mod_b, tabs, g_next, dx2, dh_next, send=None):
    m, lc = g.m, g.lc
    w = tp["w"]
    gw, gs = {}, {}
    x, z = tp["x"], tp["z"]
    if g_next is None:
        args = [a_tok(g, tp["x1"]), a_tok(g, tp["mo"]), a_par(sp["g3"]), a_mod(g, mod_b)]
        dx1, dmo, gs["g3"], dmod_b = vbwd("resid_b", f_resid, g.grid, args, [o_tok(g, D_MODEL, F32)], [[dx2]])
    else:
        args = [a_tok(g, tp["x1"]), a_tok(g, tp["mo"]), a_par(sp["g3"]), a_par(g_next), a_mod(g, mod_b)]
        dx1, dmo, gs["g3"], gs["g0_next"], dmod_b = vbwd(
            "resnorm_b", f_resnorm, g.grid, args, [o_tok(g, D_MODEL, F32), o_tok(g, D_MODEL, BF16)], [[dx2], [dh_next]])
    dmo2 = dmo.reshape(m, D_MODEL)
    (da2,) = matmul("wmlp2_dx", dmo2, w["w2"], trans_b=True, extras=[tp["a"]],
                    epilogue=lambda p, a: (p * (2.0 * jnp.maximum(a, 0.0)),), out_dtypes=(BF16,))
    gw["w2"] = matmul("wmlp2_dw", tp["r"], dmo2, trans_a=True)
    dh2 = matmul("wmlp1_dx", da2, w["w1"], trans_b=True).reshape(g.b, g.t, D_MODEL)
    gw["w1"] = matmul("wmlp1_dw", tp["h2"].reshape(m, D_MODEL), da2, trans_a=True)
    args = [a_tok(g, x), a_tok(g, tp["y"]), a_par(sp["g1"]), a_par(sp["g2"]), a_mod(g, mod_a)]
    dx, dy, gs["g1"], gs["g2"], dmod_a = vbwd(
        "resnorm_b", f_resnorm, g.grid, args, [o_tok(g, D_MODEL, F32), o_tok(g, D_MODEL, BF16)], [[dx1], [dh2]])
    dy2 = dy.reshape(m, D_MODEL)
    du = matmul("wout_dx", dy2, w["wout"], trans_b=True).reshape(g.b, g.t, D_MODEL)
    gw["wout"] = matmul("wout_dw", tp["u"].reshape(m, D_MODEL), dy2, trans_a=True)
    args, outs = _merge_io(g, tp["ps"], z)
    *dps, dz_gates = vbwd("merge_b", f_merge, g.grid, args, outs, [[du]])
    dys, gwbr = [], []
    for i in range(N_BRANCH):
        dp2 = dps[i].reshape(m, D_MODEL)
        dys.append(matmul("wbranch_dx", dp2, w["wbr"][i], trans_b=True).reshape(g.b, g.t, BRANCH_W))
        gwbr.append(matmul("wbranch_dw", tp["ys"][i].reshape(m, BRANCH_W), dp2, trans_a=True))
    gw["wbr"] = jnp.stack(gwbr)
    dy_att, dy_dn, dy_rt, dy_sc = dys
    o_dn, o_rt = tp["o_dn"], tp["o_rt"]
    args, outs = _finish_io(g, o_dn[0], o_dn[1], z, o_rt[0], o_rt[1], sp)
    do_dn_f, do_dn_b, dz_dnz, do_rt_f, do_rt_b, dz_retg, gs["ngain"] = vbwd(
        "finish_b", f_finish, g.grid, args, outs, [[dy_dn], [dy_rt]])
    comm = ("scatter", list(send) + grad_slots(gw, WKEYS[1:])) if send is not None else None
    res = scan2_bwd("dnscan_b", dn_chunks, _dn_dirs(tp["qn"], tp["kn"], tp["vn"], tp["gates"]), [], DN_STATE, SCAN_OUT, lc,
                   tp["dn_s"], [[do_dn_f], [do_dn_b]], comm, [tp["dn_inv"]])
    dqn, dkn, dvn, dgates = [res[0], res[5]], [res[1], res[6]], [res[2], res[7]], [res[3], res[4], res[8], res[9]]
    received = list(res[10:]) if comm is not None else None
    swap = ("swap", received) if comm is not None else None
    res = scan2_bwd("retscan_b", ret_chunks, _ret_dirs(tp["q_ret"], tp["k_ret"], z), [(sp["ret"], RET_PIECES)], RET_STATE, SCAN_OUT, lc,
                   tp["rt_s"], [[do_rt_f], [do_rt_b]], swap)
    dq_ret, dk_ret, gs["ret"] = [res[0], res[3]], [res[1], res[4]], res[6]
    carried = (received, list(res[7:])) if comm is not None else None
    dz_retv = addn("sum_retv", [res[2], res[5]], BF16)
    sgrid, args, outs = _shortconv_io(g, z, sp)
    dz_scb, dz_scc, dz_scx, gs["sc_conv"] = vbwd("shortconv_b", make_f_shortconv(g.t, lc), sgrid, args, outs, [[dy_sc]])
    dq_att, dk_att, dv_att = [], [], []
    for latent, dy in ((False, dy_att[:, :lc]), (True, dy_att[:, lc:])):
        agrid, args, outs = _attn_io(g, tp["q_att"], tp["k_att"], z, latent)
        dq, dk, dv = vbwd("attn_b", make_f_attn(g.tq), agrid, args, outs, [[dy]])
        dq_att.append(dq); dk_att.append(_pad_rows(dk, g.t)); dv_att.append(_pad_rows(dv, g.t))
    dq_att = jnp.concatenate(dq_att, axis=1)
    dz_attv = addn("sum_attv", dv_att, BF16)
    dgrid, args, outs = _dnprep_io(g, z, sp)
    dz_dnq, dz_dnk, dz_dnv, gc_q, gc_k, gc_v = vbwd("dnprep_b", make_f_dnprep(g.t, lc), dgrid, args, outs, [dqn, dkn, dvn])
    gs["dn_conv"] = [gc_q, gc_k, gc_v]
    args, outs = _prep_io(g, z, sp, tabs)
    dz_attq, dz_attk, dz_retq, dz_retk, dz_nar, gs["qgain"], gs["kgain"], gs["alog"], gs["dtb"] = vbwd(
        "prep_b", f_prep, g.grid, args, outs, [[dq_att], dk_att, dq_ret, dk_ret, dgates])
    pad = jnp.zeros((g.b, g.t, NZ - (ZOFF["narrow"] + 128)), BF16)
    dz = jnp.concatenate([dz_gates, dz_attq, dz_dnq, dz_dnk, dz_dnv, dz_dnz, dz_retv, dz_retg, dz_scb, dz_scc, dz_scx,
                          dz_retq, dz_retk, dz_attk, dz_attv, dz_nar, pad], axis=-1)
    dz2 = dz.reshape(m, NZ)
    dh = matmul("win_dx", dz2, w["win"], trans_b=True).reshape(g.b, g.t, D_MODEL)
    gw["win"] = matmul("win_dw", tp["h"].reshape(m, D_MODEL), dz2, trans_a=True)
    return dx, dh, gw, gs, dmod_a, dmod_b, carried


def loss_call(g, xa, tgt):
    n = g.nctx
    inv_d = 1.0 / D_MODEL

    def body(x_ref, t_ref, loss_ref, dx_ref):
        j = pl.program_id(1)

        @pl.when((pl.program_id(0) == 0) & (j == 0))
        def _():
            loss_ref[...] = jnp.zeros(loss_ref.shape, F32)

        @pl.when(j < n)
        def _():
            dx_ref[...] = jnp.zeros(dx_ref.shape, F32)

        @pl.when(j >= n)
        def _():
            e = x_ref[...] - t_ref[...]
            dx_ref[...] = e * inv_d
            s = jnp.sum(jnp.sum(e * e, axis=1, keepdims=True), axis=0, keepdims=True)
            loss_ref[...] += jnp.broadcast_to(0.5 * inv_d * s, loss_ref.shape)

    tok = pl.BlockSpec((None, g.tm, D_MODEL), lambda b, j: (b, j, 0))
    return pl.pallas_call(
        body, name="loss", grid=g.grid,
        in_specs=[tok, pl.BlockSpec((None, g.tm, D_MODEL), lambda b, j: (b, jnp.maximum(j - n, 0), 0))],
        out_specs=[pl.BlockSpec((1, LANES), lambda b, j: (0, 0)), tok],
        out_shape=[jax.ShapeDtypeStruct((1, LANES), F32), jax.ShapeDtypeStruct((g.b, g.t, D_MODEL), F32)],
        compiler_params=_params(2),
    )(xa, tgt)


def rope_tables(g):
    seq = g.t - g.lc
    rows = seq // GRID_W
    r, col = jnp.meshgrid(jnp.arange(rows), jnp.arange(GRID_W), indexing="ij")
    quarter = HEAD64 // 4
    inv_freq = ROPE_THETA ** (-jnp.arange(quarter, dtype=F32) / quarter)
    ang = jnp.concatenate([r.reshape(-1, 1).astype(F32) * inv_freq, col.reshape(-1, 1).astype(F32) * inv_freq], axis=-1)
    cos, sin = jnp.cos(ang), jnp.sin(ang)
    cos = jnp.concatenate([jnp.ones((g.lc, HEAD64 // 2), F32), cos], axis=0)
    sin = jnp.concatenate([jnp.zeros((g.lc, HEAD64 // 2), F32), sin], axis=0)
    reps = 512 // (HEAD64 // 2)
    return jnp.tile(cos, (1, reps)), jnp.tile(sin, (1, reps))


def _row128(v):
    v = v.reshape(1, -1).astype(F32)
    return jnp.pad(v, ((0, 0), (0, LANES - v.shape[1])))


def layer_small(small, l):
    gn = small["g_norm"][l]
    return {
        "g0": gn[0:1], "g1": gn[1:2], "g2": gn[2:3], "g3": gn[3:4],
        "qgain": small["att_q_gain"][l][None], "kgain": small["att_k_gain"][l][None],
        "dn_conv": [small["dn_conv"][l][:, i * 512:(i + 1) * 512] for i in range(3)], "alog": _row128(small["dn_a_log"][l]), "dtb": _row128(small["dn_dt_bias"][l]),
        "ngain": small["dn_norm_gain"][l][None], "ret": _row128(small["ret_decay"][l]), "sc_conv": small["sc_conv"][l],
    }


def permute_win(w):
    parts = [w[..., off:off + width] for _, off, width in _SEGS]
    parts.append(jnp.zeros(w.shape[:-1] + (NZ - N_IN,), w.dtype))
    return jnp.concatenate(parts, axis=-1)


def unpermute_win(gw):
    order = sorted(_SEGS, key=lambda s: s[1])
    return jnp.concatenate([gw[..., ZOFF[name]:ZOFF[name] + width] for name, _, width in order], axis=-1)


BIG = (("w_in", -1), ("w_branch", -1), ("w_out", -2), ("w_mlp_in", -1), ("w_mlp_out", -2))
WKEYS = ("win", "wbr", "wout", "w1", "w2")


def weights_of_slots(slots, first=0):
    out = {}
    for s, key, (_, axis) in zip(slots, WKEYS[first:], BIG[first:]):
        if axis == -2:
            full = s.reshape((-1,) + s.shape[2:])
        else:
            full = jnp.concatenate([s[p] for p in range(N_XY)], axis=axis)
        out[key] = permute_win(full) if key == "win" else full
    return out


def grad_slots(gw, keys):
    out = []
    for key in keys:
        full = unpermute_win(gw[key]) if key == "win" else gw[key]
        axis = BIG[WKEYS.index(key)][1]
        if axis == -2:
            out.append(full.reshape((N_XY, -1) + full.shape[1:]).astype(BF16))
        else:
            out.append(jnp.stack(jnp.split(full, N_XY, axis=axis)).astype(BF16))
    return out


def model_step(g, xa, tgt, mod, shards, small, local=False):
    depth = len(shards)
    tabs = rope_tables(g)
    sps = [layer_small(small, l) for l in range(depth)]
    mods_a = [mod[l][:, :, 2:5] for l in range(depth)]
    mods_b = [jnp.concatenate([mod[l][:, :, 5:6], mod[l + 1][:, :, 0:2]], axis=2) if l + 1 < depth else mod[l][:, :, 5:6]
              for l in range(depth)]
    mod0 = mod[0][:, :, 0:2]
    args0 = [a_tok(g, xa), a_par(sps[0]["g0"]), a_mod(g, mod0)]
    (h,) = vfwd("modnorm", f_modnorm, g.grid, args0, [o_tok(g, D_MODEL, BF16)])
    x, tapes = xa, []
    if local:
        w, rest = weights_of_slots(shards[0]), None
    else:
        w, rest = weights_of_slots(exchange("gather_win", "gather", shards[0][:1])), shards[0][1:]
    for l in range(depth):
        g_next = sps[l + 1]["g0"] if l + 1 < depth else None
        comm = ("gather", shards[l + 1]) if l + 1 < depth and not local else None
        x, h, tp = layer_fwd(g, x, h, w, sps[l], mods_a[l], mods_b[l], tabs, g_next, comm, rest)
        if l + 1 < depth:
            w, rest = weights_of_slots(shards[l + 1] if local else tp["carried"]), None
        tapes.append(tp)
    loss_row, dx = loss_call(g, x, tgt)
    dh, gss, dmods = None, [None] * depth, [None] * depth
    mine, theirs = [[None] * len(BIG) for _ in range(depth)], [[None] * len(BIG) for _ in range(depth)]
    win_slots = None
    for l in reversed(range(depth)):
        g_next = sps[l + 1]["g0"] if l + 1 < depth else None
        send = None if local else ([win_slots] if win_slots is not None else [])
        dx, dh, gw, gss[l], dma, dmb, carried = layer_bwd(g, tapes[l], sps[l], mods_a[l], mods_b[l], tabs, g_next, dx, dh, send)
        (win_new,) = grad_slots(gw, WKEYS[:1])
        if local:
            mine[l] = [win_new] + grad_slots(gw, WKEYS[1:])
        else:
            received, swapped = carried
            if win_slots is not None:
                mine[l + 1][0], theirs[l + 1][0] = received[0], swapped[0]
            mine[l][1:], theirs[l][1:] = received[-4:], swapped[-4:]
        win_slots = win_new
        dmods[l] = (dma, dmb)
    if not local:
        mine[0][:1] = exchange("scatter_win", "scatter", [win_slots])
        theirs[0][:1] = exchange("swap_win", "swap", mine[0][:1])
    dxh, g0_first, dmod0 = vbwd("modnorm_b", f_modnorm, g.grid, args0, [o_tok(g, D_MODEL, BF16)], [[dh]])
    dxa = addn("sum_dx", [dx, dxh], F32)
    dmod = []
    for l in range(depth):
        first2 = dmod0 if l == 0 else dmods[l - 1][1][:, :, 1:3]
        dmod.append(jnp.concatenate([first2, dmods[l][0], dmods[l][1][:, :, 0:1]], axis=2))
    dmod = jnp.stack(dmod)
    def rows(key, n):
        return jnp.stack([gs[key][0, :n] for gs in gss])
    g_norm = jnp.stack([jnp.concatenate([g0_first if l == 0 else gss[l - 1]["g0_next"], gss[l]["g1"], gss[l]["g2"], gss[l]["g3"]], axis=0)
                        for l in range(depth)])
    gsmall = {
        "g_norm": g_norm,
        "att_q_gain": jnp.stack([gs["qgain"][0] for gs in gss]), "att_k_gain": jnp.stack([gs["kgain"][0] for gs in gss]),
        "dn_conv": jnp.stack([jnp.concatenate(gs["dn_conv"], axis=1) for gs in gss]),
        "dn_a_log": rows("alog", 8).reshape(depth, 2, N_HEAD4), "dn_dt_bias": rows("dtb", 8).reshape(depth, 2, N_HEAD4),
        "dn_norm_gain": jnp.stack([gs["ngain"][0] for gs in gss]),
        "ret_decay": rows("ret", 8).reshape(depth, 2, N_HEAD4),
        "sc_conv": jnp.stack([gs["sc_conv"] for gs in gss]),
    }
    return loss_row[0, 0], dxa, mine, theirs, gsmall, dmod


N_DEV = 8
N_XY = 4
HBM_SPEC = pl.BlockSpec(memory_space=pltpu.HBM)
VMEM_SPEC = pl.BlockSpec(memory_space=pltpu.VMEM)


def _coords():
    return lax.axis_index("x"), lax.axis_index("y"), lax.axis_index("c")


def _flip(coords, k):
    x, y, c = coords
    return (1 - x if k & 4 else x, 1 - y if k & 2 else y, 1 - c if k & 1 else c)


def allgather8(name, v):
    def body(v_ref, out_ref, send_sems, recv_sems, local_sem):
        me3 = _coords()
        me = 4 * me3[0] + 2 * me3[1] + me3[2]
        mine = pltpu.make_async_copy(v_ref, out_ref.at[me], local_sem)
        mine.start()
        sends = []
        for k in range(1, N_DEV):
            cp = pltpu.make_async_remote_copy(src_ref=v_ref, dst_ref=out_ref.at[me], send_sem=send_sems.at[k - 1],
                                              recv_sem=recv_sems.at[k - 1], device_id=_flip(me3, k), device_id_type=MESH)
            cp.start()
            sends.append(cp)
        for k in range(1, N_DEV):
            pltpu.make_async_remote_copy(src_ref=v_ref, dst_ref=out_ref.at[jnp.bitwise_xor(me, k)], send_sem=send_sems.at[k - 1],
                                         recv_sem=recv_sems.at[k - 1], device_id=_flip(me3, k), device_id_type=MESH).wait_recv()
        for cp in sends:
            cp.wait_send()
        mine.wait()

    return pl.pallas_call(
        body, name=name, out_shape=jax.ShapeDtypeStruct((N_DEV,) + v.shape, v.dtype),
        in_specs=[VMEM_SPEC], out_specs=VMEM_SPEC,
        scratch_shapes=[pltpu.SemaphoreType.DMA((N_DEV - 1,)), pltpu.SemaphoreType.DMA((N_DEV - 1,)), pltpu.SemaphoreType.DMA],
        compiler_params=pltpu.CompilerParams(vmem_limit_bytes=VMEM_LIMIT),
    )(v)


def exchange(name, kind, arrays):
    n = len(arrays)

    def body(*refs):
        xrefs = (refs[:n], refs[n:2 * n]) + tuple(refs[2 * n:])
        _exchange_start(xrefs, kind)
        _exchange_wait(xrefs, kind)

    return pl.pallas_call(
        body, name=name, out_shape=[_exchange_shape(kind, a) for a in arrays], in_specs=[HBM_SPEC] * n,
        out_specs=[HBM_SPEC] * n, scratch_shapes=_exchange_sems(n),
    )(*arrays)


def _exchange_shape(kind, a):
    return jax.ShapeDtypeStruct((N_XY,) + a.shape if kind == "gather" else a.shape, a.dtype)


def _exchange_sems(n):
    return [pltpu.SemaphoreType.DMA((n, N_XY - 1)), pltpu.SemaphoreType.DMA((n, N_XY - 1)), pltpu.SemaphoreType.DMA((n,))]


def _exchange_copies(xrefs, kind):
    srcs, dsts, send_sems, recv_sems, local_sems = xrefs
    me3 = _coords()
    me = 2 * me3[0] + me3[1]
    locals_, sends, recvs = [], [], []
    for i, (v_ref, out_ref) in enumerate(zip(srcs, dsts)):
        if kind == "swap":
            cp = pltpu.make_async_remote_copy(src_ref=v_ref, dst_ref=out_ref, send_sem=send_sems.at[i, 0],
                                              recv_sem=recv_sems.at[i, 0], device_id=_flip(me3, 1), device_id_type=MESH)
            sends.append(cp)
            recvs.append(cp)
            continue
        src = (lambda s, r=v_ref: r) if kind == "gather" else (lambda s, r=v_ref: r.at[s])

        def remote(k, src_slot, dst_slot):
            return pltpu.make_async_remote_copy(src_ref=src(src_slot), dst_ref=out_ref.at[dst_slot], send_sem=send_sems.at[i, k - 1],
                                                recv_sem=recv_sems.at[i, k - 1], device_id=_flip(me3, 2 * k), device_id_type=MESH)

        locals_.append(pltpu.make_async_copy(src(me), out_ref.at[me], local_sems.at[i]))
        sends += [remote(k, jnp.bitwise_xor(me, k), me) for k in range(1, N_XY)]
        recvs += [remote(k, me, jnp.bitwise_xor(me, k)) for k in range(1, N_XY)]
    return locals_, sends, recvs


def _exchange_start(xrefs, kind):
    locals_, sends, _ = _exchange_copies(xrefs, kind)
    for cp in locals_ + sends:
        cp.start()


def _exchange_wait(xrefs, kind):
    locals_, sends, recvs = _exchange_copies(xrefs, kind)
    for cp in recvs:
        cp.wait_recv()
    for cp in sends:
        cp.wait_send()
    for cp in locals_:
        cp.wait()


BLOCK_BYTES = 1 << 20


def _rows_block(rows, cols, limit=BLOCK_BYTES):
    for tr in (1024, 512, 256, 128, 64, 32, 16, 8):
        if rows % tr == 0 and tr * cols * 4 <= limit:
            return tr
    return rows


def adamw(name, w, m, v, parts):
    rows, cols = w.shape
    tr = _rows_block(rows, cols)
    n = len(parts)
    c1 = 1.0 - ADAM_B1 ** ADAM_STEP
    c2 = 1.0 - ADAM_B2 ** ADAM_STEP

    def body(*refs):
        w_ref, m_ref, v_ref = refs[:3]
        g_ref, d_ref, nm_ref, nv_ref = refs[3 + n:]
        g = refs[3][...]
        for r in refs[4:3 + n]:
            g = g + r[...]
        nm = ADAM_B1 * m_ref[...] + (1.0 - ADAM_B1) * g
        nv = ADAM_B2 * v_ref[...] + (1.0 - ADAM_B2) * jnp.square(g)
        d_ref[...] = -ADAM_LR * ((nm / c1) / (jnp.sqrt(nv / c2) + ADAM_EPS) + ADAM_WD * w_ref[...])
        g_ref[...], nm_ref[...], nv_ref[...] = g, nm, nv

    spec = pl.BlockSpec((tr, cols), lambda i: (i, 0))
    sds = jax.ShapeDtypeStruct((rows, cols), F32)
    return pl.pallas_call(
        body, name=name, grid=(rows // tr,), in_specs=[spec] * (3 + n), out_specs=[spec] * 4, out_shape=[sds] * 4,
        compiler_params=_params(1),
    )(w, m, v, *parts)


def adamw_slots(name, w, m, v, mine, theirs):
    depth = len(mine)
    rows, cols = mine[0].shape[1:]
    tr = _rows_block(rows, cols, BLOCK_BYTES // 2)
    nblk = rows // tr
    c1 = 1.0 - ADAM_B1 ** ADAM_STEP
    c2 = 1.0 - ADAM_B2 ** ADAM_STEP

    def body(*refs):
        w_ref, m_ref, v_ref = refs[:3]
        slot_refs = refs[3:3 + 2 * depth]
        g_ref, d_ref, nm_ref, nv_ref = refs[3 + 2 * depth:]
        for j in range(depth):
            @pl.when(pl.program_id(0) == j)
            def _():
                def total(ref):
                    tot = ref[0].astype(F32)
                    for s in range(1, N_XY):
                        tot = tot + ref[s].astype(F32)
                    return tot
                g = total(slot_refs[j]) + total(slot_refs[depth + j])
                nm = ADAM_B1 * m_ref[...] + (1.0 - ADAM_B1) * g
                nv = ADAM_B2 * v_ref[...] + (1.0 - ADAM_B2) * jnp.square(g)
                d_ref[...] = -ADAM_LR * ((nm / c1) / (jnp.sqrt(nv / c2) + ADAM_EPS) + ADAM_WD * w_ref[...])
                g_ref[...], nm_ref[...], nv_ref[...] = g, nm, nv

    spec = pl.BlockSpec((tr, cols), lambda l, i: (l * nblk + i, 0))
    slot_specs = [pl.BlockSpec((N_XY, tr, cols), lambda l, i, j=j: (0, jnp.where(l == j, i, 0), 0)) for j in range(depth)] * 2
    sds = jax.ShapeDtypeStruct((depth * rows, cols), F32)
    return pl.pallas_call(
        body, name=name, grid=(depth, nblk), in_specs=[spec] * 3 + slot_specs, out_specs=[spec] * 4, out_shape=[sds] * 4,
        compiler_params=_params(2),
    )(w, m, v, *mine, *theirs)


MOD_COLS = 512


def mod_fwd(call, w_mod, b_sh):
    depth, _, cols = w_mod.shape
    nr = call.shape[0]

    def body(c_ref, w_ref, b_ref, o_ref):
        o_ref[...] = _dg(_silu(c_ref[...]), w_ref[...], 1, 0) + b_ref[...]

    return pl.pallas_call(
        body, name="mod_fwd", grid=(depth, cols // MOD_COLS),
        in_specs=[pl.BlockSpec((nr, D_MODEL), lambda l, j: (0, 0)), pl.BlockSpec((None, D_MODEL, MOD_COLS), lambda l, j: (l, 0, j)),
                  pl.BlockSpec((None, 1, MOD_COLS), lambda l, j: (l, 0, j))],
        out_specs=pl.BlockSpec((None, nr, MOD_COLS), lambda l, j: (l, 0, j)),
        out_shape=jax.ShapeDtypeStruct((depth, nr, cols), F32), compiler_params=_params(2),
    )(call, w_mod, b_sh)


def mod_bwd(call, c_ctx, d_lat, d_ctx, w_mod, ctx_row):
    depth, _, cols = w_mod.shape
    nr, ns = call.shape[0], d_ctx.shape[1]

    def body(c_ref, cc_ref, dl_ref, dc_ref, w_ref, gw_ref, gc_ref):
        crow = jnp.sum(dc_ref[...], axis=0, keepdims=True)
        row = lax.broadcasted_iota(jnp.int32, (nr, 1), 0)
        dm = jnp.where(row == ctx_row, crow, dl_ref[...])
        gw_ref[...] = _dg(_silu(c_ref[...]), dm, 0, 0)
        ds = jnp.sum(_dg(jnp.broadcast_to(crow, (8, MOD_COLS)), w_ref[...], 1, 1), axis=0, keepdims=True) * 0.125
        _, vjp = jax.vjp(_silu, cc_ref[...])
        (part,) = vjp(ds)
        first = (pl.program_id(0) == 0) & (pl.program_id(1) == 0)

        @pl.when(first)
        def _():
            gc_ref[...] = part

        @pl.when(jnp.logical_not(first))
        def _():
            gc_ref[...] += part

    return pl.pallas_call(
        body, name="mod_bwd", grid=(depth, cols // MOD_COLS),
        in_specs=[pl.BlockSpec((nr, D_MODEL), lambda l, j: (0, 0)), pl.BlockSpec((1, D_MODEL), lambda l, j: (0, 0)),
                  pl.BlockSpec((None, nr, MOD_COLS), lambda l, j: (l, 0, j)), pl.BlockSpec((None, ns, MOD_COLS), lambda l, j: (l, 0, j)),
                  pl.BlockSpec((None, D_MODEL, MOD_COLS), lambda l, j: (l, 0, j))],
        out_specs=[pl.BlockSpec((None, D_MODEL, MOD_COLS), lambda l, j: (l, 0, j)), pl.BlockSpec((1, D_MODEL), lambda l, j: (0, 0))],
        out_shape=[jax.ShapeDtypeStruct((depth, D_MODEL, cols), F32), jax.ShapeDtypeStruct((1, D_MODEL), F32)],
        compiler_params=_params(2),
    )(call, c_ctx, d_lat, d_ctx, w_mod)


def bmod_grad(dm_all):
    ndev, depth, ns, cols = dm_all.shape

    def body(d_ref, o_ref):
        tot = d_ref[0]
        for i in range(1, ndev):
            tot = tot + d_ref[i]
        o_ref[...] = jnp.sum(tot, axis=0, keepdims=True)

    return pl.pallas_call(
        body, name="bmod_grad", grid=(depth,), in_specs=[pl.BlockSpec((ndev, None, ns, cols), lambda l: (0, l, 0, 0))],
        out_specs=pl.BlockSpec((None, 1, cols), lambda l: (l, 0, 0)), out_shape=jax.ShapeDtypeStruct((depth, 1, cols), F32),
        compiler_params=_params(1),
    )(dm_all)


def small_reduce(gathered, rows_all):
    ndev, rows, lanes = gathered.shape

    def body(g_ref, o_ref):
        tot = g_ref[0, 0:rows_all]
        for i in range(1, ndev):
            tot = tot + g_ref[i, 0:rows_all]
        o_ref[0:rows_all] = tot
        part = g_ref[0, rows_all:rows]
        for i in range(2, ndev, 2):
            part = part + g_ref[i, rows_all:rows]
        o_ref[rows_all:rows] = part

    return pl.pallas_call(body, name="small_reduce", out_shape=jax.ShapeDtypeStruct((rows, lanes), F32),
                          in_specs=[VMEM_SPEC], out_specs=VMEM_SPEC)(gathered)


def pack_rows(arrays, row_multiple=8):
    flat = jnp.concatenate([a.reshape(-1).astype(F32) for a in arrays])
    per = LANES * row_multiple
    padded = -(-flat.shape[0] // per) * per
    return jnp.pad(flat, (0, padded - flat.shape[0])).reshape(-1, LANES)


def unpack_rows(buf, shapes):
    flat, out, off = buf.reshape(-1), [], 0
    for s in shapes:
        n = int(np.prod(s))
        out.append(flat[off:off + n].reshape(s))
        off += n
    return out


SMALL_SHARDED = ("g_norm", "dn_conv", "sc_conv")
SMALL_ORDER = ("g_norm", "att_q_gain", "att_k_gain", "dn_conv", "dn_a_log", "dn_dt_bias", "dn_norm_gain", "ret_decay", "sc_conv")


def kernel(x, c, ctx, c_ctx, w_mod, b_mod, g_norm, w_in, att_q_gain, att_k_gain, dn_conv, dn_a_log, dn_dt_bias, dn_norm_gain, ret_decay, sc_conv, w_branch, w_out, w_mlp_in, w_mlp_out, loss_target, m_c_ctx, m_w_mod, m_b_mod, m_g_norm, m_w_in, m_att_q_gain, m_att_k_gain, m_dn_conv, m_dn_a_log, m_dn_dt_bias, m_dn_norm_gain, m_ret_decay, m_sc_conv, m_w_branch, m_w_out, m_w_mlp_in, m_w_mlp_out, v_c_ctx, v_w_mod, v_b_mod, v_g_norm, v_w_in, v_att_q_gain, v_att_k_gain, v_dn_conv, v_dn_a_log, v_dn_dt_bias, v_dn_norm_gain, v_ret_decay, v_sc_conv, v_w_branch, v_w_out, v_w_mlp_in, v_w_mlp_out):
    wts = dict(c_ctx=c_ctx, w_mod=w_mod, b_mod=b_mod, g_norm=g_norm, w_in=w_in, att_q_gain=att_q_gain, att_k_gain=att_k_gain,
               dn_conv=dn_conv, dn_a_log=dn_a_log, dn_dt_bias=dn_dt_bias, dn_norm_gain=dn_norm_gain, ret_decay=ret_decay,
               sc_conv=sc_conv, w_branch=w_branch, w_out=w_out, w_mlp_in=w_mlp_in, w_mlp_out=w_mlp_out)
    mom = dict(c_ctx=m_c_ctx, w_mod=m_w_mod, b_mod=m_b_mod, g_norm=m_g_norm, w_in=m_w_in, att_q_gain=m_att_q_gain,
               att_k_gain=m_att_k_gain, dn_conv=m_dn_conv, dn_a_log=m_dn_a_log, dn_dt_bias=m_dn_dt_bias,
               dn_norm_gain=m_dn_norm_gain, ret_decay=m_ret_decay, sc_conv=m_sc_conv, w_branch=m_w_branch, w_out=m_w_out,
               w_mlp_in=m_w_mlp_in, w_mlp_out=m_w_mlp_out)
    var = dict(c_ctx=v_c_ctx, w_mod=v_w_mod, b_mod=v_b_mod, g_norm=v_g_norm, w_in=v_w_in, att_q_gain=v_att_q_gain,
               att_k_gain=v_att_k_gain, dn_conv=v_dn_conv, dn_a_log=v_dn_a_log, dn_dt_bias=v_dn_dt_bias,
               dn_norm_gain=v_dn_norm_gain, ret_decay=v_ret_decay, sc_conv=v_sc_conv, w_branch=v_w_branch, w_out=v_w_out,
               w_mlp_in=v_w_mlp_in, w_mlp_out=v_w_mlp_out)
    names = list(wts)
    depth, bsz, seq, lc = w_mod.shape[0], x.shape[0], x.shape[1], ctx.shape[1]
    g = Geo(bsz, lc + seq, lc)
    xi, yi, ci = _coords()
    dev, xy = 4 * xi + 2 * yi + ci, 2 * xi + yi
    n_batch = N_DEV * bsz
    nr = -(-(n_batch + 1) // 16) * 16
    mod_cols = w_mod.shape[2]

    c_all = allgather8("gather_c", c).reshape(n_batch, D_MODEL)
    call = jnp.concatenate([c_all, c_ctx[None], jnp.zeros((nr - n_batch - 1, D_MODEL), F32)], axis=0)
    b_sh = lax.dynamic_slice_in_dim(b_mod, xy * mod_cols, mod_cols, axis=1)[:, None, :]
    mod_sh = mod_fwd(call, w_mod, b_sh)
    mod_g = allgather8("gather_mod", mod_sh.reshape(depth * nr, mod_cols)).reshape(N_XY, 2, depth, nr, mod_cols)[:, 0]
    mod_all = mod_g.transpose(1, 2, 0, 3).reshape(depth, nr, N_XY * mod_cols)
    mod_lat = lax.dynamic_slice_in_dim(mod_all, dev * bsz, bsz, axis=1).reshape(depth, bsz, 6, D_MODEL)
    mod_ctx = jnp.broadcast_to(mod_all[:, n_batch].reshape(depth, 1, 6, D_MODEL), (depth, bsz, 6, D_MODEL))
    mod = jnp.stack([mod_ctx, mod_lat], axis=2)

    sm_shapes = [wts[k].shape for k in SMALL_SHARDED]
    sm_g = allgather8("gather_small", pack_rows([wts[k] for k in SMALL_SHARDED])).reshape(N_XY, 2, -1)[:, 0]
    small = {k: wts[k] for k in SMALL_ORDER}
    for k, parts in zip(SMALL_SHARDED, zip(*[unpack_rows(sm_g[p], sm_shapes) for p in range(N_XY)])):
        small[k] = jnp.concatenate(parts, axis=-1)

    shards = [[wts[k][l].astype(BF16) for k, _ in BIG] for l in range(depth)]

    xa = jnp.concatenate([ctx, x], axis=1)
    loss_part, dxa, mine, theirs, gsmall, dmod = model_step(g, xa, loss_target, mod, shards, small)
    loss = lax.psum(loss_part, ("x", "y", "c"))
    grad_x = dxa[:, lc:]

    grads, deltas, new_m, new_v = {}, {}, {}, {}

    def update(k, parts, shape2d):
        res = adamw("adamw_" + k, wts[k].reshape(shape2d), mom[k].reshape(shape2d), var[k].reshape(shape2d), parts)
        grads[k], deltas[k], new_m[k], new_v[k] = (r.reshape(wts[k].shape) for r in res)

    for i, (k, _) in enumerate(BIG):
        cols = wts[k].shape[-1]
        slots = lambda per_layer: [s[i].reshape(N_XY, -1, cols) for s in per_layer]
        res = adamw_slots("adamw_" + k, wts[k].reshape(-1, cols), mom[k].reshape(-1, cols), var[k].reshape(-1, cols),
                          slots(mine), slots(theirs))
        grads[k], deltas[k], new_m[k], new_v[k] = (r.reshape(wts[k].shape) for r in res)

    dm_mine = jnp.concatenate([dmod[:, :, 1], dmod[:, :, 0]], axis=1).reshape(depth * 2 * bsz, 6 * D_MODEL)
    dm_all = allgather8("gather_dmod", dm_mine).reshape(N_DEV, depth, 2 * bsz, 6 * D_MODEL)
    gb = bmod_grad(dm_all).reshape(depth, 6 * D_MODEL)
    dm_cols = lax.dynamic_slice_in_dim(dm_all, xy * mod_cols, mod_cols, axis=3)
    d_lat = dm_cols[:, :, :bsz].transpose(1, 0, 2, 3).reshape(depth, n_batch, mod_cols)
    d_lat = jnp.pad(d_lat, ((0, 0), (0, nr - n_batch), (0, 0)))
    d_ctx = dm_cols[:, :, bsz:].transpose(1, 0, 2, 3).reshape(depth, n_batch, mod_cols)
    gw_mod, gc_part = mod_bwd(call, c_ctx[None], d_lat, d_ctx, w_mod, n_batch)
    update("w_mod", [gw_mod.reshape(depth * D_MODEL, mod_cols)], (depth * D_MODEL, mod_cols))
    update("b_mod", [gb], b_mod.shape)

    pack_all = pack_rows([gsmall[k] for k in SMALL_ORDER])
    pack_xy = pack_rows([gc_part])
    rows_all = pack_all.shape[0]
    tot = small_reduce(allgather8("gather_gsmall", jnp.concatenate([pack_all, pack_xy], axis=0)), rows_all)
    gtot = dict(zip(SMALL_ORDER, unpack_rows(tot[:rows_all], [gsmall[k].shape for k in SMALL_ORDER])))
    gtot["c_ctx"] = unpack_rows(tot[rows_all:], [c_ctx.shape])[0]
    for k in SMALL_SHARDED:
        width = wts[k].shape[-1]
        gtot[k] = lax.dynamic_slice_in_dim(gtot[k], xy * width, width, axis=gtot[k].ndim - 1)
    sm_names = ("c_ctx",) + SMALL_ORDER
    sm_shapes = [wts[k].shape for k in sm_names]
    res = adamw("adamw_small", pack_rows([wts[k] for k in sm_names]), pack_rows([mom[k] for k in sm_names]),
                pack_rows([var[k] for k in sm_names]), [pack_rows([gtot[k] for k in sm_names])])
    for dst, buf in zip((grads, deltas, new_m, new_v), res):
        dst.update(zip(sm_names, unpack_rows(buf, sm_shapes)))

    return (loss, grad_x, *[grads[k] for k in names], *[deltas[k] for k in names], *[new_m[k] for k in names],
            *[new_v[k] for k in names])
```

```python
import functools
import math

import numpy as np
import jax
import jax.numpy as jnp
from jax import lax
from jax.experimental import pallas as pl
from jax.experimental.pallas import tpu as pltpu

F32, BF16 = jnp.float32, jnp.bfloat16
HIGHEST = lax.Precision.HIGHEST
MESH = pl.DeviceIdType.MESH

D_MODEL = 1024
GRID_W = 64
N_BRANCH = 4
BRANCH_W = 512
HEAD64 = 64
HEAD128 = 128
N_HEAD4 = 4
ATT_HEADS = 8
ATT_KV_HEADS = 2
CHUNK = 64
MLP_HIDDEN = 4 * D_MODEL
ROPE_THETA = 10000.0
EPS = 1e-6
N_IN = 10000
ADAM_LR, ADAM_B1, ADAM_B2, ADAM_EPS, ADAM_WD, ADAM_STEP = 0.001, 0.9, 0.999, 1e-08, 0.01, 10

LANES = 128
VMEM_LIMIT = 56 * 1024 * 1024

_SEGS = (
    ("gates", 5904, 4096),
    ("att_q", 0, 512), ("dn_q", 768, 512), ("dn_k", 1280, 512), ("dn_v", 1792, 512), ("dn_z", 2304, 512),
    ("ret_v", 3344, 512), ("ret_g", 3856, 512), ("sc_b", 4368, 512), ("sc_c", 4880, 512), ("sc_x", 5392, 512),
    ("ret_q", 2832, 256), ("ret_k", 3088, 256),
    ("att_k", 512, 128), ("att_v", 640, 128),
    ("narrow", 2816, 16),
)
NZ = 10240


def _seg_offsets():
    off, out = 0, {}
    for name, _, width in _SEGS:
        out[name] = off
        off += width
    return out


ZOFF = _seg_offsets()


def _pick(n, cands):
    for c in cands:
        if n % c == 0:
            return c
    return n


def _dg(a, b, ca, cb, batch=False):
    dn = (((ca,), (cb,)), ((0,), (0,))) if batch else (((ca,), (cb,)), ((), ()))
    return lax.dot_general(a.astype(BF16), b.astype(BF16), dn, preferred_element_type=F32)


@functools.partial(jax.custom_vjp, nondiff_argnums=(2, 3))
def _mm(a, b, ca, cb):
    return _dg(a, b, ca, cb)


def _mm_fwd(a, b, ca, cb):
    return _dg(a, b, ca, cb), (a, b)


def _mm_bwd(ca, cb, res, g):
    a, b = res
    if ca == 1:
        da = _mm(g, b, 1, 1) if cb == 0 else _mm(g, b, 1, 0)
    else:
        da = _mm(b, g, 1, 1) if cb == 0 else _mm(b, g, 0, 1)
    if cb == 0:
        db = _mm(a, g, 0, 0) if ca == 1 else _mm(a, g, 1, 0)
    else:
        db = _mm(g, a, 0, 0) if ca == 1 else _mm(g, a, 0, 1)
    return da.astype(a.dtype), db.astype(b.dtype)


_mm.defvjp(_mm_fwd, _mm_bwd)


@functools.partial(jax.custom_vjp, nondiff_argnums=(2, 3))
def _bmm(a, b, ca, cb):
    return _dg(a, b, ca, cb, True)


def _bmm_fwd(a, b, ca, cb):
    return _dg(a, b, ca, cb, True), (a, b)


def _bmm_bwd(ca, cb, res, g):
    a, b = res
    if ca == 2:
        da = _bmm(g, b, 2, 2) if cb == 1 else _bmm(g, b, 2, 1)
    else:
        da = _bmm(b, g, 2, 2) if cb == 1 else _bmm(b, g, 1, 2)
    if cb == 1:
        db = _bmm(a, g, 1, 1) if ca == 2 else _bmm(a, g, 2, 1)
    else:
        db = _bmm(g, a, 1, 1) if ca == 2 else _bmm(g, a, 1, 2)
    return da.astype(a.dtype), db.astype(b.dtype)


_bmm.defvjp(_bmm_fwd, _bmm_bwd)


def _split_bf16(x):
    hi = x.astype(BF16)
    lo = (x - hi.astype(F32)).astype(BF16)
    return hi, lo


def _select3(a, b, cb):
    hi = a.astype(BF16)
    rest = a - hi.astype(F32)
    mid = rest.astype(BF16)
    lo = (rest - mid.astype(F32)).astype(BF16)
    bb = b.astype(BF16)
    d = lambda u: lax.dot_general(u, bb, (((1,), (cb,)), ((), ())), preferred_element_type=F32)
    return d(hi) + (d(mid) + d(lo))


@jax.custom_vjp
def _mm_exact(a, b):
    return _select3(a, b, 0)


_mm_exact.defvjp(lambda a, b: (_select3(a, b, 0), b), lambda b, g: (_select3(g, b, 1), jnp.zeros_like(b)))


class Arg:
    def __init__(self, x, block, imap, diff=True, first=None, gdtype=F32, pieces=None, gshape=None, gimap=None):
        self.x, self.block, self.imap = x, tuple(block), imap
        self.diff, self.first, self.gdtype, self.pieces = diff, first, gdtype, pieces
        self.upcast = x.dtype == BF16 and diff
        self.gshape = tuple(x.shape) if gshape is None else tuple(gshape)
        self.gimap = imap if gimap is None else gimap

    def spec(self):
        return pl.BlockSpec(self.block, self.imap)

    def gspec(self):
        return pl.BlockSpec(self.block, self.gimap)


class Out:
    def __init__(self, shape, dtype, block, imap, pieces=None):
        self.shape, self.dtype, self.block, self.imap, self.pieces = tuple(shape), dtype, tuple(block), imap, pieces

    def spec(self):
        return pl.BlockSpec(self.block, self.imap)

    def sds(self):
        return jax.ShapeDtypeStruct(self.shape, self.dtype)


def _lanes(ref, s, w):
    return (slice(None),) * (len(ref.shape) - 1) + (slice(s, s + w),)


def _load(ref, pieces):
    if pieces is None:
        return ref[...]
    return tuple(ref[_lanes(ref, s, w)] for s, w in pieces)


def _store(ref, val, pieces, accumulate=False):
    if pieces is None:
        if accumulate:
            ref[...] += val.astype(ref.dtype)
        else:
            ref[...] = val.astype(ref.dtype)
        return
    if not accumulate:
        covered = sum(w for _, w in pieces)
        if covered != ref.shape[-1]:
            ref[...] = jnp.zeros(ref.shape, ref.dtype)
    for (s, w), v in zip(pieces, val):
        if accumulate:
            ref[_lanes(ref, s, w)] += v.astype(ref.dtype)
        else:
            ref[_lanes(ref, s, w)] = v.astype(ref.dtype)


def _params(n_grid):
    return pltpu.CompilerParams(dimension_semantics=("arbitrary",) * n_grid, vmem_limit_bytes=VMEM_LIMIT)


def _load_arg(ref, a):
    val = _load(ref, a.pieces)
    return jax.tree.map(lambda t: t.astype(F32), val) if a.upcast else val


def vfwd(name, f, grid, args, outs, comm=None):
    n_in = len(args)

    def body(*refs):
        in_refs, out_refs, _, xrefs = _carried(refs, n_in, len(outs), comm)
        _carry_start(xrefs, comm, grid)
        ids = tuple(pl.program_id(i) for i in range(len(grid)))
        vals = [_load_arg(r, a) for r, a in zip(in_refs, args)]
        res = f(ids, *vals)
        for r, o, spec in zip(out_refs, res, outs):
            _store(r, o, spec.pieces)
        _carry_wait(xrefs, comm, grid)

    return _carrier_call(body, name, grid, [a.spec() for a in args], [a.x for a in args], [o.spec() for o in outs],
                         [o.sds() for o in outs], [], comm)


def vbwd(name, f, grid, args, outs, cts):
    n_in = len(args)
    diff_idx = [i for i, a in enumerate(args) if a.diff]
    ct_flat = [c for per_out in cts for c in per_out]
    ct_specs = [o.spec() for o, per_out in zip(outs, cts) for _ in per_out]
    n_ct = len(ct_flat)

    def body(*refs):
        ids = tuple(pl.program_id(i) for i in range(len(grid)))
        vals = [_load_arg(r, a) for r, a in zip(refs[:n_in], args)]
        ct_refs = refs[n_in:n_in + n_ct]
        g_refs = refs[n_in + n_ct:]

        def g(*dvals):
            full = list(vals)
            for i, v in zip(diff_idx, dvals):
                full[i] = v
            return tuple(f(ids, *full))

        _, vjp = jax.vjp(g, *[vals[i] for i in diff_idx])
        ct_vals, k = [], 0
        for o, per_out in zip(outs, cts):
            tot = None
            for _ in per_out:
                v = _load(ct_refs[k], o.pieces)
                k += 1
                v = jax.tree.map(lambda t: t.astype(F32), v)
                tot = v if tot is None else jax.tree.map(jnp.add, tot, v)
            ct_vals.append(tot)
        grads = vjp(tuple(ct_vals))
        for gr, gv, i in zip(g_refs, grads, diff_idx):
            a = args[i]
            if a.first is None:
                _store(gr, gv, a.pieces)
            else:
                is_first = a.first(ids)

                @pl.when(is_first)
                def _():
                    _store(gr, gv, a.pieces)

                @pl.when(jnp.logical_not(is_first))
                def _():
                    _store(gr, gv, a.pieces, accumulate=True)

    g_specs = [args[i].gspec() for i in diff_idx]
    g_shapes = [jax.ShapeDtypeStruct(args[i].gshape, args[i].gdtype) for i in diff_idx]
    return pl.pallas_call(
        body, name=name, grid=grid,
        in_specs=[a.spec() for a in args] + ct_specs, out_specs=g_specs, out_shape=g_shapes,
        compiler_params=_params(len(grid)),
    )(*[a.x for a in args], *ct_flat)


def matmul(name, a, b, trans_a=False, trans_b=False, epilogue=None, extras=(), out_dtypes=(F32,)):
    if trans_a:
        kdim, m = a.shape
    else:
        m, kdim = a.shape
    n = b.shape[0] if trans_b else b.shape[1]
    assert b.shape[1 if trans_b else 0] == kdim
    tm = _pick(m, (1024, 512, 256, 192, 128, 64))
    tn = _pick(n, (1024, 512, 256, 128))
    tk = _pick(kdim, (1024, 512, 256, 192, 128, 64))
    nk = kdim // tk
    n_x = len(extras)
    assert epilogue is None or nk == 1
    assert epilogue is not None or tuple(out_dtypes) == (F32,)

    def body(a_ref, b_ref, *rest):
        o_ref = rest[n_x]
        part = _dg(a_ref[...], b_ref[...], 0 if trans_a else 1, 1 if trans_b else 0)
        if epilogue is not None:
            for r, val in zip(rest[n_x:], epilogue(part, *[x[...] for x in rest[:n_x]])):
                r[...] = val.astype(r.dtype)
        elif nk == 1:
            o_ref[...] = part
        else:
            k = pl.program_id(2)

            @pl.when(k == 0)
            def _():
                o_ref[...] = part

            @pl.when(k > 0)
            def _():
                o_ref[...] += part

    a_spec = pl.BlockSpec((tk, tm), lambda i, j, k: (k, i)) if trans_a else pl.BlockSpec((tm, tk), lambda i, j, k: (i, k))
    b_spec = pl.BlockSpec((tn, tk), lambda i, j, k: (j, k)) if trans_b else pl.BlockSpec((tk, tn), lambda i, j, k: (k, j))
    mn_spec = pl.BlockSpec((tm, tn), lambda i, j, k: (i, j))
    res = pl.pallas_call(
        body, name=name, grid=(m // tm, n // tn, nk),
        in_specs=[a_spec, b_spec] + [mn_spec] * n_x,
        out_specs=[mn_spec] * len(out_dtypes),
        out_shape=[jax.ShapeDtypeStruct((m, n), dt) for dt in out_dtypes],
        compiler_params=pltpu.CompilerParams(dimension_semantics=("parallel", "parallel", "arbitrary"),
                                             vmem_limit_bytes=VMEM_LIMIT),
    )(a, b, *extras)
    return res if epilogue is not None else res[0]


def _rms(x, gain):
    return x * lax.rsqrt(jnp.mean(x * x, axis=-1, keepdims=True) + EPS) * gain


def _silu(x):
    return x * jax.nn.sigmoid(x)


def f_modnorm(ids, x, gain, mod):
    return (_rms(x, gain) * (1.0 + mod[1:2]) + mod[0:1],)


def f_resnorm(ids, x, y, g_res, g_next, mod):
    x_new = x + mod[0:1] * _rms(y, g_res)
    return x_new, _rms(x_new, g_next) * (1.0 + mod[2:3]) + mod[1:2]


def f_resid(ids, x, y, g_res, mod):
    return (x + mod[0:1] * _rms(y, g_res),)


def _head_consts(width, head):
    i = lax.broadcasted_iota(jnp.int32, (width, width), 0)
    j = lax.broadcasted_iota(jnp.int32, (width, width), 1)
    shift = int(math.log2(head))
    same = (i >> shift) == (j >> shift)
    group = jnp.where(same, 1.0 / head, 0.0).astype(F32)
    half = head // 2
    ii, jj = i & (head - 1), j & (head - 1)
    rot = jnp.where(same & (ii == jj + half) & (jj < half), -1.0, 0.0) + jnp.where(same & (ii + half == jj) & (jj >= half), 1.0, 0.0)
    ti = lax.broadcasted_iota(jnp.int32, (head, width), 0)
    tj = lax.broadcasted_iota(jnp.int32, (head, width), 1)
    tile = jnp.where(ti == (tj & (head - 1)), 1.0, 0.0).astype(F32)
    return group, rot.astype(F32), tile


def _rope(x, cos, sin, rot):
    return x * cos + _mm_exact(x, rot) * sin


def _softplus(x):
    return jnp.maximum(x, 0.0) + jnp.log(1.0 + jnp.exp(-jnp.abs(x)))


def f_prep(ids, zq, zk, rq, rk, zn, qgain, kgain, alog, dtb, cos, sin):
    grp_q, rot_q, tile_q = _head_consts(ATT_HEADS * HEAD64, HEAD64)
    grp_k, rot_k, tile_k = _head_consts(ATT_KV_HEADS * HEAD64, HEAD64)
    grp_r, rot_r, _ = _head_consts(N_HEAD4 * HEAD64, HEAD64)
    wq, wk, wr = zq.shape[-1], zk.shape[-1], rq.shape[-1]
    tile = lambda gain, sel: jnp.dot(gain, sel, precision=HIGHEST, preferred_element_type=F32)
    qn = zq * lax.rsqrt(_mm_exact(zq * zq, grp_q) + EPS) * tile(qgain, tile_q)
    kn = zk * lax.rsqrt(_mm_exact(zk * zk, grp_k) + EPS) * tile(kgain, tile_k)
    q_att = _rope(qn, cos[:, :wq], sin[:, :wq], rot_q) * (HEAD64 ** -0.5)
    k_att = _rope(kn, cos[:, :wk], sin[:, :wk], rot_k)
    q_ret = _rope(rq, cos[:, :wr], sin[:, :wr], rot_r)
    k_ret = _rope(rk * (HEAD64 ** -0.5), cos[:, :wr], sin[:, :wr], rot_r)
    lane = lax.broadcasted_iota(jnp.int32, zn.shape, 1)
    log_a = -jnp.exp(alog) * _softplus(zn + dtb)
    gates = jnp.where(lane < 8, log_a, jnp.where(lane < 16, jax.nn.sigmoid(zn), 0.0))
    return q_att, k_att, q_ret, k_ret, gates


def make_f_attn(tq):
    def f_attn(ids, q, k, v):
        outs = []
        per = ATT_HEADS // ATT_KV_HEADS
        for g in range(ATT_KV_HEADS):
            qg = jnp.concatenate(q[g * per:(g + 1) * per], axis=0)
            s = _mm(qg, k[g], 1, 1)
            e = jnp.exp(s - lax.stop_gradient(jnp.max(s, axis=-1, keepdims=True)))
            o = _mm(e, v[g], 1, 0) * (1.0 / jnp.sum(e, axis=-1, keepdims=True))
            outs += [o[i * tq:(i + 1) * tq] for i in range(per)]
        return (tuple(outs),)
    return f_attn


def _roll_rows(x, shift):
    return pltpu.roll(x, shift, 0)


def make_shifts(t, lc):
    def _down(x):
        row = lax.broadcasted_iota(jnp.int32, x.shape, 0)
        return jnp.where((row == 0) | (row == lc), 0.0, _roll_rows(x, 1))

    def _up(x):
        row = lax.broadcasted_iota(jnp.int32, x.shape, 0)
        return jnp.where((row == lc - 1) | (row == t - 1), 0.0, _roll_rows(x, t - 1))

    @jax.custom_vjp
    def down(x):
        return _down(x)

    @jax.custom_vjp
    def up(x):
        return _up(x)

    down.defvjp(lambda x: (_down(x), None), lambda _, g: (up(g),))
    up.defvjp(lambda x: (_up(x), None), lambda _, g: (down(g),))
    return down, up


def make_conv3(t, lc):
    down, up = make_shifts(t, lc)

    def conv3(x, w):
        return w[0:1] * down(x) + w[1:2] * x + w[2:3] * up(x)
    return conv3


def make_f_dnprep(t, lc):
    conv3 = make_conv3(t, lc)

    def l2n(x):
        return x * lax.rsqrt(jnp.sum(x * x, axis=-1, keepdims=True) + EPS)

    def f_dnprep(ids, q, k, v, wq, wk, wv):
        qn = l2n(_silu(conv3(q, wq))) * (HEAD128 ** -0.5)
        kn = l2n(_silu(conv3(k, wk)))
        return qn, kn, _silu(conv3(v, wv))
    return f_dnprep


def make_f_shortconv(t, lc):
    conv3 = make_conv3(t, lc)

    def f_shortconv(ids, b, c, x, w):
        return (b * conv3(c * x, w),)
    return f_shortconv


def f_finish(ids, o_dn_f, o_dn_b, z_dn, o_rt_f, o_rt_b, g_rt, ngain):
    y_dn, y_rt = [], []
    for h in range(N_HEAD4):
        o = o_dn_f[h] + o_dn_b[h]
        y_dn.append(_rms(o, ngain) * _silu(z_dn[h]))
        r = o_rt_f[h] + o_rt_b[h]
        mu = jnp.mean(r, axis=-1, keepdims=True)
        var = jnp.mean(jnp.square(r - mu), axis=-1, keepdims=True)
        y_rt.append((r - mu) * lax.rsqrt(var + EPS) * _silu(g_rt[h]))
    return tuple(y_dn), tuple(y_rt)


def f_merge(ids, p0, p1, p2, p3, gates):
    u = jax.nn.sigmoid(gates[0]) * p0
    for g, p in zip(gates[1:], (p1, p2, p3)):
        u = u + jax.nn.sigmoid(g) * p
    return (u,)


def _lane_pieces(width, n=N_HEAD4):
    return [(h * width, width) for h in range(n)]


class Tok:
    def __init__(self, x, pieces, width=None, col=0):
        self.x, self.pieces, self.col = x, pieces, col
        self.width = x.shape[-1] if width is None else width
        self.upcast = x.dtype == BF16


def _chunk_of(direction, step, nc, nctx):
    if direction == 0:
        return step
    return jnp.where(step < nctx, nctx - 1 - step, nc + nctx - 1 - step)


ROWS = N_HEAD4 * CHUNK


def _bmm3(a, b, ca, cb):
    ah, al = _split_bf16(a)
    bh, bl = _split_bf16(b)
    dn = (((ca,), (cb,)), ((0,), (0,)))
    d = lambda u, v: lax.dot_general(u, v, dn, preferred_element_type=F32)
    return d(ah, bh) + (d(ah, bl) + d(al, bh))


def _inv_unit_tri_b(low):
    n = low.shape[-1]
    eye = (lax.broadcasted_iota(jnp.int32, (1, n, n), 1) == lax.broadcasted_iota(jnp.int32, (1, n, n), 2)).astype(F32)
    m = -low
    p = eye + m
    for _ in range(int(math.log2(CHUNK)) - 1):
        m = _bmm3(m, m, 2, 1)
        p = p + _bmm3(p, m, 2, 1)
    return p


@jax.custom_vjp
def _tri_solve_b(low, rhs, inv):
    return _bmm3(inv, rhs, 2, 1)


def _tri_solve_b_fwd(low, rhs, inv):
    x = _bmm3(inv, rhs, 2, 1)
    return x, (inv, x)


def _tri_solve_b_bwd(res, g):
    inv, x = res
    d_rhs = _bmm3(inv, g, 1, 1)
    return -_bmm3(d_rhs, x, 2, 2), d_rhs, jnp.zeros_like(inv)


_tri_solve_b.defvjp(_tri_solve_b_fwd, _tri_solve_b_bwd)


def _stack_dirs(toks, i):
    return jnp.concatenate([jnp.concatenate(t[i], axis=1) for t in toks], axis=0)


def _problem_masks(p, nb):
    shape = (p, ROWS, ROWS)
    up = lax.broadcasted_iota(jnp.int32, shape, 0) >= nb
    i = lax.broadcasted_iota(jnp.int32, shape, 1)
    j = lax.broadcasted_iota(jnp.int32, shape, 2)
    same = (i >> 6) == (j >> 6)
    pi, pj = i & (CHUNK - 1), j & (CHUNK - 1)
    rel = jnp.where(up, pj - pi, pi - pj)
    last = same & (pj == jnp.where(up, 0, CHUNK - 1))
    return same & (rel >= 0), same & (rel > 0), last, rel


def _split_states(s_new, nd, nb):
    return [s_new[d * nb * N_HEAD4:(d + 1) * nb * N_HEAD4].reshape((nb, N_HEAD4) + s_new.shape[1:]) for d in range(nd)]


def _split_outs(o, nd, nb):
    return [(tuple(o[d * nb:(d + 1) * nb, h * CHUNK:(h + 1) * CHUNK] for h in range(N_HEAD4)),) for d in range(nd)]


def dn_chunks(states, toks, params, aux=None):
    nd, nb = len(toks), states[0].shape[0]
    p = nd * nb
    qs, ks, vs, la, beta = (_stack_dirs(toks, i) for i in range(5))
    incl, strict, last, _ = _problem_masks(p, nb)
    sq = (p, ROWS, ROWS)
    g = jnp.sum(jnp.where(incl, jnp.swapaxes(jnp.broadcast_to(la, sq), 1, 2), 0.0), axis=2, keepdims=True)
    gb = jnp.broadcast_to(g, sq)
    gbt = jnp.swapaxes(gb, 1, 2)
    dec_incl = jnp.where(incl, jnp.exp(jnp.where(incl, gb - gbt, 0.0)), 0.0)
    dec_strict = jnp.where(strict, dec_incl, 0.0)
    low = beta * _bmm(ks, ks, 2, 2) * dec_strict
    eg = jnp.exp(g)
    inv = _inv_unit_tri_b(low) if aux is None else aux[0]
    sol = _tri_solve_b(low, jnp.concatenate([beta * vs, (beta * eg) * ks], axis=2), inv)
    w_v, w_k = sol[:, :, :HEAD128], sol[:, :, HEAD128:]
    a_qk = _bmm(qs, ks, 2, 2) * dec_incl
    g_last = jnp.sum(jnp.where(last, gbt, 0.0), axis=2, keepdims=True)
    k_dec = ks * jnp.exp(g_last - g)
    s = jnp.concatenate(states, axis=0).reshape(p * N_HEAD4, HEAD128, HEAD128)
    h3 = lambda x: x.reshape(p * N_HEAD4, CHUNK, x.shape[-1])
    u = w_v - _bmm(h3(w_k), s, 2, 1).reshape(p, ROWS, HEAD128)
    o = _bmm(h3(qs * eg), s, 2, 1).reshape(p, ROWS, HEAD128) + _bmm(a_qk, u, 2, 1)
    decay = jnp.exp(jnp.mean(h3(g_last), axis=1, keepdims=True))
    s_new = s * decay + _bmm(h3(k_dec), h3(u), 1, 1)
    return _split_outs(o, nd, nb), _split_states(s_new, nd, nb), [inv]


def ret_chunks(states, toks, params, aux=None):
    nd, nb = len(toks), states[0].shape[0]
    p = nd * nb
    (decay_log,) = params
    qs, ks, vs = (_stack_dirs(toks, i) for i in range(3))
    incl, _, _, rel = _problem_masks(p, nb)
    col = (p, ROWS, 1)
    up = lax.broadcasted_iota(jnp.int32, col, 0) >= nb
    row = lax.broadcasted_iota(jnp.int32, col, 1)
    head = row >> 6
    lg = jnp.zeros(col, F32)
    for h in range(N_HEAD4):
        rate = jnp.where(up, -jnp.exp(decay_log[N_HEAD4 + h]), -jnp.exp(decay_log[h]))
        lg = jnp.where(head == h, rate, lg)
    place = row & (CHUNK - 1)
    pos = jnp.where(up, CHUNK - 1 - place, place).astype(F32)
    dmask = jnp.where(incl, jnp.exp(jnp.where(incl, rel.astype(F32) * lg, 0.0)), 0.0)
    o = _bmm(_bmm(qs, ks, 2, 2) * dmask, vs, 2, 1)
    s = jnp.concatenate(states, axis=0).reshape(p * N_HEAD4, HEAD64, HEAD128)
    h3 = lambda x: x.reshape(p * N_HEAD4, CHUNK, x.shape[-1])
    kv = _bmm(h3(ks * jnp.exp((CHUNK - 1.0 - pos) * lg)), h3(vs), 1, 1)
    o = o + _bmm(h3(qs * jnp.exp((pos + 1.0) * lg)), s, 2, 1).reshape(p, ROWS, HEAD128)
    decay = jnp.exp(CHUNK * jnp.mean(h3(lg), axis=1, keepdims=True))
    s_new = s * decay + kv
    return _split_outs(o, nd, nb), _split_states(s_new, nd, nb), []


def _scan_nb(bsz, most=2):
    return next(n for n in (4, 2, 1) if n <= most and bsz % n == 0)


def _carried(refs, n_in, n_out, comm):
    if comm is None:
        return refs[:n_in], refs[n_in:n_in + n_out], refs[n_in + n_out:], None
    n = len(comm[1])
    ins, srcs = refs[:n_in], refs[n_in:n_in + n]
    outs, dsts = refs[n_in + n:n_in + n + n_out], refs[n_in + n + n_out:n_in + 2 * n + n_out]
    scratch, sems = refs[n_in + 2 * n + n_out:-3], refs[-3:]
    return ins, outs, scratch, (srcs, dsts) + tuple(sems)


def _carry_start(xrefs, comm, grid):
    if comm is None:
        return
    first = functools.reduce(jnp.logical_and, [pl.program_id(i) == 0 for i in range(len(grid))])

    @pl.when(first)
    def _():
        _exchange_start(xrefs, comm[0])


def _carry_wait(xrefs, comm, grid):
    if comm is None:
        return
    last = functools.reduce(jnp.logical_and, [pl.program_id(i) == n - 1 for i, n in enumerate(grid)])

    @pl.when(last)
    def _():
        _exchange_wait(xrefs, comm[0])


def scan2_fwd(name, chunks_fn, dirs, params, state_shape, outs, lc, comm=None, aux_shapes=(), nb_most=2):
    bsz, t = dirs[0][0].x.shape[:2]
    nb = _scan_nb(bsz, nb_most)
    nc, nctx = t // CHUNK, lc // CHUNK
    nd, n_t, n_p, n_o, n_a = len(dirs), len(dirs[0]), len(params), len(outs), len(aux_shapes)
    grid = (bsz // nb, nc)

    def body(*refs):
        ins, out_refs, s_refs, xrefs = _carried(refs, nd * n_t + n_p, nd * (n_o + 1) + n_a, comm)
        tok_refs, par_refs = ins[:nd * n_t], ins[nd * n_t:]
        _carry_start(xrefs, comm, grid)

        @pl.when(pl.program_id(1) == 0)
        def _():
            for s_ref in s_refs:
                s_ref[...] = jnp.zeros(s_ref.shape, F32)

        pv = [_load(r, p) for r, (_, p) in zip(par_refs, params)]
        states = [s_ref[...] for s_ref in s_refs]
        tv = [[_load_arg(r, tk) for r, tk in zip(tok_refs[d * n_t:(d + 1) * n_t], dirs[d])] for d in range(nd)]
        res, s_new, aux = chunks_fn(states, tv, pv)
        for d in range(nd):
            o_refs = out_refs[d * (n_o + 1):(d + 1) * (n_o + 1)]
            o_refs[-1][...] = states[d]
            s_refs[d][...] = s_new[d]
            for r, o, (_, _, pieces) in zip(o_refs, res[d], outs):
                _store(r, o, pieces)
        for r, a in zip(out_refs[nd * (n_o + 1):], aux):
            r[...] = a
        _carry_wait(xrefs, comm, grid)

    nstate = len(state_shape)
    tok_specs, out_specs, out_shapes, operands = [], [], [], []
    for d, toks in enumerate(dirs):
        chunk = lambda c, d=d: _chunk_of(d, c, nc, nctx)
        for tk in toks:
            tok_specs.append(pl.BlockSpec((nb, CHUNK, tk.width), lambda b, c, f=chunk, col=tk.col: (b, f(c), col)))
            operands.append(tk.x)
        for w, dt, _ in outs:
            out_specs.append(pl.BlockSpec((nb, CHUNK, w), lambda b, c, f=chunk: (b, f(c), 0)))
            out_shapes.append(jax.ShapeDtypeStruct((bsz, t, w), dt))
        out_specs.append(pl.BlockSpec((nb, None) + state_shape, lambda b, c, f=chunk: (b, f(c)) + (0,) * nstate))
        out_shapes.append(jax.ShapeDtypeStruct((bsz, nc) + state_shape, F32))
    for shape in aux_shapes:
        out_specs.append(pl.BlockSpec((None, None) + tuple(shape), lambda b, c, n=len(shape): (b, c) + (0,) * n))
        out_shapes.append(jax.ShapeDtypeStruct(grid + tuple(shape), F32))
    par_specs = [pl.BlockSpec(x.shape, lambda b, c: (0, 0)) for x, _ in params]
    operands += [x for x, _ in params]
    return _carrier_call(body, name, grid, tok_specs + par_specs, operands, out_specs, out_shapes,
                         [pltpu.VMEM((nb,) + state_shape, F32)] * nd, comm)


def _carrier_call(body, name, grid, in_specs, operands, out_specs, out_shapes, scratch, comm):
    if comm is not None:
        kind, arrays = comm
        in_specs, operands = in_specs + [HBM_SPEC] * len(arrays), operands + list(arrays)
        out_specs, out_shapes = out_specs + [HBM_SPEC] * len(arrays), out_shapes + [_exchange_shape(kind, a) for a in arrays]
        scratch = scratch + _exchange_sems(len(arrays))
    return pl.pallas_call(body, name=name, grid=grid, in_specs=in_specs, out_specs=out_specs, out_shape=out_shapes,
                          scratch_shapes=scratch, compiler_params=_params(len(grid)))(*operands)


def scan2_bwd(name, chunks_fn, dirs, params, state_shape, outs, lc, sprevs, cts, comm=None, aux=(), nb_most=2):
    bsz, t = dirs[0][0].x.shape[:2]
    nb = _scan_nb(bsz, nb_most)
    nc, nctx = t // CHUNK, lc // CHUNK
    nd, n_t, n_p, n_o, n_a = len(dirs), len(dirs[0]), len(params), len(outs), len(aux)
    per_in = n_t + 1 + n_o
    grid = (bsz // nb, nc)

    def body(*refs):
        refs, g_refs, ds_refs, xrefs = _carried(refs, nd * per_in + n_p + n_a, nd * n_t + n_p, comm)
        par_refs = refs[nd * per_in:nd * per_in + n_p]
        saved = [r[...] for r in refs[nd * per_in + n_p:]]
        fn = chunks_fn if not saved else (lambda s, tk, pr: chunks_fn(s, tk, pr, saved))
        _carry_start(xrefs, comm, grid)
        first = pl.program_id(1) == 0

        @pl.when(first)
        def _():
            for ds_ref in ds_refs:
                ds_ref[...] = jnp.zeros(ds_ref.shape, F32)

        pv = [_load(r, p) for r, (_, p) in zip(par_refs, params)]
        ins = [refs[d * per_in:(d + 1) * per_in] for d in range(nd)]
        tv = [[_load_arg(r, tk) for r, tk in zip(ins[d][:n_t], dirs[d])] for d in range(nd)]
        states = [ins[d][n_t][...] for d in range(nd)]
        _, vjp = jax.vjp(lambda s, tk, pr: fn(s, tk, pr)[:2], states, tv, pv)
        ct = [tuple(_load(r, pieces) for r, (_, _, pieces) in zip(ins[d][n_t + 1:], outs)) for d in range(nd)]
        d_s, d_tv, d_pv = vjp((ct, [ds_ref[...] for ds_ref in ds_refs]))
        for d in range(nd):
            ds_refs[d][...] = d_s[d]
            for r, gv, tk in zip(g_refs[d * n_t:(d + 1) * n_t], d_tv[d], dirs[d]):
                _store(r, gv, tk.pieces)
        very_first = first & (pl.program_id(0) == 0)
        for r, gv, (_, p) in zip(g_refs[nd * n_t:], d_pv, params):
            @pl.when(very_first)
            def _():
                _store(r, gv, p)

            @pl.when(jnp.logical_not(very_first))
            def _():
                _store(r, gv, p, accumulate=True)
        _carry_wait(xrefs, comm, grid)

    nstate = len(state_shape)
    in_specs, operands, g_specs, g_shapes = [], [], [], []
    for d, toks in enumerate(dirs):
        chunk = lambda c, d=d: _chunk_of(d, nc - 1 - c, nc, nctx)
        for tk in toks:
            in_specs.append(pl.BlockSpec((nb, CHUNK, tk.width), lambda b, c, f=chunk, col=tk.col: (b, f(c), col)))
            operands.append(tk.x)
            g_specs.append(pl.BlockSpec((nb, CHUNK, tk.width), lambda b, c, f=chunk: (b, f(c), 0)))
            g_shapes.append(jax.ShapeDtypeStruct((bsz, t, tk.width), F32))
        in_specs.append(pl.BlockSpec((nb, None) + state_shape, lambda b, c, f=chunk: (b, f(c)) + (0,) * nstate))
        operands.append(sprevs[d])
        for (w, _, _), ct in zip(outs, cts[d]):
            in_specs.append(pl.BlockSpec((nb, CHUNK, w), lambda b, c, f=chunk: (b, f(c), 0)))
            operands.append(ct)
    par_specs = [pl.BlockSpec(x.shape, lambda b, c: (0, 0)) for x, _ in params]
    aux_specs = [pl.BlockSpec((None, None) + a.shape[2:], lambda b, c, n=a.ndim - 2: (b, nc - 1 - c) + (0,) * n) for a in aux]
    operands += [x for x, _ in params] + list(aux)
    return _carrier_call(body, name, grid, in_specs + par_specs + aux_specs, operands, g_specs + par_specs,
                         g_shapes + [jax.ShapeDtypeStruct(x.shape, F32) for x, _ in params],
                         [pltpu.VMEM((nb,) + state_shape, F32)] * nd, comm)


class Geo:
    def __init__(self, bsz, t, lc):
        self.b, self.t, self.lc, self.m = bsz, t, lc, bsz * t
        self.tm = _pick(lc, (256, 128, 64))
        self.tq = _pick(lc, (128, 64))
        assert t % self.tm == 0 and t % CHUNK == 0 and lc % CHUNK == 0
        self.nctx, self.nctx_q = lc // self.tm, lc // self.tq
        self.grid = (bsz, t // self.tm)


def a_tok(g, x, tm=None, **kw):
    tm = tm or g.tm
    return Arg(x, (None, tm, x.shape[-1]), lambda b, j: (b, j, 0), **kw)


def a_ztok(g, z, name, width, tm=None, **kw):
    tm = tm or g.tm
    col = ZOFF[name] // width
    return Arg(z, (None, tm, width), lambda b, j: (b, j, col),
               gshape=(g.b, g.t, width), gimap=lambda b, j: (b, j, 0), **kw)


def a_par(x, **kw):
    return Arg(x, x.shape, lambda b, j: (0, 0), first=lambda ids: (ids[0] == 0) & (ids[1] == 0), **kw)


def a_mod(g, x):
    n = g.nctx
    return Arg(x, (None, None, x.shape[2], D_MODEL), lambda b, j: (b, jnp.where(j >= n, 1, 0), 0, 0),
               first=lambda ids: (ids[1] == 0) | (ids[1] == n))


def a_tab(g, x):
    return Arg(x, (g.tm, x.shape[-1]), lambda b, j: (j, 0), diff=False)


def o_tok(g, w, dtype, tm=None, pieces=None):
    tm = tm or g.tm
    return Out((g.b, g.t, w), dtype, (None, tm, w), lambda b, j: (b, j, 0), pieces)


def addn(name, xs, dtype):
    bsz, t, w = xs[0].shape
    tm = _pick(t, (256, 192, 128, 64))

    def body(*refs):
        tot = refs[0][...].astype(F32)
        for r in refs[1:-1]:
            tot = tot + r[...].astype(F32)
        refs[-1][...] = tot.astype(dtype)

    spec = pl.BlockSpec((None, tm, w), lambda b, j: (b, j, 0))
    return pl.pallas_call(body, name=name, grid=(bsz, t // tm), in_specs=[spec] * len(xs), out_specs=spec,
                          out_shape=jax.ShapeDtypeStruct((bsz, t, w), dtype), compiler_params=_params(2))(*xs)


P64x8 = _lane_pieces(HEAD64, ATT_HEADS)
P64x4 = _lane_pieces(HEAD64, N_HEAD4)
P64x2 = _lane_pieces(HEAD64, ATT_KV_HEADS)
P128x4 = _lane_pieces(HEAD128, N_HEAD4)
P1024x4 = _lane_pieces(D_MODEL, N_BRANCH)


def _gate_pieces(direction):
    la = [(direction * N_HEAD4 + h, 1) for h in range(N_HEAD4)]
    beta = [(8 + direction * N_HEAD4 + h, 1) for h in range(N_HEAD4)]
    return la, beta


RET_PIECES = [(i, 1) for i in range(2 * N_HEAD4)]


def _prep_io(g, z, sp, tabs):
    args = [a_ztok(g, z, "att_q", 512, gdtype=BF16), a_ztok(g, z, "att_k", 128, gdtype=BF16),
            a_ztok(g, z, "ret_q", 256, gdtype=BF16), a_ztok(g, z, "ret_k", 256, gdtype=BF16),
            a_ztok(g, z, "narrow", 128, gdtype=BF16),
            a_par(sp["qgain"]), a_par(sp["kgain"]), a_par(sp["alog"]), a_par(sp["dtb"]),
            a_tab(g, tabs[0]), a_tab(g, tabs[1])]
    outs = [o_tok(g, 512, BF16), o_tok(g, 128, BF16), o_tok(g, 256, F32), o_tok(g, 256, F32), o_tok(g, 128, F32)]
    return args, outs


def _attn_io(g, q_att, k_att, z, latent):
    col = ZOFF["att_v"] // 128
    first = lambda ids: ids[1] == 0
    if latent:
        rows, off, nq = g.t, g.nctx_q, (g.t - g.lc) // g.tq
    else:
        rows, off, nq = g.lc, 0, g.nctx_q
    args = [Arg(q_att, (None, g.tq, 512), lambda b, j: (b, j + off, 0), pieces=P64x8,
                gshape=(g.b, nq * g.tq, 512), gimap=lambda b, j: (b, j, 0)),
            Arg(k_att, (None, rows, 128), lambda b, j: (b, 0, 0), first=first, pieces=P64x2, gshape=(g.b, rows, 128)),
            Arg(z, (None, rows, 128), lambda b, j: (b, 0, col), first=first, pieces=P64x2,
                gshape=(g.b, rows, 128), gimap=lambda b, j: (b, 0, 0))]
    outs = [Out((g.b, nq * g.tq, 512), BF16, (None, g.tq, 512), lambda b, j: (b, j, 0), P64x8)]
    return (g.b, nq), args, outs


def _pad_rows(x, t):
    return jnp.pad(x, ((0, 0), (0, t - x.shape[1]), (0, 0)))


def _colgrid_arg(g, z, name, width_total, **kw):
    col = ZOFF[name] // 128
    return Arg(z, (None, g.t, 128), lambda h, b: (b, 0, col + h),
               gshape=(g.b, g.t, width_total), gimap=lambda h, b: (b, 0, h), **kw)


def _dnprep_io(g, z, sp):
    wfirst = lambda ids: ids[1] == 0
    args = [_colgrid_arg(g, z, "dn_q", 512, gdtype=BF16), _colgrid_arg(g, z, "dn_k", 512, gdtype=BF16),
            _colgrid_arg(g, z, "dn_v", 512, gdtype=BF16)]
    for i in range(3):
        args.append(Arg(sp["dn_conv"][i], (3, 128), lambda h, b: (0, h), first=wfirst))
    outs = [Out((g.b, g.t, 512), F32, (None, g.t, 128), lambda h, b: (b, 0, h)) for _ in range(3)]
    return (N_HEAD4, g.b), args, outs


def _shortconv_io(g, z, sp):
    args = [_colgrid_arg(g, z, "sc_b", 512, gdtype=BF16), _colgrid_arg(g, z, "sc_c", 512, gdtype=BF16),
            _colgrid_arg(g, z, "sc_x", 512, gdtype=BF16),
            Arg(sp["sc_conv"], (3, 128), lambda h, b: (0, h), first=lambda ids: ids[1] == 0)]
    outs = [Out((g.b, g.t, 512), BF16, (None, g.t, 128), lambda h, b: (b, 0, h))]
    return (BRANCH_W // 128, g.b), args, outs


def _finish_io(g, o_dn_f, o_dn_b, z, o_rt_f, o_rt_b, sp):
    args = [a_tok(g, o_dn_f, pieces=P128x4), a_tok(g, o_dn_b, pieces=P128x4),
            a_ztok(g, z, "dn_z", 512, gdtype=BF16, pieces=P128x4),
            a_tok(g, o_rt_f, pieces=P128x4), a_tok(g, o_rt_b, pieces=P128x4),
            a_ztok(g, z, "ret_g", 512, gdtype=BF16, pieces=P128x4), a_par(sp["ngain"])]
    outs = [o_tok(g, 512, BF16, pieces=P128x4), o_tok(g, 512, BF16, pieces=P128x4)]
    return args, outs


def _merge_io(g, ps, z):
    args = [a_tok(g, p, gdtype=BF16) for p in ps] + [a_ztok(g, z, "gates", 4096, gdtype=BF16, pieces=P1024x4)]
    return args, [o_tok(g, D_MODEL, BF16)]


def _dn_dirs(qn, kn, vn, gates):
    dirs = []
    for d in range(2):
        la, beta = _gate_pieces(d)
        dirs.append([Tok(qn, P128x4), Tok(kn, P128x4), Tok(vn, P128x4), Tok(gates, la), Tok(gates, beta)])
    return dirs


def _ret_dirs(q_ret, k_ret, z):
    col = ZOFF["ret_v"] // 512
    return [[Tok(q_ret, P64x4), Tok(k_ret, P64x4), Tok(z, P128x4, 512, col)] for _ in range(2)]


DN_STATE = (N_HEAD4, HEAD128, HEAD128)
RET_STATE = (N_HEAD4, HEAD64, HEAD128)
RET_NB = 4
SCAN_OUT = [(512, F32, P128x4)]


def layer_fwd(g, x, h, w, sp, mod_a, mod_b, tabs, g_next, comm=None, rest=None):
    tp = {"x": x, "h": h}
    m, lc = g.m, g.lc
    z = matmul("win", h.reshape(m, D_MODEL), w["win"], epilogue=lambda p: (p,), out_dtypes=(BF16,))[0].reshape(g.b, g.t, NZ)
    tp["z"] = z
    args, outs = _prep_io(g, z, sp, tabs)
    q_att, k_att, q_ret, k_ret, gates = vfwd("prep", f_prep, g.grid, args, outs)
    tp.update(q_att=q_att, k_att=k_att, q_ret=q_ret, k_ret=k_ret, gates=gates)
    y_att = []
    for latent in (False, True):
        agrid, args, outs = _attn_io(g, q_att, k_att, z, latent)
        ride = ("gather", rest) if latent and rest is not None else None
        y, *gathered = vfwd("attn", make_f_attn(g.tq), agrid, args, outs, ride)
        y_att.append(y)
        if gathered:
            w = {**w, **weights_of_slots(gathered, 1)}
    tp["w"] = w
    y_att = jnp.concatenate(y_att, axis=1)
    dgrid, args, outs = _dnprep_io(g, z, sp)
    qn, kn, vn = vfwd("dnprep", make_f_dnprep(g.t, lc), dgrid, args, outs)
    tp.update(qn=qn, kn=kn, vn=vn)
    inv_shape = (2 * _scan_nb(g.b), ROWS, ROWS)
    o_f, s_f, o_b, s_b, tp["dn_inv"], *carried = scan2_fwd("dnscan", dn_chunks, _dn_dirs(qn, kn, vn, gates), [], DN_STATE,
                                                            SCAN_OUT, lc, comm, [inv_shape])
    tp["carried"] = carried if carried else None
    o_dn, tp["dn_s"] = [o_f, o_b], [s_f, s_b]
    o_f, s_f, o_b, s_b = scan2_fwd("retscan", ret_chunks, _ret_dirs(q_ret, k_ret, z), [(sp["ret"], RET_PIECES)], RET_STATE, SCAN_OUT, lc,
                                   nb_most=RET_NB)
    o_rt, tp["rt_s"] = [o_f, o_b], [s_f, s_b]
    tp.update(o_dn=o_dn, o_rt=o_rt)
    args, outs = _finish_io(g, o_dn[0], o_dn[1], z, o_rt[0], o_rt[1], sp)
    y_dn, y_rt = vfwd("finish", f_finish, g.grid, args, outs)
    sgrid, args, outs = _shortconv_io(g, z, sp)
    (y_sc,) = vfwd("shortconv", make_f_shortconv(g.t, lc), sgrid, args, outs)
    ys = [y_att, y_dn, y_rt, y_sc]
    tp["ys"] = ys
    ps = [matmul("wbranch", y.reshape(m, BRANCH_W), w["wbr"][i], epilogue=lambda p: (p,), out_dtypes=(BF16,))[0]
          .reshape(g.b, g.t, D_MODEL) for i, y in enumerate(ys)]
    tp["ps"] = ps
    args, outs = _merge_io(g, ps, z)
    (u,) = vfwd("merge", f_merge, g.grid, args, outs)
    tp["u"] = u
    y = matmul("wout", u.reshape(m, D_MODEL), w["wout"]).reshape(g.b, g.t, D_MODEL)
    tp["y"] = y
    args = [a_tok(g, x), a_tok(g, y), a_par(sp["g1"]), a_par(sp["g2"]), a_mod(g, mod_a)]
    x1, h2 = vfwd("resnorm", f_resnorm, g.grid, args, [o_tok(g, D_MODEL, F32), o_tok(g, D_MODEL, BF16)])
    tp.update(x1=x1, h2=h2)
    tp["a"], tp["r"] = matmul("wmlp1", h2.reshape(m, D_MODEL), w["w1"], epilogue=lambda p: (p, jnp.square(jnp.maximum(p, 0.0))),
                              out_dtypes=(F32, BF16))
    mo = matmul("wmlp2", tp["r"], w["w2"]).reshape(g.b, g.t, D_MODEL)
    tp["mo"] = mo
    if g_next is None:
        args = [a_tok(g, x1), a_tok(g, mo), a_par(sp["g3"]), a_mod(g, mod_b)]
        (x2,) = vfwd("resid", f_resid, g.grid, args, [o_tok(g, D_MODEL, F32)])
        return x2, None, tp
    args = [a_tok(g, x1), a_tok(g, mo), a_par(sp["g3"]), a_par(g_next), a_mod(g, mod_b)]
    x2, h_next = vfwd("resnorm", f_resnorm, g.grid, args, [o_tok(g, D_MODEL, F32), o_tok(g, D_MODEL, BF16)])
    return x2, h_next, tp


def layer_bwd(g, tp, sp, mod_a, mod_b, tabs, g_next, dx2, dh_next, send=None):
    m, lc = g.m, g.lc
    w = tp["w"]
    gw, gs = {}, {}
    x, z = tp["x"], tp["z"]
    if g_next is None:
        args = [a_tok(g, tp["x1"]), a_tok(g, tp["mo"]), a_par(sp["g3"]), a_mod(g, mod_b)]
        dx1, dmo, gs["g3"], dmod_b = vbwd("resid_b", f_resid, g.grid, args, [o_tok(g, D_MODEL, F32)], [[dx2]])
    else:
        args = [a_tok(g, tp["x1"]), a_tok(g, tp["mo"]), a_par(sp["g3"]), a_par(g_next), a_mod(g, mod_b)]
        dx1, dmo, gs["g3"], gs["g0_next"], dmod_b = vbwd(
            "resnorm_b", f_resnorm, g.grid, args, [o_tok(g, D_MODEL, F32), o_tok(g, D_MODEL, BF16)], [[dx2], [dh_next]])
    dmo2 = dmo.reshape(m, D_MODEL)
    (da2,) = matmul("wmlp2_dx", dmo2, w["w2"], trans_b=True, extras=[tp["a"]],
                    epilogue=lambda p, a: (p * (2.0 * jnp.maximum(a, 0.0)),), out_dtypes=(BF16,))
    gw["w2"] = matmul("wmlp2_dw", tp["r"], dmo2, trans_a=True)
    dh2 = matmul("wmlp1_dx", da2, w["w1"], trans_b=True).reshape(g.b, g.t, D_MODEL)
    gw["w1"] = matmul("wmlp1_dw", tp["h2"].reshape(m, D_MODEL), da2, trans_a=True)
    args = [a_tok(g, x), a_tok(g, tp["y"]), a_par(sp["g1"]), a_par(sp["g2"]), a_mod(g, mod_a)]
    dx, dy, gs["g1"], gs["g2"], dmod_a = vbwd(
        "resnorm_b", f_resnorm, g.grid, args, [o_tok(g, D_MODEL, F32), o_tok(g, D_MODEL, BF16)], [[dx1], [dh2]])
    dy2 = dy.reshape(m, D_MODEL)
    du = matmul("wout_dx", dy2, w["wout"], trans_b=True).reshape(g.b, g.t, D_MODEL)
    gw["wout"] = matmul("wout_dw", tp["u"].reshape(m, D_MODEL), dy2, trans_a=True)
    args, outs = _merge_io(g, tp["ps"], z)
    *dps, dz_gates = vbwd("merge_b", f_merge, g.grid, args, outs, [[du]])
    dys, gwbr = [], []
    for i in range(N_BRANCH):
        dp2 = dps[i].reshape(m, D_MODEL)
        dys.append(matmul("wbranch_dx", dp2, w["wbr"][i], trans_b=True).reshape(g.b, g.t, BRANCH_W))
        gwbr.append(matmul("wbranch_dw", tp["ys"][i].reshape(m, BRANCH_W), dp2, trans_a=True))
    gw["wbr"] = jnp.stack(gwbr)
    dy_att, dy_dn, dy_rt, dy_sc = dys
    o_dn, o_rt = tp["o_dn"], tp["o_rt"]
    args, outs = _finish_io(g, o_dn[0], o_dn[1], z, o_rt[0], o_rt[1], sp)
    do_dn_f, do_dn_b, dz_dnz, do_rt_f, do_rt_b, dz_retg, gs["ngain"] = vbwd(
        "finish_b", f_finish, g.grid, args, outs, [[dy_dn], [dy_rt]])
    comm = ("scatter", list(send) + grad_slots(gw, WKEYS[1:])) if send is not None else None
    res = scan2_bwd("dnscan_b", dn_chunks, _dn_dirs(tp["qn"], tp["kn"], tp["vn"], tp["gates"]), [], DN_STATE, SCAN_OUT, lc,
                   tp["dn_s"], [[do_dn_f], [do_dn_b]], comm, [tp["dn_inv"]])
    dqn, dkn, dvn, dgates = [res[0], res[5]], [res[1], res[6]], [res[2], res[7]], [res[3], res[4], res[8], res[9]]
    received = list(res[10:]) if comm is not None else None
    swap = ("swap", received) if comm is not None else None
    res = scan2_bwd("retscan_b", ret_chunks, _ret_dirs(tp["q_ret"], tp["k_ret"], z), [(sp["ret"], RET_PIECES)], RET_STATE, SCAN_OUT, lc,
                   tp["rt_s"], [[do_rt_f], [do_rt_b]], swap, nb_most=RET_NB)
    dq_ret, dk_ret, gs["ret"] = [res[0], res[3]], [res[1], res[4]], res[6]
    carried = (received, list(res[7:])) if comm is not None else None
    dz_retv = addn("sum_retv", [res[2], res[5]], BF16)
    sgrid, args, outs = _shortconv_io(g, z, sp)
    dz_scb, dz_scc, dz_scx, gs["sc_conv"] = vbwd("shortconv_b", make_f_shortconv(g.t, lc), sgrid, args, outs, [[dy_sc]])
    dq_att, dk_att, dv_att = [], [], []
    for latent, dy in ((False, dy_att[:, :lc]), (True, dy_att[:, lc:])):
        agrid, args, outs = _attn_io(g, tp["q_att"], tp["k_att"], z, latent)
        dq, dk, dv = vbwd("attn_b", make_f_attn(g.tq), agrid, args, outs, [[dy]])
        dq_att.append(dq); dk_att.append(_pad_rows(dk, g.t)); dv_att.append(_pad_rows(dv, g.t))
    dq_att = jnp.concatenate(dq_att, axis=1)
    dz_attv = addn("sum_attv", dv_att, BF16)
    dgrid, args, outs = _dnprep_io(g, z, sp)
    dz_dnq, dz_dnk, dz_dnv, gc_q, gc_k, gc_v = vbwd("dnprep_b", make_f_dnprep(g.t, lc), dgrid, args, outs, [dqn, dkn, dvn])
    gs["dn_conv"] = [gc_q, gc_k, gc_v]
    args, outs = _prep_io(g, z, sp, tabs)
    dz_attq, dz_attk, dz_retq, dz_retk, dz_nar, gs["qgain"], gs["kgain"], gs["alog"], gs["dtb"] = vbwd(
        "prep_b", f_prep, g.grid, args, outs, [[dq_att], dk_att, dq_ret, dk_ret, dgates])
    pad = jnp.zeros((g.b, g.t, NZ - (ZOFF["narrow"] + 128)), BF16)
    dz = jnp.concatenate([dz_gates, dz_attq, dz_dnq, dz_dnk, dz_dnv, dz_dnz, dz_retv, dz_retg, dz_scb, dz_scc, dz_scx,
                          dz_retq, dz_retk, dz_attk, dz_attv, dz_nar, pad], axis=-1)
    dz2 = dz.reshape(m, NZ)
    dh = matmul("win_dx", dz2, w["win"], trans_b=True).reshape(g.b, g.t, D_MODEL)
    gw["win"] = matmul("win_dw", tp["h"].reshape(m, D_MODEL), dz2, trans_a=True)
    return dx, dh, gw, gs, dmod_a, dmod_b, carried


def loss_call(g, xa, tgt):
    n = g.nctx
    inv_d = 1.0 / D_MODEL

    def body(x_ref, t_ref, loss_ref, dx_ref):
        j = pl.program_id(1)

        @pl.when((pl.program_id(0) == 0) & (j == 0))
        def _():
            loss_ref[...] = jnp.zeros(loss_ref.shape, F32)

        @pl.when(j < n)
        def _():
            dx_ref[...] = jnp.zeros(dx_ref.shape, F32)

        @pl.when(j >= n)
        def _():
            e = x_ref[...] - t_ref[...]
            dx_ref[...] = e * inv_d
            s = jnp.sum(jnp.sum(e * e, axis=1, keepdims=True), axis=0, keepdims=True)
            loss_ref[...] += jnp.broadcast_to(0.5 * inv_d * s, loss_ref.shape)

    tok = pl.BlockSpec((None, g.tm, D_MODEL), lambda b, j: (b, j, 0))
    return pl.pallas_call(
        body, name="loss", grid=g.grid,
        in_specs=[tok, pl.BlockSpec((None, g.tm, D_MODEL), lambda b, j: (b, jnp.maximum(j - n, 0), 0))],
        out_specs=[pl.BlockSpec((1, LANES), lambda b, j: (0, 0)), tok],
        out_shape=[jax.ShapeDtypeStruct((1, LANES), F32), jax.ShapeDtypeStruct((g.b, g.t, D_MODEL), F32)],
        compiler_params=_params(2),
    )(xa, tgt)


def rope_tables(g):
    seq = g.t - g.lc
    rows = seq // GRID_W
    r, col = jnp.meshgrid(jnp.arange(rows), jnp.arange(GRID_W), indexing="ij")
    quarter = HEAD64 // 4
    inv_freq = ROPE_THETA ** (-jnp.arange(quarter, dtype=F32) / quarter)
    ang = jnp.concatenate([r.reshape(-1, 1).astype(F32) * inv_freq, col.reshape(-1, 1).astype(F32) * inv_freq], axis=-1)
    cos, sin = jnp.cos(ang), jnp.sin(ang)
    cos = jnp.concatenate([jnp.ones((g.lc, HEAD64 // 2), F32), cos], axis=0)
    sin = jnp.concatenate([jnp.zeros((g.lc, HEAD64 // 2), F32), sin], axis=0)
    reps = 512 // (HEAD64 // 2)
    return jnp.tile(cos, (1, reps)), jnp.tile(sin, (1, reps))


def _row128(v):
    v = v.reshape(1, -1).astype(F32)
    return jnp.pad(v, ((0, 0), (0, LANES - v.shape[1])))


def layer_small(small, l):
    gn = small["g_norm"][l]
    return {
        "g0": gn[0:1], "g1": gn[1:2], "g2": gn[2:3], "g3": gn[3:4],
        "qgain": small["att_q_gain"][l][None], "kgain": small["att_k_gain"][l][None],
        "dn_conv": [small["dn_conv"][l][:, i * 512:(i + 1) * 512] for i in range(3)], "alog": _row128(small["dn_a_log"][l]), "dtb": _row128(small["dn_dt_bias"][l]),
        "ngain": small["dn_norm_gain"][l][None], "ret": _row128(small["ret_decay"][l]), "sc_conv": small["sc_conv"][l],
    }


def permute_win(w):
    parts = [w[..., off:off + width] for _, off, width in _SEGS]
    parts.append(jnp.zeros(w.shape[:-1] + (NZ - N_IN,), w.dtype))
    return jnp.concatenate(parts, axis=-1)


def unpermute_win(gw):
    order = sorted(_SEGS, key=lambda s: s[1])
    return jnp.concatenate([gw[..., ZOFF[name]:ZOFF[name] + width] for name, _, width in order], axis=-1)


BIG = (("w_in", -1), ("w_branch", -1), ("w_out", -2), ("w_mlp_in", -1), ("w_mlp_out", -2))
WKEYS = ("win", "wbr", "wout", "w1", "w2")


def weights_of_slots(slots, first=0):
    out = {}
    for s, key, (_, axis) in zip(slots, WKEYS[first:], BIG[first:]):
        if axis == -2:
            full = s.reshape((-1,) + s.shape[2:])
        else:
            full = jnp.concatenate([s[p] for p in range(N_XY)], axis=axis)
        out[key] = permute_win(full) if key == "win" else full
    return out


def grad_slots(gw, keys):
    out = []
    for key in keys:
        full = unpermute_win(gw[key]) if key == "win" else gw[key]
        axis = BIG[WKEYS.index(key)][1]
        if axis == -2:
            out.append(full.reshape((N_XY, -1) + full.shape[1:]).astype(BF16))
        else:
            out.append(jnp.stack(jnp.split(full, N_XY, axis=axis)).astype(BF16))
    return out


def model_step(g, xa, tgt, mod, shards, small, local=False):
    depth = len(shards)
    tabs = rope_tables(g)
    sps = [layer_small(small, l) for l in range(depth)]
    mods_a = [mod[l][:, :, 2:5] for l in range(depth)]
    mods_b = [jnp.concatenate([mod[l][:, :, 5:6], mod[l + 1][:, :, 0:2]], axis=2) if l + 1 < depth else mod[l][:, :, 5:6]
              for l in range(depth)]
    mod0 = mod[0][:, :, 0:2]
    args0 = [a_tok(g, xa), a_par(sps[0]["g0"]), a_mod(g, mod0)]
    (h,) = vfwd("modnorm", f_modnorm, g.grid, args0, [o_tok(g, D_MODEL, BF16)])
    x, tapes = xa, []
    if local:
        w, rest = weights_of_slots(shards[0]), None
    else:
        w, rest = weights_of_slots(exchange("gather_win", "gather", shards[0][:1])), shards[0][1:]
    for l in range(depth):
        g_next = sps[l + 1]["g0"] if l + 1 < depth else None
        comm = ("gather", shards[l + 1]) if l + 1 < depth and not local else None
        x, h, tp = layer_fwd(g, x, h, w, sps[l], mods_a[l], mods_b[l], tabs, g_next, comm, rest)
        if l + 1 < depth:
            w, rest = weights_of_slots(shards[l + 1] if local else tp["carried"]), None
        tapes.append(tp)
    loss_row, dx = loss_call(g, x, tgt)
    dh, gss, dmods = None, [None] * depth, [None] * depth
    mine, theirs = [[None] * len(BIG) for _ in range(depth)], [[None] * len(BIG) for _ in range(depth)]
    win_slots = None
    for l in reversed(range(depth)):
        g_next = sps[l + 1]["g0"] if l + 1 < depth else None
        send = None if local else ([win_slots] if win_slots is not None else [])
        dx, dh, gw, gss[l], dma, dmb, carried = layer_bwd(g, tapes[l], sps[l], mods_a[l], mods_b[l], tabs, g_next, dx, dh, send)
        (win_new,) = grad_slots(gw, WKEYS[:1])
        if local:
            mine[l] = [win_new] + grad_slots(gw, WKEYS[1:])
        else:
            received, swapped = carried
            if win_slots is not None:
                mine[l + 1][0], theirs[l + 1][0] = received[0], swapped[0]
            mine[l][1:], theirs[l][1:] = received[-4:], swapped[-4:]
        win_slots = win_new
        dmods[l] = (dma, dmb)
    if not local:
        mine[0][:1] = exchange("scatter_win", "scatter", [win_slots])
        theirs[0][:1] = exchange("swap_win", "swap", mine[0][:1])
    dxh, g0_first, dmod0 = vbwd("modnorm_b", f_modnorm, g.grid, args0, [o_tok(g, D_MODEL, BF16)], [[dh]])
    dxa = addn("sum_dx", [dx, dxh], F32)
    dmod = []
    for l in range(depth):
        first2 = dmod0 if l == 0 else dmods[l - 1][1][:, :, 1:3]
        dmod.append(jnp.concatenate([first2, dmods[l][0], dmods[l][1][:, :, 0:1]], axis=2))
    dmod = jnp.stack(dmod)
    def rows(key, n):
        return jnp.stack([gs[key][0, :n] for gs in gss])
    g_norm = jnp.stack([jnp.concatenate([g0_first if l == 0 else gss[l - 1]["g0_next"], gss[l]["g1"], gss[l]["g2"], gss[l]["g3"]], axis=0)
                        for l in range(depth)])
    gsmall = {
        "g_norm": g_norm,
        "att_q_gain": jnp.stack([gs["qgain"][0] for gs in gss]), "att_k_gain": jnp.stack([gs["kgain"][0] for gs in gss]),
        "dn_conv": jnp.stack([jnp.concatenate(gs["dn_conv"], axis=1) for gs in gss]),
        "dn_a_log": rows("alog", 8).reshape(depth, 2, N_HEAD4), "dn_dt_bias": rows("dtb", 8).reshape(depth, 2, N_HEAD4),
        "dn_norm_gain": jnp.stack([gs["ngain"][0] for gs in gss]),
        "ret_decay": rows("ret", 8).reshape(depth, 2, N_HEAD4),
        "sc_conv": jnp.stack([gs["sc_conv"] for gs in gss]),
    }
    return loss_row[0, 0], dxa, mine, theirs, gsmall, dmod


N_DEV = 8
N_XY = 4
HBM_SPEC = pl.BlockSpec(memory_space=pltpu.HBM)
VMEM_SPEC = pl.BlockSpec(memory_space=pltpu.VMEM)


def _coords():
    return lax.axis_index("x"), lax.axis_index("y"), lax.axis_index("c")


def _flip(coords, k):
    x, y, c = coords
    return (1 - x if k & 4 else x, 1 - y if k & 2 else y, 1 - c if k & 1 else c)


def allgather8(name, v):
    def body(v_ref, out_ref, send_sems, recv_sems, local_sem):
        me3 = _coords()
        me = 4 * me3[0] + 2 * me3[1] + me3[2]
        mine = pltpu.make_async_copy(v_ref, out_ref.at[me], local_sem)
        mine.start()
        sends = []
        for k in range(1, N_DEV):
            cp = pltpu.make_async_remote_copy(src_ref=v_ref, dst_ref=out_ref.at[me], send_sem=send_sems.at[k - 1],
                                              recv_sem=recv_sems.at[k - 1], device_id=_flip(me3, k), device_id_type=MESH)
            cp.start()
            sends.append(cp)
        for k in range(1, N_DEV):
            pltpu.make_async_remote_copy(src_ref=v_ref, dst_ref=out_ref.at[jnp.bitwise_xor(me, k)], send_sem=send_sems.at[k - 1],
                                         recv_sem=recv_sems.at[k - 1], device_id=_flip(me3, k), device_id_type=MESH).wait_recv()
        for cp in sends:
            cp.wait_send()
        mine.wait()

    return pl.pallas_call(
        body, name=name, out_shape=jax.ShapeDtypeStruct((N_DEV,) + v.shape, v.dtype),
        in_specs=[VMEM_SPEC], out_specs=VMEM_SPEC,
        scratch_shapes=[pltpu.SemaphoreType.DMA((N_DEV - 1,)), pltpu.SemaphoreType.DMA((N_DEV - 1,)), pltpu.SemaphoreType.DMA],
        compiler_params=pltpu.CompilerParams(vmem_limit_bytes=VMEM_LIMIT),
    )(v)


def exchange(name, kind, arrays):
    n = len(arrays)

    def body(*refs):
        xrefs = (refs[:n], refs[n:2 * n]) + tuple(refs[2 * n:])
        _exchange_start(xrefs, kind)
        _exchange_wait(xrefs, kind)

    return pl.pallas_call(
        body, name=name, out_shape=[_exchange_shape(kind, a) for a in arrays], in_specs=[HBM_SPEC] * n,
        out_specs=[HBM_SPEC] * n, scratch_shapes=_exchange_sems(n),
    )(*arrays)


def _exchange_shape(kind, a):
    return jax.ShapeDtypeStruct((N_XY,) + a.shape if kind == "gather" else a.shape, a.dtype)


def _exchange_sems(n):
    return [pltpu.SemaphoreType.DMA((n, N_XY - 1)), pltpu.SemaphoreType.DMA((n, N_XY - 1)), pltpu.SemaphoreType.DMA((n,))]


def _exchange_copies(xrefs, kind):
    srcs, dsts, send_sems, recv_sems, local_sems = xrefs
    me3 = _coords()
    me = 2 * me3[0] + me3[1]
    locals_, sends, recvs = [], [], []
    for i, (v_ref, out_ref) in enumerate(zip(srcs, dsts)):
        if kind == "swap":
            cp = pltpu.make_async_remote_copy(src_ref=v_ref, dst_ref=out_ref, send_sem=send_sems.at[i, 0],
                                              recv_sem=recv_sems.at[i, 0], device_id=_flip(me3, 1), device_id_type=MESH)
            sends.append(cp)
            recvs.append(cp)
            continue
        src = (lambda s, r=v_ref: r) if kind == "gather" else (lambda s, r=v_ref: r.at[s])

        def remote(k, src_slot, dst_slot):
            return pltpu.make_async_remote_copy(src_ref=src(src_slot), dst_ref=out_ref.at[dst_slot], send_sem=send_sems.at[i, k - 1],
                                                recv_sem=recv_sems.at[i, k - 1], device_id=_flip(me3, 2 * k), device_id_type=MESH)

        locals_.append(pltpu.make_async_copy(src(me), out_ref.at[me], local_sems.at[i]))
        sends += [remote(k, jnp.bitwise_xor(me, k), me) for k in range(1, N_XY)]
        recvs += [remote(k, me, jnp.bitwise_xor(me, k)) for k in range(1, N_XY)]
    return locals_, sends, recvs


def _exchange_start(xrefs, kind):
    locals_, sends, _ = _exchange_copies(xrefs, kind)
    for cp in locals_ + sends:
        cp.start()


def _exchange_wait(xrefs, kind):
    locals_, sends, recvs = _exchange_copies(xrefs, kind)
    for cp in recvs:
        cp.wait_recv()
    for cp in sends:
        cp.wait_send()
    for cp in locals_:
        cp.wait()


BLOCK_BYTES = 1 << 20


def _rows_block(rows, cols, limit=BLOCK_BYTES):
    for tr in (1024, 512, 256, 128, 64, 32, 16, 8):
        if rows % tr == 0 and tr * cols * 4 <= limit:
            return tr
    return rows


def adamw(name, w, m, v, parts):
    rows, cols = w.shape
    tr = _rows_block(rows, cols)
    n = len(parts)
    c1 = 1.0 - ADAM_B1 ** ADAM_STEP
    c2 = 1.0 - ADAM_B2 ** ADAM_STEP

    def body(*refs):
        w_ref, m_ref, v_ref = refs[:3]
        g_ref, d_ref, nm_ref, nv_ref = refs[3 + n:]
        g = refs[3][...]
        for r in refs[4:3 + n]:
            g = g + r[...]
        nm = ADAM_B1 * m_ref[...] + (1.0 - ADAM_B1) * g
        nv = ADAM_B2 * v_ref[...] + (1.0 - ADAM_B2) * jnp.square(g)
        d_ref[...] = -ADAM_LR * ((nm / c1) / (jnp.sqrt(nv / c2) + ADAM_EPS) + ADAM_WD * w_ref[...])
        g_ref[...], nm_ref[...], nv_ref[...] = g, nm, nv

    spec = pl.BlockSpec((tr, cols), lambda i: (i, 0))
    sds = jax.ShapeDtypeStruct((rows, cols), F32)
    return pl.pallas_call(
        body, name=name, grid=(rows // tr,), in_specs=[spec] * (3 + n), out_specs=[spec] * 4, out_shape=[sds] * 4,
        compiler_params=_params(1),
    )(w, m, v, *parts)


def adamw_slots(name, w, m, v, mine, theirs):
    depth = len(mine)
    rows, cols = mine[0].shape[1:]
    tr = _rows_block(rows, cols, BLOCK_BYTES // 2)
    nblk = rows // tr
    c1 = 1.0 - ADAM_B1 ** ADAM_STEP
    c2 = 1.0 - ADAM_B2 ** ADAM_STEP

    def body(*refs):
        w_ref, m_ref, v_ref = refs[:3]
        slot_refs = refs[3:3 + 2 * depth]
        g_ref, d_ref, nm_ref, nv_ref = refs[3 + 2 * depth:]
        for j in range(depth):
            @pl.when(pl.program_id(0) == j)
            def _():
                def total(ref):
                    tot = ref[0].astype(F32)
                    for s in range(1, N_XY):
                        tot = tot + ref[s].astype(F32)
                    return tot
                g = total(slot_refs[j]) + total(slot_refs[depth + j])
                nm = ADAM_B1 * m_ref[...] + (1.0 - ADAM_B1) * g
                nv = ADAM_B2 * v_ref[...] + (1.0 - ADAM_B2) * jnp.square(g)
                d_ref[...] = -ADAM_LR * ((nm / c1) / (jnp.sqrt(nv / c2) + ADAM_EPS) + ADAM_WD * w_ref[...])
                g_ref[...], nm_ref[...], nv_ref[...] = g, nm, nv

    spec = pl.BlockSpec((tr, cols), lambda l, i: (l * nblk + i, 0))
    slot_specs = [pl.BlockSpec((N_XY, tr, cols), lambda l, i, j=j: (0, jnp.where(l == j, i, 0), 0)) for j in range(depth)] * 2
    sds = jax.ShapeDtypeStruct((depth * rows, cols), F32)
    return pl.pallas_call(
        body, name=name, grid=(depth, nblk), in_specs=[spec] * 3 + slot_specs, out_specs=[spec] * 4, out_shape=[sds] * 4,
        compiler_params=_params(2),
    )(w, m, v, *mine, *theirs)


MOD_COLS = 512


def mod_fwd(call, w_mod, b_sh):
    depth, _, cols = w_mod.shape
    nr = call.shape[0]

    def body(c_ref, w_ref, b_ref, o_ref):
        o_ref[...] = _dg(_silu(c_ref[...]), w_ref[...], 1, 0) + b_ref[...]

    return pl.pallas_call(
        body, name="mod_fwd", grid=(depth, cols // MOD_COLS),
        in_specs=[pl.BlockSpec((nr, D_MODEL), lambda l, j: (0, 0)), pl.BlockSpec((None, D_MODEL, MOD_COLS), lambda l, j: (l, 0, j)),
                  pl.BlockSpec((None, 1, MOD_COLS), lambda l, j: (l, 0, j))],
        out_specs=pl.BlockSpec((None, nr, MOD_COLS), lambda l, j: (l, 0, j)),
        out_shape=jax.ShapeDtypeStruct((depth, nr, cols), F32), compiler_params=_params(2),
    )(call, w_mod, b_sh)


def mod_bwd(call, c_ctx, d_lat, d_ctx, w_mod, ctx_row):
    depth, _, cols = w_mod.shape
    nr, ns = call.shape[0], d_ctx.shape[1]

    def body(c_ref, cc_ref, dl_ref, dc_ref, w_ref, gw_ref, gc_ref):
        crow = jnp.sum(dc_ref[...], axis=0, keepdims=True)
        row = lax.broadcasted_iota(jnp.int32, (nr, 1), 0)
        dm = jnp.where(row == ctx_row, crow, dl_ref[...])
        gw_ref[...] = _dg(_silu(c_ref[...]), dm, 0, 0)
        ds = jnp.sum(_dg(jnp.broadcast_to(crow, (8, MOD_COLS)), w_ref[...], 1, 1), axis=0, keepdims=True) * 0.125
        _, vjp = jax.vjp(_silu, cc_ref[...])
        (part,) = vjp(ds)
        first = (pl.program_id(0) == 0) & (pl.program_id(1) == 0)

        @pl.when(first)
        def _():
            gc_ref[...] = part

        @pl.when(jnp.logical_not(first))
        def _():
            gc_ref[...] += part

    return pl.pallas_call(
        body, name="mod_bwd", grid=(depth, cols // MOD_COLS),
        in_specs=[pl.BlockSpec((nr, D_MODEL), lambda l, j: (0, 0)), pl.BlockSpec((1, D_MODEL), lambda l, j: (0, 0)),
                  pl.BlockSpec((None, nr, MOD_COLS), lambda l, j: (l, 0, j)), pl.BlockSpec((None, ns, MOD_COLS), lambda l, j: (l, 0, j)),
                  pl.BlockSpec((None, D_MODEL, MOD_COLS), lambda l, j: (l, 0, j))],
        out_specs=[pl.BlockSpec((None, D_MODEL, MOD_COLS), lambda l, j: (l, 0, j)), pl.BlockSpec((1, D_MODEL), lambda l, j: (0, 0))],
        out_shape=[jax.ShapeDtypeStruct((depth, D_MODEL, cols), F32), jax.ShapeDtypeStruct((1, D_MODEL), F32)],
        compiler_params=_params(2),
    )(call, c_ctx, d_lat, d_ctx, w_mod)


def bmod_grad(dm_all):
    ndev, depth, ns, cols = dm_all.shape

    def body(d_ref, o_ref):
        tot = d_ref[0]
        for i in range(1, ndev):
            tot = tot + d_ref[i]
        o_ref[...] = jnp.sum(tot, axis=0, keepdims=True)

    return pl.pallas_call(
        body, name="bmod_grad", grid=(depth,), in_specs=[pl.BlockSpec((ndev, None, ns, cols), lambda l: (0, l, 0, 0))],
        out_specs=pl.BlockSpec((None, 1, cols), lambda l: (l, 0, 0)), out_shape=jax.ShapeDtypeStruct((depth, 1, cols), F32),
        compiler_params=_params(1),
    )(dm_all)


def small_reduce(gathered, rows_all):
    ndev, rows, lanes = gathered.shape

    def body(g_ref, o_ref):
        tot = g_ref[0, 0:rows_all]
        for i in range(1, ndev):
            tot = tot + g_ref[i, 0:rows_all]
        o_ref[0:rows_all] = tot
        part = g_ref[0, rows_all:rows]
        for i in range(2, ndev, 2):
            part = part + g_ref[i, rows_all:rows]
        o_ref[rows_all:rows] = part

    return pl.pallas_call(body, name="small_reduce", out_shape=jax.ShapeDtypeStruct((rows, lanes), F32),
                          in_specs=[VMEM_SPEC], out_specs=VMEM_SPEC)(gathered)


def pack_rows(arrays, row_multiple=8):
    flat = jnp.concatenate([a.reshape(-1).astype(F32) for a in arrays])
    per = LANES * row_multiple
    padded = -(-flat.shape[0] // per) * per
    return jnp.pad(flat, (0, padded - flat.shape[0])).reshape(-1, LANES)


def unpack_rows(buf, shapes):
    flat, out, off = buf.reshape(-1), [], 0
    for s in shapes:
        n = int(np.prod(s))
        out.append(flat[off:off + n].reshape(s))
        off += n
    return out


SMALL_SHARDED = ("g_norm", "dn_conv", "sc_conv")
SMALL_ORDER = ("g_norm", "att_q_gain", "att_k_gain", "dn_conv", "dn_a_log", "dn_dt_bias", "dn_norm_gain", "ret_decay", "sc_conv")


def kernel(x, c, ctx, c_ctx, w_mod, b_mod, g_norm, w_in, att_q_gain, att_k_gain, dn_conv, dn_a_log, dn_dt_bias, dn_norm_gain, ret_decay, sc_conv, w_branch, w_out, w_mlp_in, w_mlp_out, loss_target, m_c_ctx, m_w_mod, m_b_mod, m_g_norm, m_w_in, m_att_q_gain, m_att_k_gain, m_dn_conv, m_dn_a_log, m_dn_dt_bias, m_dn_norm_gain, m_ret_decay, m_sc_conv, m_w_branch, m_w_out, m_w_mlp_in, m_w_mlp_out, v_c_ctx, v_w_mod, v_b_mod, v_g_norm, v_w_in, v_att_q_gain, v_att_k_gain, v_dn_conv, v_dn_a_log, v_dn_dt_bias, v_dn_norm_gain, v_ret_decay, v_sc_conv, v_w_branch, v_w_out, v_w_mlp_in, v_w_mlp_out):
    wts = dict(c_ctx=c_ctx, w_mod=w_mod, b_mod=b_mod, g_norm=g_norm, w_in=w_in, att_q_gain=att_q_gain, att_k_gain=att_k_gain,
               dn_conv=dn_conv, dn_a_log=dn_a_log, dn_dt_bias=dn_dt_bias, dn_norm_gain=dn_norm_gain, ret_decay=ret_decay,
               sc_conv=sc_conv, w_branch=w_branch, w_out=w_out, w_mlp_in=w_mlp_in, w_mlp_out=w_mlp_out)
    mom = dict(c_ctx=m_c_ctx, w_mod=m_w_mod, b_mod=m_b_mod, g_norm=m_g_norm, w_in=m_w_in, att_q_gain=m_att_q_gain,
               att_k_gain=m_att_k_gain, dn_conv=m_dn_conv, dn_a_log=m_dn_a_log, dn_dt_bias=m_dn_dt_bias,
               dn_norm_gain=m_dn_norm_gain, ret_decay=m_ret_decay, sc_conv=m_sc_conv, w_branch=m_w_branch, w_out=m_w_out,
               w_mlp_in=m_w_mlp_in, w_mlp_out=m_w_mlp_out)
    var = dict(c_ctx=v_c_ctx, w_mod=v_w_mod, b_mod=v_b_mod, g_norm=v_g_norm, w_in=v_w_in, att_q_gain=v_att_q_gain,
               att_k_gain=v_att_k_gain, dn_conv=v_dn_conv, dn_a_log=v_dn_a_log, dn_dt_bias=v_dn_dt_bias,
               dn_norm_gain=v_dn_norm_gain, ret_decay=v_ret_decay, sc_conv=v_sc_conv, w_branch=v_w_branch, w_out=v_w_out,
               w_mlp_in=v_w_mlp_in, w_mlp_out=v_w_mlp_out)
    names = list(wts)
    depth, bsz, seq, lc = w_mod.shape[0], x.shape[0], x.shape[1], ctx.shape[1]
    g = Geo(bsz, lc + seq, lc)
    xi, yi, ci = _coords()
    dev, xy = 4 * xi + 2 * yi + ci, 2 * xi + yi
    n_batch = N_DEV * bsz
    nr = -(-(n_batch + 1) // 16) * 16
    mod_cols = w_mod.shape[2]

    c_all = allgather8("gather_c", c).reshape(n_batch, D_MODEL)
    call = jnp.concatenate([c_all, c_ctx[None], jnp.zeros((nr - n_batch - 1, D_MODEL), F32)], axis=0)
    b_sh = lax.dynamic_slice_in_dim(b_mod, xy * mod_cols, mod_cols, axis=1)[:, None, :]
    mod_sh = mod_fwd(call, w_mod, b_sh)
    mod_g = allgather8("gather_mod", mod_sh.reshape(depth * nr, mod_cols)).reshape(N_XY, 2, depth, nr, mod_cols)[:, 0]
    mod_all = mod_g.transpose(1, 2, 0, 3).reshape(depth, nr, N_XY * mod_cols)
    mod_lat = lax.dynamic_slice_in_dim(mod_all, dev * bsz, bsz, axis=1).reshape(depth, bsz, 6, D_MODEL)
    mod_ctx = jnp.broadcast_to(mod_all[:, n_batch].reshape(depth, 1, 6, D_MODEL), (depth, bsz, 6, D_MODEL))
    mod = jnp.stack([mod_ctx, mod_lat], axis=2)

    sm_shapes = [wts[k].shape for k in SMALL_SHARDED]
    sm_g = allgather8("gather_small", pack_rows([wts[k] for k in SMALL_SHARDED])).reshape(N_XY, 2, -1)[:, 0]
    small = {k: wts[k] for k in SMALL_ORDER}
    for k, parts in zip(SMALL_SHARDED, zip(*[unpack_rows(sm_g[p], sm_shapes) for p in range(N_XY)])):
        small[k] = jnp.concatenate(parts, axis=-1)

    shards = [[wts[k][l].astype(BF16) for k, _ in BIG] for l in range(depth)]

    xa = jnp.concatenate([ctx, x], axis=1)
    loss_part, dxa, mine, theirs, gsmall, dmod = model_step(g, xa, loss_target, mod, shards, small)
    loss = lax.psum(loss_part, ("x", "y", "c"))
    grad_x = dxa[:, lc:]

    grads, deltas, new_m, new_v = {}, {}, {}, {}

    def update(k, parts, shape2d):
        res = adamw("adamw_" + k, wts[k].reshape(shape2d), mom[k].reshape(shape2d), var[k].reshape(shape2d), parts)
        grads[k], deltas[k], new_m[k], new_v[k] = (r.reshape(wts[k].shape) for r in res)

    for i, (k, _) in enumerate(BIG):
        cols = wts[k].shape[-1]
        slots = lambda per_layer: [s[i].reshape(N_XY, -1, cols) for s in per_layer]
        res = adamw_slots("adamw_" + k, wts[k].reshape(-1, cols), mom[k].reshape(-1, cols), var[k].reshape(-1, cols),
                          slots(mine), slots(theirs))
        grads[k], deltas[k], new_m[k], new_v[k] = (r.reshape(wts[k].shape) for r in res)

    dm_mine = jnp.concatenate([dmod[:, :, 1], dmod[:, :, 0]], axis=1).reshape(depth * 2 * bsz, 6 * D_MODEL)
    dm_all = allgather8("gather_dmod", dm_mine).reshape(N_DEV, depth, 2 * bsz, 6 * D_MODEL)
    gb = bmod_grad(dm_all).reshape(depth, 6 * D_MODEL)
    dm_cols = lax.dynamic_slice_in_dim(dm_all, xy * mod_cols, mod_cols, axis=3)
    d_lat = dm_cols[:, :, :bsz].transpose(1, 0, 2, 3).reshape(depth, n_batch, mod_cols)
    d_lat = jnp.pad(d_lat, ((0, 0), (0, nr - n_batch), (0, 0)))
    d_ctx = dm_cols[:, :, bsz:].transpose(1, 0, 2, 3).reshape(depth, n_batch, mod_cols)
    gw_mod, gc_part = mod_bwd(call, c_ctx[None], d_lat, d_ctx, w_mod, n_batch)
    update("w_mod", [gw_mod.reshape(depth * D_MODEL, mod_cols)], (depth * D_MODEL, mod_cols))
    update("b_mod", [gb], b_mod.shape)

    pack_all = pack_rows([gsmall[k] for k in SMALL_ORDER])
    pack_xy = pack_rows([gc_part])
    rows_all = pack_all.shape[0]
    tot = small_reduce(allgather8("gather_gsmall", jnp.concatenate([pack_all, pack_xy], axis=0)), rows_all)
    gtot = dict(zip(SMALL_ORDER, unpack_rows(tot[:rows_all], [gsmall[k].shape for k in SMALL_ORDER])))
    gtot["c_ctx"] = unpack_rows(tot[rows_all:], [c_ctx.shape])[0]
    for k in SMALL_SHARDED:
        width = wts[k].shape[-1]
        gtot[k] = lax.dynamic_slice_in_dim(gtot[k], xy * width, width, axis=gtot[k].ndim - 1)
    sm_names = ("c_ctx",) + SMALL_ORDER
    sm_shapes = [wts[k].shape for k in sm_names]
    res = adamw("adamw_small", pack_rows([wts[k] for k in sm_names]), pack_rows([mom[k] for k in sm_names]),
                pack_rows([var[k] for k in sm_names]), [pack_rows([gtot[k] for k in sm_names])])
    for dst, buf in zip((grads, deltas, new_m, new_v), res):
        dst.update(zip(sm_names, unpack_rows(buf, sm_shapes)))

    return (loss, grad_x, *[grads[k] for k in names], *[deltas[k] for k in names], *[new_m[k] for k in names],
            *[new_v[k] for k in names])
```

```python
import functools
import math

import numpy as np
import jax
import jax.numpy as jnp
from jax import lax
from jax.experimental import pallas as pl
from jax.experimental.pallas import tpu as pltpu

F32, BF16 = jnp.float32, jnp.bfloat16
HIGHEST = lax.Precision.HIGHEST
MESH = pl.DeviceIdType.MESH

D_MODEL = 1024
GRID_W = 64
N_BRANCH = 4
BRANCH_W = 512
HEAD64 = 64
HEAD128 = 128
N_HEAD4 = 4
ATT_HEADS = 8
ATT_KV_HEADS = 2
CHUNK = 64
MLP_HIDDEN = 4 * D_MODEL
ROPE_THETA = 10000.0
EPS = 1e-6
N_IN = 10000
ADAM_LR, ADAM_B1, ADAM_B2, ADAM_EPS, ADAM_WD, ADAM_STEP = 0.001, 0.9, 0.999, 1e-08, 0.01, 10

LANES = 128
VMEM_LIMIT = 56 * 1024 * 1024

_SEGS = (
    ("gates", 5904, 4096),
    ("att_q", 0, 512), ("dn_q", 768, 512), ("dn_k", 1280, 512), ("dn_v", 1792, 512), ("dn_z", 2304, 512),
    ("ret_v", 3344, 512), ("ret_g", 3856, 512), ("sc_b", 4368, 512), ("sc_c", 4880, 512), ("sc_x", 5392, 512),
    ("ret_q", 2832, 256), ("ret_k", 3088, 256),
    ("att_k", 512, 128), ("att_v", 640, 128),
    ("narrow", 2816, 16),
)
NZ = 10240


def _seg_offsets():
    off, out = 0, {}
    for name, _, width in _SEGS:
        out[name] = off
        off += width
    return out


ZOFF = _seg_offsets()


def _pick(n, cands):
    for c in cands:
        if n % c == 0:
            return c
    return n


def _dg(a, b, ca, cb, batch=False):
    dn = (((ca,), (cb,)), ((0,), (0,))) if batch else (((ca,), (cb,)), ((), ()))
    return lax.dot_general(a.astype(BF16), b.astype(BF16), dn, preferred_element_type=F32)


@functools.partial(jax.custom_vjp, nondiff_argnums=(2, 3))
def _mm(a, b, ca, cb):
    return _dg(a, b, ca, cb)


def _mm_fwd(a, b, ca, cb):
    return _dg(a, b, ca, cb), (a, b)


def _mm_bwd(ca, cb, res, g):
    a, b = res
    if ca == 1:
        da = _mm(g, b, 1, 1) if cb == 0 else _mm(g, b, 1, 0)
    else:
        da = _mm(b, g, 1, 1) if cb == 0 else _mm(b, g, 0, 1)
    if cb == 0:
        db = _mm(a, g, 0, 0) if ca == 1 else _mm(a, g, 1, 0)
    else:
        db = _mm(g, a, 0, 0) if ca == 1 else _mm(g, a, 0, 1)
    return da.astype(a.dtype), db.astype(b.dtype)


_mm.defvjp(_mm_fwd, _mm_bwd)


@functools.partial(jax.custom_vjp, nondiff_argnums=(2, 3))
def _bmm(a, b, ca, cb):
    return _dg(a, b, ca, cb, True)


def _bmm_fwd(a, b, ca, cb):
    return _dg(a, b, ca, cb, True), (a, b)


def _bmm_bwd(ca, cb, res, g):
    a, b = res
    if ca == 2:
        da = _bmm(g, b, 2, 2) if cb == 1 else _bmm(g, b, 2, 1)
    else:
        da = _bmm(b, g, 2, 2) if cb == 1 else _bmm(b, g, 1, 2)
    if cb == 1:
        db = _bmm(a, g, 1, 1) if ca == 2 else _bmm(a, g, 2, 1)
    else:
        db = _bmm(g, a, 1, 1) if ca == 2 else _bmm(g, a, 1, 2)
    return da.astype(a.dtype), db.astype(b.dtype)


_bmm.defvjp(_bmm_fwd, _bmm_bwd)


def _split_bf16(x):
    hi = x.astype(BF16)
    lo = (x - hi.astype(F32)).astype(BF16)
    return hi, lo


def _select3(a, b, cb):
    hi = a.astype(BF16)
    rest = a - hi.astype(F32)
    mid = rest.astype(BF16)
    lo = (rest - mid.astype(F32)).astype(BF16)
    bb = b.astype(BF16)
    d = lambda u: lax.dot_general(u, bb, (((1,), (cb,)), ((), ())), preferred_element_type=F32)
    return d(hi) + (d(mid) + d(lo))


@jax.custom_vjp
def _mm_exact(a, b):
    return _select3(a, b, 0)


_mm_exact.defvjp(lambda a, b: (_select3(a, b, 0), b), lambda b, g: (_select3(g, b, 1), jnp.zeros_like(b)))


class Arg:
    def __init__(self, x, block, imap, diff=True, first=None, gdtype=F32, pieces=None, gshape=None, gimap=None):
        self.x, self.block, self.imap = x, tuple(block), imap
        self.diff, self.first, self.gdtype, self.pieces = diff, first, gdtype, pieces
        self.upcast = x.dtype == BF16 and diff
        self.gshape = tuple(x.shape) if gshape is None else tuple(gshape)
        self.gimap = imap if gimap is None else gimap

    def spec(self):
        return pl.BlockSpec(self.block, self.imap)

    def gspec(self):
        return pl.BlockSpec(self.block, self.gimap)


class Out:
    def __init__(self, shape, dtype, block, imap, pieces=None):
        self.shape, self.dtype, self.block, self.imap, self.pieces = tuple(shape), dtype, tuple(block), imap, pieces

    def spec(self):
        return pl.BlockSpec(self.block, self.imap)

    def sds(self):
        return jax.ShapeDtypeStruct(self.shape, self.dtype)


def _lanes(ref, s, w):
    return (slice(None),) * (len(ref.shape) - 1) + (slice(s, s + w),)


def _load(ref, pieces):
    if pieces is None:
        return ref[...]
    return tuple(ref[_lanes(ref, s, w)] for s, w in pieces)


def _store(ref, val, pieces, accumulate=False):
    if pieces is None:
        if accumulate:
            ref[...] += val.astype(ref.dtype)
        else:
            ref[...] = val.astype(ref.dtype)
        return
    if not accumulate:
        covered = sum(w for _, w in pieces)
        if covered != ref.shape[-1]:
            ref[...] = jnp.zeros(ref.shape, ref.dtype)
    for (s, w), v in zip(pieces, val):
        if accumulate:
            ref[_lanes(ref, s, w)] += v.astype(ref.dtype)
        else:
            ref[_lanes(ref, s, w)] = v.astype(ref.dtype)


def _params(n_grid):
    return pltpu.CompilerParams(dimension_semantics=("arbitrary",) * n_grid, vmem_limit_bytes=VMEM_LIMIT)


def _load_arg(ref, a):
    val = _load(ref, a.pieces)
    return jax.tree.map(lambda t: t.astype(F32), val) if a.upcast else val


def vfwd(name, f, grid, args, outs, comm=None):
    n_in = len(args)

    def body(*refs):
        in_refs, out_refs, _, xrefs = _carried(refs, n_in, len(outs), comm)
        _carry_start(xrefs, comm, grid)
        ids = tuple(pl.program_id(i) for i in range(len(grid)))
        vals = [_load_arg(r, a) for r, a in zip(in_refs, args)]
        res = f(ids, *vals)
        for r, o, spec in zip(out_refs, res, outs):
            _store(r, o, spec.pieces)
        _carry_wait(xrefs, comm, grid)

    return _carrier_call(body, name, grid, [a.spec() for a in args], [a.x for a in args], [o.spec() for o in outs],
                         [o.sds() for o in outs], [], comm)


def vbwd(name, f, grid, args, outs, cts):
    n_in = len(args)
    diff_idx = [i for i, a in enumerate(args) if a.diff]
    ct_flat = [c for per_out in cts for c in per_out]
    ct_specs = [o.spec() for o, per_out in zip(outs, cts) for _ in per_out]
    n_ct = len(ct_flat)

    def body(*refs):
        ids = tuple(pl.program_id(i) for i in range(len(grid)))
        vals = [_load_arg(r, a) for r, a in zip(refs[:n_in], args)]
        ct_refs = refs[n_in:n_in + n_ct]
        g_refs = refs[n_in + n_ct:]

        def g(*dvals):
            full = list(vals)
            for i, v in zip(diff_idx, dvals):
                full[i] = v
            return tuple(f(ids, *full))

        _, vjp = jax.vjp(g, *[vals[i] for i in diff_idx])
        ct_vals, k = [], 0
        for o, per_out in zip(outs, cts):
            tot = None
            for _ in per_out:
                v = _load(ct_refs[k], o.pieces)
                k += 1
                v = jax.tree.map(lambda t: t.astype(F32), v)
                tot = v if tot is None else jax.tree.map(jnp.add, tot, v)
            ct_vals.append(tot)
        grads = vjp(tuple(ct_vals))
        for gr, gv, i in zip(g_refs, grads, diff_idx):
            a = args[i]
            if a.first is None:
                _store(gr, gv, a.pieces)
            else:
                is_first = a.first(ids)

                @pl.when(is_first)
                def _():
                    _store(gr, gv, a.pieces)

                @pl.when(jnp.logical_not(is_first))
                def _():
                    _store(gr, gv, a.pieces, accumulate=True)

    g_specs = [args[i].gspec() for i in diff_idx]
    g_shapes = [jax.ShapeDtypeStruct(args[i].gshape, args[i].gdtype) for i in diff_idx]
    return pl.pallas_call(
        body, name=name, grid=grid,
        in_specs=[a.spec() for a in args] + ct_specs, out_specs=g_specs, out_shape=g_shapes,
        compiler_params=_params(len(grid)),
    )(*[a.x for a in args], *ct_flat)


def matmul(name, a, b, trans_a=False, trans_b=False, epilogue=None, extras=(), out_dtypes=(F32,)):
    if trans_a:
        kdim, m = a.shape
    else:
        m, kdim = a.shape
    n = b.shape[0] if trans_b else b.shape[1]
    assert b.shape[1 if trans_b else 0] == kdim
    tm = _pick(m, (1024, 512, 256, 192, 128, 64))
    tn = _pick(n, (1024, 512, 256, 128))
    tk = _pick(kdim, (1024, 512, 256, 192, 128, 64))
    nk = kdim // tk
    n_x = len(extras)
    assert epilogue is None or nk == 1
    assert epilogue is not None or tuple(out_dtypes) == (F32,)

    def body(a_ref, b_ref, *rest):
        o_ref = rest[n_x]
        part = _dg(a_ref[...], b_ref[...], 0 if trans_a else 1, 1 if trans_b else 0)
        if epilogue is not None:
            for r, val in zip(rest[n_x:], epilogue(part, *[x[...] for x in rest[:n_x]])):
                r[...] = val.astype(r.dtype)
        elif nk == 1:
            o_ref[...] = part
        else:
            k = pl.program_id(2)

            @pl.when(k == 0)
            def _():
                o_ref[...] = part

            @pl.when(k > 0)
            def _():
                o_ref[...] += part

    a_spec = pl.BlockSpec((tk, tm), lambda i, j, k: (k, i)) if trans_a else pl.BlockSpec((tm, tk), lambda i, j, k: (i, k))
    b_spec = pl.BlockSpec((tn, tk), lambda i, j, k: (j, k)) if trans_b else pl.BlockSpec((tk, tn), lambda i, j, k: (k, j))
    mn_spec = pl.BlockSpec((tm, tn), lambda i, j, k: (i, j))
    res = pl.pallas_call(
        body, name=name, grid=(m // tm, n // tn, nk),
        in_specs=[a_spec, b_spec] + [mn_spec] * n_x,
        out_specs=[mn_spec] * len(out_dtypes),
        out_shape=[jax.ShapeDtypeStruct((m, n), dt) for dt in out_dtypes],
        compiler_params=pltpu.CompilerParams(dimension_semantics=("parallel", "parallel", "arbitrary"),
                                             vmem_limit_bytes=VMEM_LIMIT),
    )(a, b, *extras)
    return res if epilogue is not None else res[0]


def _rms(x, gain):
    return x * lax.rsqrt(jnp.mean(x * x, axis=-1, keepdims=True) + EPS) * gain


def _silu(x):
    return x * jax.nn.sigmoid(x)


def f_modnorm(ids, x, gain, mod):
    return (_rms(x, gain) * (1.0 + mod[1:2]) + mod[0:1],)


def f_resnorm(ids, x, y, g_res, g_next, mod):
    x_new = x + mod[0:1] * _rms(y, g_res)
    return x_new, _rms(x_new, g_next) * (1.0 + mod[2:3]) + mod[1:2]


def f_resid(ids, x, y, g_res, mod):
    return (x + mod[0:1] * _rms(y, g_res),)


def _head_consts(width, head):
    i = lax.broadcasted_iota(jnp.int32, (width, width), 0)
    j = lax.broadcasted_iota(jnp.int32, (width, width), 1)
    shift = int(math.log2(head))
    same = (i >> shift) == (j >> shift)
    group = jnp.where(same, 1.0 / head, 0.0).astype(F32)
    half = head // 2
    ii, jj = i & (head - 1), j & (head - 1)
    rot = jnp.where(same & (ii == jj + half) & (jj < half), -1.0, 0.0) + jnp.where(same & (ii + half == jj) & (jj >= half), 1.0, 0.0)
    ti = lax.broadcasted_iota(jnp.int32, (head, width), 0)
    tj = lax.broadcasted_iota(jnp.int32, (head, width), 1)
    tile = jnp.where(ti == (tj & (head - 1)), 1.0, 0.0).astype(F32)
    return group, rot.astype(F32), tile


def _rope(x, cos, sin, rot):
    return x * cos + _mm_exact(x, rot) * sin


def _softplus(x):
    return jnp.maximum(x, 0.0) + jnp.log(1.0 + jnp.exp(-jnp.abs(x)))


def f_prep(ids, zq, zk, rq, rk, zn, qgain, kgain, alog, dtb, cos, sin):
    grp_q, rot_q, tile_q = _head_consts(ATT_HEADS * HEAD64, HEAD64)
    grp_k, rot_k, tile_k = _head_consts(ATT_KV_HEADS * HEAD64, HEAD64)
    grp_r, rot_r, _ = _head_consts(N_HEAD4 * HEAD64, HEAD64)
    wq, wk, wr = zq.shape[-1], zk.shape[-1], rq.shape[-1]
    tile = lambda gain, sel: jnp.dot(gain, sel, precision=HIGHEST, preferred_element_type=F32)
    qn = zq * lax.rsqrt(_mm_exact(zq * zq, grp_q) + EPS) * tile(qgain, tile_q)
    kn = zk * lax.rsqrt(_mm_exact(zk * zk, grp_k) + EPS) * tile(kgain, tile_k)
    q_att = _rope(qn, cos[:, :wq], sin[:, :wq], rot_q) * (HEAD64 ** -0.5)
    k_att = _rope(kn, cos[:, :wk], sin[:, :wk], rot_k)
    q_ret = _rope(rq, cos[:, :wr], sin[:, :wr], rot_r)
    k_ret = _rope(rk * (HEAD64 ** -0.5), cos[:, :wr], sin[:, :wr], rot_r)
    lane = lax.broadcasted_iota(jnp.int32, zn.shape, 1)
    log_a = -jnp.exp(alog) * _softplus(zn + dtb)
    gates = jnp.where(lane < 8, log_a, jnp.where(lane < 16, jax.nn.sigmoid(zn), 0.0))
    return q_att, k_att, q_ret, k_ret, gates


def make_f_attn(tq):
    def f_attn(ids, q, k, v):
        outs = []
        per = ATT_HEADS // ATT_KV_HEADS
        for g in range(ATT_KV_HEADS):
            qg = jnp.concatenate(q[g * per:(g + 1) * per], axis=0)
            s = _mm(qg, k[g], 1, 1)
            e = jnp.exp(s - lax.stop_gradient(jnp.max(s, axis=-1, keepdims=True)))
            o = _mm(e, v[g], 1, 0) * (1.0 / jnp.sum(e, axis=-1, keepdims=True))
            outs += [o[i * tq:(i + 1) * tq] for i in range(per)]
        return (tuple(outs),)
    return f_attn


def _roll_rows(x, shift):
    return pltpu.roll(x, shift, 0)


def make_shifts(t, lc):
    def _down(x):
        row = lax.broadcasted_iota(jnp.int32, x.shape, 0)
        return jnp.where((row == 0) | (row == lc), 0.0, _roll_rows(x, 1))

    def _up(x):
        row = lax.broadcasted_iota(jnp.int32, x.shape, 0)
        return jnp.where((row == lc - 1) | (row == t - 1), 0.0, _roll_rows(x, t - 1))

    @jax.custom_vjp
    def down(x):
        return _down(x)

    @jax.custom_vjp
    def up(x):
        return _up(x)

    down.defvjp(lambda x: (_down(x), None), lambda _, g: (up(g),))
    up.defvjp(lambda x: (_up(x), None), lambda _, g: (down(g),))
    return down, up


def make_conv3(t, lc):
    down, up = make_shifts(t, lc)

    def conv3(x, w):
        return w[0:1] * down(x) + w[1:2] * x + w[2:3] * up(x)
    return conv3


def make_f_dnprep(t, lc):
    conv3 = make_conv3(t, lc)

    def l2n(x):
        return x * lax.rsqrt(jnp.sum(x * x, axis=-1, keepdims=True) + EPS)

    def f_dnprep(ids, q, k, v, wq, wk, wv):
        qn = l2n(_silu(conv3(q, wq))) * (HEAD128 ** -0.5)
        kn = l2n(_silu(conv3(k, wk)))
        return qn, kn, _silu(conv3(v, wv))
    return f_dnprep


def make_f_shortconv(t, lc):
    conv3 = make_conv3(t, lc)

    def f_shortconv(ids, b, c, x, w):
        return (b * conv3(c * x, w),)
    return f_shortconv


def f_finish(ids, o_dn_f, o_dn_b, z_dn, o_rt_f, o_rt_b, g_rt, ngain):
    y_dn, y_rt = [], []
    for h in range(N_HEAD4):
        o = o_dn_f[h] + o_dn_b[h]
        y_dn.append(_rms(o, ngain) * _silu(z_dn[h]))
        r = o_rt_f[h] + o_rt_b[h]
        mu = jnp.mean(r, axis=-1, keepdims=True)
        var = jnp.mean(jnp.square(r - mu), axis=-1, keepdims=True)
        y_rt.append((r - mu) * lax.rsqrt(var + EPS) * _silu(g_rt[h]))
    return tuple(y_dn), tuple(y_rt)


def f_merge(ids, p0, p1, p2, p3, gates):
    u = jax.nn.sigmoid(gates[0]) * p0
    for g, p in zip(gates[1:], (p1, p2, p3)):
        u = u + jax.nn.sigmoid(g) * p
    return (u,)


def _lane_pieces(width, n=N_HEAD4):
    return [(h * width, width) for h in range(n)]


class Tok:
    def __init__(self, x, pieces, width=None, col=0):
        self.x, self.pieces, self.col = x, pieces, col
        self.width = x.shape[-1] if width is None else width
        self.upcast = x.dtype == BF16


def _chunk_of(direction, step, nc, nctx):
    if direction == 0:
        return step
    return jnp.where(step < nctx, nctx - 1 - step, nc + nctx - 1 - step)


ROWS = N_HEAD4 * CHUNK


def _bmm3(a, b, ca, cb):
    ah, al = _split_bf16(a)
    bh, bl = _split_bf16(b)
    dn = (((ca,), (cb,)), ((0,), (0,)))
    d = lambda u, v: lax.dot_general(u, v, dn, preferred_element_type=F32)
    return d(ah, bh) + (d(ah, bl) + d(al, bh))


def _inv_unit_tri_b(low):
    n = low.shape[-1]
    eye = (lax.broadcasted_iota(jnp.int32, (1, n, n), 1) == lax.broadcasted_iota(jnp.int32, (1, n, n), 2)).astype(F32)
    m = -low
    p = eye + m
    for _ in range(int(math.log2(CHUNK)) - 1):
        m = _bmm3(m, m, 2, 1)
        p = p + _bmm3(p, m, 2, 1)
    return p


@jax.custom_vjp
def _tri_solve_b(low, rhs, inv):
    return _bmm3(inv, rhs, 2, 1)


def _tri_solve_b_fwd(low, rhs, inv):
    x = _bmm3(inv, rhs, 2, 1)
    return x, (inv, x)


def _tri_solve_b_bwd(res, g):
    inv, x = res
    d_rhs = _bmm3(inv, g, 1, 1)
    return -_bmm3(d_rhs, x, 2, 2), d_rhs, jnp.zeros_like(inv)


_tri_solve_b.defvjp(_tri_solve_b_fwd, _tri_solve_b_bwd)


def _stack_dirs(toks, i):
    return jnp.concatenate([jnp.concatenate(t[i], axis=1) for t in toks], axis=0)


def _problem_masks(p, nb):
    shape = (p, ROWS, ROWS)
    up = lax.broadcasted_iota(jnp.int32, shape, 0) >= nb
    i = lax.broadcasted_iota(jnp.int32, shape, 1)
    j = lax.broadcasted_iota(jnp.int32, shape, 2)
    same = (i >> 6) == (j >> 6)
    pi, pj = i & (CHUNK - 1), j & (CHUNK - 1)
    rel = jnp.where(up, pj - pi, pi - pj)
    last = same & (pj == jnp.where(up, 0, CHUNK - 1))
    return same & (rel >= 0), same & (rel > 0), last, rel


def _split_states(s_new, nd, nb):
    return [s_new[d * nb * N_HEAD4:(d + 1) * nb * N_HEAD4].reshape((nb, N_HEAD4) + s_new.shape[1:]) for d in range(nd)]


def _split_outs(o, nd, nb):
    return [(tuple(o[d * nb:(d + 1) * nb, h * CHUNK:(h + 1) * CHUNK] for h in range(N_HEAD4)),) for d in range(nd)]


def dn_chunks(states, toks, params, aux=None):
    nd, nb = len(toks), states[0].shape[0]
    p = nd * nb
    qs, ks, vs, la, beta = (_stack_dirs(toks, i) for i in range(5))
    incl, strict, last, _ = _problem_masks(p, nb)
    sq = (p, ROWS, ROWS)
    g = jnp.sum(jnp.where(incl, jnp.swapaxes(jnp.broadcast_to(la, sq), 1, 2), 0.0), axis=2, keepdims=True)
    gb = jnp.broadcast_to(g, sq)
    gbt = jnp.swapaxes(gb, 1, 2)
    dec_incl = jnp.where(incl, jnp.exp(jnp.where(incl, gb - gbt, 0.0)), 0.0)
    dec_strict = jnp.where(strict, dec_incl, 0.0)
    low = beta * _bmm(ks, ks, 2, 2) * dec_strict
    eg = jnp.exp(g)
    inv = _inv_unit_tri_b(low) if aux is None else jnp.concatenate(aux, axis=0)
    sol = _tri_solve_b(low, jnp.concatenate([beta * vs, (beta * eg) * ks], axis=2), inv)
    w_v, w_k = sol[:, :, :HEAD128], sol[:, :, HEAD128:]
    a_qk = _bmm(qs, ks, 2, 2) * dec_incl
    g_last = jnp.sum(jnp.where(last, gbt, 0.0), axis=2, keepdims=True)
    k_dec = ks * jnp.exp(g_last - g)
    s = jnp.concatenate(states, axis=0).reshape(p * N_HEAD4, HEAD128, HEAD128)
    h3 = lambda x: x.reshape(p * N_HEAD4, CHUNK, x.shape[-1])
    u = w_v - _bmm(h3(w_k), s, 2, 1).reshape(p, ROWS, HEAD128)
    o = _bmm(h3(qs * eg), s, 2, 1).reshape(p, ROWS, HEAD128) + _bmm(a_qk, u, 2, 1)
    decay = jnp.exp(jnp.mean(h3(g_last), axis=1, keepdims=True))
    s_new = s * decay + _bmm(h3(k_dec), h3(u), 1, 1)
    return _split_outs(o, nd, nb), _split_states(s_new, nd, nb), [inv[d * nb:(d + 1) * nb] for d in range(nd)]


def ret_chunks(states, toks, params, aux=None):
    nd, nb = len(toks), states[0].shape[0]
    p = nd * nb
    (decay_log,) = params
    qs, ks, vs = (_stack_dirs(toks, i) for i in range(3))
    incl, _, _, rel = _problem_masks(p, nb)
    col = (p, ROWS, 1)
    up = lax.broadcasted_iota(jnp.int32, col, 0) >= nb
    row = lax.broadcasted_iota(jnp.int32, col, 1)
    head = row >> 6
    lg = jnp.zeros(col, F32)
    for h in range(N_HEAD4):
        rate = jnp.where(up, -jnp.exp(decay_log[N_HEAD4 + h]), -jnp.exp(decay_log[h]))
        lg = jnp.where(head == h, rate, lg)
    place = row & (CHUNK - 1)
    pos = jnp.where(up, CHUNK - 1 - place, place).astype(F32)
    dmask = jnp.where(incl, jnp.exp(jnp.where(incl, rel.astype(F32) * lg, 0.0)), 0.0)
    o = _bmm(_bmm(qs, ks, 2, 2) * dmask, vs, 2, 1)
    s = jnp.concatenate(states, axis=0).reshape(p * N_HEAD4, HEAD64, HEAD128)
    h3 = lambda x: x.reshape(p * N_HEAD4, CHUNK, x.shape[-1])
    kv = _bmm(h3(ks * jnp.exp((CHUNK - 1.0 - pos) * lg)), h3(vs), 1, 1)
    o = o + _bmm(h3(qs * jnp.exp((pos + 1.0) * lg)), s, 2, 1).reshape(p, ROWS, HEAD128)
    decay = jnp.exp(CHUNK * jnp.mean(h3(lg), axis=1, keepdims=True))
    s_new = s * decay + kv
    return _split_outs(o, nd, nb), _split_states(s_new, nd, nb), []


def _scan_nb(bsz, most=2):
    return next(n for n in (4, 2, 1) if n <= most and bsz % n == 0)


def _carried(refs, n_in, n_out, comm):
    if comm is None:
        return refs[:n_in], refs[n_in:n_in + n_out], refs[n_in + n_out:], None
    n = len(comm[1])
    ins, srcs = refs[:n_in], refs[n_in:n_in + n]
    outs, dsts = refs[n_in + n:n_in + n + n_out], refs[n_in + n + n_out:n_in + 2 * n + n_out]
    scratch, sems = refs[n_in + 2 * n + n_out:-3], refs[-3:]
    return ins, outs, scratch, (srcs, dsts) + tuple(sems)


def _carry_start(xrefs, comm, grid):
    if comm is None:
        return
    first = functools.reduce(jnp.logical_and, [pl.program_id(i) == 0 for i in range(len(grid))])

    @pl.when(first)
    def _():
        _exchange_start(xrefs, comm[0])


def _carry_wait(xrefs, comm, grid):
    if comm is None:
        return
    last = functools.reduce(jnp.logical_and, [pl.program_id(i) == n - 1 for i, n in enumerate(grid)])

    @pl.when(last)
    def _():
        _exchange_wait(xrefs, comm[0])


def scan2_fwd(name, chunks_fn, dirs, params, state_shape, outs, lc, comm=None, aux_shapes=(), nb_most=2):
    bsz, t = dirs[0][0].x.shape[:2]
    nb = _scan_nb(bsz, nb_most)
    nc, nctx = t // CHUNK, lc // CHUNK
    nd, n_t, n_p, n_o, n_a = len(dirs), len(dirs[0]), len(params), len(outs), len(aux_shapes)
    grid = (bsz // nb, nc)

    def body(*refs):
        ins, out_refs, s_refs, xrefs = _carried(refs, nd * n_t + n_p, nd * (n_o + 1) + n_a, comm)
        tok_refs, par_refs = ins[:nd * n_t], ins[nd * n_t:]
        _carry_start(xrefs, comm, grid)

        @pl.when(pl.program_id(1) == 0)
        def _():
            for s_ref in s_refs:
                s_ref[...] = jnp.zeros(s_ref.shape, F32)

        pv = [_load(r, p) for r, (_, p) in zip(par_refs, params)]
        states = [s_ref[...] for s_ref in s_refs]
        tv = [[_load_arg(r, tk) for r, tk in zip(tok_refs[d * n_t:(d + 1) * n_t], dirs[d])] for d in range(nd)]
        res, s_new, aux = chunks_fn(states, tv, pv)
        for d in range(nd):
            o_refs = out_refs[d * (n_o + 1):(d + 1) * (n_o + 1)]
            o_refs[-1][...] = states[d]
            s_refs[d][...] = s_new[d]
            for r, o, (_, _, pieces) in zip(o_refs, res[d], outs):
                _store(r, o, pieces)
        for r, a in zip(out_refs[nd * (n_o + 1):], aux):
            r[...] = a
        _carry_wait(xrefs, comm, grid)

    nstate = len(state_shape)
    tok_specs, out_specs, out_shapes, operands = [], [], [], []
    for d, toks in enumerate(dirs):
        chunk = lambda c, d=d: _chunk_of(d, c, nc, nctx)
        for tk in toks:
            tok_specs.append(pl.BlockSpec((nb, CHUNK, tk.width), lambda b, c, f=chunk, col=tk.col: (b, f(c), col)))
            operands.append(tk.x)
        for w, dt, _ in outs:
            out_specs.append(pl.BlockSpec((nb, CHUNK, w), lambda b, c, f=chunk: (b, f(c), 0)))
            out_shapes.append(jax.ShapeDtypeStruct((bsz, t, w), dt))
        out_specs.append(pl.BlockSpec((nb, None) + state_shape, lambda b, c, f=chunk: (b, f(c)) + (0,) * nstate))
        out_shapes.append(jax.ShapeDtypeStruct((bsz, nc) + state_shape, F32))
    for shape in aux_shapes:
        out_specs.append(pl.BlockSpec((nb, None) + tuple(shape), lambda b, c, n=len(shape): (b, c) + (0,) * n))
        out_shapes.append(jax.ShapeDtypeStruct((bsz, nc) + tuple(shape), F32))
    par_specs = [pl.BlockSpec(x.shape, lambda b, c: (0, 0)) for x, _ in params]
    operands += [x for x, _ in params]
    return _carrier_call(body, name, grid, tok_specs + par_specs, operands, out_specs, out_shapes,
                         [pltpu.VMEM((nb,) + state_shape, F32)] * nd, comm)


def _carrier_call(body, name, grid, in_specs, operands, out_specs, out_shapes, scratch, comm):
    if comm is not None:
        kind, arrays = comm
        in_specs, operands = in_specs + [HBM_SPEC] * len(arrays), operands + list(arrays)
        out_specs, out_shapes = out_specs + [HBM_SPEC] * len(arrays), out_shapes + [_exchange_shape(kind, a) for a in arrays]
        scratch = scratch + _exchange_sems(len(arrays))
    return pl.pallas_call(body, name=name, grid=grid, in_specs=in_specs, out_specs=out_specs, out_shape=out_shapes,
                          scratch_shapes=scratch, compiler_params=_params(len(grid)))(*operands)


def scan2_bwd(name, chunks_fn, dirs, params, state_shape, outs, lc, sprevs, cts, comm=None, aux=(), nb_most=2):
    bsz, t = dirs[0][0].x.shape[:2]
    nb = _scan_nb(bsz, nb_most)
    nc, nctx = t // CHUNK, lc // CHUNK
    nd, n_t, n_p, n_o, n_a = len(dirs), len(dirs[0]), len(params), len(outs), len(aux)
    per_in = n_t + 1 + n_o
    grid = (bsz // nb, nc)

    def body(*refs):
        refs, g_refs, ds_refs, xrefs = _carried(refs, nd * per_in + n_p + n_a, nd * n_t + n_p, comm)
        par_refs = refs[nd * per_in:nd * per_in + n_p]
        saved = [r[...] for r in refs[nd * per_in + n_p:]]
        fn = chunks_fn if not saved else (lambda s, tk, pr: chunks_fn(s, tk, pr, saved))
        _carry_start(xrefs, comm, grid)
        first = pl.program_id(1) == 0

        @pl.when(first)
        def _():
            for ds_ref in ds_refs:
                ds_ref[...] = jnp.zeros(ds_ref.shape, F32)

        pv = [_load(r, p) for r, (_, p) in zip(par_refs, params)]
        ins = [refs[d * per_in:(d + 1) * per_in] for d in range(nd)]
        tv = [[_load_arg(r, tk) for r, tk in zip(ins[d][:n_t], dirs[d])] for d in range(nd)]
        states = [ins[d][n_t][...] for d in range(nd)]
        _, vjp = jax.vjp(lambda s, tk, pr: fn(s, tk, pr)[:2], states, tv, pv)
        ct = [tuple(_load(r, pieces) for r, (_, _, pieces) in zip(ins[d][n_t + 1:], outs)) for d in range(nd)]
        d_s, d_tv, d_pv = vjp((ct, [ds_ref[...] for ds_ref in ds_refs]))
        for d in range(nd):
            ds_refs[d][...] = d_s[d]
            for r, gv, tk in zip(g_refs[d * n_t:(d + 1) * n_t], d_tv[d], dirs[d]):
                _store(r, gv, tk.pieces)
        very_first = first & (pl.program_id(0) == 0)
        for r, gv, (_, p) in zip(g_refs[nd * n_t:], d_pv, params):
            @pl.when(very_first)
            def _():
                _store(r, gv, p)

            @pl.when(jnp.logical_not(very_first))
            def _():
                _store(r, gv, p, accumulate=True)
        _carry_wait(xrefs, comm, grid)

    nstate = len(state_shape)
    in_specs, operands, g_specs, g_shapes = [], [], [], []
    for d, toks in enumerate(dirs):
        chunk = lambda c, d=d: _chunk_of(d, nc - 1 - c, nc, nctx)
        for tk in toks:
            in_specs.append(pl.BlockSpec((nb, CHUNK, tk.width), lambda b, c, f=chunk, col=tk.col: (b, f(c), col)))
            operands.append(tk.x)
            g_specs.append(pl.BlockSpec((nb, CHUNK, tk.width), lambda b, c, f=chunk: (b, f(c), 0)))
            g_shapes.append(jax.ShapeDtypeStruct((bsz, t, tk.width), F32))
        in_specs.append(pl.BlockSpec((nb, None) + state_shape, lambda b, c, f=chunk: (b, f(c)) + (0,) * nstate))
        operands.append(sprevs[d])
        for (w, _, _), ct in zip(outs, cts[d]):
            in_specs.append(pl.BlockSpec((nb, CHUNK, w), lambda b, c, f=chunk: (b, f(c), 0)))
            operands.append(ct)
    par_specs = [pl.BlockSpec(x.shape, lambda b, c: (0, 0)) for x, _ in params]
    aux_specs = [pl.BlockSpec((nb, None) + a.shape[2:], lambda b, c, n=a.ndim - 2: (b, nc - 1 - c) + (0,) * n) for a in aux]
    operands += [x for x, _ in params] + list(aux)
    return _carrier_call(body, name, grid, in_specs + par_specs + aux_specs, operands, g_specs + par_specs,
                         g_shapes + [jax.ShapeDtypeStruct(x.shape, F32) for x, _ in params],
                         [pltpu.VMEM((nb,) + state_shape, F32)] * nd, comm)


class Geo:
    def __init__(self, bsz, t, lc):
        self.b, self.t, self.lc, self.m = bsz, t, lc, bsz * t
        self.tm = _pick(lc, (256, 128, 64))
        self.tq = _pick(lc, (128, 64))
        assert t % self.tm == 0 and t % CHUNK == 0 and lc % CHUNK == 0
        self.nctx, self.nctx_q = lc // self.tm, lc // self.tq
        self.grid = (bsz, t // self.tm)


def a_tok(g, x, tm=None, **kw):
    tm = tm or g.tm
    return Arg(x, (None, tm, x.shape[-1]), lambda b, j: (b, j, 0), **kw)


def a_ztok(g, z, name, width, tm=None, **kw):
    tm = tm or g.tm
    col = ZOFF[name] // width
    return Arg(z, (None, tm, width), lambda b, j: (b, j, col),
               gshape=(g.b, g.t, width), gimap=lambda b, j: (b, j, 0), **kw)


def a_par(x, **kw):
    return Arg(x, x.shape, lambda b, j: (0, 0), first=lambda ids: (ids[0] == 0) & (ids[1] == 0), **kw)


def a_mod(g, x):
    n = g.nctx
    return Arg(x, (None, None, x.shape[2], D_MODEL), lambda b, j: (b, jnp.where(j >= n, 1, 0), 0, 0),
               first=lambda ids: (ids[1] == 0) | (ids[1] == n))


def a_tab(g, x):
    return Arg(x, (g.tm, x.shape[-1]), lambda b, j: (j, 0), diff=False)


def o_tok(g, w, dtype, tm=None, pieces=None):
    tm = tm or g.tm
    return Out((g.b, g.t, w), dtype, (None, tm, w), lambda b, j: (b, j, 0), pieces)


def addn(name, xs, dtype):
    bsz, t, w = xs[0].shape
    tm = _pick(t, (256, 192, 128, 64))

    def body(*refs):
        tot = refs[0][...].astype(F32)
        for r in refs[1:-1]:
            tot = tot + r[...].astype(F32)
        refs[-1][...] = tot.astype(dtype)

    spec = pl.BlockSpec((None, tm, w), lambda b, j: (b, j, 0))
    return pl.pallas_call(body, name=name, grid=(bsz, t // tm), in_specs=[spec] * len(xs), out_specs=spec,
                          out_shape=jax.ShapeDtypeStruct((bsz, t, w), dtype), compiler_params=_params(2))(*xs)


P64x8 = _lane_pieces(HEAD64, ATT_HEADS)
P64x4 = _lane_pieces(HEAD64, N_HEAD4)
P64x2 = _lane_pieces(HEAD64, ATT_KV_HEADS)
P128x4 = _lane_pieces(HEAD128, N_HEAD4)
P1024x4 = _lane_pieces(D_MODEL, N_BRANCH)


def _gate_pieces(direction):
    la = [(direction * N_HEAD4 + h, 1) for h in range(N_HEAD4)]
    beta = [(8 + direction * N_HEAD4 + h, 1) for h in range(N_HEAD4)]
    return la, beta


RET_PIECES = [(i, 1) for i in range(2 * N_HEAD4)]


def _prep_io(g, z, sp, tabs):
    args = [a_ztok(g, z, "att_q", 512, gdtype=BF16), a_ztok(g, z, "att_k", 128, gdtype=BF16),
            a_ztok(g, z, "ret_q", 256, gdtype=BF16), a_ztok(g, z, "ret_k", 256, gdtype=BF16),
            a_ztok(g, z, "narrow", 128, gdtype=BF16),
            a_par(sp["qgain"]), a_par(sp["kgain"]), a_par(sp["alog"]), a_par(sp["dtb"]),
            a_tab(g, tabs[0]), a_tab(g, tabs[1])]
    outs = [o_tok(g, 512, BF16), o_tok(g, 128, BF16), o_tok(g, 256, F32), o_tok(g, 256, F32), o_tok(g, 128, F32)]
    return args, outs


def _attn_io(g, q_att, k_att, z, latent):
    col = ZOFF["att_v"] // 128
    first = lambda ids: ids[1] == 0
    if latent:
        rows, off, nq = g.t, g.nctx_q, (g.t - g.lc) // g.tq
    else:
        rows, off, nq = g.lc, 0, g.nctx_q
    args = [Arg(q_att, (None, g.tq, 512), lambda b, j: (b, j + off, 0), pieces=P64x8,
                gshape=(g.b, nq * g.tq, 512), gimap=lambda b, j: (b, j, 0)),
            Arg(k_att, (None, rows, 128), lambda b, j: (b, 0, 0), first=first, pieces=P64x2, gshape=(g.b, rows, 128)),
            Arg(z, (None, rows, 128), lambda b, j: (b, 0, col), first=first, pieces=P64x2,
                gshape=(g.b, rows, 128), gimap=lambda b, j: (b, 0, 0))]
    outs = [Out((g.b, nq * g.tq, 512), BF16, (None, g.tq, 512), lambda b, j: (b, j, 0), P64x8)]
    return (g.b, nq), args, outs


def _pad_rows(x, t):
    return jnp.pad(x, ((0, 0), (0, t - x.shape[1]), (0, 0)))


def _colgrid_arg(g, z, name, width_total, **kw):
    col = ZOFF[name] // 128
    return Arg(z, (None, g.t, 128), lambda h, b: (b, 0, col + h),
               gshape=(g.b, g.t, width_total), gimap=lambda h, b: (b, 0, h), **kw)


def _dnprep_io(g, z, sp):
    wfirst = lambda ids: ids[1] == 0
    args = [_colgrid_arg(g, z, "dn_q", 512, gdtype=BF16), _colgrid_arg(g, z, "dn_k", 512, gdtype=BF16),
            _colgrid_arg(g, z, "dn_v", 512, gdtype=BF16)]
    for i in range(3):
        args.append(Arg(sp["dn_conv"][i], (3, 128), lambda h, b: (0, h), first=wfirst))
    outs = [Out((g.b, g.t, 512), F32, (None, g.t, 128), lambda h, b: (b, 0, h)) for _ in range(3)]
    return (N_HEAD4, g.b), args, outs


def _shortconv_io(g, z, sp):
    args = [_colgrid_arg(g, z, "sc_b", 512, gdtype=BF16), _colgrid_arg(g, z, "sc_c", 512, gdtype=BF16),
            _colgrid_arg(g, z, "sc_x", 512, gdtype=BF16),
            Arg(sp["sc_conv"], (3, 128), lambda h, b: (0, h), first=lambda ids: ids[1] == 0)]
    outs = [Out((g.b, g.t, 512), BF16, (None, g.t, 128), lambda h, b: (b, 0, h))]
    return (BRANCH_W // 128, g.b), args, outs


def _finish_io(g, o_dn_f, o_dn_b, z, o_rt_f, o_rt_b, sp):
    args = [a_tok(g, o_dn_f, pieces=P128x4), a_tok(g, o_dn_b, pieces=P128x4),
            a_ztok(g, z, "dn_z", 512, gdtype=BF16, pieces=P128x4),
            a_tok(g, o_rt_f, pieces=P128x4), a_tok(g, o_rt_b, pieces=P128x4),
            a_ztok(g, z, "ret_g", 512, gdtype=BF16, pieces=P128x4), a_par(sp["ngain"])]
    outs = [o_tok(g, 512, BF16, pieces=P128x4), o_tok(g, 512, BF16, pieces=P128x4)]
    return args, outs


def _merge_io(g, ps, z):
    args = [a_tok(g, p, gdtype=BF16) for p in ps] + [a_ztok(g, z, "gates", 4096, gdtype=BF16, pieces=P1024x4)]
    return args, [o_tok(g, D_MODEL, BF16)]


def _dn_dirs(qn, kn, vn, gates):
    dirs = []
    for d in range(2):
        la, beta = _gate_pieces(d)
        dirs.append([Tok(qn, P128x4), Tok(kn, P128x4), Tok(vn, P128x4), Tok(gates, la), Tok(gates, beta)])
    return dirs


def _ret_dirs(q_ret, k_ret, z):
    col = ZOFF["ret_v"] // 512
    return [[Tok(q_ret, P64x4), Tok(k_ret, P64x4), Tok(z, P128x4, 512, col)] for _ in range(2)]


DN_STATE = (N_HEAD4, HEAD128, HEAD128)
RET_STATE = (N_HEAD4, HEAD64, HEAD128)
RET_NB = 4
DN_FWD_NB = 4
SCAN_OUT = [(512, F32, P128x4)]


def layer_fwd(g, x, h, w, sp, mod_a, mod_b, tabs, g_next, comm=None, rest=None):
    tp = {"x": x, "h": h}
    m, lc = g.m, g.lc
    z = matmul("win", h.reshape(m, D_MODEL), w["win"], epilogue=lambda p: (p,), out_dtypes=(BF16,))[0].reshape(g.b, g.t, NZ)
    tp["z"] = z
    args, outs = _prep_io(g, z, sp, tabs)
    q_att, k_att, q_ret, k_ret, gates = vfwd("prep", f_prep, g.grid, args, outs)
    tp.update(q_att=q_att, k_att=k_att, q_ret=q_ret, k_ret=k_ret, gates=gates)
    y_att = []
    for latent in (False, True):
        agrid, args, outs = _attn_io(g, q_att, k_att, z, latent)
        ride = ("gather", rest) if latent and rest is not None else None
        y, *gathered = vfwd("attn", make_f_attn(g.tq), agrid, args, outs, ride)
        y_att.append(y)
        if gathered:
            w = {**w, **weights_of_slots(gathered, 1)}
    tp["w"] = w
    y_att = jnp.concatenate(y_att, axis=1)
    dgrid, args, outs = _dnprep_io(g, z, sp)
    qn, kn, vn = vfwd("dnprep", make_f_dnprep(g.t, lc), dgrid, args, outs)
    tp.update(qn=qn, kn=kn, vn=vn)
    o_f, s_f, o_b, s_b, inv_f, inv_b, *carried = scan2_fwd("dnscan", dn_chunks, _dn_dirs(qn, kn, vn, gates), [], DN_STATE,
                                                           SCAN_OUT, lc, comm, [(ROWS, ROWS)] * 2, nb_most=DN_FWD_NB)
    tp["dn_inv"] = [inv_f, inv_b]
    tp["carried"] = carried if carried else None
    o_dn, tp["dn_s"] = [o_f, o_b], [s_f, s_b]
    o_f, s_f, o_b, s_b = scan2_fwd("retscan", ret_chunks, _ret_dirs(q_ret, k_ret, z), [(sp["ret"], RET_PIECES)], RET_STATE, SCAN_OUT, lc,
                                   nb_most=RET_NB)
    o_rt, tp["rt_s"] = [o_f, o_b], [s_f, s_b]
    tp.update(o_dn=o_dn, o_rt=o_rt)
    args, outs = _finish_io(g, o_dn[0], o_dn[1], z, o_rt[0], o_rt[1], sp)
    y_dn, y_rt = vfwd("finish", f_finish, g.grid, args, outs)
    sgrid, args, outs = _shortconv_io(g, z, sp)
    (y_sc,) = vfwd("shortconv", make_f_shortconv(g.t, lc), sgrid, args, outs)
    ys = [y_att, y_dn, y_rt, y_sc]
    tp["ys"] = ys
    ps = [matmul("wbranch", y.reshape(m, BRANCH_W), w["wbr"][i], epilogue=lambda p: (p,), out_dtypes=(BF16,))[0]
          .reshape(g.b, g.t, D_MODEL) for i, y in enumerate(ys)]
    tp["ps"] = ps
    args, outs = _merge_io(g, ps, z)
    (u,) = vfwd("merge", f_merge, g.grid, args, outs)
    tp["u"] = u
    y = matmul("wout", u.reshape(m, D_MODEL), w["wout"]).reshape(g.b, g.t, D_MODEL)
    tp["y"] = y
    args = [a_tok(g, x), a_tok(g, y), a_par(sp["g1"]), a_par(sp["g2"]), a_mod(g, mod_a)]
    x1, h2 = vfwd("resnorm", f_resnorm, g.grid, args, [o_tok(g, D_MODEL, F32), o_tok(g, D_MODEL, BF16)])
    tp.update(x1=x1, h2=h2)
    tp["a"], tp["r"] = matmul("wmlp1", h2.reshape(m, D_MODEL), w["w1"], epilogue=lambda p: (p, jnp.square(jnp.maximum(p, 0.0))),
                              out_dtypes=(F32, BF16))
    mo = matmul("wmlp2", tp["r"], w["w2"]).reshape(g.b, g.t, D_MODEL)
    tp["mo"] = mo
    if g_next is None:
        args = [a_tok(g, x1), a_tok(g, mo), a_par(sp["g3"]), a_mod(g, mod_b)]
        (x2,) = vfwd("resid", f_resid, g.grid, args, [o_tok(g, D_MODEL, F32)])
        return x2, None, tp
    args = [a_tok(g, x1), a_tok(g, mo), a_par(sp["g3"]), a_par(g_next), a_mod(g, mod_b)]
    x2, h_next = vfwd("resnorm", f_resnorm, g.grid, args, [o_tok(g, D_MODEL, F32), o_tok(g, D_MODEL, BF16)])
    return x2, h_next, tp


def layer_bwd(g, tp, sp, mod_a, mod_b, tabs, g_next, dx2, dh_next, send=None):
    m, lc = g.m, g.lc
    w = tp["w"]
    gw, gs = {}, {}
    x, z = tp["x"], tp["z"]
    if g_next is None:
        args = [a_tok(g, tp["x1"]), a_tok(g, tp["mo"]), a_par(sp["g3"]), a_mod(g, mod_b)]
        dx1, dmo, gs["g3"], dmod_b = vbwd("resid_b", f_resid, g.grid, args, [o_tok(g, D_MODEL, F32)], [[dx2]])
    else:
        args = [a_tok(g, tp["x1"]), a_tok(g, tp["mo"]), a_par(sp["g3"]), a_par(g_next), a_mod(g, mod_b)]
        dx1, dmo, gs["g3"], gs["g0_next"], dmod_b = vbwd(
            "resnorm_b", f_resnorm, g.grid, args, [o_tok(g, D_MODEL, F32), o_tok(g, D_MODEL, BF16)], [[dx2], [dh_next]])
    dmo2 = dmo.reshape(m, D_MODEL)
    (da2,) = matmul("wmlp2_dx", dmo2, w["w2"], trans_b=True, extras=[tp["a"]],
                    epilogue=lambda p, a: (p * (2.0 * jnp.maximum(a, 0.0)),), out_dtypes=(BF16,))
    gw["w2"] = matmul("wmlp2_dw", tp["r"], dmo2, trans_a=True)
    dh2 = matmul("wmlp1_dx", da2, w["w1"], trans_b=True).reshape(g.b, g.t, D_MODEL)
    gw["w1"] = matmul("wmlp1_dw", tp["h2"].reshape(m, D_MODEL), da2, trans_a=True)
    args = [a_tok(g, x), a_tok(g, tp["y"]), a_par(sp["g1"]), a_par(sp["g2"]), a_mod(g, mod_a)]
    dx, dy, gs["g1"], gs["g2"], dmod_a = vbwd(
        "resnorm_b", f_resnorm, g.grid, args, [o_tok(g, D_MODEL, F32), o_tok(g, D_MODEL, BF16)], [[dx1], [dh2]])
    dy2 = dy.reshape(m, D_MODEL)
    du = matmul("wout_dx", dy2, w["wout"], trans_b=True).reshape(g.b, g.t, D_MODEL)
    gw["wout"] = matmul("wout_dw", tp["u"].reshape(m, D_MODEL), dy2, trans_a=True)
    args, outs = _merge_io(g, tp["ps"], z)
    *dps, dz_gates = vbwd("merge_b", f_merge, g.grid, args, outs, [[du]])
    dys, gwbr = [], []
    for i in range(N_BRANCH):
        dp2 = dps[i].reshape(m, D_MODEL)
        dys.append(matmul("wbranch_dx", dp2, w["wbr"][i], trans_b=True).reshape(g.b, g.t, BRANCH_W))
        gwbr.append(matmul("wbranch_dw", tp["ys"][i].reshape(m, BRANCH_W), dp2, trans_a=True))
    gw["wbr"] = jnp.stack(gwbr)
    dy_att, dy_dn, dy_rt, dy_sc = dys
    o_dn, o_rt = tp["o_dn"], tp["o_rt"]
    args, outs = _finish_io(g, o_dn[0], o_dn[1], z, o_rt[0], o_rt[1], sp)
    do_dn_f, do_dn_b, dz_dnz, do_rt_f, do_rt_b, dz_retg, gs["ngain"] = vbwd(
        "finish_b", f_finish, g.grid, args, outs, [[dy_dn], [dy_rt]])
    comm = ("scatter", list(send) + grad_slots(gw, WKEYS[1:])) if send is not None else None
    res = scan2_bwd("dnscan_b", dn_chunks, _dn_dirs(tp["qn"], tp["kn"], tp["vn"], tp["gates"]), [], DN_STATE, SCAN_OUT, lc,
                   tp["dn_s"], [[do_dn_f], [do_dn_b]], comm, tp["dn_inv"])
    dqn, dkn, dvn, dgates = [res[0], res[5]], [res[1], res[6]], [res[2], res[7]], [res[3], res[4], res[8], res[9]]
    received = list(res[10:]) if comm is not None else None
    swap = ("swap", received) if comm is not None else None
    res = scan2_bwd("retscan_b", ret_chunks, _ret_dirs(tp["q_ret"], tp["k_ret"], z), [(sp["ret"], RET_PIECES)], RET_STATE, SCAN_OUT, lc,
                   tp["rt_s"], [[do_rt_f], [do_rt_b]], swap, nb_most=RET_NB)
    dq_ret, dk_ret, gs["ret"] = [res[0], res[3]], [res[1], res[4]], res[6]
    carried = (received, list(res[7:])) if comm is not None else None
    dz_retv = addn("sum_retv", [res[2], res[5]], BF16)
    sgrid, args, outs = _shortconv_io(g, z, sp)
    dz_scb, dz_scc, dz_scx, gs["sc_conv"] = vbwd("shortconv_b", make_f_shortconv(g.t, lc), sgrid, args, outs, [[dy_sc]])
    dq_att, dk_att, dv_att = [], [], []
    for latent, dy in ((False, dy_att[:, :lc]), (True, dy_att[:, lc:])):
        agrid, args, outs = _attn_io(g, tp["q_att"], tp["k_att"], z, latent)
        dq, dk, dv = vbwd("attn_b", make_f_attn(g.tq), agrid, args, outs, [[dy]])
        dq_att.append(dq); dk_att.append(_pad_rows(dk, g.t)); dv_att.append(_pad_rows(dv, g.t))
    dq_att = jnp.concatenate(dq_att, axis=1)
    dz_attv = addn("sum_attv", dv_att, BF16)
    dgrid, args, outs = _dnprep_io(g, z, sp)
    dz_dnq, dz_dnk, dz_dnv, gc_q, gc_k, gc_v = vbwd("dnprep_b", make_f_dnprep(g.t, lc), dgrid, args, outs, [dqn, dkn, dvn])
    gs["dn_conv"] = [gc_q, gc_k, gc_v]
    args, outs = _prep_io(g, z, sp, tabs)
    dz_attq, dz_attk, dz_retq, dz_retk, dz_nar, gs["qgain"], gs["kgain"], gs["alog"], gs["dtb"] = vbwd(
        "prep_b", f_prep, g.grid, args, outs, [[dq_att], dk_att, dq_ret, dk_ret, dgates])
    pad = jnp.zeros((g.b, g.t, NZ - (ZOFF["narrow"] + 128)), BF16)
    dz = jnp.concatenate([dz_gates, dz_attq, dz_dnq, dz_dnk, dz_dnv, dz_dnz, dz_retv, dz_retg, dz_scb, dz_scc, dz_scx,
                          dz_retq, dz_retk, dz_attk, dz_attv, dz_nar, pad], axis=-1)
    dz2 = dz.reshape(m, NZ)
    dh = matmul("win_dx", dz2, w["win"], trans_b=True).reshape(g.b, g.t, D_MODEL)
    gw["win"] = matmul("win_dw", tp["h"].reshape(m, D_MODEL), dz2, trans_a=True)
    return dx, dh, gw, gs, dmod_a, dmod_b, carried


def loss_call(g, xa, tgt):
    n = g.nctx
    inv_d = 1.0 / D_MODEL

    def body(x_ref, t_ref, loss_ref, dx_ref):
        j = pl.program_id(1)

        @pl.when((pl.program_id(0) == 0) & (j == 0))
        def _():
            loss_ref[...] = jnp.zeros(loss_ref.shape, F32)

        @pl.when(j < n)
        def _():
            dx_ref[...] = jnp.zeros(dx_ref.shape, F32)

        @pl.when(j >= n)
        def _():
            e = x_ref[...] - t_ref[...]
            dx_ref[...] = e * inv_d
            s = jnp.sum(jnp.sum(e * e, axis=1, keepdims=True), axis=0, keepdims=True)
            loss_ref[...] += jnp.broadcast_to(0.5 * inv_d * s, loss_ref.shape)

    tok = pl.BlockSpec((None, g.tm, D_MODEL), lambda b, j: (b, j, 0))
    return pl.pallas_call(
        body, name="loss", grid=g.grid,
        in_specs=[tok, pl.BlockSpec((None, g.tm, D_MODEL), lambda b, j: (b, jnp.maximum(j - n, 0), 0))],
        out_specs=[pl.BlockSpec((1, LANES), lambda b, j: (0, 0)), tok],
        out_shape=[jax.ShapeDtypeStruct((1, LANES), F32), jax.ShapeDtypeStruct((g.b, g.t, D_MODEL), F32)],
        compiler_params=_params(2),
    )(xa, tgt)


def rope_tables(g):
    seq = g.t - g.lc
    rows = seq // GRID_W
    r, col = jnp.meshgrid(jnp.arange(rows), jnp.arange(GRID_W), indexing="ij")
    quarter = HEAD64 // 4
    inv_freq = ROPE_THETA ** (-jnp.arange(quarter, dtype=F32) / quarter)
    ang = jnp.concatenate([r.reshape(-1, 1).astype(F32) * inv_freq, col.reshape(-1, 1).astype(F32) * inv_freq], axis=-1)
    cos, sin = jnp.cos(ang), jnp.sin(ang)
    cos = jnp.concatenate([jnp.ones((g.lc, HEAD64 // 2), F32), cos], axis=0)
    sin = jnp.concatenate([jnp.zeros((g.lc, HEAD64 // 2), F32), sin], axis=0)
    reps = 512 // (HEAD64 // 2)
    return jnp.tile(cos, (1, reps)), jnp.tile(sin, (1, reps))


def _row128(v):
    v = v.reshape(1, -1).astype(F32)
    return jnp.pad(v, ((0, 0), (0, LANES - v.shape[1])))


def layer_small(small, l):
    gn = small["g_norm"][l]
    return {
        "g0": gn[0:1], "g1": gn[1:2], "g2": gn[2:3], "g3": gn[3:4],
        "qgain": small["att_q_gain"][l][None], "kgain": small["att_k_gain"][l][None],
        "dn_conv": [small["dn_conv"][l][:, i * 512:(i + 1) * 512] for i in range(3)], "alog": _row128(small["dn_a_log"][l]), "dtb": _row128(small["dn_dt_bias"][l]),
        "ngain": small["dn_norm_gain"][l][None], "ret": _row128(small["ret_decay"][l]), "sc_conv": small["sc_conv"][l],
    }


def permute_win(w):
    parts = [w[..., off:off + width] for _, off, width in _SEGS]
    parts.append(jnp.zeros(w.shape[:-1] + (NZ - N_IN,), w.dtype))
    return jnp.concatenate(parts, axis=-1)


def unpermute_win(gw):
    order = sorted(_SEGS, key=lambda s: s[1])
    return jnp.concatenate([gw[..., ZOFF[name]:ZOFF[name] + width] for name, _, width in order], axis=-1)


BIG = (("w_in", -1), ("w_branch", -1), ("w_out", -2), ("w_mlp_in", -1), ("w_mlp_out", -2))
WKEYS = ("win", "wbr", "wout", "w1", "w2")


def weights_of_slots(slots, first=0):
    out = {}
    for s, key, (_, axis) in zip(slots, WKEYS[first:], BIG[first:]):
        if axis == -2:
            full = s.reshape((-1,) + s.shape[2:])
        else:
            full = jnp.concatenate([s[p] for p in range(N_XY)], axis=axis)
        out[key] = permute_win(full) if key == "win" else full
    return out


def grad_slots(gw, keys):
    out = []
    for key in keys:
        full = unpermute_win(gw[key]) if key == "win" else gw[key]
        axis = BIG[WKEYS.index(key)][1]
        if axis == -2:
            out.append(full.reshape((N_XY, -1) + full.shape[1:]).astype(BF16))
        else:
            out.append(jnp.stack(jnp.split(full, N_XY, axis=axis)).astype(BF16))
    return out


def model_step(g, xa, tgt, mod, shards, small, local=False):
    depth = len(shards)
    tabs = rope_tables(g)
    sps = [layer_small(small, l) for l in range(depth)]
    mods_a = [mod[l][:, :, 2:5] for l in range(depth)]
    mods_b = [jnp.concatenate([mod[l][:, :, 5:6], mod[l + 1][:, :, 0:2]], axis=2) if l + 1 < depth else mod[l][:, :, 5:6]
              for l in range(depth)]
    mod0 = mod[0][:, :, 0:2]
    args0 = [a_tok(g, xa), a_par(sps[0]["g0"]), a_mod(g, mod0)]
    (h,) = vfwd("modnorm", f_modnorm, g.grid, args0, [o_tok(g, D_MODEL, BF16)])
    x, tapes = xa, []
    if local:
        w, rest = weights_of_slots(shards[0]), None
    else:
        w, rest = weights_of_slots(exchange("gather_win", "gather", shards[0][:1])), shards[0][1:]
    for l in range(depth):
        g_next = sps[l + 1]["g0"] if l + 1 < depth else None
        comm = ("gather", shards[l + 1]) if l + 1 < depth and not local else None
        x, h, tp = layer_fwd(g, x, h, w, sps[l], mods_a[l], mods_b[l], tabs, g_next, comm, rest)
        if l + 1 < depth:
            w, rest = weights_of_slots(shards[l + 1] if local else tp["carried"]), None
        tapes.append(tp)
    loss_row, dx = loss_call(g, x, tgt)
    dh, gss, dmods = None, [None] * depth, [None] * depth
    mine, theirs = [[None] * len(BIG) for _ in range(depth)], [[None] * len(BIG) for _ in range(depth)]
    win_slots = None
    for l in reversed(range(depth)):
        g_next = sps[l + 1]["g0"] if l + 1 < depth else None
        send = None if local else ([win_slots] if win_slots is not None else [])
        dx, dh, gw, gss[l], dma, dmb, carried = layer_bwd(g, tapes[l], sps[l], mods_a[l], mods_b[l], tabs, g_next, dx, dh, send)
        (win_new,) = grad_slots(gw, WKEYS[:1])
        if local:
            mine[l] = [win_new] + grad_slots(gw, WKEYS[1:])
        else:
            received, swapped = carried
            if win_slots is not None:
                mine[l + 1][0], theirs[l + 1][0] = received[0], swapped[0]
            mine[l][1:], theirs[l][1:] = received[-4:], swapped[-4:]
        win_slots = win_new
        dmods[l] = (dma, dmb)
    if not local:
        mine[0][:1] = exchange("scatter_win", "scatter", [win_slots])
        theirs[0][:1] = exchange("swap_win", "swap", mine[0][:1])
    dxh, g0_first, dmod0 = vbwd("modnorm_b", f_modnorm, g.grid, args0, [o_tok(g, D_MODEL, BF16)], [[dh]])
    dxa = addn("sum_dx", [dx, dxh], F32)
    dmod = []
    for l in range(depth):
        first2 = dmod0 if l == 0 else dmods[l - 1][1][:, :, 1:3]
        dmod.append(jnp.concatenate([first2, dmods[l][0], dmods[l][1][:, :, 0:1]], axis=2))
    dmod = jnp.stack(dmod)
    def rows(key, n):
        return jnp.stack([gs[key][0, :n] for gs in gss])
    g_norm = jnp.stack([jnp.concatenate([g0_first if l == 0 else gss[l - 1]["g0_next"], gss[l]["g1"], gss[l]["g2"], gss[l]["g3"]], axis=0)
                        for l in range(depth)])
    gsmall = {
        "g_norm": g_norm,
        "att_q_gain": jnp.stack([gs["qgain"][0] for gs in gss]), "att_k_gain": jnp.stack([gs["kgain"][0] for gs in gss]),
        "dn_conv": jnp.stack([jnp.concatenate(gs["dn_conv"], axis=1) for gs in gss]),
        "dn_a_log": rows("alog", 8).reshape(depth, 2, N_HEAD4), "dn_dt_bias": rows("dtb", 8).reshape(depth, 2, N_HEAD4),
        "dn_norm_gain": jnp.stack([gs["ngain"][0] for gs in gss]),
        "ret_decay": rows("ret", 8).reshape(depth, 2, N_HEAD4),
        "sc_conv": jnp.stack([gs["sc_conv"] for gs in gss]),
    }
    return loss_row[0, 0], dxa, mine, theirs, gsmall, dmod


N_DEV = 8
N_XY = 4
HBM_SPEC = pl.BlockSpec(memory_space=pltpu.HBM)
VMEM_SPEC = pl.BlockSpec(memory_space=pltpu.VMEM)


def _coords():
    return lax.axis_index("x"), lax.axis_index("y"), lax.axis_index("c")


def _flip(coords, k):
    x, y, c = coords
    return (1 - x if k & 4 else x, 1 - y if k & 2 else y, 1 - c if k & 1 else c)


def allgather8(name, v):
    def body(v_ref, out_ref, send_sems, recv_sems, local_sem):
        me3 = _coords()
        me = 4 * me3[0] + 2 * me3[1] + me3[2]
        mine = pltpu.make_async_copy(v_ref, out_ref.at[me], local_sem)
        mine.start()
        sends = []
        for k in range(1, N_DEV):
            cp = pltpu.make_async_remote_copy(src_ref=v_ref, dst_ref=out_ref.at[me], send_sem=send_sems.at[k - 1],
                                              recv_sem=recv_sems.at[k - 1], device_id=_flip(me3, k), device_id_type=MESH)
            cp.start()
            sends.append(cp)
        for k in range(1, N_DEV):
            pltpu.make_async_remote_copy(src_ref=v_ref, dst_ref=out_ref.at[jnp.bitwise_xor(me, k)], send_sem=send_sems.at[k - 1],
                                         recv_sem=recv_sems.at[k - 1], device_id=_flip(me3, k), device_id_type=MESH).wait_recv()
        for cp in sends:
            cp.wait_send()
        mine.wait()

    return pl.pallas_call(
        body, name=name, out_shape=jax.ShapeDtypeStruct((N_DEV,) + v.shape, v.dtype),
        in_specs=[VMEM_SPEC], out_specs=VMEM_SPEC,
        scratch_shapes=[pltpu.SemaphoreType.DMA((N_DEV - 1,)), pltpu.SemaphoreType.DMA((N_DEV - 1,)), pltpu.SemaphoreType.DMA],
        compiler_params=pltpu.CompilerParams(vmem_limit_bytes=VMEM_LIMIT),
    )(v)


def exchange(name, kind, arrays):
    n = len(arrays)

    def body(*refs):
        xrefs = (refs[:n], refs[n:2 * n]) + tuple(refs[2 * n:])
        _exchange_start(xrefs, kind)
        _exchange_wait(xrefs, kind)

    return pl.pallas_call(
        body, name=name, out_shape=[_exchange_shape(kind, a) for a in arrays], in_specs=[HBM_SPEC] * n,
        out_specs=[HBM_SPEC] * n, scratch_shapes=_exchange_sems(n),
    )(*arrays)


def _exchange_shape(kind, a):
    return jax.ShapeDtypeStruct((N_XY,) + a.shape if kind == "gather" else a.shape, a.dtype)


def _exchange_sems(n):
    return [pltpu.SemaphoreType.DMA((n, N_XY - 1)), pltpu.SemaphoreType.DMA((n, N_XY - 1)), pltpu.SemaphoreType.DMA((n,))]


def _exchange_copies(xrefs, kind):
    srcs, dsts, send_sems, recv_sems, local_sems = xrefs
    me3 = _coords()
    me = 2 * me3[0] + me3[1]
    locals_, sends, recvs = [], [], []
    for i, (v_ref, out_ref) in enumerate(zip(srcs, dsts)):
        if kind == "swap":
            cp = pltpu.make_async_remote_copy(src_ref=v_ref, dst_ref=out_ref, send_sem=send_sems.at[i, 0],
                                              recv_sem=recv_sems.at[i, 0], device_id=_flip(me3, 1), device_id_type=MESH)
            sends.append(cp)
            recvs.append(cp)
            continue
        src = (lambda s, r=v_ref: r) if kind == "gather" else (lambda s, r=v_ref: r.at[s])

        def remote(k, src_slot, dst_slot):
            return pltpu.make_async_remote_copy(src_ref=src(src_slot), dst_ref=out_ref.at[dst_slot], send_sem=send_sems.at[i, k - 1],
                                                recv_sem=recv_sems.at[i, k - 1], device_id=_flip(me3, 2 * k), device_id_type=MESH)

        locals_.append(pltpu.make_async_copy(src(me), out_ref.at[me], local_sems.at[i]))
        sends += [remote(k, jnp.bitwise_xor(me, k), me) for k in range(1, N_XY)]
        recvs += [remote(k, me, jnp.bitwise_xor(me, k)) for k in range(1, N_XY)]
    return locals_, sends, recvs


def _exchange_start(xrefs, kind):
    locals_, sends, _ = _exchange_copies(xrefs, kind)
    for cp in locals_ + sends:
        cp.start()


def _exchange_wait(xrefs, kind):
    locals_, sends, recvs = _exchange_copies(xrefs, kind)
    for cp in recvs:
        cp.wait_recv()
    for cp in sends:
        cp.wait_send()
    for cp in locals_:
        cp.wait()


BLOCK_BYTES = 1 << 20


def _rows_block(rows, cols, limit=BLOCK_BYTES):
    for tr in (1024, 512, 256, 128, 64, 32, 16, 8):
        if rows % tr == 0 and tr * cols * 4 <= limit:
            return tr
    return rows


def adamw(name, w, m, v, parts):
    rows, cols = w.shape
    tr = _rows_block(rows, cols)
    n = len(parts)
    c1 = 1.0 - ADAM_B1 ** ADAM_STEP
    c2 = 1.0 - ADAM_B2 ** ADAM_STEP

    def body(*refs):
        w_ref, m_ref, v_ref = refs[:3]
        g_ref, d_ref, nm_ref, nv_ref = refs[3 + n:]
        g = refs[3][...]
        for r in refs[4:3 + n]:
            g = g + r[...]
        nm = ADAM_B1 * m_ref[...] + (1.0 - ADAM_B1) * g
        nv = ADAM_B2 * v_ref[...] + (1.0 - ADAM_B2) * jnp.square(g)
        d_ref[...] = -ADAM_LR * ((nm / c1) / (jnp.sqrt(nv / c2) + ADAM_EPS) + ADAM_WD * w_ref[...])
        g_ref[...], nm_ref[...], nv_ref[...] = g, nm, nv

    spec = pl.BlockSpec((tr, cols), lambda i: (i, 0))
    sds = jax.ShapeDtypeStruct((rows, cols), F32)
    return pl.pallas_call(
        body, name=name, grid=(rows // tr,), in_specs=[spec] * (3 + n), out_specs=[spec] * 4, out_shape=[sds] * 4,
        compiler_params=_params(1),
    )(w, m, v, *parts)


def adamw_slots(name, w, m, v, mine, theirs):
    depth = len(mine)
    rows, cols = mine[0].shape[1:]
    tr = _rows_block(rows, cols, BLOCK_BYTES // 2)
    nblk = rows // tr
    c1 = 1.0 - ADAM_B1 ** ADAM_STEP
    c2 = 1.0 - ADAM_B2 ** ADAM_STEP

    def body(*refs):
        w_ref, m_ref, v_ref = refs[:3]
        slot_refs = refs[3:3 + 2 * depth]
        g_ref, d_ref, nm_ref, nv_ref = refs[3 + 2 * depth:]
        for j in range(depth):
            @pl.when(pl.program_id(0) == j)
            def _():
                def total(ref):
                    tot = ref[0].astype(F32)
                    for s in range(1, N_XY):
                        tot = tot + ref[s].astype(F32)
                    return tot
                g = total(slot_refs[j]) + total(slot_refs[depth + j])
                nm = ADAM_B1 * m_ref[...] + (1.0 - ADAM_B1) * g
                nv = ADAM_B2 * v_ref[...] + (1.0 - ADAM_B2) * jnp.square(g)
                d_ref[...] = -ADAM_LR * ((nm / c1) / (jnp.sqrt(nv / c2) + ADAM_EPS) + ADAM_WD * w_ref[...])
                g_ref[...], nm_ref[...], nv_ref[...] = g, nm, nv

    spec = pl.BlockSpec((tr, cols), lambda l, i: (l * nblk + i, 0))
    slot_specs = [pl.BlockSpec((N_XY, tr, cols), lambda l, i, j=j: (0, jnp.where(l == j, i, 0), 0)) for j in range(depth)] * 2
    sds = jax.ShapeDtypeStruct((depth * rows, cols), F32)
    return pl.pallas_call(
        body, name=name, grid=(depth, nblk), in_specs=[spec] * 3 + slot_specs, out_specs=[spec] * 4, out_shape=[sds] * 4,
        compiler_params=_params(2),
    )(w, m, v, *mine, *theirs)


MOD_COLS = 512


def mod_fwd(call, w_mod, b_sh):
    depth, _, cols = w_mod.shape
    nr = call.shape[0]

    def body(c_ref, w_ref, b_ref, o_ref):
        o_ref[...] = _dg(_silu(c_ref[...]), w_ref[...], 1, 0) + b_ref[...]

    return pl.pallas_call(
        body, name="mod_fwd", grid=(depth, cols // MOD_COLS),
        in_specs=[pl.BlockSpec((nr, D_MODEL), lambda l, j: (0, 0)), pl.BlockSpec((None, D_MODEL, MOD_COLS), lambda l, j: (l, 0, j)),
                  pl.BlockSpec((None, 1, MOD_COLS), lambda l, j: (l, 0, j))],
        out_specs=pl.BlockSpec((None, nr, MOD_COLS), lambda l, j: (l, 0, j)),
        out_shape=jax.ShapeDtypeStruct((depth, nr, cols), F32), compiler_params=_params(2),
    )(call, w_mod, b_sh)


def mod_bwd(call, c_ctx, d_lat, d_ctx, w_mod, ctx_row):
    depth, _, cols = w_mod.shape
    nr, ns = call.shape[0], d_ctx.shape[1]

    def body(c_ref, cc_ref, dl_ref, dc_ref, w_ref, gw_ref, gc_ref):
        crow = jnp.sum(dc_ref[...], axis=0, keepdims=True)
        row = lax.broadcasted_iota(jnp.int32, (nr, 1), 0)
        dm = jnp.where(row == ctx_row, crow, dl_ref[...])
        gw_ref[...] = _dg(_silu(c_ref[...]), dm, 0, 0)
        ds = jnp.sum(_dg(jnp.broadcast_to(crow, (8, MOD_COLS)), w_ref[...], 1, 1), axis=0, keepdims=True) * 0.125
        _, vjp = jax.vjp(_silu, cc_ref[...])
        (part,) = vjp(ds)
        first = (pl.program_id(0) == 0) & (pl.program_id(1) == 0)

        @pl.when(first)
        def _():
            gc_ref[...] = part

        @pl.when(jnp.logical_not(first))
        def _():
            gc_ref[...] += part

    return pl.pallas_call(
        body, name="mod_bwd", grid=(depth, cols // MOD_COLS),
        in_specs=[pl.BlockSpec((nr, D_MODEL), lambda l, j: (0, 0)), pl.BlockSpec((1, D_MODEL), lambda l, j: (0, 0)),
                  pl.BlockSpec((None, nr, MOD_COLS), lambda l, j: (l, 0, j)), pl.BlockSpec((None, ns, MOD_COLS), lambda l, j: (l, 0, j)),
                  pl.BlockSpec((None, D_MODEL, MOD_COLS), lambda l, j: (l, 0, j))],
        out_specs=[pl.BlockSpec((None, D_MODEL, MOD_COLS), lambda l, j: (l, 0, j)), pl.BlockSpec((1, D_MODEL), lambda l, j: (0, 0))],
        out_shape=[jax.ShapeDtypeStruct((depth, D_MODEL, cols), F32), jax.ShapeDtypeStruct((1, D_MODEL), F32)],
        compiler_params=_params(2),
    )(call, c_ctx, d_lat, d_ctx, w_mod)


def bmod_grad(dm_all):
    ndev, depth, ns, cols = dm_all.shape

    def body(d_ref, o_ref):
        tot = d_ref[0]
        for i in range(1, ndev):
            tot = tot + d_ref[i]
        o_ref[...] = jnp.sum(tot, axis=0, keepdims=True)

    return pl.pallas_call(
        body, name="bmod_grad", grid=(depth,), in_specs=[pl.BlockSpec((ndev, None, ns, cols), lambda l: (0, l, 0, 0))],
        out_specs=pl.BlockSpec((None, 1, cols), lambda l: (l, 0, 0)), out_shape=jax.ShapeDtypeStruct((depth, 1, cols), F32),
        compiler_params=_params(1),
    )(dm_all)


def small_reduce(gathered, rows_all):
    ndev, rows, lanes = gathered.shape

    def body(g_ref, o_ref):
        tot = g_ref[0, 0:rows_all]
        for i in range(1, ndev):
            tot = tot + g_ref[i, 0:rows_all]
        o_ref[0:rows_all] = tot
        part = g_ref[0, rows_all:rows]
        for i in range(2, ndev, 2):
            part = part + g_ref[i, rows_all:rows]
        o_ref[rows_all:rows] = part

    return pl.pallas_call(body, name="small_reduce", out_shape=jax.ShapeDtypeStruct((rows, lanes), F32),
                          in_specs=[VMEM_SPEC], out_specs=VMEM_SPEC)(gathered)


def pack_rows(arrays, row_multiple=8):
    flat = jnp.concatenate([a.reshape(-1).astype(F32) for a in arrays])
    per = LANES * row_multiple
    padded = -(-flat.shape[0] // per) * per
    return jnp.pad(flat, (0, padded - flat.shape[0])).reshape(-1, LANES)


def unpack_rows(buf, shapes):
    flat, out, off = buf.reshape(-1), [], 0
    for s in shapes:
        n = int(np.prod(s))
        out.append(flat[off:off + n].reshape(s))
        off += n
    return out


SMALL_SHARDED = ("g_norm", "dn_conv", "sc_conv")
SMALL_ORDER = ("g_norm", "att_q_gain", "att_k_gain", "dn_conv", "dn_a_log", "dn_dt_bias", "dn_norm_gain", "ret_decay", "sc_conv")


def kernel(x, c, ctx, c_ctx, w_mod, b_mod, g_norm, w_in, att_q_gain, att_k_gain, dn_conv, dn_a_log, dn_dt_bias, dn_norm_gain, ret_decay, sc_conv, w_branch, w_out, w_mlp_in, w_mlp_out, loss_target, m_c_ctx, m_w_mod, m_b_mod, m_g_norm, m_w_in, m_att_q_gain, m_att_k_gain, m_dn_conv, m_dn_a_log, m_dn_dt_bias, m_dn_norm_gain, m_ret_decay, m_sc_conv, m_w_branch, m_w_out, m_w_mlp_in, m_w_mlp_out, v_c_ctx, v_w_mod, v_b_mod, v_g_norm, v_w_in, v_att_q_gain, v_att_k_gain, v_dn_conv, v_dn_a_log, v_dn_dt_bias, v_dn_norm_gain, v_ret_decay, v_sc_conv, v_w_branch, v_w_out, v_w_mlp_in, v_w_mlp_out):
    wts = dict(c_ctx=c_ctx, w_mod=w_mod, b_mod=b_mod, g_norm=g_norm, w_in=w_in, att_q_gain=att_q_gain, att_k_gain=att_k_gain,
               dn_conv=dn_conv, dn_a_log=dn_a_log, dn_dt_bias=dn_dt_bias, dn_norm_gain=dn_norm_gain, ret_decay=ret_decay,
               sc_conv=sc_conv, w_branch=w_branch, w_out=w_out, w_mlp_in=w_mlp_in, w_mlp_out=w_mlp_out)
    mom = dict(c_ctx=m_c_ctx, w_mod=m_w_mod, b_mod=m_b_mod, g_norm=m_g_norm, w_in=m_w_in, att_q_gain=m_att_q_gain,
               att_k_gain=m_att_k_gain, dn_conv=m_dn_conv, dn_a_log=m_dn_a_log, dn_dt_bias=m_dn_dt_bias,
               dn_norm_gain=m_dn_norm_gain, ret_decay=m_ret_decay, sc_conv=m_sc_conv, w_branch=m_w_branch, w_out=m_w_out,
               w_mlp_in=m_w_mlp_in, w_mlp_out=m_w_mlp_out)
    var = dict(c_ctx=v_c_ctx, w_mod=v_w_mod, b_mod=v_b_mod, g_norm=v_g_norm, w_in=v_w_in, att_q_gain=v_att_q_gain,
               att_k_gain=v_att_k_gain, dn_conv=v_dn_conv, dn_a_log=v_dn_a_log, dn_dt_bias=v_dn_dt_bias,
               dn_norm_gain=v_dn_norm_gain, ret_decay=v_ret_decay, sc_conv=v_sc_conv, w_branch=v_w_branch, w_out=v_w_out,
               w_mlp_in=v_w_mlp_in, w_mlp_out=v_w_mlp_out)
    names = list(wts)
    depth, bsz, seq, lc = w_mod.shape[0], x.shape[0], x.shape[1], ctx.shape[1]
    g = Geo(bsz, lc + seq, lc)
    xi, yi, ci = _coords()
    dev, xy = 4 * xi + 2 * yi + ci, 2 * xi + yi
    n_batch = N_DEV * bsz
    nr = -(-(n_batch + 1) // 16) * 16
    mod_cols = w_mod.shape[2]

    c_all = allgather8("gather_c", c).reshape(n_batch, D_MODEL)
    call = jnp.concatenate([c_all, c_ctx[None], jnp.zeros((nr - n_batch - 1, D_MODEL), F32)], axis=0)
    b_sh = lax.dynamic_slice_in_dim(b_mod, xy * mod_cols, mod_cols, axis=1)[:, None, :]
    mod_sh = mod_fwd(call, w_mod, b_sh)
    mod_g = allgather8("gather_mod", mod_sh.reshape(depth * nr, mod_cols)).reshape(N_XY, 2, depth, nr, mod_cols)[:, 0]
    mod_all = mod_g.transpose(1, 2, 0, 3).reshape(depth, nr, N_XY * mod_cols)
    mod_lat = lax.dynamic_slice_in_dim(mod_all, dev * bsz, bsz, axis=1).reshape(depth, bsz, 6, D_MODEL)
    mod_ctx = jnp.broadcast_to(mod_all[:, n_batch].reshape(depth, 1, 6, D_MODEL), (depth, bsz, 6, D_MODEL))
    mod = jnp.stack([mod_ctx, mod_lat], axis=2)

    sm_shapes = [wts[k].shape for k in SMALL_SHARDED]
    sm_g = allgather8("gather_small", pack_rows([wts[k] for k in SMALL_SHARDED])).reshape(N_XY, 2, -1)[:, 0]
    small = {k: wts[k] for k in SMALL_ORDER}
    for k, parts in zip(SMALL_SHARDED, zip(*[unpack_rows(sm_g[p], sm_shapes) for p in range(N_XY)])):
        small[k] = jnp.concatenate(parts, axis=-1)

    shards = [[wts[k][l].astype(BF16) for k, _ in BIG] for l in range(depth)]

    xa = jnp.concatenate([ctx, x], axis=1)
    loss_part, dxa, mine, theirs, gsmall, dmod = model_step(g, xa, loss_target, mod, shards, small)
    loss = lax.psum(loss_part, ("x", "y", "c"))
    grad_x = dxa[:, lc:]

    grads, deltas, new_m, new_v = {}, {}, {}, {}

    def update(k, parts, shape2d):
        res = adamw("adamw_" + k, wts[k].reshape(shape2d), mom[k].reshape(shape2d), var[k].reshape(shape2d), parts)
        grads[k], deltas[k], new_m[k], new_v[k] = (r.reshape(wts[k].shape) for r in res)

    for i, (k, _) in enumerate(BIG):
        cols = wts[k].shape[-1]
        slots = lambda per_layer: [s[i].reshape(N_XY, -1, cols) for s in per_layer]
        res = adamw_slots("adamw_" + k, wts[k].reshape(-1, cols), mom[k].reshape(-1, cols), var[k].reshape(-1, cols),
                          slots(mine), slots(theirs))
        grads[k], deltas[k], new_m[k], new_v[k] = (r.reshape(wts[k].shape) for r in res)

    dm_mine = jnp.concatenate([dmod[:, :, 1], dmod[:, :, 0]], axis=1).reshape(depth * 2 * bsz, 6 * D_MODEL)
    dm_all = allgather8("gather_dmod", dm_mine).reshape(N_DEV, depth, 2 * bsz, 6 * D_MODEL)
    gb = bmod_grad(dm_all).reshape(depth, 6 * D_MODEL)
    dm_cols = lax.dynamic_slice_in_dim(dm_all, xy * mod_cols, mod_cols, axis=3)
    d_lat = dm_cols[:, :, :bsz].transpose(1, 0, 2, 3).reshape(depth, n_batch, mod_cols)
    d_lat = jnp.pad(d_lat, ((0, 0), (0, nr - n_batch), (0, 0)))
    d_ctx = dm_cols[:, :, bsz:].transpose(1, 0, 2, 3).reshape(depth, n_batch, mod_cols)
    gw_mod, gc_part = mod_bwd(call, c_ctx[None], d_lat, d_ctx, w_mod, n_batch)
    update("w_mod", [gw_mod.reshape(depth * D_MODEL, mod_cols)], (depth * D_MODEL, mod_cols))
    update("b_mod", [gb], b_mod.shape)

    pack_all = pack_rows([gsmall[k] for k in SMALL_ORDER])
    pack_xy = pack_rows([gc_part])
    rows_all = pack_all.shape[0]
    tot = small_reduce(allgather8("gather_gsmall", jnp.concatenate([pack_all, pack_xy], axis=0)), rows_all)
    gtot = dict(zip(SMALL_ORDER, unpack_rows(tot[:rows_all], [gsmall[k].shape for k in SMALL_ORDER])))
    gtot["c_ctx"] = unpack_rows(tot[rows_all:], [c_ctx.shape])[0]
    for k in SMALL_SHARDED:
        width = wts[k].shape[-1]
        gtot[k] = lax.dynamic_slice_in_dim(gtot[k], xy * width, width, axis=gtot[k].ndim - 1)
    sm_names = ("c_ctx",) + SMALL_ORDER
    sm_shapes = [wts[k].shape for k in sm_names]
    res = adamw("adamw_small", pack_rows([wts[k] for k in sm_names]), pack_rows([mom[k] for k in sm_names]),
                pack_rows([var[k] for k in sm_names]), [pack_rows([gtot[k] for k in sm_names])])
    for dst, buf in zip((grads, deltas, new_m, new_v), res):
        dst.update(zip(sm_names, unpack_rows(buf, sm_shapes)))

    return (loss, grad_x, *[grads[k] for k in names], *[deltas[k] for k in names], *[new_m[k] for k in names],
            *[new_v[k] for k in names])
```
